```python
import math
import jax, jax.numpy as jnp
from jax import lax
import numpy as np


D_MODEL = 1024
BATCH = 8
SEQ = 4096
DEPTH = 1

SSD_EXPAND = 2
SSD_D_INNER = SSD_EXPAND * D_MODEL
SSD_HEAD_DIM = 64
SSD_N_HEADS = SSD_D_INNER // SSD_HEAD_DIM
SSD_N_GROUPS = 4
SSD_D_STATE = 128
SSD_CONV = 4
SSD_CHUNK = 128
SSD_CONV_DIM = SSD_D_INNER + 2 * SSD_N_GROUPS * SSD_D_STATE
ATT_HEAD_DIM = 64
ATT_N_HEADS = D_MODEL // ATT_HEAD_DIM
ATT_N_KV = 4
ATT_WINDOW = 128
ATT_BLOCK = 128
ATT_SCALE = ATT_HEAD_DIM ** -0.5
ROPE_THETA = 500000.0
ROPE_DIM = ATT_HEAD_DIM // 4
MOE_N_GROUPS = 8
MOE_EXPERTS_PER_GROUP = 8
MOE_N_EXPERTS = MOE_N_GROUPS * MOE_EXPERTS_PER_GROUP
MOE_TOP_K = 2
MOE_D_FF = 256
MOE_BLOCK = 128
RMS_EPS = 1e-6

COL_Z = SSD_D_INNER
COL_XBC = SSD_CONV_DIM
COL_DT = SSD_N_HEADS
COL_Q = ATT_N_HEADS * ATT_HEAD_DIM
COL_K = ATT_N_KV * ATT_HEAD_DIM
COL_V = ATT_N_KV * ATT_HEAD_DIM
COL_GATE = D_MODEL
IN_COLS = COL_Z + COL_XBC + COL_DT + COL_Q + COL_K + COL_V + 2 * COL_GATE
IN_SPLITS = (COL_Z,
             COL_Z + COL_XBC,
             COL_Z + COL_XBC + COL_DT,
             COL_Z + COL_XBC + COL_DT + COL_Q,
             COL_Z + COL_XBC + COL_DT + COL_Q + COL_K,
             COL_Z + COL_XBC + COL_DT + COL_Q + COL_K + COL_V,
             COL_Z + COL_XBC + COL_DT + COL_Q + COL_K + COL_V + COL_GATE)

kernel_name = 'hybrid_ssd_swa_sink_hmoe_block'


def rms_norm(t, g):
    tf = t.astype(jnp.float32)
    tf = tf * lax.rsqrt(jnp.mean(tf * tf, axis=-1, keepdims=True) + RMS_EPS)
    return (tf * g.astype(jnp.float32)).astype(t.dtype)


def ssd_branch(z, xbc, dt_raw, conv_w, conv_b, dt_bias, a_log, d_skip, ssd_norm_g):
    b, s, _ = xbc.shape
    G, J, P, N, L = SSD_N_GROUPS, SSD_N_HEADS // SSD_N_GROUPS, SSD_HEAD_DIM, SSD_D_STATE, SSD_CHUNK
    nc = s // L
    xbc = lax.conv_general_dilated(xbc, conv_w[:, None, :].astype(xbc.dtype), window_strides=(1,),
                                   padding=[(SSD_CONV - 1, 0)], dimension_numbers=('NWC', 'WIO', 'NWC'),
                                   feature_group_count=SSD_CONV_DIM) + conv_b
    xbc = jax.nn.silu(xbc)
    xs, bm, cm = jnp.split(xbc, [SSD_D_INNER, SSD_D_INNER + G * N], axis=-1)
    xs = xs.reshape(b, nc, L, G, J, P)
    bm = bm.reshape(b, nc, L, G, N)
    cm = cm.reshape(b, nc, L, G, N)
    dt = jax.nn.softplus(dt_raw.astype(jnp.float32) + dt_bias.astype(jnp.float32))
    a = -jnp.exp(a_log.astype(jnp.float32))
    dt_c = dt.reshape(b, nc, L, G, J)
    a_cum = jnp.cumsum(dt_c * a.reshape(G, J), axis=2)
    xdt = xs * dt_c[..., None].astype(xs.dtype)
    causal = (jnp.arange(L)[:, None] >= jnp.arange(L)[None, :])[None, None, :, :, None, None]
    seg = a_cum[:, :, :, None] - a_cum[:, :, None, :]
    decay = jnp.exp(jnp.where(causal, seg, -jnp.inf)).astype(xs.dtype)
    cb = jnp.einsum('bclgn,bcsgn->bclsg', cm, bm)
    y_diag = jnp.einsum('bclsgj,bcsgjp->bclgjp', cb[..., None] * decay, xdt)
    decay_to_end = jnp.exp(a_cum[:, :, -1:] - a_cum).astype(xs.dtype)
    states = jnp.einsum('bclgn,bclgjp->bcgjpn', bm, xdt * decay_to_end[..., None])
    chunk_decay = jnp.exp(a_cum[:, :, -1]).astype(states.dtype)

    def step(carry, inp):
        st, dec = inp
        return carry * dec[..., None, None] + st, carry

    init = jnp.zeros((b, G, J, P, N), states.dtype)
    _, prev = lax.scan(step, init, (jnp.moveaxis(states, 1, 0), jnp.moveaxis(chunk_decay, 1, 0)))
    prev = jnp.moveaxis(prev, 0, 1)
    y_off = jnp.einsum('bclgn,bcgjpn->bclgjp', cm, prev) * jnp.exp(a_cum)[..., None].astype(xs.dtype)
    y = (y_diag + y_off + xs * d_skip.reshape(G, J)[:, :, None].astype(xs.dtype)).reshape(b, s, SSD_D_INNER)
    yz = (y * jax.nn.silu(z)).reshape(b, s, G, SSD_D_INNER // G)
    return rms_norm(yz, ssd_norm_g.reshape(G, -1)).reshape(b, s, SSD_D_INNER)


def partial_rope(t, cos, sin):
    half = ROPE_DIM // 2
    t1, t2, rest = t[..., :half], t[..., half:ROPE_DIM], t[..., ROPE_DIM:]
    cos = cos.astype(t.dtype)
    sin = sin.astype(t.dtype)
    return jnp.concatenate([t1 * cos - t2 * sin, t2 * cos + t1 * sin, rest], axis=-1)


def swa_sink_branch(q, k, v, positions, q_norm_g, k_norm_g, sinks):
    b, s, _ = q.shape
    HD, KV, QG, BLK = ATT_HEAD_DIM, ATT_N_KV, ATT_N_HEADS // ATT_N_KV, ATT_BLOCK
    nb = s // BLK
    q = rms_norm(q.reshape(b, s, ATT_N_HEADS, HD), q_norm_g)
    k = rms_norm(k.reshape(b, s, KV, HD), k_norm_g)
    v = v.reshape(b, s, KV, HD)
    inv_freq = ROPE_THETA ** (-jnp.arange(0, ROPE_DIM, 2, dtype=jnp.float32) / ROPE_DIM)
    ang = positions.astype(jnp.float32)[..., None] * inv_freq
    cos, sin = jnp.cos(ang)[:, :, None, :], jnp.sin(ang)[:, :, None, :]
    q = partial_rope(q, cos, sin)
    k = partial_rope(k, cos, sin)
    qb = q.reshape(b, nb, BLK, KV, QG, HD)
    kb = k.reshape(b, nb, BLK, KV, HD)
    vb = v.reshape(b, nb, BLK, KV, HD)
    pad = ((0, 0), (1, 0), (0, 0), (0, 0), (0, 0))
    kk = jnp.concatenate([jnp.pad(kb, pad)[:, :-1], kb], axis=2)
    vv = jnp.concatenate([jnp.pad(vb, pad)[:, :-1], vb], axis=2)
    scores = jnp.einsum('bnqhgd,bnkhd->bnhgqk', qb, kk).astype(jnp.float32) * ATT_SCALE
    qi = jnp.arange(nb)[:, None, None] * BLK + jnp.arange(BLK)[None, :, None]
    ki = (jnp.arange(nb)[:, None, None] - 1) * BLK + jnp.arange(2 * BLK)[None, None, :]
    diff = qi - ki
    mask = (diff >= 0) & (diff < ATT_WINDOW) & (ki >= 0)
    scores = jnp.where(mask[None, :, None, None], scores, -jnp.inf)
    sink = sinks.astype(jnp.float32).reshape(KV, QG)[None, None, :, :, None]
    m = jnp.maximum(jnp.max(scores, axis=-1), sink)
    p = jnp.exp(scores - m[..., None])
    denom = jnp.sum(p, axis=-1) + jnp.exp(sink - m)
    p = (p / denom[..., None]).astype(v.dtype)
    out = jnp.einsum('bnhgqk,bnkhd->bnqhgd', p, vv)
    return out.reshape(b, s, ATT_N_HEADS * HD)


def hier_moe(h2, w_rg, b_rg, w_re, b_re, w_gate_e, w_up_e, w_down_e):
    t = h2.shape[0]
    rows = jnp.arange(t)
    g_logits = jnp.dot(h2, w_rg).astype(jnp.float32) + b_rg.astype(jnp.float32)
    g_prob = jax.nn.softmax(g_logits, axis=-1)
    g_sel = jnp.argmax(g_logits, axis=-1).astype(jnp.int32)
    p_group = g_prob[rows, g_sel]
    e_logits_all = jnp.einsum('td,gde->tge', h2, w_re).astype(jnp.float32) + b_re.astype(jnp.float32)
    e_logits = e_logits_all[rows, g_sel]
    top_v, top_i = lax.top_k(e_logits, MOE_TOP_K)
    gate = p_group[:, None] * jax.nn.softmax(top_v, axis=-1)
    expert_id = g_sel[:, None] * MOE_EXPERTS_PER_GROUP + top_i.astype(jnp.int32)
    n_assign = t * MOE_TOP_K
    flat_e = expert_id.reshape(-1)
    flat_w = gate.reshape(-1)
    flat_t = jnp.arange(n_assign, dtype=jnp.int32) // MOE_TOP_K
    order = jnp.argsort(flat_e)
    sorted_e = flat_e[order]
    counts = jnp.bincount(flat_e, length=MOE_N_EXPERTS)
    starts = jnp.cumsum(counts) - counts
    padded = (counts + MOE_BLOCK - 1) // MOE_BLOCK * MOE_BLOCK
    padded_end = jnp.cumsum(padded)
    padded_start = padded_end - padded
    dest = padded_start[sorted_e] + jnp.arange(n_assign) - starts[sorted_e]
    n_rows = n_assign + MOE_N_EXPERTS * MOE_BLOCK
    n_blocks = n_rows // MOE_BLOCK
    tok = jnp.zeros((n_rows,), jnp.int32).at[dest].set(flat_t[order])
    wt = jnp.zeros((n_rows,), jnp.float32).at[dest].set(flat_w[order])
    block_e = jnp.minimum(jnp.searchsorted(padded_end, jnp.arange(n_blocks) * MOE_BLOCK, side='right'),
                          MOE_N_EXPERTS - 1)
    xs = h2[tok].reshape(n_blocks, MOE_BLOCK, -1)

    def expert_block(args):
        xb, e = args
        hid = jax.nn.silu(xb @ w_gate_e[e]) * (xb @ w_up_e[e])
        return hid @ w_down_e[e]

    ys = lax.map(expert_block, (xs, block_e)).reshape(n_rows, -1)
    return jnp.zeros_like(h2).at[tok].add(ys * wt[:, None].astype(ys.dtype))


def hybrid_layer(x, positions, norm1_g, w_in, conv_w, conv_b, dt_bias, a_log, d_skip, ssd_norm_g,
                 w_ssd_out, q_norm_g, k_norm_g, sinks, w_attn_out, w_out, norm2_g, w_router_group,
                 b_router_group, w_router_expert, b_router_expert, w_gate_e, w_up_e, w_down_e):
    b, s, d = x.shape
    h = rms_norm(x, norm1_g)
    proj = h @ w_in
    z, xbc, dt_raw, q, k, v, g_ssd, g_att = jnp.split(proj, IN_SPLITS, axis=-1)
    y_ssd = ssd_branch(z, xbc, dt_raw, conv_w, conv_b, dt_bias, a_log, d_skip, ssd_norm_g) @ w_ssd_out
    y_att = swa_sink_branch(q, k, v, positions, q_norm_g, k_norm_g, sinks) @ w_attn_out
    merged = jax.nn.sigmoid(g_ssd) * y_ssd + jax.nn.sigmoid(g_att) * y_att
    x = x + merged @ w_out
    h2 = rms_norm(x, norm2_g).reshape(b * s, d)
    y_moe = hier_moe(h2, w_router_group, b_router_group, w_router_expert, b_router_expert,
                     w_gate_e, w_up_e, w_down_e)
    return x + y_moe.reshape(b, s, d)


def setup_inputs(seed: int = 0) -> dict:
    key = jax.random.key(seed)
    ks = jax.random.split(key, 24)
    L = DEPTH

    def nrm(k, shape, scale):
        return jax.random.normal(k, shape, jnp.float32) * scale

    x = nrm(ks[0], (BATCH, SEQ, D_MODEL), 1.0)
    positions = (jax.random.randint(ks[1], (BATCH, 1), 0, 1024, dtype=jnp.int32)
                 + jnp.arange(SEQ, dtype=jnp.int32)[None, :])
    dt0 = jnp.exp(jax.random.uniform(ks[6], (L, SSD_N_HEADS), jnp.float32,
                                     minval=math.log(1e-3), maxval=math.log(1e-1)))
    return {
        'x': x,
        'positions': positions,
        'norm1_g': 1.0 + nrm(ks[2], (L, D_MODEL), 0.02),
        'w_in': nrm(ks[3], (L, D_MODEL, IN_COLS), D_MODEL ** -0.5),
        'conv_w': nrm(ks[4], (L, SSD_CONV, SSD_CONV_DIM), SSD_CONV ** -0.5),
        'conv_b': nrm(ks[5], (L, SSD_CONV_DIM), 0.02),
        'dt_bias': dt0 + jnp.log(-jnp.expm1(-dt0)),
        'a_log': jnp.log(jax.random.uniform(ks[7], (L, SSD_N_HEADS), jnp.float32, minval=1.0, maxval=16.0)),
        'd_skip': 1.0 + nrm(ks[8], (L, SSD_N_HEADS), 0.1),
        'ssd_norm_g': 1.0 + nrm(ks[9], (L, SSD_D_INNER), 0.02),
        'w_ssd_out': nrm(ks[10], (L, SSD_D_INNER, D_MODEL), SSD_D_INNER ** -0.5),
        'q_norm_g': 1.0 + nrm(ks[11], (L, ATT_HEAD_DIM), 0.02),
        'k_norm_g': 1.0 + nrm(ks[12], (L, ATT_HEAD_DIM), 0.02),
        'sinks': nrm(ks[13], (L, ATT_N_HEADS), 0.5),
        'w_attn_out': nrm(ks[14], (L, ATT_N_HEADS * ATT_HEAD_DIM, D_MODEL), (ATT_N_HEADS * ATT_HEAD_DIM) ** -0.5),
        'w_out': nrm(ks[15], (L, D_MODEL, D_MODEL), D_MODEL ** -0.5),
        'norm2_g': 1.0 + nrm(ks[16], (L, D_MODEL), 0.02),
        'w_router_group': nrm(ks[17], (L, D_MODEL, MOE_N_GROUPS), D_MODEL ** -0.5),
        'b_router_group': nrm(ks[18], (L, MOE_N_GROUPS), 0.01),
        'w_router_expert': nrm(ks[19], (L, MOE_N_GROUPS, D_MODEL, MOE_EXPERTS_PER_GROUP), D_MODEL ** -0.5),
        'b_router_expert': nrm(ks[20], (L, MOE_N_GROUPS, MOE_EXPERTS_PER_GROUP), 0.01),
        'w_gate_e': nrm(ks[21], (L, MOE_N_EXPERTS, D_MODEL, MOE_D_FF), D_MODEL ** -0.5),
        'w_up_e': nrm(ks[22], (L, MOE_N_EXPERTS, D_MODEL, MOE_D_FF), D_MODEL ** -0.5),
        'w_down_e': nrm(ks[23], (L, MOE_N_EXPERTS, MOE_D_FF, D_MODEL), MOE_D_FF ** -0.5),
    }


def reference(x, positions, norm1_g, w_in, conv_w, conv_b, dt_bias, a_log, d_skip, ssd_norm_g,
              w_ssd_out, q_norm_g, k_norm_g, sinks, w_attn_out, w_out, norm2_g, w_router_group,
              b_router_group, w_router_expert, b_router_expert, w_gate_e, w_up_e, w_down_e):
    for l in range(DEPTH):
        x = hybrid_layer(x, positions, norm1_g[l], w_in[l], conv_w[l], conv_b[l], dt_bias[l], a_log[l],
                         d_skip[l], ssd_norm_g[l], w_ssd_out[l], q_norm_g[l], k_norm_g[l], sinks[l],
                         w_attn_out[l], w_out[l], norm2_g[l], w_router_group[l], b_router_group[l],
                         w_router_expert[l], b_router_expert[l], w_gate_e[l], w_up_e[l], w_down_e[l])
    return x
```

```python
import functools

import jax
import jax.numpy as jnp
from jax import lax
from jax.experimental import pallas as pl
from jax.experimental.pallas import tpu as pltpu

f32 = jnp.float32
bf16 = jnp.bfloat16

D_MODEL = 1024
SSD_D_INNER = 2048
SSD_HEAD_DIM = 64
SSD_N_HEADS = 32
SSD_N_GROUPS = 4
SSD_D_STATE = 128
SSD_CONV = 4
SSD_CHUNK = 128
SSD_CONV_DIM = 3072
ATT_HEAD_DIM = 64
ATT_N_HEADS = 16
ATT_N_KV = 4
ATT_BLOCK = 128
ATT_SCALE = ATT_HEAD_DIM ** -0.5
ROPE_THETA = 500000.0
ROPE_DIM = 16
MOE_N_GROUPS = 8
MOE_EPG = 8
MOE_N_EXPERTS = 64
MOE_D_FF = 256
MOE_BLOCK = 128
RMS_EPS = 1e-6

LANES = 128
CONV_HALO = 8
NEG_BIG = -1e30
VMEM_LIMIT = 56 * 1024 * 1024

COL_Z = SSD_D_INNER
COL_XBC = SSD_CONV_DIM
COL_DT = SSD_N_HEADS
COL_Q = ATT_N_HEADS * ATT_HEAD_DIM
COL_KV = ATT_N_KV * ATT_HEAD_DIM


def _sigmoid(x):
    return 1.0 / (1.0 + jnp.exp(-x))


def _silu(x):
    return x * _sigmoid(x)


def _split3(x):
    hi = x.astype(bf16)
    r1 = x - hi.astype(f32)
    mid = r1.astype(bf16)
    lo = (r1 - mid.astype(f32)).astype(bf16)
    return hi, mid, lo


def _dot(a, b):
    return jnp.dot(a, b, preferred_element_type=f32)


def _dot_nt(a, b):
    return lax.dot_general(a, b, (((1,), (1,)), ((), ())), preferred_element_type=f32)


def _trig_kernel(freq_ref, pos_ref, cos_ref, sin_ref):
    ang = pos_ref[...].astype(f32) * freq_ref[pl.program_id(0)]
    cos_ref[0] = jnp.cos(ang)
    sin_ref[0] = jnp.sin(ang)


def _rope_patterns(positions):
    t = positions.size
    half = ROPE_DIM // 2
    inv_freq = ROPE_THETA ** (-jnp.arange(0, ROPE_DIM, 2, dtype=f32) / ROPE_DIM)
    pos2d = positions.reshape(t // LANES, LANES)
    cos_t, sin_t = pl.pallas_call(
        _trig_kernel,
        grid_spec=pltpu.PrefetchScalarGridSpec(
            num_scalar_prefetch=1,
            grid=(half,),
            in_specs=[pl.BlockSpec((t // LANES, LANES), lambda j, f: (0, 0))],
            out_specs=[pl.BlockSpec((1, t // LANES, LANES), lambda j, f: (j, 0, 0))] * 2,
        ),
        out_shape=[jax.ShapeDtypeStruct((half, t // LANES, LANES), f32)] * 2,
        name="trig",
    )(inv_freq, pos2d)
    cos = cos_t.reshape(half, t).T
    sin = sin_t.reshape(half, t).T
    rest = ATT_HEAD_DIM - ROPE_DIM
    c_head = jnp.concatenate([cos, cos, jnp.ones((t, rest), f32)], axis=1)
    s1_head = jnp.concatenate([-sin, jnp.zeros((t, ATT_HEAD_DIM - half), f32)], axis=1)
    s2_head = jnp.concatenate([jnp.zeros((t, half), f32), sin, jnp.zeros((t, rest), f32)], axis=1)
    rep = LANES // ATT_HEAD_DIM
    return jnp.tile(c_head, (1, rep)), jnp.tile(s1_head, (1, rep)), jnp.tile(s2_head, (1, rep))


INPROJ_TM = 256
INPROJ_CH = 512


def _inproj_kernel(x_ref, g_ref, wm_ref, wdt_ref,
                   z_ref, xbc_ref, q_ref, k_ref, v_ref, gs_ref, ga_ref, dt_ref, h_scr):
    x = x_ref[...]
    ms = jnp.mean(x * x, axis=-1, keepdims=True)
    h_scr[...] = (x * lax.rsqrt(ms + RMS_EPS) * g_ref[...]).astype(bf16)
    off = 0
    for ref in (z_ref, xbc_ref, q_ref, k_ref, v_ref, gs_ref, ga_ref):
        width = ref.shape[1]
        for c in range(0, width, INPROJ_CH):
            cw = min(INPROJ_CH, width - c)
            ref[:, c:c + cw] = _dot(h_scr[...], wm_ref[:, off + c:off + c + cw]).astype(ref.dtype)
        off += width
    dt_ref[...] = _dot(h_scr[...], wdt_ref[...])


def _inproj(x2, norm1_g, w_in):
    t = x2.shape[0]
    tm = min(INPROJ_TM, t)
    s0 = COL_Z
    s1 = s0 + COL_XBC
    s2 = s1 + COL_DT
    w_main = jnp.concatenate([w_in[:, :s1], w_in[:, s2:]], axis=1).astype(bf16)
    w_dt = jnp.pad(w_in[:, s1:s2], ((0, 0), (0, LANES - COL_DT))).astype(bf16)
    widths = (COL_Z, COL_XBC, COL_Q, COL_KV, COL_KV, D_MODEL, D_MODEL)
    n_main = sum(widths)
    const = lambda i: (0, 0)
    row = lambda i: (i, 0)
    outs = pl.pallas_call(
        _inproj_kernel,
        grid=(t // tm,),
        in_specs=[pl.BlockSpec((tm, D_MODEL), row),
                  pl.BlockSpec((1, D_MODEL), const),
                  pl.BlockSpec((D_MODEL, n_main), const, pipeline_mode=pl.Buffered(1)),
                  pl.BlockSpec((D_MODEL, LANES), const, pipeline_mode=pl.Buffered(1))],
        out_specs=[pl.BlockSpec((tm, w), row) for w in widths] + [pl.BlockSpec((tm, LANES), row)],
        out_shape=[jax.ShapeDtypeStruct((t, w), bf16) for w in widths]
                  + [jax.ShapeDtypeStruct((t, LANES), f32)],
        scratch_shapes=[pltpu.VMEM((tm, D_MODEL), bf16)],
        compiler_params=pltpu.CompilerParams(dimension_semantics=("arbitrary",),
                                             vmem_limit_bytes=VMEM_LIMIT),
        name="inproj",
    )(x2, norm1_g.reshape(1, D_MODEL), w_main, w_dt)
    return outs


def _ssd_kernel(xbc_ref, z_ref, dt_ref, gs_ref, cw_ref, cb_ref, dtb_ref, alog_ref, dexp_ref, ng_ref,
                wout_ref, out_ref, ext_ref, st_ref, xs_ref, bm_ref, cm_ref, y_ref, hn_ref):
    L = SSD_CHUNK
    c = pl.program_id(1)

    @pl.when(c == 0)
    def _():
        ext_ref[0:CONV_HALO, :] = jnp.zeros((CONV_HALO, SSD_CONV_DIM), f32)
        st_ref[...] = jnp.zeros(st_ref.shape, f32)

    ext_ref[CONV_HALO:CONV_HALO + L, :] = xbc_ref[...].astype(f32)
    cch = 512
    for cc in range(0, SSD_CONV_DIM, cch):
        acc = cb_ref[:, cc:cc + cch]
        for k in range(SSD_CONV):
            lo = CONV_HALO - (SSD_CONV - 1) + k
            acc = acc + ext_ref[lo:lo + L, cc:cc + cch] * cw_ref[k:k + 1, cc:cc + cch]
        act = _silu(acc)
        if cc < SSD_D_INNER:
            xs_ref[:, cc:cc + cch] = act
        elif cc < SSD_D_INNER + SSD_N_GROUPS * SSD_D_STATE:
            bm_ref[:, cc - SSD_D_INNER:cc - SSD_D_INNER + cch] = act
        else:
            o = cc - SSD_D_INNER - SSD_N_GROUPS * SSD_D_STATE
            cm_ref[:, o:o + cch] = act
    ext_ref[0:CONV_HALO, :] = ext_ref[L:L + CONV_HALO, :]

    lane_row = lax.broadcasted_iota(jnp.int32, (1, LANES), 1)
    row_i = lax.broadcasted_iota(jnp.int32, (L, L), 0)
    col_i = lax.broadcasted_iota(jnp.int32, (L, L), 1)
    causal = row_i >= col_i
    left = col_i < SSD_HEAD_DIM

    xdt = dt_ref[...] + dtb_ref[...]
    dtv = jnp.maximum(xdt, 0.0) + jnp.log1p(jnp.exp(-jnp.abs(xdt)))
    a = jnp.where(lane_row < SSD_N_HEADS, -jnp.exp(alog_ref[...]), 0.0)
    d_a = dtv * a
    tril = jnp.where(causal, 1.0, 0.0).astype(bf16)
    hi, mid, lo3 = _split3(d_a)
    a_cum = _dot(tril, hi) + _dot(tril, mid) + _dot(tril, lo3)
    a_end = a_cum[L - 1:L, :]
    exp_a = jnp.exp(a_cum)
    w_end = jnp.exp(a_end - a_cum) * dtv
    cd = jnp.exp(a_end)
    a_t = a_cum.T
    dt_t = dtv.T
    w_t = w_end.T

    n_pairs = SSD_N_HEADS // 2
    pairs_per_group = n_pairs // SSD_N_GROUPS
    for g in range(SSD_N_GROUPS):
        b_g = bm_ref[:, g * SSD_D_STATE:(g + 1) * SSD_D_STATE]
        c_g = cm_ref[:, g * SSD_D_STATE:(g + 1) * SSD_D_STATE]
        cb = _dot_nt(c_g.astype(bf16), b_g.astype(bf16))
        b_t = b_g.T
        for pi in range(pairs_per_group):
            i = g * pairs_per_group + pi
            xpair = xs_ref[:, i * LANES:(i + 1) * LANES]
            xpair_b = xpair.astype(bf16)
            s_prev = st_ref[i]
            rhs = jnp.concatenate([xpair_b, s_prev.astype(bf16)], axis=0)
            ys = []
            sn = []
            for h in (2 * i, 2 * i + 1):
                acol = jnp.broadcast_to(a_cum[:, h:h + 1], (L, L))
                arow = jnp.broadcast_to(a_t[h:h + 1, :], (L, L))
                dtrow = jnp.broadcast_to(dt_t[h:h + 1, :], (L, L))
                dec = jnp.exp(jnp.where(causal, acol - arow, NEG_BIG))
                m = cb * dec * dtrow
                cs = c_g * jnp.broadcast_to(exp_a[:, h:h + 1], (L, L))
                lhs = jnp.concatenate([m.astype(bf16), cs.astype(bf16)], axis=1)
                ys.append(_dot(lhs, rhs))
                btw = (b_t * jnp.broadcast_to(w_t[h:h + 1, :], (L, L))).astype(bf16)
                sn.append(_dot(btw, xpair_b))
            h0 = 2 * i
            cd_pair = jnp.where(lane_row < SSD_HEAD_DIM, cd[:, h0:h0 + 1], cd[:, h0 + 1:h0 + 2])
            st_ref[i] = jnp.where(left, sn[0], sn[1]) + s_prev * cd_pair
            y_pair = jnp.where(left, ys[0], ys[1])
            y_ref[:, i * LANES:(i + 1) * LANES] = y_pair + xpair * dexp_ref[:, i * LANES:(i + 1) * LANES]

    gw = SSD_D_INNER // SSD_N_GROUPS
    for g in range(SSD_N_GROUPS):
        sl = slice(g * gw, (g + 1) * gw)
        yz = y_ref[:, sl] * _silu(z_ref[:, sl].astype(f32))
        ms = jnp.mean(yz * yz, axis=-1, keepdims=True)
        hn_ref[:, sl] = (yz * lax.rsqrt(ms + RMS_EPS) * ng_ref[:, sl]).astype(bf16)
    y_ssd = _dot(hn_ref[...], wout_ref[...])
    out_ref[...] = _sigmoid(gs_ref[...].astype(f32)) * y_ssd


def _ssd(xbc, z, dt, gs, conv_w, conv_b, dt_bias, a_log, d_skip, ssd_norm_g, w_ssd_out, batch, seq):
    t = batch * seq
    L = SSD_CHUNK
    nc = seq // L
    pad_h = LANES - SSD_N_HEADS
    dtb = jnp.pad(dt_bias, (0, pad_h)).reshape(1, LANES)
    alog = jnp.pad(a_log, (0, pad_h)).reshape(1, LANES)
    dexp = jnp.repeat(d_skip, SSD_HEAD_DIM).reshape(1, SSD_D_INNER)
    const = lambda b, c: (0, 0)
    row = lambda b, c: (b * nc + c, 0)
    return pl.pallas_call(
        _ssd_kernel,
        grid=(batch, nc),
        in_specs=[pl.BlockSpec((L, SSD_CONV_DIM), row),
                  pl.BlockSpec((L, SSD_D_INNER), row),
                  pl.BlockSpec((L, LANES), row),
                  pl.BlockSpec((L, D_MODEL), row),
                  pl.BlockSpec((SSD_CONV, SSD_CONV_DIM), const),
                  pl.BlockSpec((1, SSD_CONV_DIM), const),
                  pl.BlockSpec((1, LANES), const),
                  pl.BlockSpec((1, LANES), const),
                  pl.BlockSpec((1, SSD_D_INNER), const),
                  pl.BlockSpec((1, SSD_D_INNER), const),
                  pl.BlockSpec((SSD_D_INNER, D_MODEL), const, pipeline_mode=pl.Buffered(1))],
        out_specs=pl.BlockSpec((L, D_MODEL), row),
        out_shape=jax.ShapeDtypeStruct((t, D_MODEL), f32),
        scratch_shapes=[pltpu.VMEM((CONV_HALO + L, SSD_CONV_DIM), f32),
                        pltpu.VMEM((SSD_N_HEADS // 2, SSD_D_STATE, LANES), f32),
                        pltpu.VMEM((L, SSD_D_INNER), f32),
                        pltpu.VMEM((L, SSD_N_GROUPS * SSD_D_STATE), f32),
                        pltpu.VMEM((L, SSD_N_GROUPS * SSD_D_STATE), f32),
                        pltpu.VMEM((L, SSD_D_INNER), f32),
                        pltpu.VMEM((L, SSD_D_INNER), bf16)],
        compiler_params=pltpu.CompilerParams(dimension_semantics=("arbitrary", "arbitrary"),
                                             vmem_limit_bytes=VMEM_LIMIT),
        name="ssd",
    )(xbc, z, dt, gs, conv_w, conv_b.reshape(1, -1), dtb, alog, dexp,
      ssd_norm_g.reshape(1, -1), w_ssd_out.astype(bf16))


ROUTE_E, ROUTE_RANK, ROUTE_GATE = 0, 2, 4


def _attn_kernel(sink_ref, q_ref, k_ref, v_ref, c_ref, s1_ref, s2_ref, ga_ref, ms_ref, x_ref,
                 qg_ref, kg_ref, wao_ref, wo_ref, n2g_ref, wrh_ref, wrl_ref, br_ref,
                 x1_ref, h2_ref, route_ref, cnt_ref,
                 kprev_ref, vprev_ref, att_ref, cnt_scr):
    Q = ATT_BLOCK
    b = pl.program_id(0)
    n = pl.program_id(1)

    @pl.when(n == 0)
    def _():
        kprev_ref[...] = jnp.zeros(kprev_ref.shape, f32)
        vprev_ref[...] = jnp.zeros(vprev_ref.shape, f32)

    @pl.when((b == 0) & (n == 0))
    def _():
        cnt_scr[...] = jnp.zeros(cnt_scr.shape, f32)

    lane = lax.broadcasted_iota(jnp.int32, (Q, LANES), 1)
    row_q = lax.broadcasted_iota(jnp.int32, (Q, LANES), 0)
    left = lane < ATT_HEAD_DIM
    head_mean = jnp.where((row_q // ATT_HEAD_DIM) == (lane // ATT_HEAD_DIM),
                          1.0 / ATT_HEAD_DIM, 0.0).astype(bf16)
    cpat = c_ref[...]
    s1pat = s1_ref[...]
    s2pat = s2_ref[...]

    def norm_rope(tc, gpat):
        sq = tc * tc
        hi = sq.astype(bf16)
        lo = (sq - hi.astype(f32)).astype(bf16)
        ms = _dot(hi, head_mean) + _dot(lo, head_mean)
        tn = tc * lax.rsqrt(ms + RMS_EPS) * gpat
        return (tn * cpat + pltpu.roll(tn, LANES - ROPE_DIM // 2, 1) * s1pat
                + pltpu.roll(tn, ROPE_DIM // 2, 1) * s2pat)

    kd = []
    vd = []
    for cidx in range(COL_KV // LANES):
        sl = slice(cidx * LANES, (cidx + 1) * LANES)
        k_cur = norm_rope(k_ref[:, sl].astype(f32), kg_ref[...])
        v_cur = v_ref[:, sl].astype(f32)
        k_all = jnp.concatenate([kprev_ref[:, sl], k_cur], axis=0)
        v_all = jnp.concatenate([vprev_ref[:, sl], v_cur], axis=0)
        kprev_ref[:, sl] = k_cur
        vprev_ref[:, sl] = v_cur
        lane2 = lax.broadcasted_iota(jnp.int32, (2 * Q, LANES), 1) < ATT_HEAD_DIM
        k_sw = pltpu.roll(k_all, ATT_HEAD_DIM, 1)
        v_sw = pltpu.roll(v_all, ATT_HEAD_DIM, 1)
        kd.append(jnp.where(lane2, k_all, k_sw).astype(bf16))
        kd.append(jnp.where(lane2, k_sw, k_all).astype(bf16))
        vd.append(jnp.where(lane2, v_all, v_sw).astype(bf16))
        vd.append(jnp.where(lane2, v_sw, v_all).astype(bf16))

    qg = ATT_N_HEADS // ATT_N_KV
    rows = qg * Q
    ri = lax.broadcasted_iota(jnp.int32, (rows, 2 * Q), 0) % Q
    cj = lax.broadcasted_iota(jnp.int32, (rows, 2 * Q), 1)
    allowed = (cj > ri) & (cj <= ri + Q) & ((cj >= Q) | (n > 0))
    for h in range(ATT_N_KV):
        parts = []
        for cidx in (2 * h, 2 * h + 1):
            qc = norm_rope(q_ref[:, cidx * LANES:(cidx + 1) * LANES].astype(f32), qg_ref[...]) * ATT_SCALE
            parts.append(jnp.where(left, qc, 0.0).astype(bf16))
            parts.append(jnp.where(left, 0.0, qc).astype(bf16))
        lhs = jnp.concatenate(parts, axis=0)
        s = jnp.where(allowed, _dot_nt(lhs, kd[h]), NEG_BIG)
        sink = jnp.concatenate([jnp.full((Q, 1), sink_ref[qg * h + r], f32) for r in range(qg)], axis=0)
        m = jnp.maximum(jnp.max(s, axis=-1, keepdims=True), sink)
        p = jnp.exp(s - m)
        denom = jnp.sum(p, axis=-1, keepdims=True) + jnp.exp(sink - m)
        o = _dot((p / denom).astype(bf16), vd[h])
        for r in range(2):
            cidx = 2 * h + r
            att_ref[:, cidx * LANES:(cidx + 1) * LANES] = jnp.where(
                left, o[(2 * r) * Q:(2 * r + 1) * Q], o[(2 * r + 1) * Q:(2 * r + 2) * Q]).astype(bf16)

    y_att = _dot(att_ref[...], wao_ref[...])
    merged = _sigmoid(ga_ref[...].astype(f32)) * y_att + ms_ref[...]
    x1 = x_ref[...] + _dot(merged.astype(bf16), wo_ref[...])
    x1_ref[...] = x1
    h2 = x1 * lax.rsqrt(jnp.mean(x1 * x1, axis=-1, keepdims=True) + RMS_EPS) * n2g_ref[...]
    h2_ref[...] = h2

    hi = h2.astype(bf16)
    lo = (h2 - hi.astype(f32)).astype(bf16)
    logits = _dot(hi, wrh_ref[...]) + _dot(lo, wrh_ref[...]) + _dot(hi, wrl_ref[...]) + br_ref[...]
    big = 4 * LANES
    gl = jnp.where(lane < MOE_N_GROUPS, logits, NEG_BIG)
    gmax = jnp.max(gl, axis=-1, keepdims=True)
    gsel = jnp.min(jnp.where(gl == gmax, lane, big), axis=-1, keepdims=True)
    pg = 1.0 / jnp.sum(jnp.exp(gl - gmax), axis=-1, keepdims=True)
    lo_l = MOE_N_GROUPS + MOE_EPG * gsel
    el = jnp.where((lane >= lo_l) & (lane < lo_l + MOE_EPG), logits, NEG_BIG)
    v1 = jnp.max(el, axis=-1, keepdims=True)
    i1 = jnp.min(jnp.where(el == v1, lane, big), axis=-1, keepdims=True)
    el2 = jnp.where(lane == i1, NEG_BIG, el)
    v2 = jnp.max(el2, axis=-1, keepdims=True)
    i2 = jnp.min(jnp.where(el2 == v2, lane, big), axis=-1, keepdims=True)
    e1 = i1 - MOE_N_GROUPS
    e2 = i2 - MOE_N_GROUPS
    tt = jnp.exp(v2 - v1)
    w1 = pg * (1.0 / (1.0 + tt))
    w2 = pg * (tt / (1.0 + tt))

    onehot = jnp.where((lane == e1) | (lane == e2), 1.0, 0.0)
    strict = jnp.where(row_q > lane, 1.0, 0.0).astype(bf16)
    base = _dot(strict, onehot.astype(bf16)) + cnt_scr[0:1, :]
    r1 = jnp.sum(jnp.where(lane == e1, base, 0.0), axis=-1, keepdims=True)
    r2 = jnp.sum(jnp.where(lane == e2, base, 0.0), axis=-1, keepdims=True)
    cnt_new = cnt_scr[0:1, :] + jnp.sum(onehot, axis=0, keepdims=True)
    cnt_scr[...] = jnp.broadcast_to(cnt_new, cnt_scr.shape)
    cnt_ref[...] = jnp.broadcast_to(cnt_new, cnt_ref.shape)

    rec = jnp.zeros((Q, LANES), f32)
    for off, val in ((ROUTE_E, e1.astype(f32)), (ROUTE_E + 1, e2.astype(f32)),
                     (ROUTE_RANK, r1), (ROUTE_RANK + 1, r2), (ROUTE_GATE, w1), (ROUTE_GATE + 1, w2)):
        rec = jnp.where(lane == off, val, rec)
    route_ref[...] = rec


def _attn(q, k, v, cpat, s1pat, s2pat, ga, merged_ssd, x2, q_norm_g, k_norm_g, sinks, w_attn_out, w_out,
          norm2_g, w_rg, b_rg, w_re, b_re, batch, seq):
    t = batch * seq
    Q = ATT_BLOCK
    nb = seq // Q
    rep = LANES // ATT_HEAD_DIM
    qg = jnp.tile(q_norm_g, rep).reshape(1, LANES)
    kg = jnp.tile(k_norm_g, rep).reshape(1, LANES)
    n_log = MOE_N_GROUPS + MOE_N_EXPERTS
    w_r = jnp.concatenate([w_rg, jnp.transpose(w_re, (1, 0, 2)).reshape(D_MODEL, MOE_N_EXPERTS),
                           jnp.zeros((D_MODEL, LANES - n_log), f32)], axis=1)
    b_r = jnp.concatenate([b_rg, b_re.reshape(-1), jnp.zeros((LANES - n_log,), f32)]).reshape(1, LANES)
    w_r_hi = w_r.astype(bf16)
    w_r_lo = (w_r - w_r_hi.astype(f32)).astype(bf16)
    const = lambda b, n, *_: (0, 0)
    row = lambda b, n, *_: (b * nb + n, 0)
    full = lambda shape: pl.BlockSpec(shape, const, pipeline_mode=pl.Buffered(1))
    return pl.pallas_call(
        _attn_kernel,
        grid_spec=pltpu.PrefetchScalarGridSpec(
            num_scalar_prefetch=1,
            grid=(batch, nb),
            in_specs=[pl.BlockSpec((Q, COL_Q), row),
                      pl.BlockSpec((Q, COL_KV), row),
                      pl.BlockSpec((Q, COL_KV), row),
                      pl.BlockSpec((Q, LANES), row),
                      pl.BlockSpec((Q, LANES), row),
                      pl.BlockSpec((Q, LANES), row),
                      pl.BlockSpec((Q, D_MODEL), row),
                      pl.BlockSpec((Q, D_MODEL), row),
                      pl.BlockSpec((Q, D_MODEL), row),
                      pl.BlockSpec((1, LANES), const),
                      pl.BlockSpec((1, LANES), const),
                      full((COL_Q, D_MODEL)),
                      full((D_MODEL, D_MODEL)),
                      pl.BlockSpec((1, D_MODEL), const),
                      full((D_MODEL, LANES)),
                      full((D_MODEL, LANES)),
                      pl.BlockSpec((1, LANES), const)],
            out_specs=[pl.BlockSpec((Q, D_MODEL), row),
                       pl.BlockSpec((Q, D_MODEL), row),
                       pl.BlockSpec((Q, LANES), row),
                       pl.BlockSpec((8, LANES), const)],
            scratch_shapes=[pltpu.VMEM((Q, COL_KV), f32),
                            pltpu.VMEM((Q, COL_KV), f32),
                            pltpu.VMEM((Q, COL_Q), bf16),
                            pltpu.VMEM((8, LANES), f32)],
        ),
        out_shape=[jax.ShapeDtypeStruct((t, D_MODEL), f32),
                   jax.ShapeDtypeStruct((t, D_MODEL), f32),
                   jax.ShapeDtypeStruct((t, LANES), f32),
                   jax.ShapeDtypeStruct((8, LANES), f32)],
        compiler_params=pltpu.CompilerParams(dimension_semantics=("arbitrary", "arbitrary"),
                                             vmem_limit_bytes=VMEM_LIMIT),
        name="attn",
    )(sinks, q, k, v, cpat, s1pat, s2pat, ga, merged_ssd, x2, qg, kg,
      w_attn_out.astype(bf16), w_out.astype(bf16), norm2_g.reshape(1, D_MODEL), w_r_hi, w_r_lo, b_r)


MOE_TM = 256


def _row_copy(src, i, dst, j, sem):
    return pltpu.make_async_copy(src.at[pl.ds(i, 1), :], dst.at[pl.ds(j, 1), :], sem)


def _dispatch_kernel(dest_ref, zblk_ref, h2_ref, xs_hbm, zero_ref, sem, zsem):
    tm = h2_ref.shape[0]
    base = pl.program_id(0) * (2 * tm)

    @pl.when(pl.program_id(0) == 0)
    def _():
        zero_ref[...] = jnp.zeros(zero_ref.shape, f32)

        def zcopy(i):
            start = pl.multiple_of(zblk_ref[i] * MOE_BLOCK, MOE_BLOCK)
            return pltpu.make_async_copy(zero_ref, xs_hbm.at[pl.ds(start, MOE_BLOCK), :], zsem)

        def zstart(i, carry):
            @pl.when(zblk_ref[i] >= 0)
            def _():
                zcopy(i).start()
            return carry

        def zwait(i, carry):
            @pl.when(zblk_ref[i] >= 0)
            def _():
                zcopy(i).wait()
            return carry

        lax.fori_loop(0, zblk_ref.shape[0], zstart, 0)
        lax.fori_loop(0, zblk_ref.shape[0], zwait, 0)

    def issue(i, carry):
        _row_copy(h2_ref, i, xs_hbm, dest_ref[base + 2 * i], sem).start()
        _row_copy(h2_ref, i, xs_hbm, dest_ref[base + 2 * i + 1], sem).start()
        return carry

    lax.fori_loop(0, tm, issue, 0)

    def drain(i, carry):
        _row_copy(h2_ref, 0, xs_hbm, 0, sem).wait()
        _row_copy(h2_ref, 0, xs_hbm, 0, sem).wait()
        return carry

    lax.fori_loop(0, tm, drain, 0)


def _dispatch(h2, dest, zero_blocks, n_rows):
    t = h2.shape[0]
    tm = min(MOE_TM, t)
    return pl.pallas_call(
        _dispatch_kernel,
        grid_spec=pltpu.PrefetchScalarGridSpec(
            num_scalar_prefetch=2,
            grid=(t // tm,),
            in_specs=[pl.BlockSpec((tm, D_MODEL), lambda i, d, zb: (i, 0))],
            out_specs=pl.BlockSpec(memory_space=pl.ANY),
            scratch_shapes=[pltpu.VMEM((MOE_BLOCK, D_MODEL), f32),
                            pltpu.SemaphoreType.DMA(()),
                            pltpu.SemaphoreType.DMA(())],
        ),
        out_shape=jax.ShapeDtypeStruct((n_rows, D_MODEL), f32),
        compiler_params=pltpu.CompilerParams(dimension_semantics=("arbitrary",)),
        name="dispatch",
    )(dest, zero_blocks, h2)


def _expert_kernel(be_ref, nu_ref, xs_ref, wgu_ref, wd_ref, ys_ref):
    used = pl.program_id(0) < nu_ref[0]

    @pl.when(used)
    def _():
        gu = _dot(xs_ref[...].astype(bf16), wgu_ref[0])
        hid = _silu(gu[:, :MOE_D_FF]) * gu[:, MOE_D_FF:]
        ys_ref[...] = _dot(hid.astype(bf16), wd_ref[0])

    @pl.when(jnp.logical_not(used))
    def _():
        ys_ref[...] = jnp.zeros(ys_ref.shape, f32)


def _experts(xs, block_e, n_used, w_gu, w_d):
    n_rows = xs.shape[0]
    nblk = n_rows // MOE_BLOCK
    blk_in = lambda i, be, nu: (jnp.minimum(i, nu[0] - 1), 0)
    blk_out = lambda i, be, nu: (i, 0)
    wsel = lambda i, be, nu: (be[i], 0, 0)
    return pl.pallas_call(
        _expert_kernel,
        grid_spec=pltpu.PrefetchScalarGridSpec(
            num_scalar_prefetch=2,
            grid=(nblk,),
            in_specs=[pl.BlockSpec((MOE_BLOCK, D_MODEL), blk_in),
                      pl.BlockSpec((1, D_MODEL, 2 * MOE_D_FF), wsel),
                      pl.BlockSpec((1, MOE_D_FF, D_MODEL), wsel)],
            out_specs=pl.BlockSpec((MOE_BLOCK, D_MODEL), blk_out),
        ),
        out_shape=jax.ShapeDtypeStruct((n_rows, D_MODEL), f32),
        compiler_params=pltpu.CompilerParams(dimension_semantics=("arbitrary",)),
        name="experts",
    )(block_e, n_used, xs, w_gu, w_d)


def _combine_kernel(dest_ref, x1_ref, route_ref, ys_hbm, out_ref, y0_ref, y1_ref, sem):
    tm = x1_ref.shape[0]
    base = pl.program_id(0) * (2 * tm)

    def issue(i, carry):
        _row_copy(ys_hbm, dest_ref[base + 2 * i], y0_ref, i, sem).start()
        _row_copy(ys_hbm, dest_ref[base + 2 * i + 1], y1_ref, i, sem).start()
        return carry

    lax.fori_loop(0, tm, issue, 0)

    def drain(i, carry):
        _row_copy(ys_hbm, 0, y0_ref, 0, sem).wait()
        _row_copy(ys_hbm, 0, y1_ref, 0, sem).wait()
        return carry

    lax.fori_loop(0, tm, drain, 0)
    g0 = route_ref[:, ROUTE_GATE:ROUTE_GATE + 1]
    g1 = route_ref[:, ROUTE_GATE + 1:ROUTE_GATE + 2]
    out_ref[...] = x1_ref[...] + y0_ref[...] * g0 + y1_ref[...] * g1


def _combine(x1, route, ys, dest):
    t = x1.shape[0]
    tm = min(MOE_TM, t)
    row = lambda i, d: (i, 0)
    return pl.pallas_call(
        _combine_kernel,
        grid_spec=pltpu.PrefetchScalarGridSpec(
            num_scalar_prefetch=1,
            grid=(t // tm,),
            in_specs=[pl.BlockSpec((tm, D_MODEL), row),
                      pl.BlockSpec((tm, LANES), row),
                      pl.BlockSpec(memory_space=pl.ANY)],
            out_specs=pl.BlockSpec((tm, D_MODEL), row),
            scratch_shapes=[pltpu.VMEM((tm, D_MODEL), f32),
                            pltpu.VMEM((tm, D_MODEL), f32),
                            pltpu.SemaphoreType.DMA(())],
        ),
        out_shape=jax.ShapeDtypeStruct((t, D_MODEL), f32),
        compiler_params=pltpu.CompilerParams(dimension_semantics=("arbitrary",)),
        name="combine",
    )(dest, x1, route, ys)


def _moe(x1, h2, route, counts_rec, w_gate_e, w_up_e, w_down_e):
    t = x1.shape[0]
    n_rows = 2 * t + MOE_N_EXPERTS * MOE_BLOCK
    nblk = n_rows // MOE_BLOCK
    counts = counts_rec[0, :MOE_N_EXPERTS].astype(jnp.int32)
    padded = (counts + MOE_BLOCK - 1) // MOE_BLOCK * MOE_BLOCK
    pend = jnp.cumsum(padded)
    pstart = pend - padded
    eid = route[:, ROUTE_E:ROUTE_E + 2].astype(jnp.int32)
    rank = route[:, ROUTE_RANK:ROUTE_RANK + 2].astype(jnp.int32)
    dest = (pstart[eid] + rank).reshape(-1)
    block_e = jnp.minimum(jnp.searchsorted(pend, jnp.arange(nblk, dtype=jnp.int32) * MOE_BLOCK, side='right'),
                          MOE_N_EXPERTS - 1).astype(jnp.int32)
    n_used = (pend[-1:] // MOE_BLOCK).astype(jnp.int32)
    last_blk = jnp.where(counts > 0, pend // MOE_BLOCK - 1, -1)
    tail = n_used[0] + jnp.arange(MOE_N_EXPERTS, dtype=jnp.int32)
    zero_blocks = jnp.concatenate([last_blk, jnp.where(tail < nblk, tail, -1)]).astype(jnp.int32)
    w_gu = jnp.concatenate([w_gate_e, w_up_e], axis=-1).astype(bf16)
    xs = _dispatch(h2, dest, zero_blocks, n_rows)
    ys = _experts(xs, block_e, n_used, w_gu, w_down_e.astype(bf16))
    return _combine(x1, route, ys, dest)


def _layer(x, positions, norm1_g, w_in, conv_w, conv_b, dt_bias, a_log, d_skip, ssd_norm_g, w_ssd_out,
           q_norm_g, k_norm_g, sinks, w_attn_out, w_out, norm2_g, w_rg, b_rg, w_re, b_re,
           w_gate_e, w_up_e, w_down_e):
    batch, seq, d = x.shape
    x2 = x.reshape(batch * seq, d)
    cpat, s1pat, s2pat = _rope_patterns(positions)
    z, xbc, q, k, v, gs, ga, dt = _inproj(x2, norm1_g, w_in)
    merged_ssd = _ssd(xbc, z, dt, gs, conv_w, conv_b, dt_bias, a_log, d_skip, ssd_norm_g, w_ssd_out,
                      batch, seq)
    x1, h2, route, counts = _attn(q, k, v, cpat, s1pat, s2pat, ga, merged_ssd, x2, q_norm_g, k_norm_g,
                                  sinks, w_attn_out, w_out, norm2_g, w_rg, b_rg, w_re, b_re, batch, seq)
    out = _moe(x1, h2, route, counts, w_gate_e, w_up_e, w_down_e)
    return out.reshape(batch, seq, d)


def kernel(x, positions, norm1_g, w_in, conv_w, conv_b, dt_bias, a_log, d_skip, ssd_norm_g, w_ssd_out,
           q_norm_g, k_norm_g, sinks, w_attn_out, w_out, norm2_g, w_router_group, b_router_group,
           w_router_expert, b_router_expert, w_gate_e, w_up_e, w_down_e):
    for l in range(norm1_g.shape[0]):
        x = _layer(x, positions, norm1_g[l], w_in[l], conv_w[l], conv_b[l], dt_bias[l], a_log[l],
                   d_skip[l], ssd_norm_g[l], w_ssd_out[l], q_norm_g[l], k_norm_g[l], sinks[l],
                   w_attn_out[l], w_out[l], norm2_g[l], w_router_group[l], b_router_group[l],
                   w_router_expert[l], b_router_expert[l], w_gate_e[l], w_up_e[l], w_down_e[l])
    return x
```

```python
import functools

import numpy as np

import jax
import jax.numpy as jnp
from jax import lax
from jax.experimental import pallas as pl
from jax.experimental.pallas import tpu as pltpu

f32 = jnp.float32
bf16 = jnp.bfloat16

D_MODEL = 1024
SSD_D_INNER = 2048
SSD_HEAD_DIM = 64
SSD_N_HEADS = 32
SSD_N_GROUPS = 4
SSD_D_STATE = 128
SSD_CONV = 4
SSD_CHUNK = 128
SSD_CONV_DIM = 3072
ATT_HEAD_DIM = 64
ATT_N_HEADS = 16
ATT_N_KV = 4
ATT_BLOCK = 128
ATT_SCALE = ATT_HEAD_DIM ** -0.5
ROPE_THETA = 500000.0
ROPE_DIM = 16
MOE_N_GROUPS = 8
MOE_EPG = 8
MOE_N_EXPERTS = 64
MOE_D_FF = 256
MOE_BLOCK = 128
RMS_EPS = 1e-6

LANES = 128
CONV_HALO = 8
NEG_BIG = -1e30
VMEM_LIMIT = 56 * 1024 * 1024

COL_Z = SSD_D_INNER
COL_XBC = SSD_CONV_DIM
COL_DT = SSD_N_HEADS
COL_Q = ATT_N_HEADS * ATT_HEAD_DIM
COL_KV = ATT_N_KV * ATT_HEAD_DIM


def _sigmoid(x):
    return 1.0 / (1.0 + jnp.exp(-x))


def _silu(x):
    return x * _sigmoid(x)


def _split3(x):
    hi = x.astype(bf16)
    r1 = x - hi.astype(f32)
    mid = r1.astype(bf16)
    lo = (r1 - mid.astype(f32)).astype(bf16)
    return hi, mid, lo


def _dot(a, b):
    return jnp.dot(a, b, preferred_element_type=f32)


def _dot_nt(a, b):
    return lax.dot_general(a, b, (((1,), (1,)), ((), ())), preferred_element_type=f32)


def _trig_kernel(freq_ref, pos_ref, cos_ref, sin_ref):
    ang = pos_ref[...].astype(f32) * freq_ref[pl.program_id(0)]
    cos_ref[0] = jnp.cos(ang)
    sin_ref[0] = jnp.sin(ang)


def _rope_select():
    half = ROPE_DIM // 2
    sel = np.zeros((LANES, 3 * LANES), np.float32)
    for lane in range(LANES):
        m = lane % ATT_HEAD_DIM
        if m < half:
            sel[m, lane] = 1.0
            sel[half + m, LANES + lane] = -1.0
        elif m < ROPE_DIM:
            sel[m - half, lane] = 1.0
            sel[m, 2 * LANES + lane] = 1.0
        else:
            sel[ROPE_DIM, lane] = 1.0
    return jnp.asarray(sel, bf16)


def _rope_tables(positions):
    t = positions.size
    half = ROPE_DIM // 2
    inv_freq = ROPE_THETA ** (-jnp.arange(0, ROPE_DIM, 2, dtype=f32) / ROPE_DIM)
    pos2d = positions.reshape(t // LANES, LANES)
    cos_t, sin_t = pl.pallas_call(
        _trig_kernel,
        grid_spec=pltpu.PrefetchScalarGridSpec(
            num_scalar_prefetch=1,
            grid=(half,),
            in_specs=[pl.BlockSpec((t // LANES, LANES), lambda j, f: (0, 0))],
            out_specs=[pl.BlockSpec((1, t // LANES, LANES), lambda j, f: (j, 0, 0))] * 2,
        ),
        out_shape=[jax.ShapeDtypeStruct((half, t // LANES, LANES), f32)] * 2,
        name="trig",
    )(inv_freq, pos2d)
    return cos_t.reshape(half, t), sin_t.reshape(half, t)


INPROJ_TM = 256
INPROJ_CH = 512


def _inproj_kernel(x_ref, g_ref, wm_ref, wdt_ref,
                   z_ref, xbc_ref, q_ref, k_ref, v_ref, gs_ref, ga_ref, dt_ref, h_scr):
    x = x_ref[...]
    ms = jnp.mean(x * x, axis=-1, keepdims=True)
    h_scr[...] = (x * lax.rsqrt(ms + RMS_EPS) * g_ref[...]).astype(bf16)
    off = 0
    for ref in (z_ref, xbc_ref, q_ref, k_ref, v_ref, gs_ref, ga_ref):
        width = ref.shape[1]
        for c in range(0, width, INPROJ_CH):
            cw = min(INPROJ_CH, width - c)
            ref[:, c:c + cw] = _dot(h_scr[...], wm_ref[:, off + c:off + c + cw]).astype(ref.dtype)
        off += width
    dt_ref[...] = _dot(h_scr[...], wdt_ref[...])


def _inproj(x2, norm1_g, w_in):
    t = x2.shape[0]
    tm = min(INPROJ_TM, t)
    s0 = COL_Z
    s1 = s0 + COL_XBC
    s2 = s1 + COL_DT
    w_main = jnp.concatenate([w_in[:, :s1], w_in[:, s2:]], axis=1).astype(bf16)
    w_dt = jnp.pad(w_in[:, s1:s2], ((0, 0), (0, LANES - COL_DT))).astype(bf16)
    widths = (COL_Z, COL_XBC, COL_Q, COL_KV, COL_KV, D_MODEL, D_MODEL)
    n_main = sum(widths)
    const = lambda i: (0, 0)
    row = lambda i: (i, 0)
    outs = pl.pallas_call(
        _inproj_kernel,
        grid=(t // tm,),
        in_specs=[pl.BlockSpec((tm, D_MODEL), row),
                  pl.BlockSpec((1, D_MODEL), const),
                  pl.BlockSpec((D_MODEL, n_main), const, pipeline_mode=pl.Buffered(1)),
                  pl.BlockSpec((D_MODEL, LANES), const, pipeline_mode=pl.Buffered(1))],
        out_specs=[pl.BlockSpec((tm, w), row) for w in widths] + [pl.BlockSpec((tm, LANES), row)],
        out_shape=[jax.ShapeDtypeStruct((t, w), bf16) for w in widths]
                  + [jax.ShapeDtypeStruct((t, LANES), f32)],
        scratch_shapes=[pltpu.VMEM((tm, D_MODEL), bf16)],
        compiler_params=pltpu.CompilerParams(dimension_semantics=("arbitrary",),
                                             vmem_limit_bytes=VMEM_LIMIT),
        name="inproj",
    )(x2, norm1_g.reshape(1, D_MODEL), w_main, w_dt)
    return outs


def _ssd_kernel(xbc_ref, z_ref, dt_ref, gs_ref, cw_ref, cb_ref, dtb_ref, alog_ref, dexp_ref, ng_ref,
                wout_ref, out_ref, ext_ref, st_ref, xs_ref, bm_ref, cm_ref, y_ref, hn_ref):
    L = SSD_CHUNK
    c = pl.program_id(1)

    @pl.when(c == 0)
    def _():
        ext_ref[0:CONV_HALO, :] = jnp.zeros((CONV_HALO, SSD_CONV_DIM), f32)
        st_ref[...] = jnp.zeros(st_ref.shape, f32)

    ext_ref[CONV_HALO:CONV_HALO + L, :] = xbc_ref[...].astype(f32)
    cch = 512
    for cc in range(0, SSD_CONV_DIM, cch):
        acc = cb_ref[:, cc:cc + cch]
        for k in range(SSD_CONV):
            lo = CONV_HALO - (SSD_CONV - 1) + k
            acc = acc + ext_ref[lo:lo + L, cc:cc + cch] * cw_ref[k:k + 1, cc:cc + cch]
        act = _silu(acc)
        if cc < SSD_D_INNER:
            xs_ref[:, cc:cc + cch] = act
        elif cc < SSD_D_INNER + SSD_N_GROUPS * SSD_D_STATE:
            bm_ref[:, cc - SSD_D_INNER:cc - SSD_D_INNER + cch] = act
        else:
            o = cc - SSD_D_INNER - SSD_N_GROUPS * SSD_D_STATE
            cm_ref[:, o:o + cch] = act
    ext_ref[0:CONV_HALO, :] = ext_ref[L:L + CONV_HALO, :]

    lane_row = lax.broadcasted_iota(jnp.int32, (1, LANES), 1)
    row_i = lax.broadcasted_iota(jnp.int32, (L, L), 0)
    col_i = lax.broadcasted_iota(jnp.int32, (L, L), 1)
    causal = row_i >= col_i
    left = col_i < SSD_HEAD_DIM

    xdt = dt_ref[...] + dtb_ref[...]
    dtv = jnp.maximum(xdt, 0.0) + jnp.log1p(jnp.exp(-jnp.abs(xdt)))
    a = jnp.where(lane_row < SSD_N_HEADS, -jnp.exp(alog_ref[...]), 0.0)
    d_a = dtv * a
    tril = jnp.where(causal, 1.0, 0.0).astype(bf16)
    hi, mid, lo3 = _split3(d_a)
    a_cum = _dot(tril, hi) + _dot(tril, mid) + _dot(tril, lo3)
    a_end = a_cum[L - 1:L, :]
    exp_a = jnp.exp(a_cum)
    w_end = jnp.exp(a_end - a_cum) * dtv
    cd = jnp.exp(a_end)
    a_t = a_cum.T
    dt_t = dtv.T
    w_t = w_end.T

    n_pairs = SSD_N_HEADS // 2
    pairs_per_group = n_pairs // SSD_N_GROUPS
    for g in range(SSD_N_GROUPS):
        b_g = bm_ref[:, g * SSD_D_STATE:(g + 1) * SSD_D_STATE]
        c_g = cm_ref[:, g * SSD_D_STATE:(g + 1) * SSD_D_STATE]
        cb = _dot_nt(c_g.astype(bf16), b_g.astype(bf16))
        b_t = b_g.T
        for pi in range(pairs_per_group):
            i = g * pairs_per_group + pi
            xpair = xs_ref[:, i * LANES:(i + 1) * LANES]
            xpair_b = xpair.astype(bf16)
            s_prev = st_ref[i]
            rhs = jnp.concatenate([xpair_b, s_prev.astype(bf16)], axis=0)
            ys = []
            sn = []
            for h in (2 * i, 2 * i + 1):
                acol = jnp.broadcast_to(a_cum[:, h:h + 1], (L, L))
                arow = jnp.broadcast_to(a_t[h:h + 1, :], (L, L))
                dtrow = jnp.broadcast_to(dt_t[h:h + 1, :], (L, L))
                dec = jnp.exp(jnp.where(causal, acol - arow, NEG_BIG))
                m = cb * dec * dtrow
                cs = c_g * jnp.broadcast_to(exp_a[:, h:h + 1], (L, L))
                lhs = jnp.concatenate([m.astype(bf16), cs.astype(bf16)], axis=1)
                ys.append(_dot(lhs, rhs))
                btw = (b_t * jnp.broadcast_to(w_t[h:h + 1, :], (L, L))).astype(bf16)
                sn.append(_dot(btw, xpair_b))
            h0 = 2 * i
            cd_pair = jnp.where(lane_row < SSD_HEAD_DIM, cd[:, h0:h0 + 1], cd[:, h0 + 1:h0 + 2])
            st_ref[i] = jnp.where(left, sn[0], sn[1]) + s_prev * cd_pair
            y_pair = jnp.where(left, ys[0], ys[1])
            y_ref[:, i * LANES:(i + 1) * LANES] = y_pair + xpair * dexp_ref[:, i * LANES:(i + 1) * LANES]

    gw = SSD_D_INNER // SSD_N_GROUPS
    for g in range(SSD_N_GROUPS):
        sl = slice(g * gw, (g + 1) * gw)
        yz = y_ref[:, sl] * _silu(z_ref[:, sl].astype(f32))
        ms = jnp.mean(yz * yz, axis=-1, keepdims=True)
        hn_ref[:, sl] = (yz * lax.rsqrt(ms + RMS_EPS) * ng_ref[:, sl]).astype(bf16)
    y_ssd = _dot(hn_ref[...], wout_ref[...])
    out_ref[...] = _sigmoid(gs_ref[...].astype(f32)) * y_ssd


def _ssd(xbc, z, dt, gs, conv_w, conv_b, dt_bias, a_log, d_skip, ssd_norm_g, w_ssd_out, batch, seq):
    t = batch * seq
    L = SSD_CHUNK
    nc = seq // L
    pad_h = LANES - SSD_N_HEADS
    dtb = jnp.pad(dt_bias, (0, pad_h)).reshape(1, LANES)
    alog = jnp.pad(a_log, (0, pad_h)).reshape(1, LANES)
    dexp = jnp.repeat(d_skip, SSD_HEAD_DIM).reshape(1, SSD_D_INNER)
    const = lambda b, c: (0, 0)
    row = lambda b, c: (b * nc + c, 0)
    return pl.pallas_call(
        _ssd_kernel,
        grid=(batch, nc),
        in_specs=[pl.BlockSpec((L, SSD_CONV_DIM), row),
                  pl.BlockSpec((L, SSD_D_INNER), row),
                  pl.BlockSpec((L, LANES), row),
                  pl.BlockSpec((L, D_MODEL), row),
                  pl.BlockSpec((SSD_CONV, SSD_CONV_DIM), const),
                  pl.BlockSpec((1, SSD_CONV_DIM), const),
                  pl.BlockSpec((1, LANES), const),
                  pl.BlockSpec((1, LANES), const),
                  pl.BlockSpec((1, SSD_D_INNER), const),
                  pl.BlockSpec((1, SSD_D_INNER), const),
                  pl.BlockSpec((SSD_D_INNER, D_MODEL), const, pipeline_mode=pl.Buffered(1))],
        out_specs=pl.BlockSpec((L, D_MODEL), row),
        out_shape=jax.ShapeDtypeStruct((t, D_MODEL), f32),
        scratch_shapes=[pltpu.VMEM((CONV_HALO + L, SSD_CONV_DIM), f32),
                        pltpu.VMEM((SSD_N_HEADS // 2, SSD_D_STATE, LANES), f32),
                        pltpu.VMEM((L, SSD_D_INNER), f32),
                        pltpu.VMEM((L, SSD_N_GROUPS * SSD_D_STATE), f32),
                        pltpu.VMEM((L, SSD_N_GROUPS * SSD_D_STATE), f32),
                        pltpu.VMEM((L, SSD_D_INNER), f32),
                        pltpu.VMEM((L, SSD_D_INNER), bf16)],
        compiler_params=pltpu.CompilerParams(dimension_semantics=("arbitrary", "arbitrary"),
                                             vmem_limit_bytes=VMEM_LIMIT),
        name="ssd",
    )(xbc, z, dt, gs, conv_w, conv_b.reshape(1, -1), dtb, alog, dexp,
      ssd_norm_g.reshape(1, -1), w_ssd_out.astype(bf16))


ROUTE_E, ROUTE_RANK, ROUTE_GATE = 0, 2, 4


def _attn_kernel(sink_ref, q_ref, k_ref, v_ref, cos_ref, sin_ref, sel_ref, ga_ref, ms_ref, x_ref,
                 qg_ref, kg_ref, wao_ref, wo_ref, n2g_ref, wrh_ref, wrl_ref, br_ref,
                 x1_ref, h2_ref, route_ref, route_t_ref, cnt_ref,
                 kprev_ref, vprev_ref, att_ref, cnt_scr):
    Q = ATT_BLOCK
    b = pl.program_id(0)
    n = pl.program_id(1)

    @pl.when(n == 0)
    def _():
        kprev_ref[...] = jnp.zeros(kprev_ref.shape, f32)
        vprev_ref[...] = jnp.zeros(vprev_ref.shape, f32)

    @pl.when((b == 0) & (n == 0))
    def _():
        cnt_scr[...] = jnp.zeros(cnt_scr.shape, f32)

    lane = lax.broadcasted_iota(jnp.int32, (Q, LANES), 1)
    row_q = lax.broadcasted_iota(jnp.int32, (Q, LANES), 0)
    left = lane < ATT_HEAD_DIM
    head_mean = jnp.where((row_q // ATT_HEAD_DIM) == (lane // ATT_HEAD_DIM),
                          1.0 / ATT_HEAD_DIM, 0.0).astype(bf16)
    half = ROPE_DIM // 2
    cs = jnp.concatenate([cos_ref[...], sin_ref[...], jnp.ones((half, LANES), f32),
                          jnp.zeros((LANES - 3 * half, LANES), f32)], axis=0)
    c_hi, c_mid, c_lo = _split3(cs.T)
    pat = _dot(c_hi, sel_ref[...]) + _dot(c_mid, sel_ref[...]) + _dot(c_lo, sel_ref[...])
    cpat = pat[:, 0:LANES]
    s1pat = pat[:, LANES:2 * LANES]
    s2pat = pat[:, 2 * LANES:3 * LANES]

    def norm_rope(tc, gpat):
        sq = tc * tc
        hi = sq.astype(bf16)
        lo = (sq - hi.astype(f32)).astype(bf16)
        ms = _dot(hi, head_mean) + _dot(lo, head_mean)
        tn = tc * lax.rsqrt(ms + RMS_EPS) * gpat
        return (tn * cpat + pltpu.roll(tn, LANES - ROPE_DIM // 2, 1) * s1pat
                + pltpu.roll(tn, ROPE_DIM // 2, 1) * s2pat)

    kd = []
    vd = []
    for cidx in range(COL_KV // LANES):
        sl = slice(cidx * LANES, (cidx + 1) * LANES)
        k_cur = norm_rope(k_ref[:, sl].astype(f32), kg_ref[...])
        v_cur = v_ref[:, sl].astype(f32)
        k_all = jnp.concatenate([kprev_ref[:, sl], k_cur], axis=0)
        v_all = jnp.concatenate([vprev_ref[:, sl], v_cur], axis=0)
        kprev_ref[:, sl] = k_cur
        vprev_ref[:, sl] = v_cur
        lane2 = lax.broadcasted_iota(jnp.int32, (2 * Q, LANES), 1) < ATT_HEAD_DIM
        k_sw = pltpu.roll(k_all, ATT_HEAD_DIM, 1)
        v_sw = pltpu.roll(v_all, ATT_HEAD_DIM, 1)
        kd.append(jnp.where(lane2, k_all, k_sw).astype(bf16))
        kd.append(jnp.where(lane2, k_sw, k_all).astype(bf16))
        vd.append(jnp.where(lane2, v_all, v_sw).astype(bf16))
        vd.append(jnp.where(lane2, v_sw, v_all).astype(bf16))

    qg = ATT_N_HEADS // ATT_N_KV
    rows = qg * Q
    ri = lax.broadcasted_iota(jnp.int32, (rows, 2 * Q), 0) % Q
    cj = lax.broadcasted_iota(jnp.int32, (rows, 2 * Q), 1)
    allowed = (cj > ri) & (cj <= ri + Q) & ((cj >= Q) | (n > 0))
    for h in range(ATT_N_KV):
        parts = []
        for cidx in (2 * h, 2 * h + 1):
            qc = norm_rope(q_ref[:, cidx * LANES:(cidx + 1) * LANES].astype(f32), qg_ref[...]) * ATT_SCALE
            parts.append(jnp.where(left, qc, 0.0).astype(bf16))
            parts.append(jnp.where(left, 0.0, qc).astype(bf16))
        lhs = jnp.concatenate(parts, axis=0)
        s = jnp.where(allowed, _dot_nt(lhs, kd[h]), NEG_BIG)
        sink = jnp.concatenate([jnp.full((Q, 1), sink_ref[qg * h + r], f32) for r in range(qg)], axis=0)
        m = jnp.maximum(jnp.max(s, axis=-1, keepdims=True), sink)
        p = jnp.exp(s - m)
        denom = jnp.sum(p, axis=-1, keepdims=True) + jnp.exp(sink - m)
        o = _dot((p / denom).astype(bf16), vd[h])
        for r in range(2):
            cidx = 2 * h + r
            att_ref[:, cidx * LANES:(cidx + 1) * LANES] = jnp.where(
                left, o[(2 * r) * Q:(2 * r + 1) * Q], o[(2 * r + 1) * Q:(2 * r + 2) * Q]).astype(bf16)

    y_att = _dot(att_ref[...], wao_ref[...])
    merged = _sigmoid(ga_ref[...].astype(f32)) * y_att + ms_ref[...]
    x1 = x_ref[...] + _dot(merged.astype(bf16), wo_ref[...])
    x1_ref[...] = x1
    h2 = x1 * lax.rsqrt(jnp.mean(x1 * x1, axis=-1, keepdims=True) + RMS_EPS) * n2g_ref[...]
    h2_ref[...] = h2

    hi = h2.astype(bf16)
    lo = (h2 - hi.astype(f32)).astype(bf16)
    logits = _dot(hi, wrh_ref[...]) + _dot(lo, wrh_ref[...]) + _dot(hi, wrl_ref[...]) + br_ref[...]
    big = 4 * LANES
    gl = jnp.where(lane < MOE_N_GROUPS, logits, NEG_BIG)
    gmax = jnp.max(gl, axis=-1, keepdims=True)
    gsel = jnp.min(jnp.where(gl == gmax, lane, big), axis=-1, keepdims=True)
    pg = 1.0 / jnp.sum(jnp.exp(gl - gmax), axis=-1, keepdims=True)
    lo_l = MOE_N_GROUPS + MOE_EPG * gsel
    el = jnp.where((lane >= lo_l) & (lane < lo_l + MOE_EPG), logits, NEG_BIG)
    v1 = jnp.max(el, axis=-1, keepdims=True)
    i1 = jnp.min(jnp.where(el == v1, lane, big), axis=-1, keepdims=True)
    el2 = jnp.where(lane == i1, NEG_BIG, el)
    v2 = jnp.max(el2, axis=-1, keepdims=True)
    i2 = jnp.min(jnp.where(el2 == v2, lane, big), axis=-1, keepdims=True)
    e1 = i1 - MOE_N_GROUPS
    e2 = i2 - MOE_N_GROUPS
    tt = jnp.exp(v2 - v1)
    w1 = pg * (1.0 / (1.0 + tt))
    w2 = pg * (tt / (1.0 + tt))

    onehot = jnp.where((lane == e1) | (lane == e2), 1.0, 0.0)
    strict = jnp.where(row_q > lane, 1.0, 0.0).astype(bf16)
    base = _dot(strict, onehot.astype(bf16)) + cnt_scr[0:1, :]
    r1 = jnp.sum(jnp.where(lane == e1, base, 0.0), axis=-1, keepdims=True)
    r2 = jnp.sum(jnp.where(lane == e2, base, 0.0), axis=-1, keepdims=True)
    cnt_new = cnt_scr[0:1, :] + jnp.sum(onehot, axis=0, keepdims=True)
    cnt_scr[...] = jnp.broadcast_to(cnt_new, cnt_scr.shape)
    cnt_ref[...] = jnp.broadcast_to(cnt_new, cnt_ref.shape)

    rec = jnp.zeros((Q, LANES), f32)
    for off, val in ((ROUTE_E, e1.astype(f32)), (ROUTE_E + 1, e2.astype(f32)),
                     (ROUTE_RANK, r1), (ROUTE_RANK + 1, r2), (ROUTE_GATE, w1), (ROUTE_GATE + 1, w2)):
        rec = jnp.where(lane == off, val, rec)
    route_ref[...] = rec
    route_t_ref[...] = rec.T[0:8, :]


def _attn(q, k, v, cos_t, sin_t, ga, merged_ssd, x2, q_norm_g, k_norm_g, sinks, w_attn_out, w_out,
          norm2_g, w_rg, b_rg, w_re, b_re, batch, seq):
    t = batch * seq
    Q = ATT_BLOCK
    nb = seq // Q
    rep = LANES // ATT_HEAD_DIM
    qg = jnp.tile(q_norm_g, rep).reshape(1, LANES)
    kg = jnp.tile(k_norm_g, rep).reshape(1, LANES)
    n_log = MOE_N_GROUPS + MOE_N_EXPERTS
    w_r = jnp.concatenate([w_rg, jnp.transpose(w_re, (1, 0, 2)).reshape(D_MODEL, MOE_N_EXPERTS),
                           jnp.zeros((D_MODEL, LANES - n_log), f32)], axis=1)
    b_r = jnp.concatenate([b_rg, b_re.reshape(-1), jnp.zeros((LANES - n_log,), f32)]).reshape(1, LANES)
    w_r_hi = w_r.astype(bf16)
    w_r_lo = (w_r - w_r_hi.astype(f32)).astype(bf16)
    const = lambda b, n, *_: (0, 0)
    row = lambda b, n, *_: (b * nb + n, 0)
    col = lambda b, n, *_: (0, b * nb + n)
    full = lambda shape: pl.BlockSpec(shape, const, pipeline_mode=pl.Buffered(1))
    return pl.pallas_call(
        _attn_kernel,
        grid_spec=pltpu.PrefetchScalarGridSpec(
            num_scalar_prefetch=1,
            grid=(batch, nb),
            in_specs=[pl.BlockSpec((Q, COL_Q), row),
                      pl.BlockSpec((Q, COL_KV), row),
                      pl.BlockSpec((Q, COL_KV), row),
                      pl.BlockSpec((ROPE_DIM // 2, Q), col),
                      pl.BlockSpec((ROPE_DIM // 2, Q), col),
                      full((LANES, 3 * LANES)),
                      pl.BlockSpec((Q, D_MODEL), row),
                      pl.BlockSpec((Q, D_MODEL), row),
                      pl.BlockSpec((Q, D_MODEL), row),
                      pl.BlockSpec((1, LANES), const),
                      pl.BlockSpec((1, LANES), const),
                      full((COL_Q, D_MODEL)),
                      full((D_MODEL, D_MODEL)),
                      pl.BlockSpec((1, D_MODEL), const),
                      full((D_MODEL, LANES)),
                      full((D_MODEL, LANES)),
                      pl.BlockSpec((1, LANES), const)],
            out_specs=[pl.BlockSpec((Q, D_MODEL), row),
                       pl.BlockSpec((Q, D_MODEL), row),
                       pl.BlockSpec((Q, LANES), row),
                       pl.BlockSpec((8, Q), col),
                       pl.BlockSpec((8, LANES), const)],
            scratch_shapes=[pltpu.VMEM((Q, COL_KV), f32),
                            pltpu.VMEM((Q, COL_KV), f32),
                            pltpu.VMEM((Q, COL_Q), bf16),
                            pltpu.VMEM((8, LANES), f32)],
        ),
        out_shape=[jax.ShapeDtypeStruct((t, D_MODEL), f32),
                   jax.ShapeDtypeStruct((t, D_MODEL), f32),
                   jax.ShapeDtypeStruct((t, LANES), f32),
                   jax.ShapeDtypeStruct((8, t), f32),
                   jax.ShapeDtypeStruct((8, LANES), f32)],
        compiler_params=pltpu.CompilerParams(dimension_semantics=("arbitrary", "arbitrary"),
                                             vmem_limit_bytes=VMEM_LIMIT),
        name="attn",
    )(sinks, q, k, v, cos_t, sin_t, _rope_select(), ga, merged_ssd, x2, qg, kg,
      w_attn_out.astype(bf16), w_out.astype(bf16), norm2_g.reshape(1, D_MODEL), w_r_hi, w_r_lo, b_r)


MOE_TM = 256
MOE_UNROLL = 8


def _row_copy(src, i, dst, j, sem):
    return pltpu.make_async_copy(src.at[pl.ds(i, 1), :], dst.at[pl.ds(j, 1), :], sem)


def _dispatch_kernel(dest0_ref, dest1_ref, zblk_ref, h2_ref, xs_hbm, zero_ref, sem, zsem):
    tm = h2_ref.shape[0]
    base = pl.program_id(0) * tm

    @pl.when(pl.program_id(0) == 0)
    def _():
        zero_ref[...] = jnp.zeros(zero_ref.shape, f32)

        def zcopy(i):
            start = pl.multiple_of(zblk_ref[i] * MOE_BLOCK, MOE_BLOCK)
            return pltpu.make_async_copy(zero_ref, xs_hbm.at[pl.ds(start, MOE_BLOCK), :], zsem)

        def zstart(i, carry):
            @pl.when(zblk_ref[i] >= 0)
            def _():
                zcopy(i).start()
            return carry

        def zwait(i, carry):
            @pl.when(zblk_ref[i] >= 0)
            def _():
                zcopy(i).wait()
            return carry

        lax.fori_loop(0, zblk_ref.shape[0], zstart, 0)
        lax.fori_loop(0, zblk_ref.shape[0], zwait, 0)

    def issue(i, carry):
        _row_copy(h2_ref, i, xs_hbm, dest0_ref[base + i], sem).start()
        _row_copy(h2_ref, i, xs_hbm, dest1_ref[base + i], sem).start()
        return carry

    lax.fori_loop(0, tm, issue, 0, unroll=MOE_UNROLL)
    for _ in range(2):
        pltpu.make_async_copy(h2_ref, xs_hbm.at[pl.ds(0, tm), :], sem).wait()


def _dispatch(h2, dest0, dest1, zero_blocks, n_rows):
    t = h2.shape[0]
    tm = min(MOE_TM, t)
    return pl.pallas_call(
        _dispatch_kernel,
        grid_spec=pltpu.PrefetchScalarGridSpec(
            num_scalar_prefetch=3,
            grid=(t // tm,),
            in_specs=[pl.BlockSpec((tm, D_MODEL), lambda i, d0, d1, zb: (i, 0))],
            out_specs=pl.BlockSpec(memory_space=pl.ANY),
            scratch_shapes=[pltpu.VMEM((MOE_BLOCK, D_MODEL), f32),
                            pltpu.SemaphoreType.DMA(()),
                            pltpu.SemaphoreType.DMA(())],
        ),
        out_shape=jax.ShapeDtypeStruct((n_rows, D_MODEL), f32),
        compiler_params=pltpu.CompilerParams(dimension_semantics=("arbitrary",)),
        name="dispatch",
    )(dest0, dest1, zero_blocks, h2)


def _expert_kernel(be_ref, nu_ref, xs_ref, wgu_ref, wd_ref, ys_ref):
    used = pl.program_id(0) < nu_ref[0]

    @pl.when(used)
    def _():
        gu = _dot(xs_ref[...].astype(bf16), wgu_ref[0])
        hid = _silu(gu[:, :MOE_D_FF]) * gu[:, MOE_D_FF:]
        ys_ref[...] = _dot(hid.astype(bf16), wd_ref[0])

    @pl.when(jnp.logical_not(used))
    def _():
        ys_ref[...] = jnp.zeros(ys_ref.shape, f32)


def _experts(xs, block_e, n_used, w_gu, w_d):
    n_rows = xs.shape[0]
    nblk = n_rows // MOE_BLOCK
    blk_in = lambda i, be, nu: (jnp.minimum(i, nu[0] - 1), 0)
    blk_out = lambda i, be, nu: (i, 0)
    wsel = lambda i, be, nu: (be[i], 0, 0)
    return pl.pallas_call(
        _expert_kernel,
        grid_spec=pltpu.PrefetchScalarGridSpec(
            num_scalar_prefetch=2,
            grid=(nblk,),
            in_specs=[pl.BlockSpec((MOE_BLOCK, D_MODEL), blk_in),
                      pl.BlockSpec((1, D_MODEL, 2 * MOE_D_FF), wsel),
                      pl.BlockSpec((1, MOE_D_FF, D_MODEL), wsel)],
            out_specs=pl.BlockSpec((MOE_BLOCK, D_MODEL), blk_out),
        ),
        out_shape=jax.ShapeDtypeStruct((n_rows, D_MODEL), f32),
        compiler_params=pltpu.CompilerParams(dimension_semantics=("arbitrary",)),
        name="experts",
    )(block_e, n_used, xs, w_gu, w_d)


def _combine_kernel(dest0_ref, dest1_ref, x1_ref, route_ref, ys_hbm, out_ref, y0_ref, y1_ref, sem):
    tm = x1_ref.shape[0]
    base = pl.program_id(0) * tm

    def issue(i, carry):
        _row_copy(ys_hbm, dest0_ref[base + i], y0_ref, i, sem).start()
        _row_copy(ys_hbm, dest1_ref[base + i], y1_ref, i, sem).start()
        return carry

    lax.fori_loop(0, tm, issue, 0, unroll=MOE_UNROLL)
    for y_ref in (y0_ref, y1_ref):
        pltpu.make_async_copy(ys_hbm.at[pl.ds(0, tm), :], y_ref, sem).wait()
    g0 = route_ref[:, ROUTE_GATE:ROUTE_GATE + 1]
    g1 = route_ref[:, ROUTE_GATE + 1:ROUTE_GATE + 2]
    out_ref[...] = x1_ref[...] + y0_ref[...] * g0 + y1_ref[...] * g1


def _combine(x1, route, ys, dest0, dest1):
    t = x1.shape[0]
    tm = min(MOE_TM, t)
    row = lambda i, d0, d1: (i, 0)
    return pl.pallas_call(
        _combine_kernel,
        grid_spec=pltpu.PrefetchScalarGridSpec(
            num_scalar_prefetch=2,
            grid=(t // tm,),
            in_specs=[pl.BlockSpec((tm, D_MODEL), row),
                      pl.BlockSpec((tm, LANES), row),
                      pl.BlockSpec(memory_space=pl.ANY)],
            out_specs=pl.BlockSpec((tm, D_MODEL), row),
            scratch_shapes=[pltpu.VMEM((tm, D_MODEL), f32),
                            pltpu.VMEM((tm, D_MODEL), f32),
                            pltpu.SemaphoreType.DMA(())],
        ),
        out_shape=jax.ShapeDtypeStruct((t, D_MODEL), f32),
        compiler_params=pltpu.CompilerParams(dimension_semantics=("arbitrary",)),
        name="combine",
    )(dest0, dest1, x1, route, ys)


def _moe(x1, h2, route, route_t, counts_rec, w_gate_e, w_up_e, w_down_e):
    t = x1.shape[0]
    n_rows = 2 * t + MOE_N_EXPERTS * MOE_BLOCK
    nblk = n_rows // MOE_BLOCK
    counts = counts_rec[0, :MOE_N_EXPERTS].astype(jnp.int32)
    padded = (counts + MOE_BLOCK - 1) // MOE_BLOCK * MOE_BLOCK
    pend = jnp.cumsum(padded)
    pstart = pend - padded
    expert_ids = jnp.arange(MOE_N_EXPERTS, dtype=jnp.int32)[:, None]

    def sorted_row(k):
        eid = route_t[ROUTE_E + k].astype(jnp.int32)
        rank = route_t[ROUTE_RANK + k].astype(jnp.int32)
        return jnp.sum(jnp.where(eid[None, :] == expert_ids, pstart[:, None], 0), axis=0) + rank

    dest0, dest1 = sorted_row(0), sorted_row(1)
    block_start = jnp.arange(nblk, dtype=jnp.int32) * MOE_BLOCK
    block_e = jnp.minimum(jnp.sum((pend[None, :] <= block_start[:, None]).astype(jnp.int32), axis=1),
                          MOE_N_EXPERTS - 1)
    n_used = (pend[-1:] // MOE_BLOCK).astype(jnp.int32)
    last_blk = jnp.where(counts > 0, pend // MOE_BLOCK - 1, -1)
    tail = n_used[0] + jnp.arange(MOE_N_EXPERTS, dtype=jnp.int32)
    zero_blocks = jnp.concatenate([last_blk, jnp.where(tail < nblk, tail, -1)]).astype(jnp.int32)
    w_gu = jnp.concatenate([w_gate_e, w_up_e], axis=-1).astype(bf16)
    xs = _dispatch(h2, dest0, dest1, zero_blocks, n_rows)
    ys = _experts(xs, block_e, n_used, w_gu, w_down_e.astype(bf16))
    return _combine(x1, route, ys, dest0, dest1)


def _layer(x, positions, norm1_g, w_in, conv_w, conv_b, dt_bias, a_log, d_skip, ssd_norm_g, w_ssd_out,
           q_norm_g, k_norm_g, sinks, w_attn_out, w_out, norm2_g, w_rg, b_rg, w_re, b_re,
           w_gate_e, w_up_e, w_down_e):
    batch, seq, d = x.shape
    x2 = x.reshape(batch * seq, d)
    cos_t, sin_t = _rope_tables(positions)
    z, xbc, q, k, v, gs, ga, dt = _inproj(x2, norm1_g, w_in)
    merged_ssd = _ssd(xbc, z, dt, gs, conv_w, conv_b, dt_bias, a_log, d_skip, ssd_norm_g, w_ssd_out,
                      batch, seq)
    x1, h2, route, route_t, counts = _attn(q, k, v, cos_t, sin_t, ga, merged_ssd, x2, q_norm_g, k_norm_g,
                                           sinks, w_attn_out, w_out, norm2_g, w_rg, b_rg, w_re, b_re,
                                           batch, seq)
    out = _moe(x1, h2, route, route_t, counts, w_gate_e, w_up_e, w_down_e)
    return out.reshape(batch, seq, d)


def kernel(x, positions, norm1_g, w_in, conv_w, conv_b, dt_bias, a_log, d_skip, ssd_norm_g, w_ssd_out,
           q_norm_g, k_norm_g, sinks, w_attn_out, w_out, norm2_g, w_router_group, b_router_group,
           w_router_expert, b_router_expert, w_gate_e, w_up_e, w_down_e):
    for l in range(norm1_g.shape[0]):
        x = _layer(x, positions, norm1_g[l], w_in[l], conv_w[l], conv_b[l], dt_bias[l], a_log[l],
                   d_skip[l], ssd_norm_g[l], w_ssd_out[l], q_norm_g[l], k_norm_g[l], sinks[l],
                   w_attn_out[l], w_out[l], norm2_g[l], w_router_group[l], b_router_group[l],
                   w_router_expert[l], b_router_expert[l], w_gate_e[l], w_up_e[l], w_down_e[l])
    return x
```

```python
import functools

import numpy as np

import jax
import jax.numpy as jnp
from jax import lax
from jax.experimental import pallas as pl
from jax.experimental.pallas import tpu as pltpu

f32 = jnp.float32
bf16 = jnp.bfloat16

D_MODEL = 1024
SSD_D_INNER = 2048
SSD_HEAD_DIM = 64
SSD_N_HEADS = 32
SSD_N_GROUPS = 4
SSD_D_STATE = 128
SSD_CONV = 4
SSD_CHUNK = 128
SSD_CONV_DIM = 3072
ATT_HEAD_DIM = 64
ATT_N_HEADS = 16
ATT_N_KV = 4
ATT_BLOCK = 128
ATT_SCALE = ATT_HEAD_DIM ** -0.5
ROPE_THETA = 500000.0
ROPE_DIM = 16
MOE_N_GROUPS = 8
MOE_EPG = 8
MOE_N_EXPERTS = 64
MOE_D_FF = 256
MOE_BLOCK = 256
RMS_EPS = 1e-6

LANES = 128
CONV_HALO = 8
NEG_BIG = -1e30
VMEM_LIMIT = 56 * 1024 * 1024

COL_Z = SSD_D_INNER
COL_XBC = SSD_CONV_DIM
COL_DT = SSD_N_HEADS
COL_Q = ATT_N_HEADS * ATT_HEAD_DIM
COL_KV = ATT_N_KV * ATT_HEAD_DIM


def _sigmoid(x):
    return 1.0 / (1.0 + jnp.exp(-x))


def _silu(x):
    return x * _sigmoid(x)


def _split3(x):
    hi = x.astype(bf16)
    r1 = x - hi.astype(f32)
    mid = r1.astype(bf16)
    lo = (r1 - mid.astype(f32)).astype(bf16)
    return hi, mid, lo


def _dot(a, b):
    return jnp.dot(a, b, preferred_element_type=f32)


def _dot_nt(a, b):
    return lax.dot_general(a, b, (((1,), (1,)), ((), ())), preferred_element_type=f32)


def _trig_kernel(freq_ref, pos_ref, cos_ref, sin_ref):
    ang = pos_ref[...].astype(f32) * freq_ref[pl.program_id(0)]
    cos_ref[0] = jnp.cos(ang)
    sin_ref[0] = jnp.sin(ang)


def _rope_select():
    half = ROPE_DIM // 2
    sel = np.zeros((LANES, 3 * LANES), np.float32)
    for lane in range(LANES):
        m = lane % ATT_HEAD_DIM
        if m < half:
            sel[m, lane] = 1.0
            sel[half + m, LANES + lane] = -1.0
        elif m < ROPE_DIM:
            sel[m - half, lane] = 1.0
            sel[m, 2 * LANES + lane] = 1.0
        else:
            sel[ROPE_DIM, lane] = 1.0
    return jnp.asarray(sel, bf16)


def _rope_tables(positions):
    t = positions.size
    half = ROPE_DIM // 2
    inv_freq = ROPE_THETA ** (-jnp.arange(0, ROPE_DIM, 2, dtype=f32) / ROPE_DIM)
    pos2d = positions.reshape(t // LANES, LANES)
    cos_t, sin_t = pl.pallas_call(
        _trig_kernel,
        grid_spec=pltpu.PrefetchScalarGridSpec(
            num_scalar_prefetch=1,
            grid=(half,),
            in_specs=[pl.BlockSpec((t // LANES, LANES), lambda j, f: (0, 0))],
            out_specs=[pl.BlockSpec((1, t // LANES, LANES), lambda j, f: (j, 0, 0))] * 2,
        ),
        out_shape=[jax.ShapeDtypeStruct((half, t // LANES, LANES), f32)] * 2,
        name="trig",
    )(inv_freq, pos2d)
    return cos_t.reshape(half, t), sin_t.reshape(half, t)


INPROJ_TM = 512
INPROJ_CH = 512


def _inproj_kernel(x_ref, g_ref, w_ref,
                   z_ref, xbc_ref, q_ref, k_ref, v_ref, gs_ref, ga_ref, dt_ref, h_scr):
    x = x_ref[...]
    ms = jnp.mean(x * x, axis=-1, keepdims=True)
    h_scr[...] = (x * lax.rsqrt(ms + RMS_EPS) * g_ref[...]).astype(bf16)
    off = 0
    for ref in (z_ref, xbc_ref, q_ref, k_ref, v_ref, gs_ref, ga_ref, dt_ref):
        width = ref.shape[1]
        for c in range(0, width, INPROJ_CH):
            cw = min(INPROJ_CH, width - c)
            ref[:, c:c + cw] = _dot(h_scr[...], w_ref[:, off + c:off + c + cw]).astype(ref.dtype)
        off += width


def _inproj(x2, norm1_g, w_in):
    t = x2.shape[0]
    tm = min(INPROJ_TM, t)
    s1 = COL_Z + COL_XBC
    s2 = s1 + COL_DT
    w_all = jnp.concatenate([w_in[:, :s1], w_in[:, s2:], w_in[:, s1:s2],
                             jnp.zeros((D_MODEL, LANES - COL_DT), w_in.dtype)], axis=1).astype(bf16)
    widths = (COL_Z, COL_XBC, COL_Q, COL_KV, COL_KV, D_MODEL, D_MODEL)
    const = lambda i: (0, 0)
    row = lambda i: (i, 0)
    outs = pl.pallas_call(
        _inproj_kernel,
        grid=(t // tm,),
        in_specs=[pl.BlockSpec((tm, D_MODEL), row),
                  pl.BlockSpec((1, D_MODEL), const),
                  pl.BlockSpec((D_MODEL, w_all.shape[1]), const, pipeline_mode=pl.Buffered(1))],
        out_specs=[pl.BlockSpec((tm, w), row) for w in widths] + [pl.BlockSpec((tm, LANES), row)],
        out_shape=[jax.ShapeDtypeStruct((t, w), bf16) for w in widths]
                  + [jax.ShapeDtypeStruct((t, LANES), f32)],
        scratch_shapes=[pltpu.VMEM((tm, D_MODEL), bf16)],
        compiler_params=pltpu.CompilerParams(dimension_semantics=("arbitrary",),
                                             vmem_limit_bytes=VMEM_LIMIT),
        name="inproj",
    )(x2, norm1_g.reshape(1, D_MODEL), w_all)
    return outs


def _ssd_kernel(xbc_ref, z_ref, dt_ref, gs_ref, cw_ref, cb_ref, dtb_ref, alog_ref, dexp_ref, ng_ref,
                wout_ref, out_ref, ext_ref, st_ref, xs_ref, bm_ref, cm_ref, y_ref, hn_ref):
    L = SSD_CHUNK
    c = pl.program_id(1)

    @pl.when(c == 0)
    def _():
        ext_ref[0:CONV_HALO, :] = jnp.zeros((CONV_HALO, SSD_CONV_DIM), f32)
        st_ref[...] = jnp.zeros(st_ref.shape, f32)

    bf_tile = 2 * CONV_HALO
    ext_ref[CONV_HALO:2 * CONV_HALO, :] = xbc_ref[0:bf_tile, :].astype(f32)[0:CONV_HALO]
    n_sh = SSD_CONV - 1
    sr = lax.broadcasted_iota(jnp.int32, (n_sh * L, L), 0)
    sc = lax.broadcasted_iota(jnp.int32, (n_sh * L, L), 1)
    shift = jnp.where((sr % L) - sc == (sr // L) + 1, 1.0, 0.0).astype(bf16)
    cch = 512
    for cc in range(0, SSD_CONV_DIM, cch):
        cs_ = slice(cc, cc + cch)
        xb = xbc_ref[:, cs_]
        sh = _dot(shift, xb)
        w_now = cw_ref[SSD_CONV - 1:SSD_CONV, cs_]
        acc = cb_ref[:, cs_] + xb.astype(f32) * w_now
        top = cb_ref[:, cs_] + ext_ref[CONV_HALO:2 * CONV_HALO, cs_] * w_now
        for j in range(1, SSD_CONV):
            w_j = cw_ref[SSD_CONV - 1 - j:SSD_CONV - j, cs_]
            acc = acc + sh[(j - 1) * L:j * L] * w_j
            top = top + ext_ref[CONV_HALO - j:2 * CONV_HALO - j, cs_] * w_j
        if cc < SSD_D_INNER:
            dst, o = xs_ref, cc
        elif cc < SSD_D_INNER + SSD_N_GROUPS * SSD_D_STATE:
            dst, o = bm_ref, cc - SSD_D_INNER
        else:
            dst, o = cm_ref, cc - SSD_D_INNER - SSD_N_GROUPS * SSD_D_STATE
        dst[:, o:o + cch] = _silu(acc)
        dst[0:CONV_HALO, o:o + cch] = _silu(top)
    ext_ref[0:CONV_HALO, :] = xbc_ref[L - bf_tile:L, :].astype(f32)[CONV_HALO:bf_tile]

    lane_row = lax.broadcasted_iota(jnp.int32, (1, LANES), 1)
    row_i = lax.broadcasted_iota(jnp.int32, (L, L), 0)
    col_i = lax.broadcasted_iota(jnp.int32, (L, L), 1)
    causal = row_i >= col_i
    left = col_i < SSD_HEAD_DIM

    xdt = dt_ref[...] + dtb_ref[...]
    dtv = jnp.maximum(xdt, 0.0) + jnp.log1p(jnp.exp(-jnp.abs(xdt)))
    a = jnp.where(lane_row < SSD_N_HEADS, -jnp.exp(alog_ref[...]), 0.0)
    d_a = dtv * a
    tril = jnp.where(causal, 1.0, 0.0).astype(bf16)
    hi, mid, lo3 = _split3(d_a)
    a_cum = _dot(tril, hi) + _dot(tril, mid) + _dot(tril, lo3)
    a_end = a_cum[L - 1:L, :]
    exp_a = jnp.exp(a_cum)
    w_end = jnp.exp(a_end - a_cum) * dtv
    cd = jnp.exp(a_end)
    a_t = a_cum.T
    dt_t = dtv.T
    w_t = w_end.T

    n_pairs = SSD_N_HEADS // 2
    pairs_per_group = n_pairs // SSD_N_GROUPS
    for g in range(SSD_N_GROUPS):
        b_g = bm_ref[:, g * SSD_D_STATE:(g + 1) * SSD_D_STATE]
        c_g = cm_ref[:, g * SSD_D_STATE:(g + 1) * SSD_D_STATE]
        cb = _dot_nt(c_g.astype(bf16), b_g.astype(bf16))
        b_t = b_g.T
        for pi in range(pairs_per_group):
            i = g * pairs_per_group + pi
            xpair = xs_ref[:, i * LANES:(i + 1) * LANES]
            xpair_b = xpair.astype(bf16)
            s_prev = st_ref[i]
            rhs = jnp.concatenate([xpair_b, s_prev.astype(bf16)], axis=0)
            ys = []
            sn = []
            for h in (2 * i, 2 * i + 1):
                acol = jnp.broadcast_to(a_cum[:, h:h + 1], (L, L))
                arow = jnp.broadcast_to(a_t[h:h + 1, :], (L, L))
                dtrow = jnp.broadcast_to(dt_t[h:h + 1, :], (L, L))
                dec = jnp.exp(jnp.where(causal, acol - arow, NEG_BIG))
                m = cb * dec * dtrow
                cs = c_g * jnp.broadcast_to(exp_a[:, h:h + 1], (L, L))
                lhs = jnp.concatenate([m.astype(bf16), cs.astype(bf16)], axis=1)
                ys.append(_dot(lhs, rhs))
                btw = (b_t * jnp.broadcast_to(w_t[h:h + 1, :], (L, L))).astype(bf16)
                sn.append(_dot(btw, xpair_b))
            h0 = 2 * i
            cd_pair = jnp.where(lane_row < SSD_HEAD_DIM, cd[:, h0:h0 + 1], cd[:, h0 + 1:h0 + 2])
            st_ref[i] = jnp.where(left, sn[0], sn[1]) + s_prev * cd_pair
            y_pair = jnp.where(left, ys[0], ys[1])
            y_ref[:, i * LANES:(i + 1) * LANES] = y_pair + xpair * dexp_ref[:, i * LANES:(i + 1) * LANES]

    gw = SSD_D_INNER // SSD_N_GROUPS
    for g in range(SSD_N_GROUPS):
        sl = slice(g * gw, (g + 1) * gw)
        yz = y_ref[:, sl] * _silu(z_ref[:, sl].astype(f32))
        ms = jnp.mean(yz * yz, axis=-1, keepdims=True)
        hn_ref[:, sl] = (yz * lax.rsqrt(ms + RMS_EPS) * ng_ref[:, sl]).astype(bf16)
    y_ssd = _dot(hn_ref[...], wout_ref[...])
    out_ref[...] = _sigmoid(gs_ref[...].astype(f32)) * y_ssd


def _ssd(xbc, z, dt, gs, conv_w, conv_b, dt_bias, a_log, d_skip, ssd_norm_g, w_ssd_out, batch, seq):
    t = batch * seq
    L = SSD_CHUNK
    nc = seq // L
    pad_h = LANES - SSD_N_HEADS
    dtb = jnp.pad(dt_bias, (0, pad_h)).reshape(1, LANES)
    alog = jnp.pad(a_log, (0, pad_h)).reshape(1, LANES)
    dexp = jnp.repeat(d_skip, SSD_HEAD_DIM).reshape(1, SSD_D_INNER)
    const = lambda b, c: (0, 0)
    row = lambda b, c: (b * nc + c, 0)
    return pl.pallas_call(
        _ssd_kernel,
        grid=(batch, nc),
        in_specs=[pl.BlockSpec((L, SSD_CONV_DIM), row),
                  pl.BlockSpec((L, SSD_D_INNER), row),
                  pl.BlockSpec((L, LANES), row),
                  pl.BlockSpec((L, D_MODEL), row),
                  pl.BlockSpec((SSD_CONV, SSD_CONV_DIM), const),
                  pl.BlockSpec((1, SSD_CONV_DIM), const),
                  pl.BlockSpec((1, LANES), const),
                  pl.BlockSpec((1, LANES), const),
                  pl.BlockSpec((1, SSD_D_INNER), const),
                  pl.BlockSpec((1, SSD_D_INNER), const),
                  pl.BlockSpec((SSD_D_INNER, D_MODEL), const, pipeline_mode=pl.Buffered(1))],
        out_specs=pl.BlockSpec((L, D_MODEL), row),
        out_shape=jax.ShapeDtypeStruct((t, D_MODEL), f32),
        scratch_shapes=[pltpu.VMEM((2 * CONV_HALO, SSD_CONV_DIM), f32),
                        pltpu.VMEM((SSD_N_HEADS // 2, SSD_D_STATE, LANES), f32),
                        pltpu.VMEM((L, SSD_D_INNER), f32),
                        pltpu.VMEM((L, SSD_N_GROUPS * SSD_D_STATE), f32),
                        pltpu.VMEM((L, SSD_N_GROUPS * SSD_D_STATE), f32),
                        pltpu.VMEM((L, SSD_D_INNER), f32),
                        pltpu.VMEM((L, SSD_D_INNER), bf16)],
        compiler_params=pltpu.CompilerParams(dimension_semantics=("arbitrary", "arbitrary"),
                                             vmem_limit_bytes=VMEM_LIMIT),
        name="ssd",
    )(xbc, z, dt, gs, conv_w, conv_b.reshape(1, -1), dtb, alog, dexp,
      ssd_norm_g.reshape(1, -1), w_ssd_out.astype(bf16))


ROUTE_E, ROUTE_RANK, ROUTE_GATE = 0, 2, 4
ATT_SUB = 2


def _attn_kernel(sink_ref, q_ref, k_ref, v_ref, cos_ref, sin_ref, sel_ref, ga_ref, ms_ref, x_ref,
                 qg_ref, kg_ref, wao_ref, wo_ref, n2g_ref, wrh_ref, wrl_ref, br_ref,
                 x1_ref, h2_ref, route_ref, route_t_ref, cnt_ref,
                 kprev_ref, vprev_ref, att_ref, cnt_scr, s_ref, p_ref, vd_ref):
    Q = ATT_BLOCK
    b = pl.program_id(0)
    n = pl.program_id(1)

    @pl.when(n == 0)
    def _():
        kprev_ref[...] = jnp.zeros(kprev_ref.shape, f32)
        vprev_ref[...] = jnp.zeros(vprev_ref.shape, f32)

    @pl.when((b == 0) & (n == 0))
    def _():
        cnt_scr[...] = jnp.zeros(cnt_scr.shape, f32)

    R = q_ref.shape[0]
    lane_q = lax.broadcasted_iota(jnp.int32, (Q, LANES), 1)
    row_q = lax.broadcasted_iota(jnp.int32, (Q, LANES), 0)
    left = lane_q < ATT_HEAD_DIM
    head_mean = jnp.where((row_q // ATT_HEAD_DIM) == (lane_q // ATT_HEAD_DIM),
                          1.0 / ATT_HEAD_DIM, 0.0).astype(bf16)
    lane2 = lax.broadcasted_iota(jnp.int32, (2 * Q, LANES), 1) < ATT_HEAD_DIM
    qg = ATT_N_HEADS // ATT_N_KV
    rows = qg * Q
    ri = lax.broadcasted_iota(jnp.int32, (rows, 2 * Q), 0) % Q
    cj = lax.broadcasted_iota(jnp.int32, (rows, 2 * Q), 1)
    in_window = (cj > ri) & (cj <= ri + Q)
    half = ROPE_DIM // 2

    for sb in range(R // Q):
        rs = slice(sb * Q, (sb + 1) * Q)
        cs = jnp.concatenate([cos_ref[:, rs], sin_ref[:, rs], jnp.ones((half, LANES), f32),
                              jnp.zeros((LANES - 3 * half, LANES), f32)], axis=0)
        c_hi, c_mid, c_lo = _split3(cs.T)
        pat = _dot(c_hi, sel_ref[...]) + _dot(c_mid, sel_ref[...]) + _dot(c_lo, sel_ref[...])
        cpat = pat[:, 0:LANES]
        s1pat = pat[:, LANES:2 * LANES]
        s2pat = pat[:, 2 * LANES:3 * LANES]

        def norm_rope(tc, gpat):
            sq = tc * tc
            hi = sq.astype(bf16)
            lo = (sq - hi.astype(f32)).astype(bf16)
            ms = _dot(hi, head_mean) + _dot(lo, head_mean)
            tn = tc * lax.rsqrt(ms + RMS_EPS) * gpat
            return (tn * cpat + pltpu.roll(tn, LANES - ROPE_DIM // 2, 1) * s1pat
                    + pltpu.roll(tn, ROPE_DIM // 2, 1) * s2pat)

        kd = []
        for cidx in range(COL_KV // LANES):
            sl = slice(cidx * LANES, (cidx + 1) * LANES)
            k_cur = norm_rope(k_ref[rs, sl].astype(f32), kg_ref[...])
            v_cur = v_ref[rs, sl].astype(f32)
            k_all = jnp.concatenate([kprev_ref[:, sl], k_cur], axis=0)
            v_all = jnp.concatenate([vprev_ref[:, sl], v_cur], axis=0)
            kprev_ref[:, sl] = k_cur
            vprev_ref[:, sl] = v_cur
            k_sw = pltpu.roll(k_all, ATT_HEAD_DIM, 1)
            v_sw = pltpu.roll(v_all, ATT_HEAD_DIM, 1)
            kd.append(jnp.where(lane2, k_all, k_sw).astype(bf16))
            kd.append(jnp.where(lane2, k_sw, k_all).astype(bf16))
            for half_i, vv in enumerate((jnp.where(lane2, v_all, v_sw), jnp.where(lane2, v_sw, v_all))):
                vd_ref[sb * ATT_N_KV + 2 * cidx + half_i] = vv.astype(bf16)

        allowed = in_window & ((cj >= Q) | (n > 0)) if sb == 0 else in_window
        for h in range(ATT_N_KV):
            parts = []
            for cidx in (2 * h, 2 * h + 1):
                qc = norm_rope(q_ref[rs, cidx * LANES:(cidx + 1) * LANES].astype(f32), qg_ref[...]) * ATT_SCALE
                parts.append(jnp.where(left, qc, 0.0).astype(bf16))
                parts.append(jnp.where(left, 0.0, qc).astype(bf16))
            lhs = jnp.concatenate(parts, axis=0)
            g = sb * ATT_N_KV + h
            s_ref[g * rows:(g + 1) * rows, :] = jnp.where(allowed, _dot_nt(lhs, kd[h]), NEG_BIG)

    n_grp = (R // Q) * ATT_N_KV
    sink = jnp.concatenate([jnp.full((Q, 1), sink_ref[(i // qg) % ATT_N_KV * qg + i % qg], f32)
                            for i in range(n_grp * qg)], axis=0)
    s = s_ref[...]
    m = jnp.maximum(jnp.max(s, axis=-1, keepdims=True), sink)
    p = jnp.exp(s - m)
    denom = jnp.sum(p, axis=-1, keepdims=True) + jnp.exp(sink - m)
    p_ref[...] = (p / denom).astype(bf16)

    for sb in range(R // Q):
        rs = slice(sb * Q, (sb + 1) * Q)
        for h in range(ATT_N_KV):
            g = sb * ATT_N_KV + h
            o = _dot(p_ref[g * rows:(g + 1) * rows, :], vd_ref[g])
            for r in range(2):
                cidx = 2 * h + r
                att_ref[rs, cidx * LANES:(cidx + 1) * LANES] = jnp.where(
                    left, o[(2 * r) * Q:(2 * r + 1) * Q], o[(2 * r + 1) * Q:(2 * r + 2) * Q]).astype(bf16)

    lane = lax.broadcasted_iota(jnp.int32, (R, LANES), 1)
    y_att = _dot(att_ref[...], wao_ref[...])
    merged = _sigmoid(ga_ref[...].astype(f32)) * y_att + ms_ref[...]
    x1 = x_ref[...] + _dot(merged.astype(bf16), wo_ref[...])
    x1_ref[...] = x1
    h2 = x1 * lax.rsqrt(jnp.mean(x1 * x1, axis=-1, keepdims=True) + RMS_EPS) * n2g_ref[...]
    h2_ref[...] = h2

    hi = h2.astype(bf16)
    lo = (h2 - hi.astype(f32)).astype(bf16)
    logits = _dot(hi, wrh_ref[...]) + _dot(lo, wrh_ref[...]) + _dot(hi, wrl_ref[...]) + br_ref[...]
    big = 4 * LANES
    gl = jnp.where(lane < MOE_N_GROUPS, logits, NEG_BIG)
    gmax = jnp.max(gl, axis=-1, keepdims=True)
    gsel = jnp.min(jnp.where(gl == gmax, lane, big), axis=-1, keepdims=True)
    pg = 1.0 / jnp.sum(jnp.exp(gl - gmax), axis=-1, keepdims=True)
    lo_l = MOE_N_GROUPS + MOE_EPG * gsel
    el = jnp.where((lane >= lo_l) & (lane < lo_l + MOE_EPG), logits, NEG_BIG)
    v1 = jnp.max(el, axis=-1, keepdims=True)
    i1 = jnp.min(jnp.where(el == v1, lane, big), axis=-1, keepdims=True)
    el2 = jnp.where(lane == i1, NEG_BIG, el)
    v2 = jnp.max(el2, axis=-1, keepdims=True)
    i2 = jnp.min(jnp.where(el2 == v2, lane, big), axis=-1, keepdims=True)
    e1 = i1 - MOE_N_GROUPS
    e2 = i2 - MOE_N_GROUPS
    tt = jnp.exp(v2 - v1)
    w1 = pg * (1.0 / (1.0 + tt))
    w2 = pg * (tt / (1.0 + tt))

    onehot = jnp.where((lane == e1) | (lane == e2), 1.0, 0.0)
    strict = jnp.where(lax.broadcasted_iota(jnp.int32, (R, R), 0) > lax.broadcasted_iota(jnp.int32, (R, R), 1),
                       1.0, 0.0).astype(bf16)
    base = _dot(strict, onehot.astype(bf16)) + cnt_scr[0:1, :]
    r1 = jnp.sum(jnp.where(lane == e1, base, 0.0), axis=-1, keepdims=True)
    r2 = jnp.sum(jnp.where(lane == e2, base, 0.0), axis=-1, keepdims=True)
    cnt_new = cnt_scr[0:1, :] + jnp.sum(onehot, axis=0, keepdims=True)
    cnt_scr[...] = jnp.broadcast_to(cnt_new, cnt_scr.shape)
    cnt_ref[...] = jnp.broadcast_to(cnt_new, cnt_ref.shape)

    rec = jnp.zeros((R, LANES), f32)
    for off, val in ((ROUTE_E, e1.astype(f32)), (ROUTE_E + 1, e2.astype(f32)),
                     (ROUTE_RANK, r1), (ROUTE_RANK + 1, r2), (ROUTE_GATE, w1), (ROUTE_GATE + 1, w2)):
        rec = jnp.where(lane == off, val, rec)
    route_ref[...] = rec
    for sb in range(R // Q):
        route_t_ref[:, sb * Q:(sb + 1) * Q] = rec[sb * Q:(sb + 1) * Q].T[0:8, :]


def _attn(q, k, v, cos_t, sin_t, ga, merged_ssd, x2, q_norm_g, k_norm_g, sinks, w_attn_out, w_out,
          norm2_g, w_rg, b_rg, w_re, b_re, batch, seq):
    t = batch * seq
    Q = ATT_SUB * ATT_BLOCK
    nb = seq // Q
    rep = LANES // ATT_HEAD_DIM
    qg = jnp.tile(q_norm_g, rep).reshape(1, LANES)
    kg = jnp.tile(k_norm_g, rep).reshape(1, LANES)
    n_log = MOE_N_GROUPS + MOE_N_EXPERTS
    w_r = jnp.concatenate([w_rg, jnp.transpose(w_re, (1, 0, 2)).reshape(D_MODEL, MOE_N_EXPERTS),
                           jnp.zeros((D_MODEL, LANES - n_log), f32)], axis=1)
    b_r = jnp.concatenate([b_rg, b_re.reshape(-1), jnp.zeros((LANES - n_log,), f32)]).reshape(1, LANES)
    w_r_hi = w_r.astype(bf16)
    w_r_lo = (w_r - w_r_hi.astype(f32)).astype(bf16)
    const = lambda b, n, *_: (0, 0)
    row = lambda b, n, *_: (b * nb + n, 0)
    col = lambda b, n, *_: (0, b * nb + n)
    full = lambda shape: pl.BlockSpec(shape, const, pipeline_mode=pl.Buffered(1))
    return pl.pallas_call(
        _attn_kernel,
        grid_spec=pltpu.PrefetchScalarGridSpec(
            num_scalar_prefetch=1,
            grid=(batch, nb),
            in_specs=[pl.BlockSpec((Q, COL_Q), row),
                      pl.BlockSpec((Q, COL_KV), row),
                      pl.BlockSpec((Q, COL_KV), row),
                      pl.BlockSpec((ROPE_DIM // 2, Q), col),
                      pl.BlockSpec((ROPE_DIM // 2, Q), col),
                      full((LANES, 3 * LANES)),
                      pl.BlockSpec((Q, D_MODEL), row),
                      pl.BlockSpec((Q, D_MODEL), row),
                      pl.BlockSpec((Q, D_MODEL), row),
                      pl.BlockSpec((1, LANES), const),
                      pl.BlockSpec((1, LANES), const),
                      full((COL_Q, D_MODEL)),
                      full((D_MODEL, D_MODEL)),
                      pl.BlockSpec((1, D_MODEL), const),
                      full((D_MODEL, LANES)),
                      full((D_MODEL, LANES)),
                      pl.BlockSpec((1, LANES), const)],
            out_specs=[pl.BlockSpec((Q, D_MODEL), row),
                       pl.BlockSpec((Q, D_MODEL), row),
                       pl.BlockSpec((Q, LANES), row),
                       pl.BlockSpec((8, Q), col),
                       pl.BlockSpec((8, LANES), const)],
            scratch_shapes=[pltpu.VMEM((ATT_BLOCK, COL_KV), f32),
                            pltpu.VMEM((ATT_BLOCK, COL_KV), f32),
                            pltpu.VMEM((Q, COL_Q), bf16),
                            pltpu.VMEM((8, LANES), f32),
                            pltpu.VMEM((ATT_SUB * ATT_N_HEADS * ATT_BLOCK, 2 * ATT_BLOCK), f32),
                            pltpu.VMEM((ATT_SUB * ATT_N_HEADS * ATT_BLOCK, 2 * ATT_BLOCK), bf16),
                            pltpu.VMEM((ATT_SUB * ATT_N_KV, 2 * ATT_BLOCK, LANES), bf16)],
        ),
        out_shape=[jax.ShapeDtypeStruct((t, D_MODEL), f32),
                   jax.ShapeDtypeStruct((t, D_MODEL), f32),
                   jax.ShapeDtypeStruct((t, LANES), f32),
                   jax.ShapeDtypeStruct((8, t), f32),
                   jax.ShapeDtypeStruct((8, LANES), f32)],
        compiler_params=pltpu.CompilerParams(dimension_semantics=("arbitrary", "arbitrary"),
                                             vmem_limit_bytes=VMEM_LIMIT),
        name="attn",
    )(sinks, q, k, v, cos_t, sin_t, _rope_select(), ga, merged_ssd, x2, qg, kg,
      w_attn_out.astype(bf16), w_out.astype(bf16), norm2_g.reshape(1, D_MODEL), w_r_hi, w_r_lo, b_r)


MOE_TM = 256
MOE_UNROLL = 8


def _row_copy(src, i, dst, j, sem):
    return pltpu.make_async_copy(src.at[pl.ds(i, 1), :], dst.at[pl.ds(j, 1), :], sem)


def _dispatch_kernel(dest0_ref, dest1_ref, zblk_ref, h2_ref, xs_hbm, zero_ref, sem, zsem):
    tm = h2_ref.shape[0]
    base = pl.program_id(0) * tm

    @pl.when(pl.program_id(0) == 0)
    def _():
        zero_ref[...] = jnp.zeros(zero_ref.shape, f32)

        def zcopy(i):
            start = pl.multiple_of(zblk_ref[i] * MOE_BLOCK, MOE_BLOCK)
            return pltpu.make_async_copy(zero_ref, xs_hbm.at[pl.ds(start, MOE_BLOCK), :], zsem)

        def zstart(i, carry):
            @pl.when(zblk_ref[i] >= 0)
            def _():
                zcopy(i).start()
            return carry

        def zwait(i, carry):
            @pl.when(zblk_ref[i] >= 0)
            def _():
                zcopy(i).wait()
            return carry

        lax.fori_loop(0, zblk_ref.shape[0], zstart, 0)
        lax.fori_loop(0, zblk_ref.shape[0], zwait, 0)

    def issue(i, carry):
        _row_copy(h2_ref, i, xs_hbm, dest0_ref[base + i], sem).start()
        _row_copy(h2_ref, i, xs_hbm, dest1_ref[base + i], sem).start()
        return carry

    lax.fori_loop(0, tm, issue, 0, unroll=MOE_UNROLL)
    for _ in range(2):
        pltpu.make_async_copy(h2_ref, xs_hbm.at[pl.ds(0, tm), :], sem).wait()


def _dispatch(h2, dest0, dest1, zero_blocks, n_rows):
    t = h2.shape[0]
    tm = min(MOE_TM, t)
    return pl.pallas_call(
        _dispatch_kernel,
        grid_spec=pltpu.PrefetchScalarGridSpec(
            num_scalar_prefetch=3,
            grid=(t // tm,),
            in_specs=[pl.BlockSpec((tm, D_MODEL), lambda i, d0, d1, zb: (i, 0))],
            out_specs=pl.BlockSpec(memory_space=pl.ANY),
            scratch_shapes=[pltpu.VMEM((MOE_BLOCK, D_MODEL), f32),
                            pltpu.SemaphoreType.DMA(()),
                            pltpu.SemaphoreType.DMA(())],
        ),
        out_shape=jax.ShapeDtypeStruct((n_rows, D_MODEL), f32),
        compiler_params=pltpu.CompilerParams(dimension_semantics=("arbitrary",)),
        name="dispatch",
    )(dest0, dest1, zero_blocks, h2)


def _expert_kernel(be_ref, nu_ref, xs_ref, wgu_ref, wd_ref, ys_ref):
    used = pl.program_id(0) < nu_ref[0]

    @pl.when(used)
    def _():
        gu = _dot(xs_ref[...].astype(bf16), wgu_ref[0])
        hid = _silu(gu[:, :MOE_D_FF]) * gu[:, MOE_D_FF:]
        ys_ref[...] = _dot(hid.astype(bf16), wd_ref[0])

    @pl.when(jnp.logical_not(used))
    def _():
        ys_ref[...] = jnp.zeros(ys_ref.shape, f32)


def _experts(xs, block_e, n_used, w_gu, w_d):
    n_rows = xs.shape[0]
    nblk = n_rows // MOE_BLOCK
    blk_in = lambda i, be, nu: (jnp.minimum(i, nu[0] - 1), 0)
    blk_out = lambda i, be, nu: (i, 0)
    wsel = lambda i, be, nu: (be[i], 0, 0)
    return pl.pallas_call(
        _expert_kernel,
        grid_spec=pltpu.PrefetchScalarGridSpec(
            num_scalar_prefetch=2,
            grid=(nblk,),
            in_specs=[pl.BlockSpec((MOE_BLOCK, D_MODEL), blk_in),
                      pl.BlockSpec((1, D_MODEL, 2 * MOE_D_FF), wsel),
                      pl.BlockSpec((1, MOE_D_FF, D_MODEL), wsel)],
            out_specs=pl.BlockSpec((MOE_BLOCK, D_MODEL), blk_out),
        ),
        out_shape=jax.ShapeDtypeStruct((n_rows, D_MODEL), f32),
        compiler_params=pltpu.CompilerParams(dimension_semantics=("arbitrary",)),
        name="experts",
    )(block_e, n_used, xs, w_gu, w_d)


def _combine_kernel(dest0_ref, dest1_ref, x1_ref, route_ref, ys_hbm, out_ref, y0_ref, y1_ref, sem):
    tm = x1_ref.shape[0]
    base = pl.program_id(0) * tm

    def issue(i, carry):
        _row_copy(ys_hbm, dest0_ref[base + i], y0_ref, i, sem).start()
        _row_copy(ys_hbm, dest1_ref[base + i], y1_ref, i, sem).start()
        return carry

    lax.fori_loop(0, tm, issue, 0, unroll=MOE_UNROLL)
    for y_ref in (y0_ref, y1_ref):
        pltpu.make_async_copy(ys_hbm.at[pl.ds(0, tm), :], y_ref, sem).wait()
    g0 = route_ref[:, ROUTE_GATE:ROUTE_GATE + 1]
    g1 = route_ref[:, ROUTE_GATE + 1:ROUTE_GATE + 2]
    out_ref[...] = x1_ref[...] + y0_ref[...] * g0 + y1_ref[...] * g1


def _combine(x1, route, ys, dest0, dest1):
    t = x1.shape[0]
    tm = min(MOE_TM, t)
    row = lambda i, d0, d1: (i, 0)
    return pl.pallas_call(
        _combine_kernel,
        grid_spec=pltpu.PrefetchScalarGridSpec(
            num_scalar_prefetch=2,
            grid=(t // tm,),
            in_specs=[pl.BlockSpec((tm, D_MODEL), row),
                      pl.BlockSpec((tm, LANES), row),
                      pl.BlockSpec(memory_space=pl.ANY)],
            out_specs=pl.BlockSpec((tm, D_MODEL), row),
            scratch_shapes=[pltpu.VMEM((tm, D_MODEL), f32),
                            pltpu.VMEM((tm, D_MODEL), f32),
                            pltpu.SemaphoreType.DMA(())],
        ),
        out_shape=jax.ShapeDtypeStruct((t, D_MODEL), f32),
        compiler_params=pltpu.CompilerParams(dimension_semantics=("arbitrary",)),
        name="combine",
    )(dest0, dest1, x1, route, ys)


def _moe(x1, h2, route, route_t, counts_rec, w_gate_e, w_up_e, w_down_e):
    t = x1.shape[0]
    n_rows = 2 * t + MOE_N_EXPERTS * MOE_BLOCK
    nblk = n_rows // MOE_BLOCK
    counts = counts_rec[0, :MOE_N_EXPERTS].astype(jnp.int32)
    padded = (counts + MOE_BLOCK - 1) // MOE_BLOCK * MOE_BLOCK
    pend = jnp.cumsum(padded)
    pstart = pend - padded
    expert_ids = jnp.arange(MOE_N_EXPERTS, dtype=jnp.int32)[:, None]

    def sorted_row(k):
        eid = route_t[ROUTE_E + k].astype(jnp.int32)
        rank = route_t[ROUTE_RANK + k].astype(jnp.int32)
        return jnp.sum(jnp.where(eid[None, :] == expert_ids, pstart[:, None], 0), axis=0) + rank

    dest0, dest1 = sorted_row(0), sorted_row(1)
    block_start = jnp.arange(nblk, dtype=jnp.int32) * MOE_BLOCK
    block_e = jnp.minimum(jnp.sum((pend[None, :] <= block_start[:, None]).astype(jnp.int32), axis=1),
                          MOE_N_EXPERTS - 1)
    n_used = (pend[-1:] // MOE_BLOCK).astype(jnp.int32)
    last_blk = jnp.where(counts > 0, pend // MOE_BLOCK - 1, -1)
    tail = n_used[0] + jnp.arange(MOE_N_EXPERTS, dtype=jnp.int32)
    zero_blocks = jnp.concatenate([last_blk, jnp.where(tail < nblk, tail, -1)]).astype(jnp.int32)
    w_gu = jnp.concatenate([w_gate_e, w_up_e], axis=-1).astype(bf16)
    xs = _dispatch(h2, dest0, dest1, zero_blocks, n_rows)
    ys = _experts(xs, block_e, n_used, w_gu, w_down_e.astype(bf16))
    return _combine(x1, route, ys, dest0, dest1)


def _layer(x, positions, norm1_g, w_in, conv_w, conv_b, dt_bias, a_log, d_skip, ssd_norm_g, w_ssd_out,
           q_norm_g, k_norm_g, sinks, w_attn_out, w_out, norm2_g, w_rg, b_rg, w_re, b_re,
           w_gate_e, w_up_e, w_down_e):
    batch, seq, d = x.shape
    x2 = x.reshape(batch * seq, d)
    cos_t, sin_t = _rope_tables(positions)
    z, xbc, q, k, v, gs, ga, dt = _inproj(x2, norm1_g, w_in)
    merged_ssd = _ssd(xbc, z, dt, gs, conv_w, conv_b, dt_bias, a_log, d_skip, ssd_norm_g, w_ssd_out,
                      batch, seq)
    x1, h2, route, route_t, counts = _attn(q, k, v, cos_t, sin_t, ga, merged_ssd, x2, q_norm_g, k_norm_g,
                                           sinks, w_attn_out, w_out, norm2_g, w_rg, b_rg, w_re, b_re,
                                           batch, seq)
    out = _moe(x1, h2, route, route_t, counts, w_gate_e, w_up_e, w_down_e)
    return out.reshape(batch, seq, d)


def kernel(x, positions, norm1_g, w_in, conv_w, conv_b, dt_bias, a_log, d_skip, ssd_norm_g, w_ssd_out,
           q_norm_g, k_norm_g, sinks, w_attn_out, w_out, norm2_g, w_router_group, b_router_group,
           w_router_expert, b_router_expert, w_gate_e, w_up_e, w_down_e):
    for l in range(norm1_g.shape[0]):
        x = _layer(x, positions, norm1_g[l], w_in[l], conv_w[l], conv_b[l], dt_bias[l], a_log[l],
                   d_skip[l], ssd_norm_g[l], w_ssd_out[l], q_norm_g[l], k_norm_g[l], sinks[l],
                   w_attn_out[l], w_out[l], norm2_g[l], w_router_group[l], b_router_group[l],
                   w_router_expert[l], b_router_expert[l], w_gate_e[l], w_up_e[l], w_down_e[l])
    return x
```

```python
import functools

import numpy as np

import jax
import jax.numpy as jnp
from jax import lax
from jax.experimental import pallas as pl
from jax.experimental.pallas import tpu as pltpu

f32 = jnp.float32
bf16 = jnp.bfloat16

D_MODEL = 1024
SSD_D_INNER = 2048
SSD_HEAD_DIM = 64
SSD_N_HEADS = 32
SSD_N_GROUPS = 4
SSD_D_STATE = 128
SSD_CONV = 4
SSD_CHUNK = 128
SSD_CONV_DIM = 3072
ATT_HEAD_DIM = 64
ATT_N_HEADS = 16
ATT_N_KV = 4
ATT_BLOCK = 128
ATT_SCALE = ATT_HEAD_DIM ** -0.5
ROPE_THETA = 500000.0
ROPE_DIM = 16
MOE_N_GROUPS = 8
MOE_EPG = 8
MOE_N_EXPERTS = 64
MOE_D_FF = 256
MOE_BLOCK = 256
RMS_EPS = 1e-6

LANES = 128
CONV_HALO = 8
NEG_BIG = -1e30
VMEM_LIMIT = 56 * 1024 * 1024

COL_Z = SSD_D_INNER
COL_XBC = SSD_CONV_DIM
COL_DT = SSD_N_HEADS
COL_Q = ATT_N_HEADS * ATT_HEAD_DIM
COL_KV = ATT_N_KV * ATT_HEAD_DIM


def _sigmoid(x):
    return 1.0 / (1.0 + jnp.exp(-x))


def _silu(x):
    return x * _sigmoid(x)


def _split3(x):
    hi = x.astype(bf16)
    r1 = x - hi.astype(f32)
    mid = r1.astype(bf16)
    lo = (r1 - mid.astype(f32)).astype(bf16)
    return hi, mid, lo


def _dot(a, b):
    return jnp.dot(a, b, preferred_element_type=f32)


def _dot_nt(a, b):
    return lax.dot_general(a, b, (((1,), (1,)), ((), ())), preferred_element_type=f32)


def _trig_kernel(freq_ref, pos_ref, cos_ref, sin_ref):
    ang = pos_ref[...].astype(f32) * freq_ref[pl.program_id(0)]
    cos_ref[0] = jnp.cos(ang)
    sin_ref[0] = jnp.sin(ang)


def _rope_select():
    half = ROPE_DIM // 2
    sel = np.zeros((LANES, 3 * LANES), np.float32)
    for lane in range(LANES):
        m = lane % ATT_HEAD_DIM
        if m < half:
            sel[m, lane] = 1.0
            sel[half + m, LANES + lane] = -1.0
        elif m < ROPE_DIM:
            sel[m - half, lane] = 1.0
            sel[m, 2 * LANES + lane] = 1.0
        else:
            sel[ROPE_DIM, lane] = 1.0
    return jnp.asarray(sel, bf16)


def _rope_tables(positions):
    t = positions.size
    half = ROPE_DIM // 2
    inv_freq = ROPE_THETA ** (-jnp.arange(0, ROPE_DIM, 2, dtype=f32) / ROPE_DIM)
    pos2d = positions.reshape(t // LANES, LANES)
    cos_t, sin_t = pl.pallas_call(
        _trig_kernel,
        grid_spec=pltpu.PrefetchScalarGridSpec(
            num_scalar_prefetch=1,
            grid=(half,),
            in_specs=[pl.BlockSpec((t // LANES, LANES), lambda j, f: (0, 0))],
            out_specs=[pl.BlockSpec((1, t // LANES, LANES), lambda j, f: (j, 0, 0))] * 2,
        ),
        out_shape=[jax.ShapeDtypeStruct((half, t // LANES, LANES), f32)] * 2,
        name="trig",
    )(inv_freq, pos2d)
    return cos_t.reshape(half, t), sin_t.reshape(half, t)


INPROJ_TM = 512
INPROJ_CH = 512


def _inproj_kernel(x_ref, g_ref, w_ref,
                   z_ref, xbc_ref, q_ref, k_ref, v_ref, gs_ref, ga_ref, dt_ref, h_scr):
    x = x_ref[...]
    ms = jnp.mean(x * x, axis=-1, keepdims=True)
    h_scr[...] = (x * lax.rsqrt(ms + RMS_EPS) * g_ref[...]).astype(bf16)
    off = 0
    for ref in (z_ref, xbc_ref, q_ref, k_ref, v_ref, gs_ref, ga_ref, dt_ref):
        width = ref.shape[1]
        for c in range(0, width, INPROJ_CH):
            cw = min(INPROJ_CH, width - c)
            ref[:, c:c + cw] = _dot(h_scr[...], w_ref[:, off + c:off + c + cw]).astype(ref.dtype)
        off += width


WPREP_ROWS = 128


def _wprep_kernel(w_ref, o_ref):
    s1 = COL_Z + COL_XBC
    s2 = s1 + COL_DT
    n_tail = w_ref.shape[1] - s2
    o_ref[:, 0:s1] = w_ref[:, 0:s1].astype(bf16)
    o_ref[:, s1:s1 + n_tail] = w_ref[:, s2:s2 + n_tail].astype(bf16)
    lane = lax.broadcasted_iota(jnp.int32, (w_ref.shape[0], LANES), 1)
    o_ref[:, s1 + n_tail:s1 + n_tail + LANES] = jnp.where(lane < COL_DT, w_ref[:, s1:s1 + LANES], 0.0).astype(bf16)


def _inproj(x2, norm1_g, w_in):
    t = x2.shape[0]
    tm = min(INPROJ_TM, t)
    n_in = w_in.shape[1]
    n_all = n_in - COL_DT + LANES
    w_all = pl.pallas_call(
        _wprep_kernel,
        grid=(D_MODEL // WPREP_ROWS,),
        in_specs=[pl.BlockSpec((WPREP_ROWS, n_in), lambda i: (i, 0))],
        out_specs=pl.BlockSpec((WPREP_ROWS, n_all), lambda i: (i, 0)),
        out_shape=jax.ShapeDtypeStruct((D_MODEL, n_all), bf16),
        name="wprep",
    )(w_in)
    widths = (COL_Z, COL_XBC, COL_Q, COL_KV, COL_KV, D_MODEL, D_MODEL)
    const = lambda i: (0, 0)
    row = lambda i: (i, 0)
    outs = pl.pallas_call(
        _inproj_kernel,
        grid=(t // tm,),
        in_specs=[pl.BlockSpec((tm, D_MODEL), row),
                  pl.BlockSpec((1, D_MODEL), const),
                  pl.BlockSpec((D_MODEL, w_all.shape[1]), const, pipeline_mode=pl.Buffered(1))],
        out_specs=[pl.BlockSpec((tm, w), row) for w in widths] + [pl.BlockSpec((tm, LANES), row)],
        out_shape=[jax.ShapeDtypeStruct((t, w), bf16) for w in widths]
                  + [jax.ShapeDtypeStruct((t, LANES), f32)],
        scratch_shapes=[pltpu.VMEM((tm, D_MODEL), bf16)],
        compiler_params=pltpu.CompilerParams(dimension_semantics=("arbitrary",),
                                             vmem_limit_bytes=VMEM_LIMIT),
        name="inproj",
    )(x2, norm1_g.reshape(1, D_MODEL), w_all)
    return outs


def _ssd_kernel(xbc_ref, z_ref, dt_ref, gs_ref, cw_ref, cb_ref, dtb_ref, alog_ref, dexp_ref, ng_ref,
                wout_ref, out_ref, ext_ref, st_ref, xs_ref, bm_ref, cm_ref, y_ref, hn_ref):
    L = SSD_CHUNK
    c = pl.program_id(1)

    @pl.when(c == 0)
    def _():
        ext_ref[0:CONV_HALO, :] = jnp.zeros((CONV_HALO, SSD_CONV_DIM), f32)
        st_ref[...] = jnp.zeros(st_ref.shape, f32)

    bf_tile = 2 * CONV_HALO
    ext_ref[CONV_HALO:2 * CONV_HALO, :] = xbc_ref[0:bf_tile, :].astype(f32)[0:CONV_HALO]
    n_sh = SSD_CONV - 1
    sr = lax.broadcasted_iota(jnp.int32, (n_sh * L, L), 0)
    sc = lax.broadcasted_iota(jnp.int32, (n_sh * L, L), 1)
    shift = jnp.where((sr % L) - sc == (sr // L) + 1, 1.0, 0.0).astype(bf16)
    cch = 128
    for cc in range(0, SSD_CONV_DIM, cch):
        cs_ = slice(cc, cc + cch)
        xb = xbc_ref[:, cs_]
        sh = _dot(shift, xb)
        w_now = cw_ref[SSD_CONV - 1:SSD_CONV, cs_]
        acc = cb_ref[:, cs_] + xb.astype(f32) * w_now
        top = cb_ref[:, cs_] + ext_ref[CONV_HALO:2 * CONV_HALO, cs_] * w_now
        for j in range(1, SSD_CONV):
            w_j = cw_ref[SSD_CONV - 1 - j:SSD_CONV - j, cs_]
            acc = acc + sh[(j - 1) * L:j * L] * w_j
            top = top + ext_ref[CONV_HALO - j:2 * CONV_HALO - j, cs_] * w_j
        if cc < SSD_D_INNER:
            dst, o = xs_ref, cc
        elif cc < SSD_D_INNER + SSD_N_GROUPS * SSD_D_STATE:
            dst, o = bm_ref, cc - SSD_D_INNER
        else:
            dst, o = cm_ref, cc - SSD_D_INNER - SSD_N_GROUPS * SSD_D_STATE
        dst[:, o:o + cch] = _silu(acc)
        dst[0:CONV_HALO, o:o + cch] = _silu(top)
    ext_ref[0:CONV_HALO, :] = xbc_ref[L - bf_tile:L, :].astype(f32)[CONV_HALO:bf_tile]

    lane_row = lax.broadcasted_iota(jnp.int32, (1, LANES), 1)
    row_i = lax.broadcasted_iota(jnp.int32, (L, L), 0)
    col_i = lax.broadcasted_iota(jnp.int32, (L, L), 1)
    causal = row_i >= col_i
    left = col_i < SSD_HEAD_DIM

    xdt = dt_ref[...] + dtb_ref[...]
    dtv = jnp.maximum(xdt, 0.0) + jnp.log1p(jnp.exp(-jnp.abs(xdt)))
    a = jnp.where(lane_row < SSD_N_HEADS, -jnp.exp(alog_ref[...]), 0.0)
    d_a = dtv * a
    tril = jnp.where(causal, 1.0, 0.0).astype(bf16)
    hi, mid, lo3 = _split3(d_a)
    a_cum = _dot(tril, hi) + _dot(tril, mid) + _dot(tril, lo3)
    a_end = a_cum[L - 1:L, :]
    exp_a = jnp.exp(a_cum)
    w_end = jnp.exp(a_end - a_cum) * dtv
    cd = jnp.exp(a_end)
    a_t = a_cum.T
    dt_t = dtv.T
    w_t = w_end.T

    n_pairs = SSD_N_HEADS // 2
    pairs_per_group = n_pairs // SSD_N_GROUPS
    for g in range(SSD_N_GROUPS):
        b_g = bm_ref[:, g * SSD_D_STATE:(g + 1) * SSD_D_STATE]
        c_g = cm_ref[:, g * SSD_D_STATE:(g + 1) * SSD_D_STATE]
        cb = _dot_nt(c_g.astype(bf16), b_g.astype(bf16))
        b_t = b_g.T
        for pi in range(pairs_per_group):
            i = g * pairs_per_group + pi
            xpair = xs_ref[:, i * LANES:(i + 1) * LANES]
            xpair_b = xpair.astype(bf16)
            s_prev = st_ref[i]
            rhs = jnp.concatenate([xpair_b, s_prev.astype(bf16)], axis=0)
            ys = []
            sn = []
            for h in (2 * i, 2 * i + 1):
                acol = jnp.broadcast_to(a_cum[:, h:h + 1], (L, L))
                arow = jnp.broadcast_to(a_t[h:h + 1, :], (L, L))
                dtrow = jnp.broadcast_to(dt_t[h:h + 1, :], (L, L))
                dec = jnp.exp(jnp.where(causal, acol - arow, NEG_BIG))
                m = cb * dec * dtrow
                cs = c_g * jnp.broadcast_to(exp_a[:, h:h + 1], (L, L))
                lhs = jnp.concatenate([m.astype(bf16), cs.astype(bf16)], axis=1)
                ys.append(_dot(lhs, rhs))
                btw = (b_t * jnp.broadcast_to(w_t[h:h + 1, :], (L, L))).astype(bf16)
                sn.append(_dot(btw, xpair_b))
            h0 = 2 * i
            cd_pair = jnp.where(lane_row < SSD_HEAD_DIM, cd[:, h0:h0 + 1], cd[:, h0 + 1:h0 + 2])
            st_ref[i] = jnp.where(left, sn[0], sn[1]) + s_prev * cd_pair
            y_pair = jnp.where(left, ys[0], ys[1])
            y_ref[:, i * LANES:(i + 1) * LANES] = y_pair + xpair * dexp_ref[:, i * LANES:(i + 1) * LANES]

    gw = SSD_D_INNER // SSD_N_GROUPS
    for g in range(SSD_N_GROUPS):
        sl = slice(g * gw, (g + 1) * gw)
        yz = y_ref[:, sl] * _silu(z_ref[:, sl].astype(f32))
        ms = jnp.mean(yz * yz, axis=-1, keepdims=True)
        hn_ref[:, sl] = (yz * lax.rsqrt(ms + RMS_EPS) * ng_ref[:, sl]).astype(bf16)
    y_ssd = _dot(hn_ref[...], wout_ref[...])
    out_ref[...] = _sigmoid(gs_ref[...].astype(f32)) * y_ssd


def _ssd(xbc, z, dt, gs, conv_w, conv_b, dt_bias, a_log, d_skip, ssd_norm_g, w_ssd_out, batch, seq):
    t = batch * seq
    L = SSD_CHUNK
    nc = seq // L
    pad_h = LANES - SSD_N_HEADS
    dtb = jnp.pad(dt_bias, (0, pad_h)).reshape(1, LANES)
    alog = jnp.pad(a_log, (0, pad_h)).reshape(1, LANES)
    dexp = jnp.repeat(d_skip, SSD_HEAD_DIM).reshape(1, SSD_D_INNER)
    const = lambda b, c: (0, 0)
    row = lambda b, c: (b * nc + c, 0)
    return pl.pallas_call(
        _ssd_kernel,
        grid=(batch, nc),
        in_specs=[pl.BlockSpec((L, SSD_CONV_DIM), row),
                  pl.BlockSpec((L, SSD_D_INNER), row),
                  pl.BlockSpec((L, LANES), row),
                  pl.BlockSpec((L, D_MODEL), row),
                  pl.BlockSpec((SSD_CONV, SSD_CONV_DIM), const),
                  pl.BlockSpec((1, SSD_CONV_DIM), const),
                  pl.BlockSpec((1, LANES), const),
                  pl.BlockSpec((1, LANES), const),
                  pl.BlockSpec((1, SSD_D_INNER), const),
                  pl.BlockSpec((1, SSD_D_INNER), const),
                  pl.BlockSpec((SSD_D_INNER, D_MODEL), const, pipeline_mode=pl.Buffered(1))],
        out_specs=pl.BlockSpec((L, D_MODEL), row),
        out_shape=jax.ShapeDtypeStruct((t, D_MODEL), f32),
        scratch_shapes=[pltpu.VMEM((2 * CONV_HALO, SSD_CONV_DIM), f32),
                        pltpu.VMEM((SSD_N_HEADS // 2, SSD_D_STATE, LANES), f32),
                        pltpu.VMEM((L, SSD_D_INNER), f32),
                        pltpu.VMEM((L, SSD_N_GROUPS * SSD_D_STATE), f32),
                        pltpu.VMEM((L, SSD_N_GROUPS * SSD_D_STATE), f32),
                        pltpu.VMEM((L, SSD_D_INNER), f32),
                        pltpu.VMEM((L, SSD_D_INNER), bf16)],
        compiler_params=pltpu.CompilerParams(dimension_semantics=("arbitrary", "arbitrary"),
                                             vmem_limit_bytes=VMEM_LIMIT),
        name="ssd",
    )(xbc, z, dt, gs, conv_w, conv_b.reshape(1, -1), dtb, alog, dexp,
      ssd_norm_g.reshape(1, -1), w_ssd_out.astype(bf16))


ROUTE_E, ROUTE_RANK, ROUTE_GATE = 0, 2, 4
ATT_SUB = 2


def _attn_kernel(sink_ref, q_ref, k_ref, v_ref, cos_ref, sin_ref, sel_ref, ga_ref, ms_ref, x_ref,
                 qg_ref, kg_ref, wao_ref, wo_ref, n2g_ref, wrh_ref, wrl_ref, br_ref,
                 x1_ref, h2_ref, route_ref, route_t_ref, cnt_ref,
                 kprev_ref, vprev_ref, att_ref, cnt_scr, s_ref, p_ref, vd_ref):
    Q = ATT_BLOCK
    b = pl.program_id(0)
    n = pl.program_id(1)

    @pl.when(n == 0)
    def _():
        kprev_ref[...] = jnp.zeros(kprev_ref.shape, f32)
        vprev_ref[...] = jnp.zeros(vprev_ref.shape, f32)

    @pl.when((b == 0) & (n == 0))
    def _():
        cnt_scr[...] = jnp.zeros(cnt_scr.shape, f32)

    R = q_ref.shape[0]
    lane_q = lax.broadcasted_iota(jnp.int32, (Q, LANES), 1)
    row_q = lax.broadcasted_iota(jnp.int32, (Q, LANES), 0)
    left = lane_q < ATT_HEAD_DIM
    head_mean = jnp.where((row_q // ATT_HEAD_DIM) == (lane_q // ATT_HEAD_DIM),
                          1.0 / ATT_HEAD_DIM, 0.0).astype(bf16)
    lane2 = lax.broadcasted_iota(jnp.int32, (2 * Q, LANES), 1) < ATT_HEAD_DIM
    qg = ATT_N_HEADS // ATT_N_KV
    rows = qg * Q
    ri = lax.broadcasted_iota(jnp.int32, (rows, Q), 0) % Q
    cj = lax.broadcasted_iota(jnp.int32, (rows, Q), 1)
    upper = cj > ri
    half = ROPE_DIM // 2

    for sb in range(R // Q):
        rs = slice(sb * Q, (sb + 1) * Q)
        cs = jnp.concatenate([cos_ref[:, rs], sin_ref[:, rs], jnp.ones((half, LANES), f32),
                              jnp.zeros((LANES - 3 * half, LANES), f32)], axis=0)
        c_hi, c_mid, c_lo = _split3(cs.T)
        pat = _dot(c_hi, sel_ref[...]) + _dot(c_mid, sel_ref[...]) + _dot(c_lo, sel_ref[...])
        cpat = pat[:, 0:LANES]
        s1pat = pat[:, LANES:2 * LANES]
        s2pat = pat[:, 2 * LANES:3 * LANES]

        def norm_rope(tc, gpat):
            sq = tc * tc
            hi = sq.astype(bf16)
            lo = (sq - hi.astype(f32)).astype(bf16)
            ms = _dot(hi, head_mean) + _dot(lo, head_mean)
            tn = tc * lax.rsqrt(ms + RMS_EPS) * gpat
            return (tn * cpat + pltpu.roll(tn, LANES - ROPE_DIM // 2, 1) * s1pat
                    + pltpu.roll(tn, ROPE_DIM // 2, 1) * s2pat)

        kd = []
        for cidx in range(COL_KV // LANES):
            sl = slice(cidx * LANES, (cidx + 1) * LANES)
            k_cur = norm_rope(k_ref[rs, sl].astype(f32), kg_ref[...])
            v_cur = v_ref[rs, sl].astype(f32)
            k_all = jnp.concatenate([kprev_ref[:, sl], k_cur], axis=0)
            v_all = jnp.concatenate([vprev_ref[:, sl], v_cur], axis=0)
            kprev_ref[:, sl] = k_cur
            vprev_ref[:, sl] = v_cur
            k_sw = pltpu.roll(k_all, ATT_HEAD_DIM, 1)
            v_sw = pltpu.roll(v_all, ATT_HEAD_DIM, 1)
            kd.append(jnp.where(lane2, k_all, k_sw).astype(bf16))
            kd.append(jnp.where(lane2, k_sw, k_all).astype(bf16))
            for half_i, vv in enumerate((jnp.where(lane2, v_all, v_sw), jnp.where(lane2, v_sw, v_all))):
                vd_ref[sb * ATT_N_KV + 2 * cidx + half_i] = vv.astype(bf16)

        prev_live = upper & (n > 0) if sb == 0 else upper
        for h in range(ATT_N_KV):
            parts = []
            for cidx in (2 * h, 2 * h + 1):
                qc = norm_rope(q_ref[rs, cidx * LANES:(cidx + 1) * LANES].astype(f32), qg_ref[...]) * ATT_SCALE
                parts.append(jnp.where(left, qc, 0.0).astype(bf16))
                parts.append(jnp.where(left, 0.0, qc).astype(bf16))
            lhs = jnp.concatenate(parts, axis=0)
            g = sb * ATT_N_KV + h
            s_prev = _dot_nt(lhs, kd[h][0:Q])
            s_cur = _dot_nt(lhs, kd[h][Q:2 * Q])
            if sb == 0:
                s_cur = jnp.where(upper, NEG_BIG, s_cur)
            s_ref[g * rows:(g + 1) * rows, :] = jnp.where(prev_live, s_prev, s_cur)

    n_grp = (R // Q) * ATT_N_KV
    sink = jnp.concatenate([jnp.full((Q, 1), sink_ref[(i // qg) % ATT_N_KV * qg + i % qg], f32)
                            for i in range(n_grp * qg)], axis=0)
    s = s_ref[...]
    m = jnp.maximum(jnp.max(s, axis=-1, keepdims=True), sink)
    p = jnp.exp(s - m)
    denom = jnp.sum(p, axis=-1, keepdims=True) + jnp.exp(sink - m)
    p_ref[...] = (p / denom).astype(bf16)

    for sb in range(R // Q):
        rs = slice(sb * Q, (sb + 1) * Q)
        for h in range(ATT_N_KV):
            g = sb * ATT_N_KV + h
            pf = p_ref[g * rows:(g + 1) * rows, :]
            zero = jnp.zeros_like(pf)
            p_both = jnp.concatenate([jnp.where(upper, pf, zero), jnp.where(upper, zero, pf)], axis=1)
            o = _dot(p_both, vd_ref[g])
            for r in range(2):
                cidx = 2 * h + r
                att_ref[rs, cidx * LANES:(cidx + 1) * LANES] = jnp.where(
                    left, o[(2 * r) * Q:(2 * r + 1) * Q], o[(2 * r + 1) * Q:(2 * r + 2) * Q]).astype(bf16)

    lane = lax.broadcasted_iota(jnp.int32, (R, LANES), 1)
    y_att = _dot(att_ref[...], wao_ref[...])
    merged = _sigmoid(ga_ref[...].astype(f32)) * y_att + ms_ref[...]
    x1 = x_ref[...] + _dot(merged.astype(bf16), wo_ref[...])
    x1_ref[...] = x1
    h2 = x1 * lax.rsqrt(jnp.mean(x1 * x1, axis=-1, keepdims=True) + RMS_EPS) * n2g_ref[...]
    h2_ref[...] = h2

    hi = h2.astype(bf16)
    lo = (h2 - hi.astype(f32)).astype(bf16)
    logits = _dot(hi, wrh_ref[...]) + _dot(lo, wrh_ref[...]) + _dot(hi, wrl_ref[...]) + br_ref[...]
    big = 4 * LANES
    gl = jnp.where(lane < MOE_N_GROUPS, logits, NEG_BIG)
    gmax = jnp.max(gl, axis=-1, keepdims=True)
    gsel = jnp.min(jnp.where(gl == gmax, lane, big), axis=-1, keepdims=True)
    pg = 1.0 / jnp.sum(jnp.exp(gl - gmax), axis=-1, keepdims=True)
    lo_l = MOE_N_GROUPS + MOE_EPG * gsel
    el = jnp.where((lane >= lo_l) & (lane < lo_l + MOE_EPG), logits, NEG_BIG)
    v1 = jnp.max(el, axis=-1, keepdims=True)
    i1 = jnp.min(jnp.where(el == v1, lane, big), axis=-1, keepdims=True)
    el2 = jnp.where(lane == i1, NEG_BIG, el)
    v2 = jnp.max(el2, axis=-1, keepdims=True)
    i2 = jnp.min(jnp.where(el2 == v2, lane, big), axis=-1, keepdims=True)
    e1 = i1 - MOE_N_GROUPS
    e2 = i2 - MOE_N_GROUPS
    tt = jnp.exp(v2 - v1)
    w1 = pg * (1.0 / (1.0 + tt))
    w2 = pg * (tt / (1.0 + tt))

    onehot = jnp.where((lane == e1) | (lane == e2), 1.0, 0.0)
    strict = jnp.where(lax.broadcasted_iota(jnp.int32, (R, R), 0) > lax.broadcasted_iota(jnp.int32, (R, R), 1),
                       1.0, 0.0).astype(bf16)
    base = _dot(strict, onehot.astype(bf16)) + cnt_scr[0:1, :]
    r1 = jnp.sum(jnp.where(lane == e1, base, 0.0), axis=-1, keepdims=True)
    r2 = jnp.sum(jnp.where(lane == e2, base, 0.0), axis=-1, keepdims=True)
    cnt_new = cnt_scr[0:1, :] + jnp.sum(onehot, axis=0, keepdims=True)
    cnt_scr[...] = jnp.broadcast_to(cnt_new, cnt_scr.shape)
    cnt_ref[...] = jnp.broadcast_to(cnt_new, cnt_ref.shape)

    rec = jnp.zeros((R, LANES), f32)
    for off, val in ((ROUTE_E, e1.astype(f32)), (ROUTE_E + 1, e2.astype(f32)),
                     (ROUTE_RANK, r1), (ROUTE_RANK + 1, r2), (ROUTE_GATE, w1), (ROUTE_GATE + 1, w2)):
        rec = jnp.where(lane == off, val, rec)
    route_ref[...] = rec
    for sb in range(R // Q):
        route_t_ref[:, sb * Q:(sb + 1) * Q] = rec[sb * Q:(sb + 1) * Q].T[0:8, :]


def _attn(q, k, v, cos_t, sin_t, ga, merged_ssd, x2, q_norm_g, k_norm_g, sinks, w_attn_out, w_out,
          norm2_g, w_rg, b_rg, w_re, b_re, batch, seq):
    t = batch * seq
    Q = ATT_SUB * ATT_BLOCK
    nb = seq // Q
    rep = LANES // ATT_HEAD_DIM
    qg = jnp.tile(q_norm_g, rep).reshape(1, LANES)
    kg = jnp.tile(k_norm_g, rep).reshape(1, LANES)
    n_log = MOE_N_GROUPS + MOE_N_EXPERTS
    w_r = jnp.concatenate([w_rg, jnp.transpose(w_re, (1, 0, 2)).reshape(D_MODEL, MOE_N_EXPERTS),
                           jnp.zeros((D_MODEL, LANES - n_log), f32)], axis=1)
    b_r = jnp.concatenate([b_rg, b_re.reshape(-1), jnp.zeros((LANES - n_log,), f32)]).reshape(1, LANES)
    w_r_hi = w_r.astype(bf16)
    w_r_lo = (w_r - w_r_hi.astype(f32)).astype(bf16)
    const = lambda b, n, *_: (0, 0)
    row = lambda b, n, *_: (b * nb + n, 0)
    col = lambda b, n, *_: (0, b * nb + n)
    full = lambda shape: pl.BlockSpec(shape, const, pipeline_mode=pl.Buffered(1))
    return pl.pallas_call(
        _attn_kernel,
        grid_spec=pltpu.PrefetchScalarGridSpec(
            num_scalar_prefetch=1,
            grid=(batch, nb),
            in_specs=[pl.BlockSpec((Q, COL_Q), row),
                      pl.BlockSpec((Q, COL_KV), row),
                      pl.BlockSpec((Q, COL_KV), row),
                      pl.BlockSpec((ROPE_DIM // 2, Q), col),
                      pl.BlockSpec((ROPE_DIM // 2, Q), col),
                      full((LANES, 3 * LANES)),
                      pl.BlockSpec((Q, D_MODEL), row),
                      pl.BlockSpec((Q, D_MODEL), row),
                      pl.BlockSpec((Q, D_MODEL), row),
                      pl.BlockSpec((1, LANES), const),
                      pl.BlockSpec((1, LANES), const),
                      full((COL_Q, D_MODEL)),
                      full((D_MODEL, D_MODEL)),
                      pl.BlockSpec((1, D_MODEL), const),
                      full((D_MODEL, LANES)),
                      full((D_MODEL, LANES)),
                      pl.BlockSpec((1, LANES), const)],
            out_specs=[pl.BlockSpec((Q, D_MODEL), row),
                       pl.BlockSpec((Q, D_MODEL), row),
                       pl.BlockSpec((Q, LANES), row),
                       pl.BlockSpec((8, Q), col),
                       pl.BlockSpec((8, LANES), const)],
            scratch_shapes=[pltpu.VMEM((ATT_BLOCK, COL_KV), f32),
                            pltpu.VMEM((ATT_BLOCK, COL_KV), f32),
                            pltpu.VMEM((Q, COL_Q), bf16),
                            pltpu.VMEM((8, LANES), f32),
                            pltpu.VMEM((ATT_SUB * ATT_N_HEADS * ATT_BLOCK, ATT_BLOCK), f32),
                            pltpu.VMEM((ATT_SUB * ATT_N_HEADS * ATT_BLOCK, ATT_BLOCK), bf16),
                            pltpu.VMEM((ATT_SUB * ATT_N_KV, 2 * ATT_BLOCK, LANES), bf16)],
        ),
        out_shape=[jax.ShapeDtypeStruct((t, D_MODEL), f32),
                   jax.ShapeDtypeStruct((t, D_MODEL), f32),
                   jax.ShapeDtypeStruct((t, LANES), f32),
                   jax.ShapeDtypeStruct((8, t), f32),
                   jax.ShapeDtypeStruct((8, LANES), f32)],
        compiler_params=pltpu.CompilerParams(dimension_semantics=("arbitrary", "arbitrary"),
                                             vmem_limit_bytes=VMEM_LIMIT),
        name="attn",
    )(sinks, q, k, v, cos_t, sin_t, _rope_select(), ga, merged_ssd, x2, qg, kg,
      w_attn_out.astype(bf16), w_out.astype(bf16), norm2_g.reshape(1, D_MODEL), w_r_hi, w_r_lo, b_r)


MOE_TM = 256
MOE_UNROLL = 8


def _row_copy(src, i, dst, j, sem):
    return pltpu.make_async_copy(src.at[pl.ds(i, 1), :], dst.at[pl.ds(j, 1), :], sem)


def _dispatch_kernel(dest0_ref, dest1_ref, zblk_ref, h2_ref, xs_hbm, zero_ref, sem, zsem):
    tm = h2_ref.shape[0]
    base = pl.program_id(0) * tm

    @pl.when(pl.program_id(0) == 0)
    def _():
        zero_ref[...] = jnp.zeros(zero_ref.shape, f32)

        def zcopy(i):
            start = pl.multiple_of(zblk_ref[i] * MOE_BLOCK, MOE_BLOCK)
            return pltpu.make_async_copy(zero_ref, xs_hbm.at[pl.ds(start, MOE_BLOCK), :], zsem)

        def zstart(i, carry):
            @pl.when(zblk_ref[i] >= 0)
            def _():
                zcopy(i).start()
            return carry

        def zwait(i, carry):
            @pl.when(zblk_ref[i] >= 0)
            def _():
                zcopy(i).wait()
            return carry

        lax.fori_loop(0, zblk_ref.shape[0], zstart, 0)
        lax.fori_loop(0, zblk_ref.shape[0], zwait, 0)

    def issue(i, carry):
        _row_copy(h2_ref, i, xs_hbm, dest0_ref[base + i], sem).start()
        _row_copy(h2_ref, i, xs_hbm, dest1_ref[base + i], sem).start()
        return carry

    lax.fori_loop(0, tm, issue, 0, unroll=MOE_UNROLL)
    for _ in range(2):
        pltpu.make_async_copy(h2_ref, xs_hbm.at[pl.ds(0, tm), :], sem).wait()


def _dispatch(h2, dest0, dest1, zero_blocks, n_rows):
    t = h2.shape[0]
    tm = min(MOE_TM, t)
    return pl.pallas_call(
        _dispatch_kernel,
        grid_spec=pltpu.PrefetchScalarGridSpec(
            num_scalar_prefetch=3,
            grid=(t // tm,),
            in_specs=[pl.BlockSpec((tm, D_MODEL), lambda i, d0, d1, zb: (i, 0))],
            out_specs=pl.BlockSpec(memory_space=pl.ANY),
            scratch_shapes=[pltpu.VMEM((MOE_BLOCK, D_MODEL), f32),
                            pltpu.SemaphoreType.DMA(()),
                            pltpu.SemaphoreType.DMA(())],
        ),
        out_shape=jax.ShapeDtypeStruct((n_rows, D_MODEL), f32),
        compiler_params=pltpu.CompilerParams(dimension_semantics=("arbitrary",)),
        name="dispatch",
    )(dest0, dest1, zero_blocks, h2)


def _expert_kernel(be_ref, nu_ref, xs_ref, wg_ref, wu_ref, wd_ref, ys_ref, wgu_scr, wd_scr):
    i = pl.program_id(0)
    used = i < nu_ref[0]
    fresh = (i == 0) | (be_ref[i] != be_ref[jnp.maximum(i - 1, 0)])

    @pl.when(used & fresh)
    def _():
        wgu_scr[:, 0:MOE_D_FF] = wg_ref[0].astype(bf16)
        wgu_scr[:, MOE_D_FF:2 * MOE_D_FF] = wu_ref[0].astype(bf16)
        wd_scr[...] = wd_ref[0].astype(bf16)

    @pl.when(used)
    def _():
        gu = _dot(xs_ref[...].astype(bf16), wgu_scr[...])
        hid = _silu(gu[:, :MOE_D_FF]) * gu[:, MOE_D_FF:]
        ys_ref[...] = _dot(hid.astype(bf16), wd_scr[...])

    @pl.when(jnp.logical_not(used))
    def _():
        ys_ref[...] = jnp.zeros(ys_ref.shape, f32)


def _experts(xs, block_e, n_used, w_g, w_u, w_d):
    n_rows = xs.shape[0]
    nblk = n_rows // MOE_BLOCK
    blk_in = lambda i, be, nu: (jnp.minimum(i, nu[0] - 1), 0)
    blk_out = lambda i, be, nu: (i, 0)
    wsel = lambda i, be, nu: (be[i], 0, 0)
    return pl.pallas_call(
        _expert_kernel,
        grid_spec=pltpu.PrefetchScalarGridSpec(
            num_scalar_prefetch=2,
            grid=(nblk,),
            in_specs=[pl.BlockSpec((MOE_BLOCK, D_MODEL), blk_in),
                      pl.BlockSpec((1, D_MODEL, MOE_D_FF), wsel),
                      pl.BlockSpec((1, D_MODEL, MOE_D_FF), wsel),
                      pl.BlockSpec((1, MOE_D_FF, D_MODEL), wsel)],
            out_specs=pl.BlockSpec((MOE_BLOCK, D_MODEL), blk_out),
            scratch_shapes=[pltpu.VMEM((D_MODEL, 2 * MOE_D_FF), bf16),
                            pltpu.VMEM((MOE_D_FF, D_MODEL), bf16)],
        ),
        out_shape=jax.ShapeDtypeStruct((n_rows, D_MODEL), f32),
        compiler_params=pltpu.CompilerParams(dimension_semantics=("arbitrary",)),
        name="experts",
    )(block_e, n_used, xs, w_g, w_u, w_d)


def _combine_kernel(dest0_ref, dest1_ref, x1_ref, route_ref, ys_hbm, out_ref, y0_ref, y1_ref, sem):
    tm = x1_ref.shape[0]
    base = pl.program_id(0) * tm

    def issue(i, carry):
        _row_copy(ys_hbm, dest0_ref[base + i], y0_ref, i, sem).start()
        _row_copy(ys_hbm, dest1_ref[base + i], y1_ref, i, sem).start()
        return carry

    lax.fori_loop(0, tm, issue, 0, unroll=MOE_UNROLL)
    for y_ref in (y0_ref, y1_ref):
        pltpu.make_async_copy(ys_hbm.at[pl.ds(0, tm), :], y_ref, sem).wait()
    g0 = route_ref[:, ROUTE_GATE:ROUTE_GATE + 1]
    g1 = route_ref[:, ROUTE_GATE + 1:ROUTE_GATE + 2]
    out_ref[...] = x1_ref[...] + y0_ref[...] * g0 + y1_ref[...] * g1


def _combine(x1, route, ys, dest0, dest1):
    t = x1.shape[0]
    tm = min(MOE_TM, t)
    row = lambda i, d0, d1: (i, 0)
    return pl.pallas_call(
        _combine_kernel,
        grid_spec=pltpu.PrefetchScalarGridSpec(
            num_scalar_prefetch=2,
            grid=(t // tm,),
            in_specs=[pl.BlockSpec((tm, D_MODEL), row),
                      pl.BlockSpec((tm, LANES), row),
                      pl.BlockSpec(memory_space=pl.ANY)],
            out_specs=pl.BlockSpec((tm, D_MODEL), row),
            scratch_shapes=[pltpu.VMEM((tm, D_MODEL), f32),
                            pltpu.VMEM((tm, D_MODEL), f32),
                            pltpu.SemaphoreType.DMA(())],
        ),
        out_shape=jax.ShapeDtypeStruct((t, D_MODEL), f32),
        compiler_params=pltpu.CompilerParams(dimension_semantics=("arbitrary",)),
        name="combine",
    )(dest0, dest1, x1, route, ys)


def _moe(x1, h2, route, route_t, counts_rec, w_gate_e, w_up_e, w_down_e):
    t = x1.shape[0]
    n_rows = 2 * t + MOE_N_EXPERTS * MOE_BLOCK
    nblk = n_rows // MOE_BLOCK
    counts = counts_rec[0, :MOE_N_EXPERTS].astype(jnp.int32)
    padded = (counts + MOE_BLOCK - 1) // MOE_BLOCK * MOE_BLOCK
    pend = jnp.cumsum(padded)
    pstart = pend - padded
    expert_ids = jnp.arange(MOE_N_EXPERTS, dtype=jnp.int32)[:, None]

    def sorted_row(k):
        eid = route_t[ROUTE_E + k].astype(jnp.int32)
        rank = route_t[ROUTE_RANK + k].astype(jnp.int32)
        return jnp.sum(jnp.where(eid[None, :] == expert_ids, pstart[:, None], 0), axis=0) + rank

    dest0, dest1 = sorted_row(0), sorted_row(1)
    block_start = jnp.arange(nblk, dtype=jnp.int32) * MOE_BLOCK
    block_e = jnp.minimum(jnp.sum((pend[None, :] <= block_start[:, None]).astype(jnp.int32), axis=1),
                          MOE_N_EXPERTS - 1)
    n_used = (pend[-1:] // MOE_BLOCK).astype(jnp.int32)
    last_blk = jnp.where(counts > 0, pend // MOE_BLOCK - 1, -1)
    tail = n_used[0] + jnp.arange(MOE_N_EXPERTS, dtype=jnp.int32)
    zero_blocks = jnp.concatenate([last_blk, jnp.where(tail < nblk, tail, -1)]).astype(jnp.int32)
    xs = _dispatch(h2, dest0, dest1, zero_blocks, n_rows)
    ys = _experts(xs, block_e, n_used, w_gate_e, w_up_e, w_down_e)
    return _combine(x1, route, ys, dest0, dest1)


def _layer(x, positions, norm1_g, w_in, conv_w, conv_b, dt_bias, a_log, d_skip, ssd_norm_g, w_ssd_out,
           q_norm_g, k_norm_g, sinks, w_attn_out, w_out, norm2_g, w_rg, b_rg, w_re, b_re,
           w_gate_e, w_up_e, w_down_e):
    batch, seq, d = x.shape
    x2 = x.reshape(batch * seq, d)
    cos_t, sin_t = _rope_tables(positions)
    z, xbc, q, k, v, gs, ga, dt = _inproj(x2, norm1_g, w_in)
    merged_ssd = _ssd(xbc, z, dt, gs, conv_w, conv_b, dt_bias, a_log, d_skip, ssd_norm_g, w_ssd_out,
                      batch, seq)
    x1, h2, route, route_t, counts = _attn(q, k, v, cos_t, sin_t, ga, merged_ssd, x2, q_norm_g, k_norm_g,
                                           sinks, w_attn_out, w_out, norm2_g, w_rg, b_rg, w_re, b_re,
                                           batch, seq)
    out = _moe(x1, h2, route, route_t, counts, w_gate_e, w_up_e, w_down_e)
    return out.reshape(batch, seq, d)


def kernel(x, positions, norm1_g, w_in, conv_w, conv_b, dt_bias, a_log, d_skip, ssd_norm_g, w_ssd_out,
           q_norm_g, k_norm_g, sinks, w_attn_out, w_out, norm2_g, w_router_group, b_router_group,
           w_router_expert, b_router_expert, w_gate_e, w_up_e, w_down_e):
    for l in range(norm1_g.shape[0]):
        x = _layer(x, positions, norm1_g[l], w_in[l], conv_w[l], conv_b[l], dt_bias[l], a_log[l],
                   d_skip[l], ssd_norm_g[l], w_ssd_out[l], q_norm_g[l], k_norm_g[l], sinks[l],
                   w_attn_out[l], w_out[l], norm2_g[l], w_router_group[l], b_router_group[l],
                   w_router_expert[l], b_router_expert[l], w_gate_e[l], w_up_e[l], w_down_e[l])
    return x
```

```python
import functools

import numpy as np

import jax
import jax.numpy as jnp
from jax import lax
from jax.experimental import pallas as pl
from jax.experimental.pallas import tpu as pltpu

f32 = jnp.float32
bf16 = jnp.bfloat16

D_MODEL = 1024
SSD_D_INNER = 2048
SSD_HEAD_DIM = 64
SSD_N_HEADS = 32
SSD_N_GROUPS = 4
SSD_D_STATE = 128
SSD_CONV = 4
SSD_CHUNK = 128
SSD_CONV_DIM = 3072
ATT_HEAD_DIM = 64
ATT_N_HEADS = 16
ATT_N_KV = 4
ATT_BLOCK = 128
ATT_SCALE = ATT_HEAD_DIM ** -0.5
ROPE_THETA = 500000.0
ROPE_DIM = 16
MOE_N_GROUPS = 8
MOE_EPG = 8
MOE_N_EXPERTS = 64
MOE_D_FF = 256
MOE_BLOCK = 256
RMS_EPS = 1e-6

LANES = 128
CONV_HALO = 8
NEG_BIG = -1e30
VMEM_LIMIT = 56 * 1024 * 1024

COL_Z = SSD_D_INNER
COL_XBC = SSD_CONV_DIM
COL_DT = SSD_N_HEADS
COL_Q = ATT_N_HEADS * ATT_HEAD_DIM
COL_KV = ATT_N_KV * ATT_HEAD_DIM


def _sigmoid(x):
    return 1.0 / (1.0 + jnp.exp(-x))


def _silu(x):
    return x * _sigmoid(x)


def _split3(x):
    hi = x.astype(bf16)
    r1 = x - hi.astype(f32)
    mid = r1.astype(bf16)
    lo = (r1 - mid.astype(f32)).astype(bf16)
    return hi, mid, lo


def _dot(a, b):
    return jnp.dot(a, b, preferred_element_type=f32)


def _dot_nt(a, b):
    return lax.dot_general(a, b, (((1,), (1,)), ((), ())), preferred_element_type=f32)


def _trig_kernel(freq_ref, pos_ref, cos_ref, sin_ref):
    ang = pos_ref[...].astype(f32) * freq_ref[pl.program_id(0)]
    cos_ref[0] = jnp.cos(ang)
    sin_ref[0] = jnp.sin(ang)


def _rope_select():
    half = ROPE_DIM // 2
    sel = np.zeros((LANES, 3 * LANES), np.float32)
    for lane in range(LANES):
        m = lane % ATT_HEAD_DIM
        if m < half:
            sel[m, lane] = 1.0
            sel[half + m, LANES + lane] = -1.0
        elif m < ROPE_DIM:
            sel[m - half, lane] = 1.0
            sel[m, 2 * LANES + lane] = 1.0
        else:
            sel[ROPE_DIM, lane] = 1.0
    return jnp.asarray(sel, bf16)


def _rope_tables(positions):
    t = positions.size
    half = ROPE_DIM // 2
    inv_freq = ROPE_THETA ** (-jnp.arange(0, ROPE_DIM, 2, dtype=f32) / ROPE_DIM)
    pos2d = positions.reshape(t // LANES, LANES)
    cos_t, sin_t = pl.pallas_call(
        _trig_kernel,
        grid_spec=pltpu.PrefetchScalarGridSpec(
            num_scalar_prefetch=1,
            grid=(half,),
            in_specs=[pl.BlockSpec((t // LANES, LANES), lambda j, f: (0, 0))],
            out_specs=[pl.BlockSpec((1, t // LANES, LANES), lambda j, f: (j, 0, 0))] * 2,
        ),
        out_shape=[jax.ShapeDtypeStruct((half, t // LANES, LANES), f32)] * 2,
        name="trig",
    )(inv_freq, pos2d)
    return cos_t.reshape(half, t), sin_t.reshape(half, t)


INPROJ_TM = 512
INPROJ_CH = 512


def _inproj_kernel(x_ref, g_ref, w_ref,
                   z_ref, xbc_ref, q_ref, k_ref, v_ref, gs_ref, ga_ref, dt_ref, h_scr):
    x = x_ref[...]
    ms = jnp.mean(x * x, axis=-1, keepdims=True)
    h_scr[...] = (x * lax.rsqrt(ms + RMS_EPS) * g_ref[...]).astype(bf16)
    off = 0
    for ref in (z_ref, xbc_ref, q_ref, k_ref, v_ref, gs_ref, ga_ref, dt_ref):
        width = ref.shape[1]
        for c in range(0, width, INPROJ_CH):
            cw = min(INPROJ_CH, width - c)
            ref[:, c:c + cw] = _dot(h_scr[...], w_ref[:, off + c:off + c + cw]).astype(ref.dtype)
        off += width


WPREP_ROWS = 128


def _wprep_kernel(w_ref, o_ref):
    s1 = COL_Z + COL_XBC
    s2 = s1 + COL_DT
    n_tail = w_ref.shape[1] - s2
    o_ref[:, 0:s1] = w_ref[:, 0:s1].astype(bf16)
    o_ref[:, s1:s1 + n_tail] = w_ref[:, s2:s2 + n_tail].astype(bf16)
    lane = lax.broadcasted_iota(jnp.int32, (w_ref.shape[0], LANES), 1)
    o_ref[:, s1 + n_tail:s1 + n_tail + LANES] = jnp.where(lane < COL_DT, w_ref[:, s1:s1 + LANES], 0.0).astype(bf16)


def _inproj(x2, norm1_g, w_in):
    t = x2.shape[0]
    tm = min(INPROJ_TM, t)
    n_in = w_in.shape[1]
    n_all = n_in - COL_DT + LANES
    w_all = pl.pallas_call(
        _wprep_kernel,
        grid=(D_MODEL // WPREP_ROWS,),
        in_specs=[pl.BlockSpec((WPREP_ROWS, n_in), lambda i: (i, 0))],
        out_specs=pl.BlockSpec((WPREP_ROWS, n_all), lambda i: (i, 0)),
        out_shape=jax.ShapeDtypeStruct((D_MODEL, n_all), bf16),
        name="wprep",
    )(w_in)
    widths = (COL_Z, COL_XBC, COL_Q, COL_KV, COL_KV, D_MODEL, D_MODEL)
    const = lambda i: (0, 0)
    row = lambda i: (i, 0)
    outs = pl.pallas_call(
        _inproj_kernel,
        grid=(t // tm,),
        in_specs=[pl.BlockSpec((tm, D_MODEL), row),
                  pl.BlockSpec((1, D_MODEL), const),
                  pl.BlockSpec((D_MODEL, w_all.shape[1]), const, pipeline_mode=pl.Buffered(1))],
        out_specs=[pl.BlockSpec((tm, w), row) for w in widths] + [pl.BlockSpec((tm, LANES), row)],
        out_shape=[jax.ShapeDtypeStruct((t, w), bf16) for w in widths]
                  + [jax.ShapeDtypeStruct((t, LANES), f32)],
        scratch_shapes=[pltpu.VMEM((tm, D_MODEL), bf16)],
        compiler_params=pltpu.CompilerParams(dimension_semantics=("arbitrary",),
                                             vmem_limit_bytes=VMEM_LIMIT),
        name="inproj",
    )(x2, norm1_g.reshape(1, D_MODEL), w_all)
    return outs


def _ssd_kernel(xbc_ref, z_ref, dt_ref, gs_ref, cw_ref, cb_ref, dtb_ref, alog_ref, dexp_ref, ng_ref,
                wout_ref, out_ref, ext_ref, st_ref, xs_ref, bm_ref, cm_ref, y_ref, hn_ref):
    L = SSD_CHUNK
    c = pl.program_id(1)

    @pl.when(c == 0)
    def _():
        ext_ref[0:CONV_HALO, :] = jnp.zeros((CONV_HALO, SSD_CONV_DIM), f32)
        st_ref[...] = jnp.zeros(st_ref.shape, f32)

    bf_tile = 2 * CONV_HALO
    ext_ref[CONV_HALO:2 * CONV_HALO, :] = xbc_ref[0:bf_tile, :].astype(f32)[0:CONV_HALO]
    n_sh = SSD_CONV - 1
    sr = lax.broadcasted_iota(jnp.int32, (n_sh * L, L), 0)
    sc = lax.broadcasted_iota(jnp.int32, (n_sh * L, L), 1)
    shift = jnp.where((sr % L) - sc == (sr // L) + 1, 1.0, 0.0).astype(bf16)
    cch = 128
    for cc in range(0, SSD_CONV_DIM, cch):
        cs_ = slice(cc, cc + cch)
        xb = xbc_ref[:, cs_]
        sh = _dot(shift, xb)
        w_now = cw_ref[SSD_CONV - 1:SSD_CONV, cs_]
        acc = cb_ref[:, cs_] + xb.astype(f32) * w_now
        top = cb_ref[:, cs_] + ext_ref[CONV_HALO:2 * CONV_HALO, cs_] * w_now
        for j in range(1, SSD_CONV):
            w_j = cw_ref[SSD_CONV - 1 - j:SSD_CONV - j, cs_]
            acc = acc + sh[(j - 1) * L:j * L] * w_j
            top = top + ext_ref[CONV_HALO - j:2 * CONV_HALO - j, cs_] * w_j
        if cc < SSD_D_INNER:
            dst, o = xs_ref, cc
        elif cc < SSD_D_INNER + SSD_N_GROUPS * SSD_D_STATE:
            dst, o = bm_ref, cc - SSD_D_INNER
        else:
            dst, o = cm_ref, cc - SSD_D_INNER - SSD_N_GROUPS * SSD_D_STATE
        dst[:, o:o + cch] = _silu(acc)
        dst[0:CONV_HALO, o:o + cch] = _silu(top)
    ext_ref[0:CONV_HALO, :] = xbc_ref[L - bf_tile:L, :].astype(f32)[CONV_HALO:bf_tile]

    lane_row = lax.broadcasted_iota(jnp.int32, (1, LANES), 1)
    row_i = lax.broadcasted_iota(jnp.int32, (L, L), 0)
    col_i = lax.broadcasted_iota(jnp.int32, (L, L), 1)
    causal = row_i >= col_i
    left = col_i < SSD_HEAD_DIM

    xdt = dt_ref[...] + dtb_ref[...]
    dtv = jnp.maximum(xdt, 0.0) + jnp.log1p(jnp.exp(-jnp.abs(xdt)))
    a = jnp.where(lane_row < SSD_N_HEADS, -jnp.exp(alog_ref[...]), 0.0)
    d_a = dtv * a
    tril = jnp.where(causal, 1.0, 0.0).astype(bf16)
    hi, mid, lo3 = _split3(d_a)
    a_cum = _dot(tril, hi) + _dot(tril, mid) + _dot(tril, lo3)
    a_end = a_cum[L - 1:L, :]
    exp_a = jnp.exp(a_cum)
    w_end = jnp.exp(a_end - a_cum) * dtv
    cd = jnp.exp(a_end)
    a_t = a_cum.T
    dt_t = dtv.T
    w_t = w_end.T

    n_pairs = SSD_N_HEADS // 2
    pairs_per_group = n_pairs // SSD_N_GROUPS
    for g in range(SSD_N_GROUPS):
        b_g = bm_ref[:, g * SSD_D_STATE:(g + 1) * SSD_D_STATE]
        c_g = cm_ref[:, g * SSD_D_STATE:(g + 1) * SSD_D_STATE]
        cb = _dot_nt(c_g.astype(bf16), b_g.astype(bf16))
        b_t = b_g.T
        for pi in range(pairs_per_group):
            i = g * pairs_per_group + pi
            xpair = xs_ref[:, i * LANES:(i + 1) * LANES]
            xpair_b = xpair.astype(bf16)
            s_prev = st_ref[i]
            rhs = jnp.concatenate([xpair_b, s_prev.astype(bf16)], axis=0)
            ys = []
            sn = []
            for h in (2 * i, 2 * i + 1):
                acol = jnp.broadcast_to(a_cum[:, h:h + 1], (L, L))
                arow = jnp.broadcast_to(a_t[h:h + 1, :], (L, L))
                dtrow = jnp.broadcast_to(dt_t[h:h + 1, :], (L, L))
                dec = jnp.exp(jnp.where(causal, acol - arow, NEG_BIG))
                m = cb * dec * dtrow
                cs = c_g * jnp.broadcast_to(exp_a[:, h:h + 1], (L, L))
                lhs = jnp.concatenate([m.astype(bf16), cs.astype(bf16)], axis=1)
                ys.append(_dot(lhs, rhs))
                btw = (b_t * jnp.broadcast_to(w_t[h:h + 1, :], (L, L))).astype(bf16)
                sn.append(_dot(btw, xpair_b))
            h0 = 2 * i
            cd_pair = jnp.where(lane_row < SSD_HEAD_DIM, cd[:, h0:h0 + 1], cd[:, h0 + 1:h0 + 2])
            st_ref[i] = jnp.where(left, sn[0], sn[1]) + s_prev * cd_pair
            y_pair = jnp.where(left, ys[0], ys[1])
            y_ref[:, i * LANES:(i + 1) * LANES] = y_pair + xpair * dexp_ref[:, i * LANES:(i + 1) * LANES]

    gw = SSD_D_INNER // SSD_N_GROUPS
    for g in range(SSD_N_GROUPS):
        sl = slice(g * gw, (g + 1) * gw)
        yz = y_ref[:, sl] * _silu(z_ref[:, sl].astype(f32))
        ms = jnp.mean(yz * yz, axis=-1, keepdims=True)
        hn_ref[:, sl] = (yz * lax.rsqrt(ms + RMS_EPS) * ng_ref[:, sl]).astype(bf16)
    y_ssd = _dot(hn_ref[...], wout_ref[...])
    out_ref[...] = _sigmoid(gs_ref[...].astype(f32)) * y_ssd


def _ssd(xbc, z, dt, gs, conv_w, conv_b, dt_bias, a_log, d_skip, ssd_norm_g, w_ssd_out, batch, seq):
    t = batch * seq
    L = SSD_CHUNK
    nc = seq // L
    pad_h = LANES - SSD_N_HEADS
    dtb = jnp.pad(dt_bias, (0, pad_h)).reshape(1, LANES)
    alog = jnp.pad(a_log, (0, pad_h)).reshape(1, LANES)
    dexp = jnp.repeat(d_skip, SSD_HEAD_DIM).reshape(1, SSD_D_INNER)
    const = lambda b, c: (0, 0)
    row = lambda b, c: (b * nc + c, 0)
    return pl.pallas_call(
        _ssd_kernel,
        grid=(batch, nc),
        in_specs=[pl.BlockSpec((L, SSD_CONV_DIM), row),
                  pl.BlockSpec((L, SSD_D_INNER), row),
                  pl.BlockSpec((L, LANES), row),
                  pl.BlockSpec((L, D_MODEL), row),
                  pl.BlockSpec((SSD_CONV, SSD_CONV_DIM), const),
                  pl.BlockSpec((1, SSD_CONV_DIM), const),
                  pl.BlockSpec((1, LANES), const),
                  pl.BlockSpec((1, LANES), const),
                  pl.BlockSpec((1, SSD_D_INNER), const),
                  pl.BlockSpec((1, SSD_D_INNER), const),
                  pl.BlockSpec((SSD_D_INNER, D_MODEL), const, pipeline_mode=pl.Buffered(1))],
        out_specs=pl.BlockSpec((L, D_MODEL), row),
        out_shape=jax.ShapeDtypeStruct((t, D_MODEL), f32),
        scratch_shapes=[pltpu.VMEM((2 * CONV_HALO, SSD_CONV_DIM), f32),
                        pltpu.VMEM((SSD_N_HEADS // 2, SSD_D_STATE, LANES), f32),
                        pltpu.VMEM((L, SSD_D_INNER), f32),
                        pltpu.VMEM((L, SSD_N_GROUPS * SSD_D_STATE), f32),
                        pltpu.VMEM((L, SSD_N_GROUPS * SSD_D_STATE), f32),
                        pltpu.VMEM((L, SSD_D_INNER), f32),
                        pltpu.VMEM((L, SSD_D_INNER), bf16)],
        compiler_params=pltpu.CompilerParams(dimension_semantics=("arbitrary", "arbitrary"),
                                             vmem_limit_bytes=VMEM_LIMIT),
        name="ssd",
    )(xbc, z, dt, gs, conv_w, conv_b.reshape(1, -1), dtb, alog, dexp,
      ssd_norm_g.reshape(1, -1), w_ssd_out.astype(bf16))


ROUTE_E, ROUTE_RANK, ROUTE_GATE = 0, 2, 4
ATT_SUB = 2


def _attn_kernel(sink_ref, q_ref, k_ref, v_ref, cos_ref, sin_ref, sel_ref, ga_ref, ms_ref, x_ref,
                 qg_ref, kg_ref, wao_ref, wo_ref, n2g_ref, wrh_ref, wrl_ref, br_ref,
                 x1_ref, h2_ref, route_ref, route_t_ref, cnt_ref,
                 kprev_ref, vprev_ref, att_ref, cnt_scr, s_ref, p_ref, vd_ref):
    Q = ATT_BLOCK
    b = pl.program_id(0)
    n = pl.program_id(1)

    @pl.when(n == 0)
    def _():
        kprev_ref[...] = jnp.zeros(kprev_ref.shape, f32)
        vprev_ref[...] = jnp.zeros(vprev_ref.shape, f32)

    @pl.when((b == 0) & (n == 0))
    def _():
        cnt_scr[...] = jnp.zeros(cnt_scr.shape, f32)

    R = q_ref.shape[0]
    lane_q = lax.broadcasted_iota(jnp.int32, (Q, LANES), 1)
    row_q = lax.broadcasted_iota(jnp.int32, (Q, LANES), 0)
    left = lane_q < ATT_HEAD_DIM
    head_mean = jnp.where((row_q // ATT_HEAD_DIM) == (lane_q // ATT_HEAD_DIM),
                          1.0 / ATT_HEAD_DIM, 0.0).astype(bf16)
    lane2 = lax.broadcasted_iota(jnp.int32, (2 * Q, LANES), 1) < ATT_HEAD_DIM
    qg = ATT_N_HEADS // ATT_N_KV
    rows = qg * Q
    ri = lax.broadcasted_iota(jnp.int32, (rows, Q), 0) % Q
    cj = lax.broadcasted_iota(jnp.int32, (rows, Q), 1)
    upper = cj > ri
    half = ROPE_DIM // 2

    for sb in range(R // Q):
        rs = slice(sb * Q, (sb + 1) * Q)
        cs = jnp.concatenate([cos_ref[:, rs], sin_ref[:, rs], jnp.ones((half, LANES), f32),
                              jnp.zeros((LANES - 3 * half, LANES), f32)], axis=0)
        c_hi, c_mid, c_lo = _split3(cs.T)
        pat = _dot(c_hi, sel_ref[...]) + _dot(c_mid, sel_ref[...]) + _dot(c_lo, sel_ref[...])
        cpat = pat[:, 0:LANES]
        s1pat = pat[:, LANES:2 * LANES]
        s2pat = pat[:, 2 * LANES:3 * LANES]

        def norm_rope(tc, gpat):
            sq = tc * tc
            hi = sq.astype(bf16)
            lo = (sq - hi.astype(f32)).astype(bf16)
            ms = _dot(hi, head_mean) + _dot(lo, head_mean)
            tn = tc * lax.rsqrt(ms + RMS_EPS) * gpat
            return (tn * cpat + pltpu.roll(tn, LANES - ROPE_DIM // 2, 1) * s1pat
                    + pltpu.roll(tn, ROPE_DIM // 2, 1) * s2pat)

        kd = []
        for cidx in range(COL_KV // LANES):
            sl = slice(cidx * LANES, (cidx + 1) * LANES)
            k_cur = norm_rope(k_ref[rs, sl].astype(f32), kg_ref[...])
            v_cur = v_ref[rs, sl].astype(f32)
            k_all = jnp.concatenate([kprev_ref[:, sl], k_cur], axis=0)
            v_all = jnp.concatenate([vprev_ref[:, sl], v_cur], axis=0)
            kprev_ref[:, sl] = k_cur
            vprev_ref[:, sl] = v_cur
            k_sw = pltpu.roll(k_all, ATT_HEAD_DIM, 1)
            v_sw = pltpu.roll(v_all, ATT_HEAD_DIM, 1)
            kd.append(jnp.where(lane2, k_all, k_sw).astype(bf16))
            kd.append(jnp.where(lane2, k_sw, k_all).astype(bf16))
            for half_i, vv in enumerate((jnp.where(lane2, v_all, v_sw), jnp.where(lane2, v_sw, v_all))):
                vd_ref[sb * ATT_N_KV + 2 * cidx + half_i] = vv.astype(bf16)

        prev_live = upper & (n > 0) if sb == 0 else upper
        for h in range(ATT_N_KV):
            parts = []
            for cidx in (2 * h, 2 * h + 1):
                qc = norm_rope(q_ref[rs, cidx * LANES:(cidx + 1) * LANES].astype(f32), qg_ref[...]) * ATT_SCALE
                parts.append(jnp.where(left, qc, 0.0).astype(bf16))
                parts.append(jnp.where(left, 0.0, qc).astype(bf16))
            lhs = jnp.concatenate(parts, axis=0)
            g = sb * ATT_N_KV + h
            s_prev = _dot_nt(lhs, kd[h][0:Q])
            s_cur = _dot_nt(lhs, kd[h][Q:2 * Q])
            if sb == 0:
                s_cur = jnp.where(upper, NEG_BIG, s_cur)
            s_ref[g * rows:(g + 1) * rows, :] = jnp.where(prev_live, s_prev, s_cur)

    n_grp = (R // Q) * ATT_N_KV
    sink = jnp.concatenate([jnp.full((Q, 1), sink_ref[(i // qg) % ATT_N_KV * qg + i % qg], f32)
                            for i in range(n_grp * qg)], axis=0)
    s = s_ref[...]
    m = jnp.maximum(jnp.max(s, axis=-1, keepdims=True), sink)
    p = jnp.exp(s - m)
    denom = jnp.sum(p, axis=-1, keepdims=True) + jnp.exp(sink - m)
    p_ref[...] = (p / denom).astype(bf16)

    for sb in range(R // Q):
        rs = slice(sb * Q, (sb + 1) * Q)
        for h in range(ATT_N_KV):
            g = sb * ATT_N_KV + h
            pf = p_ref[g * rows:(g + 1) * rows, :]
            zero = jnp.zeros_like(pf)
            p_both = jnp.concatenate([jnp.where(upper, pf, zero), jnp.where(upper, zero, pf)], axis=1)
            o = _dot(p_both, vd_ref[g])
            for r in range(2):
                cidx = 2 * h + r
                att_ref[rs, cidx * LANES:(cidx + 1) * LANES] = jnp.where(
                    left, o[(2 * r) * Q:(2 * r + 1) * Q], o[(2 * r + 1) * Q:(2 * r + 2) * Q]).astype(bf16)

    lane = lax.broadcasted_iota(jnp.int32, (R, LANES), 1)
    y_att = _dot(att_ref[...], wao_ref[...])
    merged = _sigmoid(ga_ref[...].astype(f32)) * y_att + ms_ref[...]
    x1 = x_ref[...] + _dot(merged.astype(bf16), wo_ref[...])
    x1_ref[...] = x1
    h2 = x1 * lax.rsqrt(jnp.mean(x1 * x1, axis=-1, keepdims=True) + RMS_EPS) * n2g_ref[...]
    h2_ref[...] = h2

    hi = h2.astype(bf16)
    lo = (h2 - hi.astype(f32)).astype(bf16)
    logits = _dot(hi, wrh_ref[...]) + _dot(lo, wrh_ref[...]) + _dot(hi, wrl_ref[...]) + br_ref[...]
    big = 4 * LANES
    gl = jnp.where(lane < MOE_N_GROUPS, logits, NEG_BIG)
    gmax = jnp.max(gl, axis=-1, keepdims=True)
    gsel = jnp.min(jnp.where(gl == gmax, lane, big), axis=-1, keepdims=True)
    pg = 1.0 / jnp.sum(jnp.exp(gl - gmax), axis=-1, keepdims=True)
    lo_l = MOE_N_GROUPS + MOE_EPG * gsel
    el = jnp.where((lane >= lo_l) & (lane < lo_l + MOE_EPG), logits, NEG_BIG)
    v1 = jnp.max(el, axis=-1, keepdims=True)
    i1 = jnp.min(jnp.where(el == v1, lane, big), axis=-1, keepdims=True)
    el2 = jnp.where(lane == i1, NEG_BIG, el)
    v2 = jnp.max(el2, axis=-1, keepdims=True)
    i2 = jnp.min(jnp.where(el2 == v2, lane, big), axis=-1, keepdims=True)
    e1 = i1 - MOE_N_GROUPS
    e2 = i2 - MOE_N_GROUPS
    tt = jnp.exp(v2 - v1)
    w1 = pg * (1.0 / (1.0 + tt))
    w2 = pg * (tt / (1.0 + tt))

    onehot = jnp.where((lane == e1) | (lane == e2), 1.0, 0.0)
    strict = jnp.where(lax.broadcasted_iota(jnp.int32, (R, R), 0) > lax.broadcasted_iota(jnp.int32, (R, R), 1),
                       1.0, 0.0).astype(bf16)
    base = _dot(strict, onehot.astype(bf16)) + cnt_scr[0:1, :]
    r1 = jnp.sum(jnp.where(lane == e1, base, 0.0), axis=-1, keepdims=True)
    r2 = jnp.sum(jnp.where(lane == e2, base, 0.0), axis=-1, keepdims=True)
    cnt_new = cnt_scr[0:1, :] + jnp.sum(onehot, axis=0, keepdims=True)
    cnt_scr[...] = jnp.broadcast_to(cnt_new, cnt_scr.shape)
    cnt_ref[...] = jnp.broadcast_to(cnt_new, cnt_ref.shape)

    rec = jnp.zeros((R, LANES), f32)
    for off, val in ((ROUTE_E, e1.astype(f32)), (ROUTE_E + 1, e2.astype(f32)),
                     (ROUTE_RANK, r1), (ROUTE_RANK + 1, r2), (ROUTE_GATE, w1), (ROUTE_GATE + 1, w2)):
        rec = jnp.where(lane == off, val, rec)
    route_ref[...] = rec
    for sb in range(R // Q):
        route_t_ref[:, sb * Q:(sb + 1) * Q] = rec[sb * Q:(sb + 1) * Q].T[0:8, :]


def _attn(q, k, v, cos_t, sin_t, ga, merged_ssd, x2, q_norm_g, k_norm_g, sinks, w_attn_out, w_out,
          norm2_g, w_rg, b_rg, w_re, b_re, batch, seq):
    t = batch * seq
    Q = ATT_SUB * ATT_BLOCK
    nb = seq // Q
    rep = LANES // ATT_HEAD_DIM
    qg = jnp.tile(q_norm_g, rep).reshape(1, LANES)
    kg = jnp.tile(k_norm_g, rep).reshape(1, LANES)
    n_log = MOE_N_GROUPS + MOE_N_EXPERTS
    w_r = jnp.concatenate([w_rg, jnp.transpose(w_re, (1, 0, 2)).reshape(D_MODEL, MOE_N_EXPERTS),
                           jnp.zeros((D_MODEL, LANES - n_log), f32)], axis=1)
    b_r = jnp.concatenate([b_rg, b_re.reshape(-1), jnp.zeros((LANES - n_log,), f32)]).reshape(1, LANES)
    w_r_hi = w_r.astype(bf16)
    w_r_lo = (w_r - w_r_hi.astype(f32)).astype(bf16)
    const = lambda b, n, *_: (0, 0)
    row = lambda b, n, *_: (b * nb + n, 0)
    col = lambda b, n, *_: (0, b * nb + n)
    full = lambda shape: pl.BlockSpec(shape, const, pipeline_mode=pl.Buffered(1))
    return pl.pallas_call(
        _attn_kernel,
        grid_spec=pltpu.PrefetchScalarGridSpec(
            num_scalar_prefetch=1,
            grid=(batch, nb),
            in_specs=[pl.BlockSpec((Q, COL_Q), row),
                      pl.BlockSpec((Q, COL_KV), row),
                      pl.BlockSpec((Q, COL_KV), row),
                      pl.BlockSpec((ROPE_DIM // 2, Q), col),
                      pl.BlockSpec((ROPE_DIM // 2, Q), col),
                      full((LANES, 3 * LANES)),
                      pl.BlockSpec((Q, D_MODEL), row),
                      pl.BlockSpec((Q, D_MODEL), row),
                      pl.BlockSpec((Q, D_MODEL), row),
                      pl.BlockSpec((1, LANES), const),
                      pl.BlockSpec((1, LANES), const),
                      full((COL_Q, D_MODEL)),
                      full((D_MODEL, D_MODEL)),
                      pl.BlockSpec((1, D_MODEL), const),
                      full((D_MODEL, LANES)),
                      full((D_MODEL, LANES)),
                      pl.BlockSpec((1, LANES), const)],
            out_specs=[pl.BlockSpec((Q, D_MODEL), row),
                       pl.BlockSpec((Q, D_MODEL), row),
                       pl.BlockSpec((Q, LANES), row),
                       pl.BlockSpec((8, Q), col),
                       pl.BlockSpec((8, LANES), const)],
            scratch_shapes=[pltpu.VMEM((ATT_BLOCK, COL_KV), f32),
                            pltpu.VMEM((ATT_BLOCK, COL_KV), f32),
                            pltpu.VMEM((Q, COL_Q), bf16),
                            pltpu.VMEM((8, LANES), f32),
                            pltpu.VMEM((ATT_SUB * ATT_N_HEADS * ATT_BLOCK, ATT_BLOCK), f32),
                            pltpu.VMEM((ATT_SUB * ATT_N_HEADS * ATT_BLOCK, ATT_BLOCK), bf16),
                            pltpu.VMEM((ATT_SUB * ATT_N_KV, 2 * ATT_BLOCK, LANES), bf16)],
        ),
        out_shape=[jax.ShapeDtypeStruct((t, D_MODEL), f32),
                   jax.ShapeDtypeStruct((t, D_MODEL), f32),
                   jax.ShapeDtypeStruct((t, LANES), f32),
                   jax.ShapeDtypeStruct((8, t), f32),
                   jax.ShapeDtypeStruct((8, LANES), f32)],
        compiler_params=pltpu.CompilerParams(dimension_semantics=("arbitrary", "arbitrary"),
                                             vmem_limit_bytes=VMEM_LIMIT),
        name="attn",
    )(sinks, q, k, v, cos_t, sin_t, _rope_select(), ga, merged_ssd, x2, qg, kg,
      w_attn_out.astype(bf16), w_out.astype(bf16), norm2_g.reshape(1, D_MODEL), w_r_hi, w_r_lo, b_r)


MOE_TM = 256
MOE_UNROLL = 8


def _row_copy(src, i, dst, j, sem):
    return pltpu.make_async_copy(src.at[pl.ds(i, 1), :], dst.at[pl.ds(j, 1), :], sem)


def _dispatch_kernel(dest0_ref, dest1_ref, zblk_ref, h2_ref, xs_hbm, zero_ref, sem, zsem):
    tm = h2_ref.shape[0]
    base = pl.program_id(0) * tm

    @pl.when(pl.program_id(0) == 0)
    def _():
        zero_ref[...] = jnp.zeros(zero_ref.shape, f32)

        def zcopy(i):
            start = pl.multiple_of(zblk_ref[i] * MOE_BLOCK, MOE_BLOCK)
            return pltpu.make_async_copy(zero_ref, xs_hbm.at[pl.ds(start, MOE_BLOCK), :], zsem)

        def zstart(i, carry):
            @pl.when(zblk_ref[i] >= 0)
            def _():
                zcopy(i).start()
            return carry

        def zwait(i, carry):
            @pl.when(zblk_ref[i] >= 0)
            def _():
                zcopy(i).wait()
            return carry

        lax.fori_loop(0, zblk_ref.shape[0], zstart, 0)
        lax.fori_loop(0, zblk_ref.shape[0], zwait, 0)

    def issue(i, carry):
        _row_copy(h2_ref, i, xs_hbm, dest0_ref[base + i], sem).start()
        _row_copy(h2_ref, i, xs_hbm, dest1_ref[base + i], sem).start()
        return carry

    lax.fori_loop(0, tm, issue, 0, unroll=MOE_UNROLL)
    for _ in range(2):
        pltpu.make_async_copy(h2_ref, xs_hbm.at[pl.ds(0, tm), :], sem).wait()


def _dispatch(h2, dest0, dest1, zero_blocks, n_rows):
    t = h2.shape[0]
    tm = min(MOE_TM, t)
    return pl.pallas_call(
        _dispatch_kernel,
        grid_spec=pltpu.PrefetchScalarGridSpec(
            num_scalar_prefetch=3,
            grid=(t // tm,),
            in_specs=[pl.BlockSpec((tm, D_MODEL), lambda i, d0, d1, zb: (i, 0))],
            out_specs=pl.BlockSpec(memory_space=pl.ANY),
            scratch_shapes=[pltpu.VMEM((MOE_BLOCK, D_MODEL), f32),
                            pltpu.SemaphoreType.DMA(()),
                            pltpu.SemaphoreType.DMA(())],
        ),
        out_shape=jax.ShapeDtypeStruct((n_rows, D_MODEL), f32),
        compiler_params=pltpu.CompilerParams(dimension_semantics=("arbitrary",)),
        name="dispatch",
    )(dest0, dest1, zero_blocks, h2)


EXPERT_SUB = 2


def _expert_kernel(be_ref, nu_ref, xs_ref, *refs):
    w_refs = refs[:3 * EXPERT_SUB]
    ys_ref, wgu_scr, wd_scr = refs[3 * EXPERT_SUB:]
    i = pl.program_id(0)
    rows = MOE_BLOCK

    for j in range(EXPERT_SUB):
        blk = EXPERT_SUB * i + j
        fresh = (i == 0) | (be_ref[blk] != be_ref[jnp.maximum(blk - EXPERT_SUB, 0)])

        @pl.when((blk < nu_ref[0]) & fresh)
        def _(j=j):
            wg_ref, wu_ref, wd_ref = w_refs[3 * j:3 * j + 3]
            wgu_scr[j, :, 0:MOE_D_FF] = wg_ref[0].astype(bf16)
            wgu_scr[j, :, MOE_D_FF:2 * MOE_D_FF] = wu_ref[0].astype(bf16)
            wd_scr[j] = wd_ref[0].astype(bf16)

    def block(j):
        rs = slice(j * rows, (j + 1) * rows)
        gu = _dot(xs_ref[rs, :].astype(bf16), wgu_scr[j])
        hid = _silu(gu[:, :MOE_D_FF]) * gu[:, MOE_D_FF:]
        ys_ref[rs, :] = _dot(hid.astype(bf16), wd_scr[j])

    n_live = jnp.clip(nu_ref[0] - EXPERT_SUB * i, 0, EXPERT_SUB)
    for live in range(EXPERT_SUB + 1):
        @pl.when(n_live == live)
        def _(live=live):
            for j in range(live):
                block(j)
            if live < EXPERT_SUB:
                ys_ref[live * rows:, :] = jnp.zeros(((EXPERT_SUB - live) * rows, D_MODEL), f32)


def _experts(xs, block_e, n_used, w_g, w_u, w_d):
    n_rows = xs.shape[0]
    rows = EXPERT_SUB * MOE_BLOCK
    nstep = n_rows // rows
    blk_in = lambda i, be, nu: (jnp.minimum(i, (nu[0] - 1) // EXPERT_SUB), 0)
    blk_out = lambda i, be, nu: (i, 0)
    w_specs = []
    for j in range(EXPERT_SUB):
        wsel = lambda i, be, nu, j=j: (be[EXPERT_SUB * i + j], 0, 0)
        w_specs += [pl.BlockSpec((1, D_MODEL, MOE_D_FF), wsel),
                    pl.BlockSpec((1, D_MODEL, MOE_D_FF), wsel),
                    pl.BlockSpec((1, MOE_D_FF, D_MODEL), wsel)]
    return pl.pallas_call(
        _expert_kernel,
        grid_spec=pltpu.PrefetchScalarGridSpec(
            num_scalar_prefetch=2,
            grid=(nstep,),
            in_specs=[pl.BlockSpec((rows, D_MODEL), blk_in)] + w_specs,
            out_specs=pl.BlockSpec((rows, D_MODEL), blk_out),
            scratch_shapes=[pltpu.VMEM((EXPERT_SUB, D_MODEL, 2 * MOE_D_FF), bf16),
                            pltpu.VMEM((EXPERT_SUB, MOE_D_FF, D_MODEL), bf16)],
        ),
        out_shape=jax.ShapeDtypeStruct((n_rows, D_MODEL), f32),
        compiler_params=pltpu.CompilerParams(dimension_semantics=("arbitrary",),
                                             vmem_limit_bytes=VMEM_LIMIT),
        name="experts",
    )(block_e, n_used, xs, *([w_g, w_u, w_d] * EXPERT_SUB))


def _combine_kernel(dest0_ref, dest1_ref, x1_ref, route_ref, ys_hbm, out_ref, y0_ref, y1_ref, sem):
    tm = x1_ref.shape[0]
    step = pl.program_id(0)

    def gather_tile(tile, slot):
        base = tile * tm

        def issue(r, carry):
            pltpu.make_async_copy(ys_hbm.at[pl.ds(dest0_ref[base + r], 1), :],
                                  y0_ref.at[slot, pl.ds(r, 1), :], sem.at[slot]).start()
            pltpu.make_async_copy(ys_hbm.at[pl.ds(dest1_ref[base + r], 1), :],
                                  y1_ref.at[slot, pl.ds(r, 1), :], sem.at[slot]).start()
            return carry

        lax.fori_loop(0, tm, issue, 0, unroll=MOE_UNROLL)

    @pl.when(step == 0)
    def _():
        gather_tile(0, 0)

    @pl.when(step + 1 < pl.num_programs(0))
    def _():
        gather_tile(step + 1, (step + 1) % 2)

    slot = step % 2
    for y_ref in (y0_ref, y1_ref):
        pltpu.make_async_copy(ys_hbm.at[pl.ds(0, tm), :], y_ref.at[slot], sem.at[slot]).wait()
    g0 = route_ref[:, ROUTE_GATE:ROUTE_GATE + 1]
    g1 = route_ref[:, ROUTE_GATE + 1:ROUTE_GATE + 2]
    out_ref[...] = x1_ref[...] + y0_ref[slot] * g0 + y1_ref[slot] * g1


def _combine(x1, route, ys, dest0, dest1):
    t = x1.shape[0]
    tm = min(MOE_TM, t)
    row = lambda i, d0, d1: (i, 0)
    return pl.pallas_call(
        _combine_kernel,
        grid_spec=pltpu.PrefetchScalarGridSpec(
            num_scalar_prefetch=2,
            grid=(t // tm,),
            in_specs=[pl.BlockSpec((tm, D_MODEL), row),
                      pl.BlockSpec((tm, LANES), row),
                      pl.BlockSpec(memory_space=pl.ANY)],
            out_specs=pl.BlockSpec((tm, D_MODEL), row),
            scratch_shapes=[pltpu.VMEM((2, tm, D_MODEL), f32),
                            pltpu.VMEM((2, tm, D_MODEL), f32),
                            pltpu.SemaphoreType.DMA((2,))],
        ),
        out_shape=jax.ShapeDtypeStruct((t, D_MODEL), f32),
        compiler_params=pltpu.CompilerParams(dimension_semantics=("arbitrary",)),
        name="combine",
    )(dest0, dest1, x1, route, ys)


def _moe(x1, h2, route, route_t, counts_rec, w_gate_e, w_up_e, w_down_e):
    t = x1.shape[0]
    n_rows = 2 * t + MOE_N_EXPERTS * MOE_BLOCK
    nblk = n_rows // MOE_BLOCK
    counts = counts_rec[0, :MOE_N_EXPERTS].astype(jnp.int32)
    padded = (counts + MOE_BLOCK - 1) // MOE_BLOCK * MOE_BLOCK
    pend = jnp.cumsum(padded)
    pstart = pend - padded
    expert_ids = jnp.arange(MOE_N_EXPERTS, dtype=jnp.int32)[:, None]

    def sorted_row(k):
        eid = route_t[ROUTE_E + k].astype(jnp.int32)
        rank = route_t[ROUTE_RANK + k].astype(jnp.int32)
        return jnp.sum(jnp.where(eid[None, :] == expert_ids, pstart[:, None], 0), axis=0) + rank

    dest0, dest1 = sorted_row(0), sorted_row(1)
    block_start = jnp.arange(nblk, dtype=jnp.int32) * MOE_BLOCK
    block_e = jnp.minimum(jnp.sum((pend[None, :] <= block_start[:, None]).astype(jnp.int32), axis=1),
                          MOE_N_EXPERTS - 1)
    n_used = (pend[-1:] // MOE_BLOCK).astype(jnp.int32)
    last_blk = jnp.where(counts > 0, pend // MOE_BLOCK - 1, -1)
    tail = n_used[0] + jnp.arange(MOE_N_EXPERTS, dtype=jnp.int32)
    zero_blocks = jnp.concatenate([last_blk, jnp.where(tail < nblk, tail, -1)]).astype(jnp.int32)
    xs = _dispatch(h2, dest0, dest1, zero_blocks, n_rows)
    ys = _experts(xs, block_e, n_used, w_gate_e, w_up_e, w_down_e)
    return _combine(x1, route, ys, dest0, dest1)


def _layer(x, positions, norm1_g, w_in, conv_w, conv_b, dt_bias, a_log, d_skip, ssd_norm_g, w_ssd_out,
           q_norm_g, k_norm_g, sinks, w_attn_out, w_out, norm2_g, w_rg, b_rg, w_re, b_re,
           w_gate_e, w_up_e, w_down_e):
    batch, seq, d = x.shape
    x2 = x.reshape(batch * seq, d)
    cos_t, sin_t = _rope_tables(positions)
    z, xbc, q, k, v, gs, ga, dt = _inproj(x2, norm1_g, w_in)
    merged_ssd = _ssd(xbc, z, dt, gs, conv_w, conv_b, dt_bias, a_log, d_skip, ssd_norm_g, w_ssd_out,
                      batch, seq)
    x1, h2, route, route_t, counts = _attn(q, k, v, cos_t, sin_t, ga, merged_ssd, x2, q_norm_g, k_norm_g,
                                           sinks, w_attn_out, w_out, norm2_g, w_rg, b_rg, w_re, b_re,
                                           batch, seq)
    out = _moe(x1, h2, route, route_t, counts, w_gate_e, w_up_e, w_down_e)
    return out.reshape(batch, seq, d)


def kernel(x, positions, norm1_g, w_in, conv_w, conv_b, dt_bias, a_log, d_skip, ssd_norm_g, w_ssd_out,
           q_norm_g, k_norm_g, sinks, w_attn_out, w_out, norm2_g, w_router_group, b_router_group,
           w_router_expert, b_router_expert, w_gate_e, w_up_e, w_down_e):
    for l in range(norm1_g.shape[0]):
        x = _layer(x, positions, norm1_g[l], w_in[l], conv_w[l], conv_b[l], dt_bias[l], a_log[l],
                   d_skip[l], ssd_norm_g[l], w_ssd_out[l], q_norm_g[l], k_norm_g[l], sinks[l],
                   w_attn_out[l], w_out[l], norm2_g[l], w_router_group[l], b_router_group[l],
                   w_router_expert[l], b_router_expert[l], w_gate_e[l], w_up_e[l], w_down_e[l])
    return x
```

```python
import functools

import numpy as np

import jax
import jax.numpy as jnp
from jax import lax
from jax.experimental import pallas as pl
from jax.experimental.pallas import tpu as pltpu

f32 = jnp.float32
bf16 = jnp.bfloat16

D_MODEL = 1024
SSD_D_INNER = 2048
SSD_HEAD_DIM = 64
SSD_N_HEADS = 32
SSD_N_GROUPS = 4
SSD_D_STATE = 128
SSD_CONV = 4
SSD_CHUNK = 128
SSD_CONV_DIM = 3072
ATT_HEAD_DIM = 64
ATT_N_HEADS = 16
ATT_N_KV = 4
ATT_BLOCK = 128
ATT_SCALE = ATT_HEAD_DIM ** -0.5
ROPE_THETA = 500000.0
ROPE_DIM = 16
MOE_N_GROUPS = 8
MOE_EPG = 8
MOE_N_EXPERTS = 64
MOE_D_FF = 256
MOE_BLOCK = 256
RMS_EPS = 1e-6

LANES = 128
CONV_HALO = 8
NEG_BIG = -1e30
VMEM_LIMIT = 56 * 1024 * 1024

COL_Z = SSD_D_INNER
COL_XBC = SSD_CONV_DIM
COL_DT = SSD_N_HEADS
COL_Q = ATT_N_HEADS * ATT_HEAD_DIM
COL_KV = ATT_N_KV * ATT_HEAD_DIM


def _sigmoid(x):
    return 1.0 / (1.0 + jnp.exp(-x))


def _silu(x):
    return x * _sigmoid(x)


def _split3(x):
    hi = x.astype(bf16)
    r1 = x - hi.astype(f32)
    mid = r1.astype(bf16)
    lo = (r1 - mid.astype(f32)).astype(bf16)
    return hi, mid, lo


HALF_D = D_MODEL // 2
_HI_MASK = np.uint32(0xFFFF0000)


def _pack_rows(x):
    bits = pltpu.bitcast(x.astype(bf16).astype(f32), jnp.uint32)
    return (bits[:, HALF_D:] & _HI_MASK) | (bits[:, :HALF_D] >> 16)


def _unpack_rows(p):
    return pltpu.bitcast(p << 16, f32), pltpu.bitcast(p & _HI_MASK, f32)


def _dot(a, b):
    return jnp.dot(a, b, preferred_element_type=f32)


def _dot_nt(a, b):
    return lax.dot_general(a, b, (((1,), (1,)), ((), ())), preferred_element_type=f32)


def _trig_kernel(freq_ref, pos_ref, cos_ref, sin_ref):
    ang = pos_ref[...].astype(f32) * freq_ref[pl.program_id(0)]
    cos_ref[0] = jnp.cos(ang)
    sin_ref[0] = jnp.sin(ang)


def _rope_select():
    half = ROPE_DIM // 2
    sel = np.zeros((LANES, 3 * LANES), np.float32)
    for lane in range(LANES):
        m = lane % ATT_HEAD_DIM
        if m < half:
            sel[m, lane] = 1.0
            sel[half + m, LANES + lane] = -1.0
        elif m < ROPE_DIM:
            sel[m - half, lane] = 1.0
            sel[m, 2 * LANES + lane] = 1.0
        else:
            sel[ROPE_DIM, lane] = 1.0
    return jnp.asarray(sel, bf16)


def _rope_tables(positions):
    t = positions.size
    half = ROPE_DIM // 2
    inv_freq = ROPE_THETA ** (-jnp.arange(0, ROPE_DIM, 2, dtype=f32) / ROPE_DIM)
    pos2d = positions.reshape(t // LANES, LANES)
    cos_t, sin_t = pl.pallas_call(
        _trig_kernel,
        grid_spec=pltpu.PrefetchScalarGridSpec(
            num_scalar_prefetch=1,
            grid=(half,),
            in_specs=[pl.BlockSpec((t // LANES, LANES), lambda j, f: (0, 0))],
            out_specs=[pl.BlockSpec((1, t // LANES, LANES), lambda j, f: (j, 0, 0))] * 2,
        ),
        out_shape=[jax.ShapeDtypeStruct((half, t // LANES, LANES), f32)] * 2,
        name="trig",
    )(inv_freq, pos2d)
    return cos_t.reshape(half, t), sin_t.reshape(half, t)


INPROJ_TM = 512
INPROJ_CH = 512


def _inproj_kernel(x_ref, g_ref, w_ref,
                   z_ref, xbc_ref, q_ref, k_ref, v_ref, gs_ref, ga_ref, dt_ref, h_scr):
    x = x_ref[...]
    ms = jnp.mean(x * x, axis=-1, keepdims=True)
    h_scr[...] = (x * lax.rsqrt(ms + RMS_EPS) * g_ref[...]).astype(bf16)
    off = 0
    for ref in (z_ref, xbc_ref, q_ref, k_ref, v_ref, gs_ref, ga_ref, dt_ref):
        width = ref.shape[1]
        for c in range(0, width, INPROJ_CH):
            cw = min(INPROJ_CH, width - c)
            ref[:, c:c + cw] = _dot(h_scr[...], w_ref[:, off + c:off + c + cw]).astype(ref.dtype)
        off += width


WPREP_ROWS = 128


def _wprep_kernel(w_ref, o_ref):
    s1 = COL_Z + COL_XBC
    s2 = s1 + COL_DT
    n_tail = w_ref.shape[1] - s2
    o_ref[:, 0:s1] = w_ref[:, 0:s1].astype(bf16)
    o_ref[:, s1:s1 + n_tail] = w_ref[:, s2:s2 + n_tail].astype(bf16)
    lane = lax.broadcasted_iota(jnp.int32, (w_ref.shape[0], LANES), 1)
    o_ref[:, s1 + n_tail:s1 + n_tail + LANES] = jnp.where(lane < COL_DT, w_ref[:, s1:s1 + LANES], 0.0).astype(bf16)


def _inproj(x2, norm1_g, w_in):
    t = x2.shape[0]
    tm = min(INPROJ_TM, t)
    n_in = w_in.shape[1]
    n_all = n_in - COL_DT + LANES
    w_all = pl.pallas_call(
        _wprep_kernel,
        grid=(D_MODEL // WPREP_ROWS,),
        in_specs=[pl.BlockSpec((WPREP_ROWS, n_in), lambda i: (i, 0))],
        out_specs=pl.BlockSpec((WPREP_ROWS, n_all), lambda i: (i, 0)),
        out_shape=jax.ShapeDtypeStruct((D_MODEL, n_all), bf16),
        name="wprep",
    )(w_in)
    widths = (COL_Z, COL_XBC, COL_Q, COL_KV, COL_KV, D_MODEL, D_MODEL)
    const = lambda i: (0, 0)
    row = lambda i: (i, 0)
    outs = pl.pallas_call(
        _inproj_kernel,
        grid=(t // tm,),
        in_specs=[pl.BlockSpec((tm, D_MODEL), row),
                  pl.BlockSpec((1, D_MODEL), const),
                  pl.BlockSpec((D_MODEL, w_all.shape[1]), const, pipeline_mode=pl.Buffered(1))],
        out_specs=[pl.BlockSpec((tm, w), row) for w in widths] + [pl.BlockSpec((tm, LANES), row)],
        out_shape=[jax.ShapeDtypeStruct((t, w), bf16) for w in widths]
                  + [jax.ShapeDtypeStruct((t, LANES), f32)],
        scratch_shapes=[pltpu.VMEM((tm, D_MODEL), bf16)],
        compiler_params=pltpu.CompilerParams(dimension_semantics=("arbitrary",),
                                             vmem_limit_bytes=VMEM_LIMIT),
        name="inproj",
    )(x2, norm1_g.reshape(1, D_MODEL), w_all)
    return outs


def _ssd_kernel(xbc_ref, z_ref, dt_ref, gs_ref, cw_ref, cb_ref, dtb_ref, alog_ref, dexp_ref, ng_ref,
                wout_ref, out_ref, ext_ref, st_ref, xs_ref, bm_ref, cm_ref, y_ref, hn_ref):
    L = SSD_CHUNK
    c = pl.program_id(1)

    @pl.when(c == 0)
    def _():
        ext_ref[0:CONV_HALO, :] = jnp.zeros((CONV_HALO, SSD_CONV_DIM), f32)
        st_ref[...] = jnp.zeros(st_ref.shape, f32)

    bf_tile = 2 * CONV_HALO
    ext_ref[CONV_HALO:2 * CONV_HALO, :] = xbc_ref[0:bf_tile, :].astype(f32)[0:CONV_HALO]
    n_sh = SSD_CONV - 1
    sr = lax.broadcasted_iota(jnp.int32, (n_sh * L, L), 0)
    sc = lax.broadcasted_iota(jnp.int32, (n_sh * L, L), 1)
    shift = jnp.where((sr % L) - sc == (sr // L) + 1, 1.0, 0.0).astype(bf16)
    cch = 128
    for cc in range(0, SSD_CONV_DIM, cch):
        cs_ = slice(cc, cc + cch)
        xb = xbc_ref[:, cs_]
        sh = _dot(shift, xb)
        w_now = cw_ref[SSD_CONV - 1:SSD_CONV, cs_]
        acc = cb_ref[:, cs_] + xb.astype(f32) * w_now
        top = cb_ref[:, cs_] + ext_ref[CONV_HALO:2 * CONV_HALO, cs_] * w_now
        for j in range(1, SSD_CONV):
            w_j = cw_ref[SSD_CONV - 1 - j:SSD_CONV - j, cs_]
            acc = acc + sh[(j - 1) * L:j * L] * w_j
            top = top + ext_ref[CONV_HALO - j:2 * CONV_HALO - j, cs_] * w_j
        if cc < SSD_D_INNER:
            dst, o = xs_ref, cc
        elif cc < SSD_D_INNER + SSD_N_GROUPS * SSD_D_STATE:
            dst, o = bm_ref, cc - SSD_D_INNER
        else:
            dst, o = cm_ref, cc - SSD_D_INNER - SSD_N_GROUPS * SSD_D_STATE
        dst[:, o:o + cch] = _silu(acc)
        dst[0:CONV_HALO, o:o + cch] = _silu(top)
    ext_ref[0:CONV_HALO, :] = xbc_ref[L - bf_tile:L, :].astype(f32)[CONV_HALO:bf_tile]

    lane_row = lax.broadcasted_iota(jnp.int32, (1, LANES), 1)
    row_i = lax.broadcasted_iota(jnp.int32, (L, L), 0)
    col_i = lax.broadcasted_iota(jnp.int32, (L, L), 1)
    causal = row_i >= col_i
    left = col_i < SSD_HEAD_DIM

    xdt = dt_ref[...] + dtb_ref[...]
    dtv = jnp.maximum(xdt, 0.0) + jnp.log1p(jnp.exp(-jnp.abs(xdt)))
    a = jnp.where(lane_row < SSD_N_HEADS, -jnp.exp(alog_ref[...]), 0.0)
    d_a = dtv * a
    tril = jnp.where(causal, 1.0, 0.0).astype(bf16)
    hi, mid, lo3 = _split3(d_a)
    a_cum = _dot(tril, hi) + _dot(tril, mid) + _dot(tril, lo3)
    a_end = a_cum[L - 1:L, :]
    exp_a = jnp.exp(a_cum)
    w_end = jnp.exp(a_end - a_cum) * dtv
    cd = jnp.exp(a_end)
    a_t = a_cum.T
    dt_t = dtv.T
    w_t = w_end.T

    n_pairs = SSD_N_HEADS // 2
    pairs_per_group = n_pairs // SSD_N_GROUPS
    for g in range(SSD_N_GROUPS):
        b_g = bm_ref[:, g * SSD_D_STATE:(g + 1) * SSD_D_STATE]
        c_g = cm_ref[:, g * SSD_D_STATE:(g + 1) * SSD_D_STATE]
        cb = _dot_nt(c_g.astype(bf16), b_g.astype(bf16))
        b_t = b_g.T
        for pi in range(pairs_per_group):
            i = g * pairs_per_group + pi
            xpair = xs_ref[:, i * LANES:(i + 1) * LANES]
            xpair_b = xpair.astype(bf16)
            s_prev = st_ref[i]
            rhs = jnp.concatenate([xpair_b, s_prev.astype(bf16)], axis=0)
            ys = []
            sn = []
            for h in (2 * i, 2 * i + 1):
                acol = jnp.broadcast_to(a_cum[:, h:h + 1], (L, L))
                arow = jnp.broadcast_to(a_t[h:h + 1, :], (L, L))
                dtrow = jnp.broadcast_to(dt_t[h:h + 1, :], (L, L))
                dec = jnp.exp(jnp.where(causal, acol - arow, NEG_BIG))
                m = cb * dec * dtrow
                cs = c_g * jnp.broadcast_to(exp_a[:, h:h + 1], (L, L))
                lhs = jnp.concatenate([m.astype(bf16), cs.astype(bf16)], axis=1)
                ys.append(_dot(lhs, rhs))
                btw = (b_t * jnp.broadcast_to(w_t[h:h + 1, :], (L, L))).astype(bf16)
                sn.append(_dot(btw, xpair_b))
            h0 = 2 * i
            cd_pair = jnp.where(lane_row < SSD_HEAD_DIM, cd[:, h0:h0 + 1], cd[:, h0 + 1:h0 + 2])
            st_ref[i] = jnp.where(left, sn[0], sn[1]) + s_prev * cd_pair
            y_pair = jnp.where(left, ys[0], ys[1])
            y_ref[:, i * LANES:(i + 1) * LANES] = y_pair + xpair * dexp_ref[:, i * LANES:(i + 1) * LANES]

    gw = SSD_D_INNER // SSD_N_GROUPS
    for g in range(SSD_N_GROUPS):
        sl = slice(g * gw, (g + 1) * gw)
        yz = y_ref[:, sl] * _silu(z_ref[:, sl].astype(f32))
        ms = jnp.mean(yz * yz, axis=-1, keepdims=True)
        hn_ref[:, sl] = (yz * lax.rsqrt(ms + RMS_EPS) * ng_ref[:, sl]).astype(bf16)
    y_ssd = _dot(hn_ref[...], wout_ref[...])
    out_ref[...] = _sigmoid(gs_ref[...].astype(f32)) * y_ssd


def _ssd(xbc, z, dt, gs, conv_w, conv_b, dt_bias, a_log, d_skip, ssd_norm_g, w_ssd_out, batch, seq):
    t = batch * seq
    L = SSD_CHUNK
    nc = seq // L
    pad_h = LANES - SSD_N_HEADS
    dtb = jnp.pad(dt_bias, (0, pad_h)).reshape(1, LANES)
    alog = jnp.pad(a_log, (0, pad_h)).reshape(1, LANES)
    dexp = jnp.repeat(d_skip, SSD_HEAD_DIM).reshape(1, SSD_D_INNER)
    const = lambda b, c: (0, 0)
    row = lambda b, c: (b * nc + c, 0)
    return pl.pallas_call(
        _ssd_kernel,
        grid=(batch, nc),
        in_specs=[pl.BlockSpec((L, SSD_CONV_DIM), row),
                  pl.BlockSpec((L, SSD_D_INNER), row),
                  pl.BlockSpec((L, LANES), row),
                  pl.BlockSpec((L, D_MODEL), row),
                  pl.BlockSpec((SSD_CONV, SSD_CONV_DIM), const),
                  pl.BlockSpec((1, SSD_CONV_DIM), const),
                  pl.BlockSpec((1, LANES), const),
                  pl.BlockSpec((1, LANES), const),
                  pl.BlockSpec((1, SSD_D_INNER), const),
                  pl.BlockSpec((1, SSD_D_INNER), const),
                  pl.BlockSpec((SSD_D_INNER, D_MODEL), const, pipeline_mode=pl.Buffered(1))],
        out_specs=pl.BlockSpec((L, D_MODEL), row),
        out_shape=jax.ShapeDtypeStruct((t, D_MODEL), f32),
        scratch_shapes=[pltpu.VMEM((2 * CONV_HALO, SSD_CONV_DIM), f32),
                        pltpu.VMEM((SSD_N_HEADS // 2, SSD_D_STATE, LANES), f32),
                        pltpu.VMEM((L, SSD_D_INNER), f32),
                        pltpu.VMEM((L, SSD_N_GROUPS * SSD_D_STATE), f32),
                        pltpu.VMEM((L, SSD_N_GROUPS * SSD_D_STATE), f32),
                        pltpu.VMEM((L, SSD_D_INNER), f32),
                        pltpu.VMEM((L, SSD_D_INNER), bf16)],
        compiler_params=pltpu.CompilerParams(dimension_semantics=("arbitrary", "arbitrary"),
                                             vmem_limit_bytes=VMEM_LIMIT),
        name="ssd",
    )(xbc, z, dt, gs, conv_w, conv_b.reshape(1, -1), dtb, alog, dexp,
      ssd_norm_g.reshape(1, -1), w_ssd_out.astype(bf16))


ROUTE_E, ROUTE_RANK, ROUTE_GATE = 0, 2, 4
ATT_SUB = 2


def _attn_kernel(sink_ref, q_ref, k_ref, v_ref, cos_ref, sin_ref, sel_ref, ga_ref, ms_ref, x_ref,
                 qg_ref, kg_ref, wao_ref, wo_ref, n2g_ref, wrh_ref, wrl_ref, br_ref,
                 x1_ref, h2_ref, route_ref, route_t_ref, cnt_ref,
                 kprev_ref, vprev_ref, att_ref, cnt_scr, s_ref, p_ref, vd_ref):
    Q = ATT_BLOCK
    b = pl.program_id(0)
    n = pl.program_id(1)

    @pl.when(n == 0)
    def _():
        kprev_ref[...] = jnp.zeros(kprev_ref.shape, f32)
        vprev_ref[...] = jnp.zeros(vprev_ref.shape, f32)

    @pl.when((b == 0) & (n == 0))
    def _():
        cnt_scr[...] = jnp.zeros(cnt_scr.shape, f32)

    R = q_ref.shape[0]
    lane_q = lax.broadcasted_iota(jnp.int32, (Q, LANES), 1)
    row_q = lax.broadcasted_iota(jnp.int32, (Q, LANES), 0)
    left = lane_q < ATT_HEAD_DIM
    head_mean = jnp.where((row_q // ATT_HEAD_DIM) == (lane_q // ATT_HEAD_DIM),
                          1.0 / ATT_HEAD_DIM, 0.0).astype(bf16)
    lane2 = lax.broadcasted_iota(jnp.int32, (2 * Q, LANES), 1) < ATT_HEAD_DIM
    qg = ATT_N_HEADS // ATT_N_KV
    rows = qg * Q
    ri = lax.broadcasted_iota(jnp.int32, (rows, Q), 0) % Q
    cj = lax.broadcasted_iota(jnp.int32, (rows, Q), 1)
    upper = cj > ri
    half = ROPE_DIM // 2

    n_sb = R // Q
    nq = COL_Q // LANES
    nk = COL_KV // LANES
    chunks = []
    for sb in range(n_sb):
        rs = slice(sb * Q, (sb + 1) * Q)
        chunks += [q_ref[rs, c * LANES:(c + 1) * LANES].astype(f32) for c in range(nq)]
        chunks += [k_ref[rs, c * LANES:(c + 1) * LANES].astype(f32) for c in range(nk)]
    u_all = jnp.concatenate(chunks, axis=0)
    sq = u_all * u_all
    sq_hi = sq.astype(bf16)
    sq_lo = (sq - sq_hi.astype(f32)).astype(bf16)
    un_all = u_all * lax.rsqrt(_dot(sq_hi, head_mean) + _dot(sq_lo, head_mean) + RMS_EPS)

    terms = []
    for sb in range(n_sb):
        rs = slice(sb * Q, (sb + 1) * Q)
        cs = jnp.concatenate([cos_ref[:, rs], sin_ref[:, rs], jnp.ones((half, LANES), f32),
                              jnp.zeros((LANES - 3 * half, LANES), f32)], axis=0)
        terms += list(_split3(cs.T))
    pat_all = _dot(jnp.concatenate(terms, axis=0), sel_ref[...])

    for sb in range(n_sb):
        rs = slice(sb * Q, (sb + 1) * Q)
        pat = pat_all[3 * sb * Q:(3 * sb + 1) * Q] + pat_all[(3 * sb + 1) * Q:(3 * sb + 2) * Q] \
            + pat_all[(3 * sb + 2) * Q:(3 * sb + 3) * Q]
        cpat = pat[:, 0:LANES]
        s1pat = pat[:, LANES:2 * LANES]
        s2pat = pat[:, 2 * LANES:3 * LANES]

        def norm_rope(idx, gpat):
            tn = un_all[idx * Q:(idx + 1) * Q] * gpat
            return (tn * cpat + pltpu.roll(tn, LANES - ROPE_DIM // 2, 1) * s1pat
                    + pltpu.roll(tn, ROPE_DIM // 2, 1) * s2pat)

        kd = []
        for cidx in range(nk):
            sl = slice(cidx * LANES, (cidx + 1) * LANES)
            k_cur = norm_rope(sb * (nq + nk) + nq + cidx, kg_ref[...])
            v_cur = v_ref[rs, sl].astype(f32)
            k_all = jnp.concatenate([kprev_ref[:, sl], k_cur], axis=0)
            v_all = jnp.concatenate([vprev_ref[:, sl], v_cur], axis=0)
            kprev_ref[:, sl] = k_cur
            vprev_ref[:, sl] = v_cur
            k_sw = pltpu.roll(k_all, ATT_HEAD_DIM, 1)
            v_sw = pltpu.roll(v_all, ATT_HEAD_DIM, 1)
            kd.append(jnp.where(lane2, k_all, k_sw).astype(bf16))
            kd.append(jnp.where(lane2, k_sw, k_all).astype(bf16))
            for half_i, vv in enumerate((jnp.where(lane2, v_all, v_sw), jnp.where(lane2, v_sw, v_all))):
                vd_ref[sb * ATT_N_KV + 2 * cidx + half_i] = vv.astype(bf16)

        prev_live = upper & (n > 0) if sb == 0 else upper
        for h in range(ATT_N_KV):
            parts = []
            for cidx in (2 * h, 2 * h + 1):
                qc = norm_rope(sb * (nq + nk) + cidx, qg_ref[...]) * ATT_SCALE
                parts.append(jnp.where(left, qc, 0.0).astype(bf16))
                parts.append(jnp.where(left, 0.0, qc).astype(bf16))
            lhs = jnp.concatenate(parts, axis=0)
            g = sb * ATT_N_KV + h
            s_both = _dot_nt(lhs, kd[h])
            s_prev = s_both[:, 0:Q]
            s_cur = s_both[:, Q:2 * Q]
            if sb == 0:
                s_cur = jnp.where(upper, NEG_BIG, s_cur)
            s_ref[g * rows:(g + 1) * rows, :] = jnp.where(prev_live, s_prev, s_cur)

    n_grp = (R // Q) * ATT_N_KV
    sink = jnp.concatenate([jnp.full((Q, 1), sink_ref[(i // qg) % ATT_N_KV * qg + i % qg], f32)
                            for i in range(n_grp * qg)], axis=0)
    s = s_ref[...]
    m = jnp.maximum(jnp.max(s, axis=-1, keepdims=True), sink)
    p = jnp.exp(s - m)
    ones = jnp.ones((Q, Q), bf16)
    p_hi = p.astype(bf16)
    p_lo = (p - p_hi.astype(f32)).astype(bf16)
    denom = _dot(p_hi, ones) + _dot(p_lo, ones) + jnp.exp(sink - m)
    p_ref[...] = (p / denom).astype(bf16)

    for sb in range(R // Q):
        rs = slice(sb * Q, (sb + 1) * Q)
        for h in range(ATT_N_KV):
            g = sb * ATT_N_KV + h
            pf = p_ref[g * rows:(g + 1) * rows, :]
            zero = jnp.zeros_like(pf)
            p_both = jnp.concatenate([jnp.where(upper, pf, zero), jnp.where(upper, zero, pf)], axis=1)
            o = _dot(p_both, vd_ref[g])
            for r in range(2):
                cidx = 2 * h + r
                att_ref[rs, cidx * LANES:(cidx + 1) * LANES] = jnp.where(
                    left, o[(2 * r) * Q:(2 * r + 1) * Q], o[(2 * r + 1) * Q:(2 * r + 2) * Q]).astype(bf16)

    lane = lax.broadcasted_iota(jnp.int32, (R, LANES), 1)
    y_att = _dot(att_ref[...], wao_ref[...])
    merged = _sigmoid(ga_ref[...].astype(f32)) * y_att + ms_ref[...]
    x1 = x_ref[...] + _dot(merged.astype(bf16), wo_ref[...])
    x1_ref[...] = x1
    h2 = x1 * lax.rsqrt(jnp.mean(x1 * x1, axis=-1, keepdims=True) + RMS_EPS) * n2g_ref[...]
    h2_ref[...] = _pack_rows(h2)

    hi = h2.astype(bf16)
    lo = (h2 - hi.astype(f32)).astype(bf16)
    logits = _dot(hi, wrh_ref[...]) + _dot(lo, wrh_ref[...]) + _dot(hi, wrl_ref[...]) + br_ref[...]
    big = 4 * LANES
    gl = jnp.where(lane < MOE_N_GROUPS, logits, NEG_BIG)
    gmax = jnp.max(gl, axis=-1, keepdims=True)
    gsel = jnp.min(jnp.where(gl == gmax, lane, big), axis=-1, keepdims=True)
    pg = 1.0 / jnp.sum(jnp.exp(gl - gmax), axis=-1, keepdims=True)
    lo_l = MOE_N_GROUPS + MOE_EPG * gsel
    el = jnp.where((lane >= lo_l) & (lane < lo_l + MOE_EPG), logits, NEG_BIG)
    v1 = jnp.max(el, axis=-1, keepdims=True)
    i1 = jnp.min(jnp.where(el == v1, lane, big), axis=-1, keepdims=True)
    el2 = jnp.where(lane == i1, NEG_BIG, el)
    v2 = jnp.max(el2, axis=-1, keepdims=True)
    i2 = jnp.min(jnp.where(el2 == v2, lane, big), axis=-1, keepdims=True)
    e1 = i1 - MOE_N_GROUPS
    e2 = i2 - MOE_N_GROUPS
    tt = jnp.exp(v2 - v1)
    w1 = pg * (1.0 / (1.0 + tt))
    w2 = pg * (tt / (1.0 + tt))

    onehot = jnp.where((lane == e1) | (lane == e2), 1.0, 0.0)
    strict = jnp.where(lax.broadcasted_iota(jnp.int32, (R, R), 0) > lax.broadcasted_iota(jnp.int32, (R, R), 1),
                       1.0, 0.0).astype(bf16)
    base = _dot(strict, onehot.astype(bf16)) + cnt_scr[0:1, :]
    r1 = jnp.sum(jnp.where(lane == e1, base, 0.0), axis=-1, keepdims=True)
    r2 = jnp.sum(jnp.where(lane == e2, base, 0.0), axis=-1, keepdims=True)
    cnt_new = cnt_scr[0:1, :] + jnp.sum(onehot, axis=0, keepdims=True)
    cnt_scr[...] = jnp.broadcast_to(cnt_new, cnt_scr.shape)
    cnt_ref[...] = jnp.broadcast_to(cnt_new, cnt_ref.shape)

    rec = jnp.zeros((R, LANES), f32)
    for off, val in ((ROUTE_E, e1.astype(f32)), (ROUTE_E + 1, e2.astype(f32)),
                     (ROUTE_RANK, r1), (ROUTE_RANK + 1, r2), (ROUTE_GATE, w1), (ROUTE_GATE + 1, w2)):
        rec = jnp.where(lane == off, val, rec)
    route_ref[...] = rec
    for sb in range(R // Q):
        route_t_ref[:, sb * Q:(sb + 1) * Q] = rec[sb * Q:(sb + 1) * Q].T[0:8, :]


def _attn(q, k, v, cos_t, sin_t, ga, merged_ssd, x2, q_norm_g, k_norm_g, sinks, w_attn_out, w_out,
          norm2_g, w_rg, b_rg, w_re, b_re, batch, seq):
    t = batch * seq
    Q = ATT_SUB * ATT_BLOCK
    nb = seq // Q
    rep = LANES // ATT_HEAD_DIM
    qg = jnp.tile(q_norm_g, rep).reshape(1, LANES)
    kg = jnp.tile(k_norm_g, rep).reshape(1, LANES)
    n_log = MOE_N_GROUPS + MOE_N_EXPERTS
    w_r = jnp.concatenate([w_rg, jnp.transpose(w_re, (1, 0, 2)).reshape(D_MODEL, MOE_N_EXPERTS),
                           jnp.zeros((D_MODEL, LANES - n_log), f32)], axis=1)
    b_r = jnp.concatenate([b_rg, b_re.reshape(-1), jnp.zeros((LANES - n_log,), f32)]).reshape(1, LANES)
    w_r_hi = w_r.astype(bf16)
    w_r_lo = (w_r - w_r_hi.astype(f32)).astype(bf16)
    const = lambda b, n, *_: (0, 0)
    row = lambda b, n, *_: (b * nb + n, 0)
    col = lambda b, n, *_: (0, b * nb + n)
    full = lambda shape: pl.BlockSpec(shape, const, pipeline_mode=pl.Buffered(1))
    return pl.pallas_call(
        _attn_kernel,
        grid_spec=pltpu.PrefetchScalarGridSpec(
            num_scalar_prefetch=1,
            grid=(batch, nb),
            in_specs=[pl.BlockSpec((Q, COL_Q), row),
                      pl.BlockSpec((Q, COL_KV), row),
                      pl.BlockSpec((Q, COL_KV), row),
                      pl.BlockSpec((ROPE_DIM // 2, Q), col),
                      pl.BlockSpec((ROPE_DIM // 2, Q), col),
                      full((LANES, 3 * LANES)),
                      pl.BlockSpec((Q, D_MODEL), row),
                      pl.BlockSpec((Q, D_MODEL), row),
                      pl.BlockSpec((Q, D_MODEL), row),
                      pl.BlockSpec((1, LANES), const),
                      pl.BlockSpec((1, LANES), const),
                      full((COL_Q, D_MODEL)),
                      full((D_MODEL, D_MODEL)),
                      pl.BlockSpec((1, D_MODEL), const),
                      full((D_MODEL, LANES)),
                      full((D_MODEL, LANES)),
                      pl.BlockSpec((1, LANES), const)],
            out_specs=[pl.BlockSpec((Q, D_MODEL), row),
                       pl.BlockSpec((Q, HALF_D), row),
                       pl.BlockSpec((Q, LANES), row),
                       pl.BlockSpec((8, Q), col),
                       pl.BlockSpec((8, LANES), const)],
            scratch_shapes=[pltpu.VMEM((ATT_BLOCK, COL_KV), f32),
                            pltpu.VMEM((ATT_BLOCK, COL_KV), f32),
                            pltpu.VMEM((Q, COL_Q), bf16),
                            pltpu.VMEM((8, LANES), f32),
                            pltpu.VMEM((ATT_SUB * ATT_N_HEADS * ATT_BLOCK, ATT_BLOCK), f32),
                            pltpu.VMEM((ATT_SUB * ATT_N_HEADS * ATT_BLOCK, ATT_BLOCK), bf16),
                            pltpu.VMEM((ATT_SUB * ATT_N_KV, 2 * ATT_BLOCK, LANES), bf16)],
        ),
        out_shape=[jax.ShapeDtypeStruct((t, D_MODEL), f32),
                   jax.ShapeDtypeStruct((t, HALF_D), jnp.uint32),
                   jax.ShapeDtypeStruct((t, LANES), f32),
                   jax.ShapeDtypeStruct((8, t), f32),
                   jax.ShapeDtypeStruct((8, LANES), f32)],
        compiler_params=pltpu.CompilerParams(dimension_semantics=("arbitrary", "arbitrary"),
                                             vmem_limit_bytes=VMEM_LIMIT),
        name="attn",
    )(sinks, q, k, v, cos_t, sin_t, _rope_select(), ga, merged_ssd, x2, qg, kg,
      w_attn_out.astype(bf16), w_out.astype(bf16), norm2_g.reshape(1, D_MODEL), w_r_hi, w_r_lo, b_r)


MOE_TM = 256
MOE_UNROLL = 8


def _row_copy(src, i, dst, j, sem):
    return pltpu.make_async_copy(src.at[pl.ds(i, 1), :], dst.at[pl.ds(j, 1), :], sem)


def _dispatch_kernel(dest0_ref, dest1_ref, zblk_ref, h2_ref, xs_hbm, zero_ref, sem, zsem):
    tm = h2_ref.shape[0]
    base = pl.program_id(0) * tm

    @pl.when(pl.program_id(0) == 0)
    def _():
        zero_ref[...] = jnp.zeros(zero_ref.shape, zero_ref.dtype)

        def zcopy(i):
            start = pl.multiple_of(zblk_ref[i] * MOE_BLOCK, MOE_BLOCK)
            return pltpu.make_async_copy(zero_ref, xs_hbm.at[pl.ds(start, MOE_BLOCK), :], zsem)

        def zstart(i, carry):
            @pl.when(zblk_ref[i] >= 0)
            def _():
                zcopy(i).start()
            return carry

        def zwait(i, carry):
            @pl.when(zblk_ref[i] >= 0)
            def _():
                zcopy(i).wait()
            return carry

        lax.fori_loop(0, zblk_ref.shape[0], zstart, 0)
        lax.fori_loop(0, zblk_ref.shape[0], zwait, 0)

    def issue(i, carry):
        _row_copy(h2_ref, i, xs_hbm, dest0_ref[base + i], sem).start()
        _row_copy(h2_ref, i, xs_hbm, dest1_ref[base + i], sem).start()
        return carry

    lax.fori_loop(0, tm, issue, 0, unroll=MOE_UNROLL)
    for _ in range(2):
        pltpu.make_async_copy(h2_ref, xs_hbm.at[pl.ds(0, tm), :], sem).wait()


def _dispatch(h2, dest0, dest1, zero_blocks, n_rows):
    t = h2.shape[0]
    tm = min(MOE_TM, t)
    return pl.pallas_call(
        _dispatch_kernel,
        grid_spec=pltpu.PrefetchScalarGridSpec(
            num_scalar_prefetch=3,
            grid=(t // tm,),
            in_specs=[pl.BlockSpec((tm, HALF_D), lambda i, d0, d1, zb: (i, 0))],
            out_specs=pl.BlockSpec(memory_space=pl.ANY),
            scratch_shapes=[pltpu.VMEM((MOE_BLOCK, HALF_D), h2.dtype),
                            pltpu.SemaphoreType.DMA(()),
                            pltpu.SemaphoreType.DMA(())],
        ),
        out_shape=jax.ShapeDtypeStruct((n_rows, HALF_D), h2.dtype),
        compiler_params=pltpu.CompilerParams(dimension_semantics=("arbitrary",)),
        name="dispatch",
    )(dest0, dest1, zero_blocks, h2)


EXPERT_SUB = 2


def _expert_kernel(be_ref, nu_ref, xs_ref, *refs):
    w_refs = refs[:3 * EXPERT_SUB]
    ys_ref, wgu_scr, wd_scr = refs[3 * EXPERT_SUB:]
    i = pl.program_id(0)
    rows = MOE_BLOCK

    for j in range(EXPERT_SUB):
        blk = EXPERT_SUB * i + j
        fresh = (i == 0) | (be_ref[blk] != be_ref[jnp.maximum(blk - EXPERT_SUB, 0)])

        @pl.when((blk < nu_ref[0]) & fresh)
        def _(j=j):
            wg_ref, wu_ref, wd_ref = w_refs[3 * j:3 * j + 3]
            wgu_scr[j, :, 0:MOE_D_FF] = wg_ref[0].astype(bf16)
            wgu_scr[j, :, MOE_D_FF:2 * MOE_D_FF] = wu_ref[0].astype(bf16)
            wd_scr[j] = wd_ref[0].astype(bf16)

    def block(j):
        rs = slice(j * rows, (j + 1) * rows)
        x_lo, x_hi = _unpack_rows(xs_ref[rs, :])
        gu = (_dot(x_lo.astype(bf16), wgu_scr[j, 0:HALF_D, :])
              + _dot(x_hi.astype(bf16), wgu_scr[j, HALF_D:D_MODEL, :]))
        hid = _silu(gu[:, :MOE_D_FF]) * gu[:, MOE_D_FF:]
        ys_ref[rs, :] = _pack_rows(_dot(hid.astype(bf16), wd_scr[j]))

    n_live = jnp.clip(nu_ref[0] - EXPERT_SUB * i, 0, EXPERT_SUB)
    for live in range(EXPERT_SUB + 1):
        @pl.when(n_live == live)
        def _(live=live):
            for j in range(live):
                block(j)
            if live < EXPERT_SUB:
                ys_ref[live * rows:, :] = jnp.zeros(((EXPERT_SUB - live) * rows, HALF_D), ys_ref.dtype)


def _experts(xs, block_e, n_used, w_g, w_u, w_d):
    n_rows = xs.shape[0]
    rows = EXPERT_SUB * MOE_BLOCK
    nstep = n_rows // rows
    blk_in = lambda i, be, nu: (jnp.minimum(i, (nu[0] - 1) // EXPERT_SUB), 0)
    blk_out = lambda i, be, nu: (i, 0)
    w_specs = []
    for j in range(EXPERT_SUB):
        wsel = lambda i, be, nu, j=j: (be[EXPERT_SUB * i + j], 0, 0)
        w_specs += [pl.BlockSpec((1, D_MODEL, MOE_D_FF), wsel),
                    pl.BlockSpec((1, D_MODEL, MOE_D_FF), wsel),
                    pl.BlockSpec((1, MOE_D_FF, D_MODEL), wsel)]
    return pl.pallas_call(
        _expert_kernel,
        grid_spec=pltpu.PrefetchScalarGridSpec(
            num_scalar_prefetch=2,
            grid=(nstep,),
            in_specs=[pl.BlockSpec((rows, HALF_D), blk_in)] + w_specs,
            out_specs=pl.BlockSpec((rows, HALF_D), blk_out),
            scratch_shapes=[pltpu.VMEM((EXPERT_SUB, D_MODEL, 2 * MOE_D_FF), bf16),
                            pltpu.VMEM((EXPERT_SUB, MOE_D_FF, D_MODEL), bf16)],
        ),
        out_shape=jax.ShapeDtypeStruct((n_rows, HALF_D), xs.dtype),
        compiler_params=pltpu.CompilerParams(dimension_semantics=("arbitrary",),
                                             vmem_limit_bytes=VMEM_LIMIT),
        name="experts",
    )(block_e, n_used, xs, *([w_g, w_u, w_d] * EXPERT_SUB))


def _combine_kernel(dest0_ref, dest1_ref, x1_ref, route_ref, ys_hbm, out_ref, y0_ref, y1_ref, sem):
    tm = x1_ref.shape[0]
    step = pl.program_id(0)

    def gather_tile(tile, slot):
        base = tile * tm

        def issue(r, carry):
            pltpu.make_async_copy(ys_hbm.at[pl.ds(dest0_ref[base + r], 1), :],
                                  y0_ref.at[slot, pl.ds(r, 1), :], sem.at[slot]).start()
            pltpu.make_async_copy(ys_hbm.at[pl.ds(dest1_ref[base + r], 1), :],
                                  y1_ref.at[slot, pl.ds(r, 1), :], sem.at[slot]).start()
            return carry

        lax.fori_loop(0, tm, issue, 0, unroll=MOE_UNROLL)

    @pl.when(step == 0)
    def _():
        gather_tile(0, 0)

    @pl.when(step + 1 < pl.num_programs(0))
    def _():
        gather_tile(step + 1, (step + 1) % 2)

    slot = step % 2
    for y_ref in (y0_ref, y1_ref):
        pltpu.make_async_copy(ys_hbm.at[pl.ds(0, tm), :], y_ref.at[slot], sem.at[slot]).wait()
    g0 = route_ref[:, ROUTE_GATE:ROUTE_GATE + 1]
    g1 = route_ref[:, ROUTE_GATE + 1:ROUTE_GATE + 2]
    y0_lo, y0_hi = _unpack_rows(y0_ref[slot])
    y1_lo, y1_hi = _unpack_rows(y1_ref[slot])
    out_ref[:, 0:HALF_D] = x1_ref[:, 0:HALF_D] + y0_lo * g0 + y1_lo * g1
    out_ref[:, HALF_D:D_MODEL] = x1_ref[:, HALF_D:D_MODEL] + y0_hi * g0 + y1_hi * g1


def _combine(x1, route, ys, dest0, dest1):
    t = x1.shape[0]
    tm = min(MOE_TM, t)
    row = lambda i, d0, d1: (i, 0)
    return pl.pallas_call(
        _combine_kernel,
        grid_spec=pltpu.PrefetchScalarGridSpec(
            num_scalar_prefetch=2,
            grid=(t // tm,),
            in_specs=[pl.BlockSpec((tm, D_MODEL), row),
                      pl.BlockSpec((tm, LANES), row),
                      pl.BlockSpec(memory_space=pl.ANY)],
            out_specs=pl.BlockSpec((tm, D_MODEL), row),
            scratch_shapes=[pltpu.VMEM((2, tm, HALF_D), ys.dtype),
                            pltpu.VMEM((2, tm, HALF_D), ys.dtype),
                            pltpu.SemaphoreType.DMA((2,))],
        ),
        out_shape=jax.ShapeDtypeStruct((t, D_MODEL), f32),
        compiler_params=pltpu.CompilerParams(dimension_semantics=("arbitrary",)),
        name="combine",
    )(dest0, dest1, x1, route, ys)


def _moe(x1, h2, route, route_t, counts_rec, w_gate_e, w_up_e, w_down_e):
    t = x1.shape[0]
    n_rows = 2 * t + MOE_N_EXPERTS * MOE_BLOCK
    nblk = n_rows // MOE_BLOCK
    counts = counts_rec[0, :MOE_N_EXPERTS].astype(jnp.int32)
    padded = (counts + MOE_BLOCK - 1) // MOE_BLOCK * MOE_BLOCK
    pend = jnp.cumsum(padded)
    pstart = pend - padded
    expert_ids = jnp.arange(MOE_N_EXPERTS, dtype=jnp.int32)[:, None]

    def sorted_row(k):
        eid = route_t[ROUTE_E + k].astype(jnp.int32)
        rank = route_t[ROUTE_RANK + k].astype(jnp.int32)
        return jnp.sum(jnp.where(eid[None, :] == expert_ids, pstart[:, None], 0), axis=0) + rank

    dest0, dest1 = sorted_row(0), sorted_row(1)
    block_start = jnp.arange(nblk, dtype=jnp.int32) * MOE_BLOCK
    block_e = jnp.minimum(jnp.sum((pend[None, :] <= block_start[:, None]).astype(jnp.int32), axis=1),
                          MOE_N_EXPERTS - 1)
    n_used = (pend[-1:] // MOE_BLOCK).astype(jnp.int32)
    last_blk = jnp.where(counts > 0, pend // MOE_BLOCK - 1, -1)
    tail = n_used[0] + jnp.arange(MOE_N_EXPERTS, dtype=jnp.int32)
    zero_blocks = jnp.concatenate([last_blk, jnp.where(tail < nblk, tail, -1)]).astype(jnp.int32)
    xs = _dispatch(h2, dest0, dest1, zero_blocks, n_rows)
    ys = _experts(xs, block_e, n_used, w_gate_e, w_up_e, w_down_e)
    return _combine(x1, route, ys, dest0, dest1)


def _layer(x, positions, norm1_g, w_in, conv_w, conv_b, dt_bias, a_log, d_skip, ssd_norm_g, w_ssd_out,
           q_norm_g, k_norm_g, sinks, w_attn_out, w_out, norm2_g, w_rg, b_rg, w_re, b_re,
           w_gate_e, w_up_e, w_down_e):
    batch, seq, d = x.shape
    x2 = x.reshape(batch * seq, d)
    cos_t, sin_t = _rope_tables(positions)
    z, xbc, q, k, v, gs, ga, dt = _inproj(x2, norm1_g, w_in)
    merged_ssd = _ssd(xbc, z, dt, gs, conv_w, conv_b, dt_bias, a_log, d_skip, ssd_norm_g, w_ssd_out,
                      batch, seq)
    x1, h2, route, route_t, counts = _attn(q, k, v, cos_t, sin_t, ga, merged_ssd, x2, q_norm_g, k_norm_g,
                                           sinks, w_attn_out, w_out, norm2_g, w_rg, b_rg, w_re, b_re,
                                           batch, seq)
    out = _moe(x1, h2, route, route_t, counts, w_gate_e, w_up_e, w_down_e)
    return out.reshape(batch, seq, d)


def kernel(x, positions, norm1_g, w_in, conv_w, conv_b, dt_bias, a_log, d_skip, ssd_norm_g, w_ssd_out,
           q_norm_g, k_norm_g, sinks, w_attn_out, w_out, norm2_g, w_router_group, b_router_group,
           w_router_expert, b_router_expert, w_gate_e, w_up_e, w_down_e):
    for l in range(norm1_g.shape[0]):
        x = _layer(x, positions, norm1_g[l], w_in[l], conv_w[l], conv_b[l], dt_bias[l], a_log[l],
                   d_skip[l], ssd_norm_g[l], w_ssd_out[l], q_norm_g[l], k_norm_g[l], sinks[l],
                   w_attn_out[l], w_out[l], norm2_g[l], w_router_group[l], b_router_group[l],
                   w_router_expert[l], b_router_expert[l], w_gate_e[l], w_up_e[l], w_down_e[l])
    return x
```

```python
import functools

import numpy as np

import jax
import jax.numpy as jnp
from jax import lax
from jax.experimental import pallas as pl
from jax.experimental.pallas import tpu as pltpu

f32 = jnp.float32
bf16 = jnp.bfloat16

D_MODEL = 1024
SSD_D_INNER = 2048
SSD_HEAD_DIM = 64
SSD_N_HEADS = 32
SSD_N_GROUPS = 4
SSD_D_STATE = 128
SSD_CONV = 4
SSD_CHUNK = 128
SSD_CONV_DIM = 3072
ATT_HEAD_DIM = 64
ATT_N_HEADS = 16
ATT_N_KV = 4
ATT_BLOCK = 128
ATT_SCALE = ATT_HEAD_DIM ** -0.5
ROPE_THETA = 500000.0
ROPE_DIM = 16
MOE_N_GROUPS = 8
MOE_EPG = 8
MOE_N_EXPERTS = 64
MOE_D_FF = 256
MOE_BLOCK = 256
RMS_EPS = 1e-6

LANES = 128
CONV_HALO = 8
NEG_BIG = -1e30
VMEM_LIMIT = 56 * 1024 * 1024

COL_Z = SSD_D_INNER
COL_XBC = SSD_CONV_DIM
COL_DT = SSD_N_HEADS
COL_Q = ATT_N_HEADS * ATT_HEAD_DIM
COL_KV = ATT_N_KV * ATT_HEAD_DIM


def _sigmoid(x):
    return 1.0 / (1.0 + jnp.exp(-x))


def _silu(x):
    return x * _sigmoid(x)


def _split3(x):
    hi = x.astype(bf16)
    r1 = x - hi.astype(f32)
    mid = r1.astype(bf16)
    lo = (r1 - mid.astype(f32)).astype(bf16)
    return hi, mid, lo


HALF_D = D_MODEL // 2
_HI_MASK = np.uint32(0xFFFF0000)


def _pack_rows(x):
    bits = pltpu.bitcast(x.astype(bf16).astype(f32), jnp.uint32)
    return (bits[:, HALF_D:] & _HI_MASK) | (bits[:, :HALF_D] >> 16)


def _unpack_rows(p):
    return pltpu.bitcast(p << 16, f32), pltpu.bitcast(p & _HI_MASK, f32)


def _dot(a, b):
    return jnp.dot(a, b, preferred_element_type=f32)


def _dot_nt(a, b):
    return lax.dot_general(a, b, (((1,), (1,)), ((), ())), preferred_element_type=f32)


def _trig_kernel(freq_ref, pos_ref, cos_ref, sin_ref):
    ang = pos_ref[...].astype(f32) * freq_ref[pl.program_id(0)]
    cos_ref[0] = jnp.cos(ang)
    sin_ref[0] = jnp.sin(ang)


def _rope_select():
    half = ROPE_DIM // 2
    sel = np.zeros((LANES, 3 * LANES), np.float32)
    for lane in range(LANES):
        m = lane % ATT_HEAD_DIM
        if m < half:
            sel[m, lane] = 1.0
            sel[half + m, LANES + lane] = -1.0
        elif m < ROPE_DIM:
            sel[m - half, lane] = 1.0
            sel[m, 2 * LANES + lane] = 1.0
        else:
            sel[ROPE_DIM, lane] = 1.0
    return jnp.asarray(sel, bf16)


def _rope_tables(positions):
    t = positions.size
    half = ROPE_DIM // 2
    inv_freq = ROPE_THETA ** (-jnp.arange(0, ROPE_DIM, 2, dtype=f32) / ROPE_DIM)
    pos2d = positions.reshape(t // LANES, LANES)
    cos_t, sin_t = pl.pallas_call(
        _trig_kernel,
        grid_spec=pltpu.PrefetchScalarGridSpec(
            num_scalar_prefetch=1,
            grid=(half,),
            in_specs=[pl.BlockSpec((t // LANES, LANES), lambda j, f: (0, 0))],
            out_specs=[pl.BlockSpec((1, t // LANES, LANES), lambda j, f: (j, 0, 0))] * 2,
        ),
        out_shape=[jax.ShapeDtypeStruct((half, t // LANES, LANES), f32)] * 2,
        name="trig",
    )(inv_freq, pos2d)
    return cos_t.reshape(half, t), sin_t.reshape(half, t)


INPROJ_TM = 512
INPROJ_CH = 512


def _inproj_kernel(x_ref, g_ref, w_ref,
                   z_ref, xbc_ref, q_ref, k_ref, v_ref, gs_ref, ga_ref, dt_ref, h_scr):
    x = x_ref[...]
    ms = jnp.mean(x * x, axis=-1, keepdims=True)
    h_scr[...] = (x * lax.rsqrt(ms + RMS_EPS) * g_ref[...]).astype(bf16)
    off = 0
    for ref in (z_ref, xbc_ref, q_ref, k_ref, v_ref, gs_ref, ga_ref, dt_ref):
        width = ref.shape[1]
        for c in range(0, width, INPROJ_CH):
            cw = min(INPROJ_CH, width - c)
            ref[:, c:c + cw] = _dot(h_scr[...], w_ref[:, off + c:off + c + cw]).astype(ref.dtype)
        off += width


WPREP_ROWS = 128


def _wprep_kernel(w_ref, o_ref):
    s1 = COL_Z + COL_XBC
    s2 = s1 + COL_DT
    n_tail = w_ref.shape[1] - s2
    o_ref[:, 0:s1] = w_ref[:, 0:s1].astype(bf16)
    o_ref[:, s1:s1 + n_tail] = w_ref[:, s2:s2 + n_tail].astype(bf16)
    lane = lax.broadcasted_iota(jnp.int32, (w_ref.shape[0], LANES), 1)
    o_ref[:, s1 + n_tail:s1 + n_tail + LANES] = jnp.where(lane < COL_DT, w_ref[:, s1:s1 + LANES], 0.0).astype(bf16)


def _inproj(x2, norm1_g, w_in):
    t = x2.shape[0]
    tm = min(INPROJ_TM, t)
    n_in = w_in.shape[1]
    n_all = n_in - COL_DT + LANES
    w_all = pl.pallas_call(
        _wprep_kernel,
        grid=(D_MODEL // WPREP_ROWS,),
        in_specs=[pl.BlockSpec((WPREP_ROWS, n_in), lambda i: (i, 0))],
        out_specs=pl.BlockSpec((WPREP_ROWS, n_all), lambda i: (i, 0)),
        out_shape=jax.ShapeDtypeStruct((D_MODEL, n_all), bf16),
        name="wprep",
    )(w_in)
    widths = (COL_Z, COL_XBC, COL_Q, COL_KV, COL_KV, D_MODEL, D_MODEL)
    const = lambda i: (0, 0)
    row = lambda i: (i, 0)
    outs = pl.pallas_call(
        _inproj_kernel,
        grid=(t // tm,),
        in_specs=[pl.BlockSpec((tm, D_MODEL), row),
                  pl.BlockSpec((1, D_MODEL), const),
                  pl.BlockSpec((D_MODEL, w_all.shape[1]), const, pipeline_mode=pl.Buffered(1))],
        out_specs=[pl.BlockSpec((tm, w), row) for w in widths] + [pl.BlockSpec((tm, LANES), row)],
        out_shape=[jax.ShapeDtypeStruct((t, w), bf16) for w in widths]
                  + [jax.ShapeDtypeStruct((t, LANES), f32)],
        scratch_shapes=[pltpu.VMEM((tm, D_MODEL), bf16)],
        compiler_params=pltpu.CompilerParams(dimension_semantics=("arbitrary",),
                                             vmem_limit_bytes=VMEM_LIMIT),
        name="inproj",
    )(x2, norm1_g.reshape(1, D_MODEL), w_all)
    return outs


def _ssd_kernel(xbc_ref, z_ref, dt_ref, gs_ref, cw_ref, cb_ref, dtb_ref, alog_ref, dexp_ref, ng_ref,
                wout_ref, out_ref, ext_ref, st_ref, xs_ref, bm_ref, cm_ref, y_ref, hn_ref):
    L = SSD_CHUNK
    c = pl.program_id(1)

    @pl.when(c == 0)
    def _():
        ext_ref[0:CONV_HALO, :] = jnp.zeros((CONV_HALO, SSD_CONV_DIM), f32)
        st_ref[...] = jnp.zeros(st_ref.shape, f32)

    bf_tile = 2 * CONV_HALO
    ext_ref[CONV_HALO:2 * CONV_HALO, :] = xbc_ref[0:bf_tile, :].astype(f32)[0:CONV_HALO]
    n_sh = SSD_CONV - 1
    sr = lax.broadcasted_iota(jnp.int32, (n_sh * L, L), 0)
    sc = lax.broadcasted_iota(jnp.int32, (n_sh * L, L), 1)
    shift = jnp.where((sr % L) - sc == (sr // L) + 1, 1.0, 0.0).astype(bf16)
    cch = 128
    for cc in range(0, SSD_CONV_DIM, cch):
        cs_ = slice(cc, cc + cch)
        xb = xbc_ref[:, cs_]
        sh = _dot(shift, xb)
        w_now = cw_ref[SSD_CONV - 1:SSD_CONV, cs_]
        acc = cb_ref[:, cs_] + xb.astype(f32) * w_now
        top = cb_ref[:, cs_] + ext_ref[CONV_HALO:2 * CONV_HALO, cs_] * w_now
        for j in range(1, SSD_CONV):
            w_j = cw_ref[SSD_CONV - 1 - j:SSD_CONV - j, cs_]
            acc = acc + sh[(j - 1) * L:j * L] * w_j
            top = top + ext_ref[CONV_HALO - j:2 * CONV_HALO - j, cs_] * w_j
        if cc < SSD_D_INNER:
            dst, o = xs_ref, cc
        elif cc < SSD_D_INNER + SSD_N_GROUPS * SSD_D_STATE:
            dst, o = bm_ref, cc - SSD_D_INNER
        else:
            dst, o = cm_ref, cc - SSD_D_INNER - SSD_N_GROUPS * SSD_D_STATE
        dst[:, o:o + cch] = _silu(acc)
        dst[0:CONV_HALO, o:o + cch] = _silu(top)
    ext_ref[0:CONV_HALO, :] = xbc_ref[L - bf_tile:L, :].astype(f32)[CONV_HALO:bf_tile]

    lane_row = lax.broadcasted_iota(jnp.int32, (1, LANES), 1)
    row_i = lax.broadcasted_iota(jnp.int32, (L, L), 0)
    col_i = lax.broadcasted_iota(jnp.int32, (L, L), 1)
    causal = row_i >= col_i
    left = col_i < SSD_HEAD_DIM

    xdt = dt_ref[...] + dtb_ref[...]
    dtv = jnp.maximum(xdt, 0.0) + jnp.log1p(jnp.exp(-jnp.abs(xdt)))
    a = jnp.where(lane_row < SSD_N_HEADS, -jnp.exp(alog_ref[...]), 0.0)
    d_a = dtv * a
    tril = jnp.where(causal, 1.0, 0.0).astype(bf16)
    hi, mid, lo3 = _split3(d_a)
    a_cum = _dot(tril, hi) + _dot(tril, mid) + _dot(tril, lo3)
    a_end = a_cum[L - 1:L, :]
    exp_a = jnp.exp(a_cum)
    w_end = jnp.exp(a_end - a_cum) * dtv
    cd = jnp.exp(a_end)
    a_t = a_cum.T
    dt_t = dtv.T
    w_t = w_end.T

    n_pairs = SSD_N_HEADS // 2
    pairs_per_group = n_pairs // SSD_N_GROUPS
    for g in range(SSD_N_GROUPS):
        b_g = bm_ref[:, g * SSD_D_STATE:(g + 1) * SSD_D_STATE]
        c_g = cm_ref[:, g * SSD_D_STATE:(g + 1) * SSD_D_STATE]
        cb = _dot_nt(c_g.astype(bf16), b_g.astype(bf16))
        b_t = b_g.T
        for pi in range(pairs_per_group):
            i = g * pairs_per_group + pi
            xpair = xs_ref[:, i * LANES:(i + 1) * LANES]
            xpair_b = xpair.astype(bf16)
            s_prev = st_ref[i]
            rhs = jnp.concatenate([xpair_b, s_prev.astype(bf16)], axis=0)
            ys = []
            sn = []
            for h in (2 * i, 2 * i + 1):
                acol = jnp.broadcast_to(a_cum[:, h:h + 1], (L, L))
                arow = jnp.broadcast_to(a_t[h:h + 1, :], (L, L))
                dtrow = jnp.broadcast_to(dt_t[h:h + 1, :], (L, L))
                dec = jnp.exp(jnp.where(causal, acol - arow, NEG_BIG))
                m = cb * dec * dtrow
                cs = c_g * jnp.broadcast_to(exp_a[:, h:h + 1], (L, L))
                lhs = jnp.concatenate([m.astype(bf16), cs.astype(bf16)], axis=1)
                ys.append(_dot(lhs, rhs))
                btw = (b_t * jnp.broadcast_to(w_t[h:h + 1, :], (L, L))).astype(bf16)
                sn.append(_dot(btw, xpair_b))
            h0 = 2 * i
            cd_pair = jnp.where(lane_row < SSD_HEAD_DIM, cd[:, h0:h0 + 1], cd[:, h0 + 1:h0 + 2])
            st_ref[i] = jnp.where(left, sn[0], sn[1]) + s_prev * cd_pair
            y_pair = jnp.where(left, ys[0], ys[1])
            y_ref[:, i * LANES:(i + 1) * LANES] = y_pair + xpair * dexp_ref[:, i * LANES:(i + 1) * LANES]

    gw = SSD_D_INNER // SSD_N_GROUPS
    for g in range(SSD_N_GROUPS):
        sl = slice(g * gw, (g + 1) * gw)
        yz = y_ref[:, sl] * _silu(z_ref[:, sl].astype(f32))
        ms = jnp.mean(yz * yz, axis=-1, keepdims=True)
        hn_ref[:, sl] = (yz * lax.rsqrt(ms + RMS_EPS) * ng_ref[:, sl]).astype(bf16)
    y_ssd = _dot(hn_ref[...], wout_ref[...])
    out_ref[...] = _sigmoid(gs_ref[...].astype(f32)) * y_ssd


def _ssd(xbc, z, dt, gs, conv_w, conv_b, dt_bias, a_log, d_skip, ssd_norm_g, w_ssd_out, batch, seq):
    t = batch * seq
    L = SSD_CHUNK
    nc = seq // L
    pad_h = LANES - SSD_N_HEADS
    dtb = jnp.pad(dt_bias, (0, pad_h)).reshape(1, LANES)
    alog = jnp.pad(a_log, (0, pad_h)).reshape(1, LANES)
    dexp = jnp.repeat(d_skip, SSD_HEAD_DIM).reshape(1, SSD_D_INNER)
    const = lambda b, c: (0, 0)
    row = lambda b, c: (b * nc + c, 0)
    return pl.pallas_call(
        _ssd_kernel,
        grid=(batch, nc),
        in_specs=[pl.BlockSpec((L, SSD_CONV_DIM), row),
                  pl.BlockSpec((L, SSD_D_INNER), row),
                  pl.BlockSpec((L, LANES), row),
                  pl.BlockSpec((L, D_MODEL), row),
                  pl.BlockSpec((SSD_CONV, SSD_CONV_DIM), const),
                  pl.BlockSpec((1, SSD_CONV_DIM), const),
                  pl.BlockSpec((1, LANES), const),
                  pl.BlockSpec((1, LANES), const),
                  pl.BlockSpec((1, SSD_D_INNER), const),
                  pl.BlockSpec((1, SSD_D_INNER), const),
                  pl.BlockSpec((SSD_D_INNER, D_MODEL), const, pipeline_mode=pl.Buffered(1))],
        out_specs=pl.BlockSpec((L, D_MODEL), row),
        out_shape=jax.ShapeDtypeStruct((t, D_MODEL), f32),
        scratch_shapes=[pltpu.VMEM((2 * CONV_HALO, SSD_CONV_DIM), f32),
                        pltpu.VMEM((SSD_N_HEADS // 2, SSD_D_STATE, LANES), f32),
                        pltpu.VMEM((L, SSD_D_INNER), f32),
                        pltpu.VMEM((L, SSD_N_GROUPS * SSD_D_STATE), f32),
                        pltpu.VMEM((L, SSD_N_GROUPS * SSD_D_STATE), f32),
                        pltpu.VMEM((L, SSD_D_INNER), f32),
                        pltpu.VMEM((L, SSD_D_INNER), bf16)],
        compiler_params=pltpu.CompilerParams(dimension_semantics=("arbitrary", "arbitrary"),
                                             vmem_limit_bytes=VMEM_LIMIT),
        name="ssd",
    )(xbc, z, dt, gs, conv_w, conv_b.reshape(1, -1), dtb, alog, dexp,
      ssd_norm_g.reshape(1, -1), w_ssd_out.astype(bf16))


ROUTE_E, ROUTE_RANK, ROUTE_GATE = 0, 2, 4
ATT_SUB = 2


def _attn_kernel(sink_ref, q_ref, k_ref, v_ref, cos_ref, sin_ref, sel_ref, ga_ref, ms_ref, x_ref,
                 qg_ref, kg_ref, wao_ref, wo_ref, n2g_ref, wrh_ref, wrl_ref, br_ref,
                 x1_ref, h2_ref, route_ref, route_t_ref, cnt_ref,
                 kprev_ref, vprev_ref, att_ref, cnt_scr, vd_ref):
    Q = ATT_BLOCK
    b = pl.program_id(0)
    n = pl.program_id(1)

    @pl.when(n == 0)
    def _():
        kprev_ref[...] = jnp.zeros(kprev_ref.shape, f32)
        vprev_ref[...] = jnp.zeros(vprev_ref.shape, f32)

    @pl.when((b == 0) & (n == 0))
    def _():
        cnt_scr[...] = jnp.zeros(cnt_scr.shape, f32)

    R = q_ref.shape[0]
    lane_q = lax.broadcasted_iota(jnp.int32, (Q, LANES), 1)
    row_q = lax.broadcasted_iota(jnp.int32, (Q, LANES), 0)
    left = lane_q < ATT_HEAD_DIM
    head_mean = jnp.where((row_q // ATT_HEAD_DIM) == (lane_q // ATT_HEAD_DIM),
                          1.0 / ATT_HEAD_DIM, 0.0).astype(bf16)
    lane2 = lax.broadcasted_iota(jnp.int32, (2 * Q, LANES), 1) < ATT_HEAD_DIM
    qg = ATT_N_HEADS // ATT_N_KV
    rows = qg * Q
    ri = lax.broadcasted_iota(jnp.int32, (rows, Q), 0) % Q
    cj = lax.broadcasted_iota(jnp.int32, (rows, Q), 1)
    upper = cj > ri
    half = ROPE_DIM // 2

    n_sb = R // Q
    nq = COL_Q // LANES
    nk = COL_KV // LANES
    chunks = []
    for sb in range(n_sb):
        rs = slice(sb * Q, (sb + 1) * Q)
        chunks += [q_ref[rs, c * LANES:(c + 1) * LANES].astype(f32) for c in range(nq)]
        chunks += [k_ref[rs, c * LANES:(c + 1) * LANES].astype(f32) for c in range(nk)]
    u_all = jnp.concatenate(chunks, axis=0)
    sq = u_all * u_all
    sq_hi = sq.astype(bf16)
    sq_lo = (sq - sq_hi.astype(f32)).astype(bf16)
    un_all = u_all * lax.rsqrt(_dot(sq_hi, head_mean) + _dot(sq_lo, head_mean) + RMS_EPS)

    terms = []
    for sb in range(n_sb):
        rs = slice(sb * Q, (sb + 1) * Q)
        cs = jnp.concatenate([cos_ref[:, rs], sin_ref[:, rs], jnp.ones((half, LANES), f32),
                              jnp.zeros((LANES - 3 * half, LANES), f32)], axis=0)
        terms += list(_split3(cs.T))
    pat_all = _dot(jnp.concatenate(terms, axis=0), sel_ref[...])

    pats, kds = [], []
    for sb in range(n_sb):
        rs = slice(sb * Q, (sb + 1) * Q)
        pat = pat_all[3 * sb * Q:(3 * sb + 1) * Q] + pat_all[(3 * sb + 1) * Q:(3 * sb + 2) * Q] \
            + pat_all[(3 * sb + 2) * Q:(3 * sb + 3) * Q]
        cpat = pat[:, 0:LANES]
        s1pat = pat[:, LANES:2 * LANES]
        s2pat = pat[:, 2 * LANES:3 * LANES]

        def norm_rope(idx, gpat):
            tn = un_all[idx * Q:(idx + 1) * Q] * gpat
            return (tn * cpat + pltpu.roll(tn, LANES - ROPE_DIM // 2, 1) * s1pat
                    + pltpu.roll(tn, ROPE_DIM // 2, 1) * s2pat)

        kd = []
        for cidx in range(nk):
            sl = slice(cidx * LANES, (cidx + 1) * LANES)
            k_cur = norm_rope(sb * (nq + nk) + nq + cidx, kg_ref[...])
            v_cur = v_ref[rs, sl].astype(f32)
            k_all = jnp.concatenate([kprev_ref[:, sl], k_cur], axis=0)
            v_all = jnp.concatenate([vprev_ref[:, sl], v_cur], axis=0)
            kprev_ref[:, sl] = k_cur
            vprev_ref[:, sl] = v_cur
            k_sw = pltpu.roll(k_all, ATT_HEAD_DIM, 1)
            v_sw = pltpu.roll(v_all, ATT_HEAD_DIM, 1)
            kd.append(jnp.where(lane2, k_all, k_sw).astype(bf16))
            kd.append(jnp.where(lane2, k_sw, k_all).astype(bf16))
            for half_i, vv in enumerate((jnp.where(lane2, v_all, v_sw), jnp.where(lane2, v_sw, v_all))):
                vd_ref[sb * ATT_N_KV + 2 * cidx + half_i] = vv.astype(bf16)

        pats.append((cpat, s1pat, s2pat))
        kds.append(kd)

    def rope_q(sb, cidx):
        cpat, s1pat, s2pat = pats[sb]
        tn = un_all[(sb * (nq + nk) + cidx) * Q:(sb * (nq + nk) + cidx + 1) * Q] * qg_ref[...]
        return (tn * cpat + pltpu.roll(tn, LANES - ROPE_DIM // 2, 1) * s1pat
                + pltpu.roll(tn, ROPE_DIM // 2, 1) * s2pat) * ATT_SCALE

    def score_tile(g):
        sb, h = divmod(g, ATT_N_KV)
        parts = []
        for cidx in (2 * h, 2 * h + 1):
            qc = rope_q(sb, cidx)
            parts.append(jnp.where(left, qc, 0.0).astype(bf16))
            parts.append(jnp.where(left, 0.0, qc).astype(bf16))
        lhs = jnp.concatenate(parts, axis=0)
        s_both = _dot_nt(lhs, kds[sb][h])
        s_prev = s_both[:, 0:Q]
        s_cur = s_both[:, Q:2 * Q]
        if sb == 0:
            return jnp.where(upper & (n > 0), s_prev, jnp.where(upper, NEG_BIG, s_cur))
        return jnp.where(upper, s_prev, s_cur)

    ones = jnp.ones((Q, Q), bf16)

    def softmax_block(s):
        sink = jnp.concatenate([jnp.full((Q, 1), sink_ref[i], f32) for i in range(ATT_N_HEADS)], axis=0)
        m = jnp.maximum(jnp.max(s, axis=-1, keepdims=True), sink)
        p = jnp.exp(s - m)
        p_hi = p.astype(bf16)
        p_lo = (p - p_hi.astype(f32)).astype(bf16)
        denom = _dot(p_hi, ones) + _dot(p_lo, ones) + jnp.exp(sink - m)
        return (p / denom).astype(bf16)

    def pv_tile(g, pf):
        sb, h = divmod(g, ATT_N_KV)
        rs = slice(sb * Q, (sb + 1) * Q)
        zero = jnp.zeros_like(pf)
        p_both = jnp.concatenate([jnp.where(upper, pf, zero), jnp.where(upper, zero, pf)], axis=1)
        o = _dot(p_both, vd_ref[g])
        for r in range(2):
            cidx = 2 * h + r
            att_ref[rs, cidx * LANES:(cidx + 1) * LANES] = jnp.where(
                left, o[(2 * r) * Q:(2 * r + 1) * Q], o[(2 * r + 1) * Q:(2 * r + 2) * Q]).astype(bf16)

    def epilogue(rs, cnt):
        E = rs.stop - rs.start
        lane = lax.broadcasted_iota(jnp.int32, (E, LANES), 1)
        y_att = _dot(att_ref[rs, :], wao_ref[...])
        merged = _sigmoid(ga_ref[rs, :].astype(f32)) * y_att + ms_ref[rs, :]
        x1 = x_ref[rs, :] + _dot(merged.astype(bf16), wo_ref[...])
        x1_ref[rs, :] = x1
        h2 = x1 * lax.rsqrt(jnp.mean(x1 * x1, axis=-1, keepdims=True) + RMS_EPS) * n2g_ref[...]
        h2_ref[rs, :] = _pack_rows(h2)

        hi = h2.astype(bf16)
        lo = (h2 - hi.astype(f32)).astype(bf16)
        logits = _dot(hi, wrh_ref[...]) + _dot(lo, wrh_ref[...]) + _dot(hi, wrl_ref[...]) + br_ref[...]
        big = 4 * LANES
        gl = jnp.where(lane < MOE_N_GROUPS, logits, NEG_BIG)
        gmax = jnp.max(gl, axis=-1, keepdims=True)
        gsel = jnp.min(jnp.where(gl == gmax, lane, big), axis=-1, keepdims=True)
        pg = 1.0 / jnp.sum(jnp.exp(gl - gmax), axis=-1, keepdims=True)
        lo_l = MOE_N_GROUPS + MOE_EPG * gsel
        el = jnp.where((lane >= lo_l) & (lane < lo_l + MOE_EPG), logits, NEG_BIG)
        v1 = jnp.max(el, axis=-1, keepdims=True)
        i1 = jnp.min(jnp.where(el == v1, lane, big), axis=-1, keepdims=True)
        el2 = jnp.where(lane == i1, NEG_BIG, el)
        v2 = jnp.max(el2, axis=-1, keepdims=True)
        i2 = jnp.min(jnp.where(el2 == v2, lane, big), axis=-1, keepdims=True)
        e1 = i1 - MOE_N_GROUPS
        e2 = i2 - MOE_N_GROUPS
        tt = jnp.exp(v2 - v1)
        w1 = pg * (1.0 / (1.0 + tt))
        w2 = pg * (tt / (1.0 + tt))

        onehot = jnp.where((lane == e1) | (lane == e2), 1.0, 0.0)
        strict = jnp.where(lax.broadcasted_iota(jnp.int32, (E, E), 0) > lax.broadcasted_iota(jnp.int32, (E, E), 1),
                           1.0, 0.0).astype(bf16)
        base = _dot(strict, onehot.astype(bf16)) + cnt
        r1 = jnp.sum(jnp.where(lane == e1, base, 0.0), axis=-1, keepdims=True)
        r2 = jnp.sum(jnp.where(lane == e2, base, 0.0), axis=-1, keepdims=True)

        rec = jnp.zeros((E, LANES), f32)
        for off, val in ((ROUTE_E, e1.astype(f32)), (ROUTE_E + 1, e2.astype(f32)),
                         (ROUTE_RANK, r1), (ROUTE_RANK + 1, r2), (ROUTE_GATE, w1), (ROUTE_GATE + 1, w2)):
            rec = jnp.where(lane == off, val, rec)
        route_ref[rs, :] = rec
        for o in range(0, E, Q):
            route_t_ref[:, rs.start + o:rs.start + o + Q] = rec[o:o + Q].T[0:8, :]
        return cnt + jnp.sum(onehot, axis=0, keepdims=True)

    s_tiles, p_tiles = {}, {}
    for stage in range(n_sb + 2):
        if stage < n_sb:
            s_tiles[stage] = jnp.concatenate(
                [score_tile(stage * ATT_N_KV + h) for h in range(ATT_N_KV)], axis=0)
        if 0 <= stage - 1 < n_sb:
            p_tiles[stage - 1] = softmax_block(s_tiles.pop(stage - 1))
        if 0 <= stage - 2 < n_sb:
            pb = p_tiles.pop(stage - 2)
            for h in range(ATT_N_KV):
                pv_tile((stage - 2) * ATT_N_KV + h, pb[h * rows:(h + 1) * rows])
    cnt = epilogue(slice(0, R), cnt_scr[0:1, :])

    cnt_scr[...] = jnp.broadcast_to(cnt, cnt_scr.shape)
    cnt_ref[...] = jnp.broadcast_to(cnt, cnt_ref.shape)


def _attn(q, k, v, cos_t, sin_t, ga, merged_ssd, x2, q_norm_g, k_norm_g, sinks, w_attn_out, w_out,
          norm2_g, w_rg, b_rg, w_re, b_re, batch, seq):
    t = batch * seq
    Q = ATT_SUB * ATT_BLOCK
    nb = seq // Q
    rep = LANES // ATT_HEAD_DIM
    qg = jnp.tile(q_norm_g, rep).reshape(1, LANES)
    kg = jnp.tile(k_norm_g, rep).reshape(1, LANES)
    n_log = MOE_N_GROUPS + MOE_N_EXPERTS
    w_r = jnp.concatenate([w_rg, jnp.transpose(w_re, (1, 0, 2)).reshape(D_MODEL, MOE_N_EXPERTS),
                           jnp.zeros((D_MODEL, LANES - n_log), f32)], axis=1)
    b_r = jnp.concatenate([b_rg, b_re.reshape(-1), jnp.zeros((LANES - n_log,), f32)]).reshape(1, LANES)
    w_r_hi = w_r.astype(bf16)
    w_r_lo = (w_r - w_r_hi.astype(f32)).astype(bf16)
    const = lambda b, n, *_: (0, 0)
    row = lambda b, n, *_: (b * nb + n, 0)
    col = lambda b, n, *_: (0, b * nb + n)
    full = lambda shape: pl.BlockSpec(shape, const, pipeline_mode=pl.Buffered(1))
    return pl.pallas_call(
        _attn_kernel,
        grid_spec=pltpu.PrefetchScalarGridSpec(
            num_scalar_prefetch=1,
            grid=(batch, nb),
            in_specs=[pl.BlockSpec((Q, COL_Q), row),
                      pl.BlockSpec((Q, COL_KV), row),
                      pl.BlockSpec((Q, COL_KV), row),
                      pl.BlockSpec((ROPE_DIM // 2, Q), col),
                      pl.BlockSpec((ROPE_DIM // 2, Q), col),
                      full((LANES, 3 * LANES)),
                      pl.BlockSpec((Q, D_MODEL), row),
                      pl.BlockSpec((Q, D_MODEL), row),
                      pl.BlockSpec((Q, D_MODEL), row),
                      pl.BlockSpec((1, LANES), const),
                      pl.BlockSpec((1, LANES), const),
                      full((COL_Q, D_MODEL)),
                      full((D_MODEL, D_MODEL)),
                      pl.BlockSpec((1, D_MODEL), const),
                      full((D_MODEL, LANES)),
                      full((D_MODEL, LANES)),
                      pl.BlockSpec((1, LANES), const)],
            out_specs=[pl.BlockSpec((Q, D_MODEL), row),
                       pl.BlockSpec((Q, HALF_D), row),
                       pl.BlockSpec((Q, LANES), row),
                       pl.BlockSpec((8, Q), col),
                       pl.BlockSpec((8, LANES), const)],
            scratch_shapes=[pltpu.VMEM((ATT_BLOCK, COL_KV), f32),
                            pltpu.VMEM((ATT_BLOCK, COL_KV), f32),
                            pltpu.VMEM((Q, COL_Q), bf16),
                            pltpu.VMEM((8, LANES), f32),
                            pltpu.VMEM((ATT_SUB * ATT_N_KV, 2 * ATT_BLOCK, LANES), bf16)],
        ),
        out_shape=[jax.ShapeDtypeStruct((t, D_MODEL), f32),
                   jax.ShapeDtypeStruct((t, HALF_D), jnp.uint32),
                   jax.ShapeDtypeStruct((t, LANES), f32),
                   jax.ShapeDtypeStruct((8, t), f32),
                   jax.ShapeDtypeStruct((8, LANES), f32)],
        compiler_params=pltpu.CompilerParams(dimension_semantics=("arbitrary", "arbitrary"),
                                             vmem_limit_bytes=VMEM_LIMIT),
        name="attn",
    )(sinks, q, k, v, cos_t, sin_t, _rope_select(), ga, merged_ssd, x2, qg, kg,
      w_attn_out.astype(bf16), w_out.astype(bf16), norm2_g.reshape(1, D_MODEL), w_r_hi, w_r_lo, b_r)


MOE_TM = 256
MOE_UNROLL = 8


def _row_copy(src, i, dst, j, sem):
    return pltpu.make_async_copy(src.at[pl.ds(i, 1), :], dst.at[pl.ds(j, 1), :], sem)


def _dispatch_kernel(dest0_ref, dest1_ref, zblk_ref, h2_ref, xs_hbm, zero_ref, sem, zsem):
    tm = h2_ref.shape[0]
    base = pl.program_id(0) * tm

    @pl.when(pl.program_id(0) == 0)
    def _():
        zero_ref[...] = jnp.zeros(zero_ref.shape, zero_ref.dtype)

        def zcopy(i):
            start = pl.multiple_of(zblk_ref[i] * MOE_BLOCK, MOE_BLOCK)
            return pltpu.make_async_copy(zero_ref, xs_hbm.at[pl.ds(start, MOE_BLOCK), :], zsem)

        def zstart(i, carry):
            @pl.when(zblk_ref[i] >= 0)
            def _():
                zcopy(i).start()
            return carry

        def zwait(i, carry):
            @pl.when(zblk_ref[i] >= 0)
            def _():
                zcopy(i).wait()
            return carry

        lax.fori_loop(0, zblk_ref.shape[0], zstart, 0)
        lax.fori_loop(0, zblk_ref.shape[0], zwait, 0)

    def issue(i, carry):
        _row_copy(h2_ref, i, xs_hbm, dest0_ref[base + i], sem).start()
        _row_copy(h2_ref, i, xs_hbm, dest1_ref[base + i], sem).start()
        return carry

    lax.fori_loop(0, tm, issue, 0, unroll=MOE_UNROLL)
    for _ in range(2):
        pltpu.make_async_copy(h2_ref, xs_hbm.at[pl.ds(0, tm), :], sem).wait()


def _dispatch(h2, dest0, dest1, zero_blocks, n_rows):
    t = h2.shape[0]
    tm = min(MOE_TM, t)
    return pl.pallas_call(
        _dispatch_kernel,
        grid_spec=pltpu.PrefetchScalarGridSpec(
            num_scalar_prefetch=3,
            grid=(t // tm,),
            in_specs=[pl.BlockSpec((tm, HALF_D), lambda i, d0, d1, zb: (i, 0))],
            out_specs=pl.BlockSpec(memory_space=pl.ANY),
            scratch_shapes=[pltpu.VMEM((MOE_BLOCK, HALF_D), h2.dtype),
                            pltpu.SemaphoreType.DMA(()),
                            pltpu.SemaphoreType.DMA(())],
        ),
        out_shape=jax.ShapeDtypeStruct((n_rows, HALF_D), h2.dtype),
        compiler_params=pltpu.CompilerParams(dimension_semantics=("arbitrary",)),
        name="dispatch",
    )(dest0, dest1, zero_blocks, h2)


EXPERT_SUB = 2


def _expert_kernel(be_ref, nu_ref, xs_ref, *refs):
    w_refs = refs[:3 * EXPERT_SUB]
    ys_ref, wgu_scr, wd_scr = refs[3 * EXPERT_SUB:]
    i = pl.program_id(0)
    rows = MOE_BLOCK

    for j in range(EXPERT_SUB):
        blk = EXPERT_SUB * i + j
        fresh = (i == 0) | (be_ref[blk] != be_ref[jnp.maximum(blk - EXPERT_SUB, 0)])

        @pl.when((blk < nu_ref[0]) & fresh)
        def _(j=j):
            wg_ref, wu_ref, wd_ref = w_refs[3 * j:3 * j + 3]
            wgu_scr[j, :, 0:MOE_D_FF] = wg_ref[0].astype(bf16)
            wgu_scr[j, :, MOE_D_FF:2 * MOE_D_FF] = wu_ref[0].astype(bf16)
            wd_scr[j] = wd_ref[0].astype(bf16)

    def block(j):
        rs = slice(j * rows, (j + 1) * rows)
        x_lo, x_hi = _unpack_rows(xs_ref[rs, :])
        gu = (_dot(x_lo.astype(bf16), wgu_scr[j, 0:HALF_D, :])
              + _dot(x_hi.astype(bf16), wgu_scr[j, HALF_D:D_MODEL, :]))
        hid = _silu(gu[:, :MOE_D_FF]) * gu[:, MOE_D_FF:]
        ys_ref[rs, :] = _pack_rows(_dot(hid.astype(bf16), wd_scr[j]))

    n_live = jnp.clip(nu_ref[0] - EXPERT_SUB * i, 0, EXPERT_SUB)
    for live in range(EXPERT_SUB + 1):
        @pl.when(n_live == live)
        def _(live=live):
            for j in range(live):
                block(j)
            if live < EXPERT_SUB:
                ys_ref[live * rows:, :] = jnp.zeros(((EXPERT_SUB - live) * rows, HALF_D), ys_ref.dtype)


def _experts(xs, block_e, n_used, w_g, w_u, w_d):
    n_rows = xs.shape[0]
    rows = EXPERT_SUB * MOE_BLOCK
    nstep = n_rows // rows
    blk_in = lambda i, be, nu: (jnp.minimum(i, (nu[0] - 1) // EXPERT_SUB), 0)
    blk_out = lambda i, be, nu: (i, 0)
    w_specs = []
    for j in range(EXPERT_SUB):
        wsel = lambda i, be, nu, j=j: (be[EXPERT_SUB * i + j], 0, 0)
        w_specs += [pl.BlockSpec((1, D_MODEL, MOE_D_FF), wsel),
                    pl.BlockSpec((1, D_MODEL, MOE_D_FF), wsel),
                    pl.BlockSpec((1, MOE_D_FF, D_MODEL), wsel)]
    return pl.pallas_call(
        _expert_kernel,
        grid_spec=pltpu.PrefetchScalarGridSpec(
            num_scalar_prefetch=2,
            grid=(nstep,),
            in_specs=[pl.BlockSpec((rows, HALF_D), blk_in)] + w_specs,
            out_specs=pl.BlockSpec((rows, HALF_D), blk_out),
            scratch_shapes=[pltpu.VMEM((EXPERT_SUB, D_MODEL, 2 * MOE_D_FF), bf16),
                            pltpu.VMEM((EXPERT_SUB, MOE_D_FF, D_MODEL), bf16)],
        ),
        out_shape=jax.ShapeDtypeStruct((n_rows, HALF_D), xs.dtype),
        compiler_params=pltpu.CompilerParams(dimension_semantics=("arbitrary",),
                                             vmem_limit_bytes=VMEM_LIMIT),
        name="experts",
    )(block_e, n_used, xs, *([w_g, w_u, w_d] * EXPERT_SUB))


def _combine_kernel(dest0_ref, dest1_ref, x1_ref, route_ref, ys_hbm, out_ref, y0_ref, y1_ref, sem):
    tm = x1_ref.shape[0]
    step = pl.program_id(0)

    def gather_tile(tile, slot):
        base = tile * tm

        def issue(r, carry):
            pltpu.make_async_copy(ys_hbm.at[pl.ds(dest0_ref[base + r], 1), :],
                                  y0_ref.at[slot, pl.ds(r, 1), :], sem.at[slot]).start()
            pltpu.make_async_copy(ys_hbm.at[pl.ds(dest1_ref[base + r], 1), :],
                                  y1_ref.at[slot, pl.ds(r, 1), :], sem.at[slot]).start()
            return carry

        lax.fori_loop(0, tm, issue, 0, unroll=MOE_UNROLL)

    @pl.when(step == 0)
    def _():
        gather_tile(0, 0)

    @pl.when(step + 1 < pl.num_programs(0))
    def _():
        gather_tile(step + 1, (step + 1) % 2)

    slot = step % 2
    for y_ref in (y0_ref, y1_ref):
        pltpu.make_async_copy(ys_hbm.at[pl.ds(0, tm), :], y_ref.at[slot], sem.at[slot]).wait()
    g0 = route_ref[:, ROUTE_GATE:ROUTE_GATE + 1]
    g1 = route_ref[:, ROUTE_GATE + 1:ROUTE_GATE + 2]
    y0_lo, y0_hi = _unpack_rows(y0_ref[slot])
    y1_lo, y1_hi = _unpack_rows(y1_ref[slot])
    out_ref[:, 0:HALF_D] = x1_ref[:, 0:HALF_D] + y0_lo * g0 + y1_lo * g1
    out_ref[:, HALF_D:D_MODEL] = x1_ref[:, HALF_D:D_MODEL] + y0_hi * g0 + y1_hi * g1


def _combine(x1, route, ys, dest0, dest1):
    t = x1.shape[0]
    tm = min(MOE_TM, t)
    row = lambda i, d0, d1: (i, 0)
    return pl.pallas_call(
        _combine_kernel,
        grid_spec=pltpu.PrefetchScalarGridSpec(
            num_scalar_prefetch=2,
            grid=(t // tm,),
            in_specs=[pl.BlockSpec((tm, D_MODEL), row),
                      pl.BlockSpec((tm, LANES), row),
                      pl.BlockSpec(memory_space=pl.ANY)],
            out_specs=pl.BlockSpec((tm, D_MODEL), row),
            scratch_shapes=[pltpu.VMEM((2, tm, HALF_D), ys.dtype),
                            pltpu.VMEM((2, tm, HALF_D), ys.dtype),
                            pltpu.SemaphoreType.DMA((2,))],
        ),
        out_shape=jax.ShapeDtypeStruct((t, D_MODEL), f32),
        compiler_params=pltpu.CompilerParams(dimension_semantics=("arbitrary",)),
        name="combine",
    )(dest0, dest1, x1, route, ys)


def _moe(x1, h2, route, route_t, counts_rec, w_gate_e, w_up_e, w_down_e):
    t = x1.shape[0]
    n_rows = 2 * t + MOE_N_EXPERTS * MOE_BLOCK
    nblk = n_rows // MOE_BLOCK
    counts = counts_rec[0, :MOE_N_EXPERTS].astype(jnp.int32)
    padded = (counts + MOE_BLOCK - 1) // MOE_BLOCK * MOE_BLOCK
    pend = jnp.cumsum(padded)
    pstart = pend - padded
    expert_ids = jnp.arange(MOE_N_EXPERTS, dtype=jnp.int32)[:, None]

    def sorted_row(k):
        eid = route_t[ROUTE_E + k].astype(jnp.int32)
        rank = route_t[ROUTE_RANK + k].astype(jnp.int32)
        return jnp.sum(jnp.where(eid[None, :] == expert_ids, pstart[:, None], 0), axis=0) + rank

    dest0, dest1 = sorted_row(0), sorted_row(1)
    block_start = jnp.arange(nblk, dtype=jnp.int32) * MOE_BLOCK
    block_e = jnp.minimum(jnp.sum((pend[None, :] <= block_start[:, None]).astype(jnp.int32), axis=1),
                          MOE_N_EXPERTS - 1)
    n_used = (pend[-1:] // MOE_BLOCK).astype(jnp.int32)
    last_blk = jnp.where(counts > 0, pend // MOE_BLOCK - 1, -1)
    tail = n_used[0] + jnp.arange(MOE_N_EXPERTS, dtype=jnp.int32)
    zero_blocks = jnp.concatenate([last_blk, jnp.where(tail < nblk, tail, -1)]).astype(jnp.int32)
    xs = _dispatch(h2, dest0, dest1, zero_blocks, n_rows)
    ys = _experts(xs, block_e, n_used, w_gate_e, w_up_e, w_down_e)
    return _combine(x1, route, ys, dest0, dest1)


def _layer(x, positions, norm1_g, w_in, conv_w, conv_b, dt_bias, a_log, d_skip, ssd_norm_g, w_ssd_out,
           q_norm_g, k_norm_g, sinks, w_attn_out, w_out, norm2_g, w_rg, b_rg, w_re, b_re,
           w_gate_e, w_up_e, w_down_e):
    batch, seq, d = x.shape
    x2 = x.reshape(batch * seq, d)
    cos_t, sin_t = _rope_tables(positions)
    z, xbc, q, k, v, gs, ga, dt = _inproj(x2, norm1_g, w_in)
    merged_ssd = _ssd(xbc, z, dt, gs, conv_w, conv_b, dt_bias, a_log, d_skip, ssd_norm_g, w_ssd_out,
                      batch, seq)
    x1, h2, route, route_t, counts = _attn(q, k, v, cos_t, sin_t, ga, merged_ssd, x2, q_norm_g, k_norm_g,
                                           sinks, w_attn_out, w_out, norm2_g, w_rg, b_rg, w_re, b_re,
                                           batch, seq)
    out = _moe(x1, h2, route, route_t, counts, w_gate_e, w_up_e, w_down_e)
    return out.reshape(batch, seq, d)


def kernel(x, positions, norm1_g, w_in, conv_w, conv_b, dt_bias, a_log, d_skip, ssd_norm_g, w_ssd_out,
           q_norm_g, k_norm_g, sinks, w_attn_out, w_out, norm2_g, w_router_group, b_router_group,
           w_router_expert, b_router_expert, w_gate_e, w_up_e, w_down_e):
    for l in range(norm1_g.shape[0]):
        x = _layer(x, positions, norm1_g[l], w_in[l], conv_w[l], conv_b[l], dt_bias[l], a_log[l],
                   d_skip[l], ssd_norm_g[l], w_ssd_out[l], q_norm_g[l], k_norm_g[l], sinks[l],
                   w_attn_out[l], w_out[l], norm2_g[l], w_router_group[l], b_router_group[l],
                   w_router_expert[l], b_router_expert[l], w_gate_e[l], w_up_e[l], w_down_e[l])
    return x
```

```python
import functools

import numpy as np

import jax
import jax.numpy as jnp
from jax import lax
from jax.experimental import pallas as pl
from jax.experimental.pallas import tpu as pltpu

f32 = jnp.float32
bf16 = jnp.bfloat16

D_MODEL = 1024
SSD_D_INNER = 2048
SSD_HEAD_DIM = 64
SSD_N_HEADS = 32
SSD_N_GROUPS = 4
SSD_D_STATE = 128
SSD_CONV = 4
SSD_CHUNK = 128
SSD_CONV_DIM = 3072
ATT_HEAD_DIM = 64
ATT_N_HEADS = 16
ATT_N_KV = 4
ATT_BLOCK = 128
ATT_SCALE = ATT_HEAD_DIM ** -0.5
ROPE_THETA = 500000.0
ROPE_DIM = 16
MOE_N_GROUPS = 8
MOE_EPG = 8
MOE_N_EXPERTS = 64
MOE_D_FF = 256
MOE_BLOCK = 256
RMS_EPS = 1e-6

LANES = 128
CONV_HALO = 8
NEG_BIG = -1e30
VMEM_LIMIT = 56 * 1024 * 1024

COL_Z = SSD_D_INNER
COL_XBC = SSD_CONV_DIM
COL_DT = SSD_N_HEADS
COL_Q = ATT_N_HEADS * ATT_HEAD_DIM
COL_KV = ATT_N_KV * ATT_HEAD_DIM


def _sigmoid(x):
    return 1.0 / (1.0 + jnp.exp(-x))


def _silu(x):
    return x * _sigmoid(x)


def _split3(x):
    hi = x.astype(bf16)
    r1 = x - hi.astype(f32)
    mid = r1.astype(bf16)
    lo = (r1 - mid.astype(f32)).astype(bf16)
    return hi, mid, lo


HALF_D = D_MODEL // 2
_HI_MASK = np.uint32(0xFFFF0000)


def _pack_rows(x):
    bits = pltpu.bitcast(x.astype(bf16).astype(f32), jnp.uint32)
    return (bits[:, HALF_D:] & _HI_MASK) | (bits[:, :HALF_D] >> 16)


def _unpack_rows(p):
    return pltpu.bitcast(p << 16, f32), pltpu.bitcast(p & _HI_MASK, f32)


def _dot(a, b):
    return jnp.dot(a, b, preferred_element_type=f32)


def _dot_nt(a, b):
    return lax.dot_general(a, b, (((1,), (1,)), ((), ())), preferred_element_type=f32)


def _trig_kernel(freq_ref, pos_ref, cos_ref, sin_ref):
    ang = pos_ref[...].astype(f32) * freq_ref[pl.program_id(0)]
    cos_ref[0] = jnp.cos(ang)
    sin_ref[0] = jnp.sin(ang)


def _rope_select():
    half = ROPE_DIM // 2
    sel = np.zeros((LANES, 3 * LANES), np.float32)
    for lane in range(LANES):
        m = lane % ATT_HEAD_DIM
        if m < half:
            sel[m, lane] = 1.0
            sel[half + m, LANES + lane] = -1.0
        elif m < ROPE_DIM:
            sel[m - half, lane] = 1.0
            sel[m, 2 * LANES + lane] = 1.0
        else:
            sel[ROPE_DIM, lane] = 1.0
    return jnp.asarray(sel, bf16)


def _rope_tables(positions):
    t = positions.size
    half = ROPE_DIM // 2
    inv_freq = ROPE_THETA ** (-jnp.arange(0, ROPE_DIM, 2, dtype=f32) / ROPE_DIM)
    pos2d = positions.reshape(t // LANES, LANES)
    cos_t, sin_t = pl.pallas_call(
        _trig_kernel,
        grid_spec=pltpu.PrefetchScalarGridSpec(
            num_scalar_prefetch=1,
            grid=(half,),
            in_specs=[pl.BlockSpec((t // LANES, LANES), lambda j, f: (0, 0))],
            out_specs=[pl.BlockSpec((1, t // LANES, LANES), lambda j, f: (j, 0, 0))] * 2,
        ),
        out_shape=[jax.ShapeDtypeStruct((half, t // LANES, LANES), f32)] * 2,
        name="trig",
    )(inv_freq, pos2d)
    return cos_t.reshape(half, t), sin_t.reshape(half, t)


INPROJ_TM = 512
INPROJ_CH = 512


def _inproj_kernel(x_ref, g_ref, w_ref,
                   z_ref, xbc_ref, q_ref, k_ref, v_ref, gs_ref, ga_ref, dt_ref, h_scr):
    x = x_ref[...]
    ms = jnp.mean(x * x, axis=-1, keepdims=True)
    h_scr[...] = (x * lax.rsqrt(ms + RMS_EPS) * g_ref[...]).astype(bf16)
    off = 0
    for ref in (z_ref, xbc_ref, q_ref, k_ref, v_ref, gs_ref, ga_ref, dt_ref):
        width = ref.shape[1]
        for c in range(0, width, INPROJ_CH):
            cw = min(INPROJ_CH, width - c)
            ref[:, c:c + cw] = _dot(h_scr[...], w_ref[:, off + c:off + c + cw]).astype(ref.dtype)
        off += width


WPREP_ROWS = 128


def _wprep_kernel(w_ref, o_ref):
    s1 = COL_Z + COL_XBC
    s2 = s1 + COL_DT
    n_tail = w_ref.shape[1] - s2
    o_ref[:, 0:s1] = w_ref[:, 0:s1].astype(bf16)
    o_ref[:, s1:s1 + n_tail] = w_ref[:, s2:s2 + n_tail].astype(bf16)
    lane = lax.broadcasted_iota(jnp.int32, (w_ref.shape[0], LANES), 1)
    o_ref[:, s1 + n_tail:s1 + n_tail + LANES] = jnp.where(lane < COL_DT, w_ref[:, s1:s1 + LANES], 0.0).astype(bf16)


def _inproj(x2, norm1_g, w_in):
    t = x2.shape[0]
    tm = min(INPROJ_TM, t)
    n_in = w_in.shape[1]
    n_all = n_in - COL_DT + LANES
    w_all = pl.pallas_call(
        _wprep_kernel,
        grid=(D_MODEL // WPREP_ROWS,),
        in_specs=[pl.BlockSpec((WPREP_ROWS, n_in), lambda i: (i, 0))],
        out_specs=pl.BlockSpec((WPREP_ROWS, n_all), lambda i: (i, 0)),
        out_shape=jax.ShapeDtypeStruct((D_MODEL, n_all), bf16),
        name="wprep",
    )(w_in)
    widths = (COL_Z, COL_XBC, COL_Q, COL_KV, COL_KV, D_MODEL, D_MODEL)
    const = lambda i: (0, 0)
    row = lambda i: (i, 0)
    outs = pl.pallas_call(
        _inproj_kernel,
        grid=(t // tm,),
        in_specs=[pl.BlockSpec((tm, D_MODEL), row),
                  pl.BlockSpec((1, D_MODEL), const),
                  pl.BlockSpec((D_MODEL, w_all.shape[1]), const, pipeline_mode=pl.Buffered(1))],
        out_specs=[pl.BlockSpec((tm, w), row) for w in widths] + [pl.BlockSpec((tm, LANES), row)],
        out_shape=[jax.ShapeDtypeStruct((t, w), bf16) for w in widths]
                  + [jax.ShapeDtypeStruct((t, LANES), f32)],
        scratch_shapes=[pltpu.VMEM((tm, D_MODEL), bf16)],
        compiler_params=pltpu.CompilerParams(dimension_semantics=("arbitrary",),
                                             vmem_limit_bytes=VMEM_LIMIT),
        name="inproj",
    )(x2, norm1_g.reshape(1, D_MODEL), w_all)
    return outs


SSD_CONV_YIELD = 1024

def _ssd_chunk_phases(rs, xbc_ref, z_ref, dt_ref, cw_ref, cb_ref, dtb_ref, alog_ref, dexp_ref, ng_ref,
                      ext_ref, st_ref, xs_ref, bm_ref, cm_ref, y_ref, hn_ref):
    L = SSD_CHUNK
    r0 = rs.start

    bf_tile = 2 * CONV_HALO
    ext_ref[CONV_HALO:2 * CONV_HALO, :] = xbc_ref[r0:r0 + bf_tile, :].astype(f32)[0:CONV_HALO]
    n_sh = SSD_CONV - 1
    sr = lax.broadcasted_iota(jnp.int32, (n_sh * L, L), 0)
    sc = lax.broadcasted_iota(jnp.int32, (n_sh * L, L), 1)
    shift = jnp.where((sr % L) - sc == (sr // L) + 1, 1.0, 0.0).astype(bf16)
    cch = 128
    for cc in range(0, SSD_CONV_DIM, cch):
        if cc and cc % SSD_CONV_YIELD == 0:
            yield
        cs_ = slice(cc, cc + cch)
        xb = xbc_ref[rs, cs_]
        sh = _dot(shift, xb)
        w_now = cw_ref[SSD_CONV - 1:SSD_CONV, cs_]
        acc = cb_ref[:, cs_] + xb.astype(f32) * w_now
        top = cb_ref[:, cs_] + ext_ref[CONV_HALO:2 * CONV_HALO, cs_] * w_now
        for j in range(1, SSD_CONV):
            w_j = cw_ref[SSD_CONV - 1 - j:SSD_CONV - j, cs_]
            acc = acc + sh[(j - 1) * L:j * L] * w_j
            top = top + ext_ref[CONV_HALO - j:2 * CONV_HALO - j, cs_] * w_j
        if cc < SSD_D_INNER:
            dst, o = xs_ref, cc
        elif cc < SSD_D_INNER + SSD_N_GROUPS * SSD_D_STATE:
            dst, o = bm_ref, cc - SSD_D_INNER
        else:
            dst, o = cm_ref, cc - SSD_D_INNER - SSD_N_GROUPS * SSD_D_STATE
        dst[rs, o:o + cch] = _silu(acc)
        dst[r0:r0 + CONV_HALO, o:o + cch] = _silu(top)
    ext_ref[0:CONV_HALO, :] = xbc_ref[rs.stop - bf_tile:rs.stop, :].astype(f32)[CONV_HALO:bf_tile]
    yield

    lane_row = lax.broadcasted_iota(jnp.int32, (1, LANES), 1)
    row_i = lax.broadcasted_iota(jnp.int32, (L, L), 0)
    col_i = lax.broadcasted_iota(jnp.int32, (L, L), 1)
    causal = row_i >= col_i
    left = col_i < SSD_HEAD_DIM

    xdt = dt_ref[rs, :] + dtb_ref[...]
    dtv = jnp.maximum(xdt, 0.0) + jnp.log1p(jnp.exp(-jnp.abs(xdt)))
    a = jnp.where(lane_row < SSD_N_HEADS, -jnp.exp(alog_ref[...]), 0.0)
    d_a = dtv * a
    tril = jnp.where(causal, 1.0, 0.0).astype(bf16)
    hi, mid, lo3 = _split3(d_a)
    a_cum = _dot(tril, hi) + _dot(tril, mid) + _dot(tril, lo3)
    a_end = a_cum[L - 1:L, :]
    exp_a = jnp.exp(a_cum)
    w_end = jnp.exp(a_end - a_cum) * dtv
    cd = jnp.exp(a_end)
    a_t = a_cum.T
    dt_t = dtv.T
    w_t = w_end.T
    yield

    n_pairs = SSD_N_HEADS // 2
    pairs_per_group = n_pairs // SSD_N_GROUPS
    for g in range(SSD_N_GROUPS):
        b_g = bm_ref[rs, g * SSD_D_STATE:(g + 1) * SSD_D_STATE]
        c_g = cm_ref[rs, g * SSD_D_STATE:(g + 1) * SSD_D_STATE]
        cb = _dot_nt(c_g.astype(bf16), b_g.astype(bf16))
        b_t = b_g.T
        for pi in range(pairs_per_group):
            i = g * pairs_per_group + pi
            xpair = xs_ref[rs, i * LANES:(i + 1) * LANES]
            xpair_b = xpair.astype(bf16)
            s_prev = st_ref[i]
            rhs = jnp.concatenate([xpair_b, s_prev.astype(bf16)], axis=0)
            ys = []
            sn = []
            for h in (2 * i, 2 * i + 1):
                acol = jnp.broadcast_to(a_cum[:, h:h + 1], (L, L))
                arow = jnp.broadcast_to(a_t[h:h + 1, :], (L, L))
                dtrow = jnp.broadcast_to(dt_t[h:h + 1, :], (L, L))
                dec = jnp.exp(jnp.where(causal, acol - arow, NEG_BIG))
                m = cb * dec * dtrow
                cs = c_g * jnp.broadcast_to(exp_a[:, h:h + 1], (L, L))
                lhs = jnp.concatenate([m.astype(bf16), cs.astype(bf16)], axis=1)
                ys.append(_dot(lhs, rhs))
                btw = (b_t * jnp.broadcast_to(w_t[h:h + 1, :], (L, L))).astype(bf16)
                sn.append(_dot(btw, xpair_b))
            h0 = 2 * i
            cd_pair = jnp.where(lane_row < SSD_HEAD_DIM, cd[:, h0:h0 + 1], cd[:, h0 + 1:h0 + 2])
            st_ref[i] = jnp.where(left, sn[0], sn[1]) + s_prev * cd_pair
            y_pair = jnp.where(left, ys[0], ys[1])
            y_ref[rs, i * LANES:(i + 1) * LANES] = y_pair + xpair * dexp_ref[:, i * LANES:(i + 1) * LANES]
            if pi % 2 == 1:
                yield

    gw = SSD_D_INNER // SSD_N_GROUPS
    for g in range(SSD_N_GROUPS):
        sl = slice(g * gw, (g + 1) * gw)
        yz = y_ref[rs, sl] * _silu(z_ref[rs, sl].astype(f32))
        ms = jnp.mean(yz * yz, axis=-1, keepdims=True)
        hn_ref[rs, sl] = (yz * lax.rsqrt(ms + RMS_EPS) * ng_ref[:, sl]).astype(bf16)
        if g % 2 == 1:
            yield


ROUTE_E, ROUTE_RANK, ROUTE_GATE = 0, 2, 4
ATT_SUB = 2
SSD_PER_STAGE = 3


def _mixer_kernel(sink_ref, q_ref, k_ref, v_ref, cos_ref, sin_ref, ga_ref, x_ref,
                  xbc_ref, z_ref, dt_ref, gs_ref,
                  sel_ref, qg_ref, kg_ref, wao_ref, wo_ref, n2g_ref, wrh_ref, wrl_ref, br_ref,
                  cw_ref, cb_ref, dtb_ref, alog_ref, dexp_ref, ng_ref, wssd_ref,
                  x1_ref, h2_ref, route_ref, route_t_ref, cnt_ref,
                  kprev_ref, vprev_ref, att_ref, cnt_scr, vd_ref,
                  ext_ref, st_ref, xs_ref, bm_ref, cm_ref, y_ref, hn_ref, ms_ref):
    Q = ATT_BLOCK
    b = pl.program_id(0)
    n = pl.program_id(1)

    @pl.when(n == 0)
    def _():
        kprev_ref[...] = jnp.zeros(kprev_ref.shape, f32)
        vprev_ref[...] = jnp.zeros(vprev_ref.shape, f32)
        ext_ref[0:CONV_HALO, :] = jnp.zeros((CONV_HALO, SSD_CONV_DIM), f32)
        st_ref[...] = jnp.zeros(st_ref.shape, f32)

    @pl.when((b == 0) & (n == 0))
    def _():
        cnt_scr[...] = jnp.zeros(cnt_scr.shape, f32)

    R = q_ref.shape[0]

    def ssd_all():
        for c in range(R // SSD_CHUNK):
            yield from _ssd_chunk_phases(slice(c * SSD_CHUNK, (c + 1) * SSD_CHUNK), xbc_ref, z_ref, dt_ref,
                                         cw_ref, cb_ref, dtb_ref, alog_ref, dexp_ref, ng_ref,
                                         ext_ref, st_ref, xs_ref, bm_ref, cm_ref, y_ref, hn_ref)

    ssd_gen = ssd_all()

    def ssd_step(k):
        for _ in range(k):
            next(ssd_gen, None)
    lane_q = lax.broadcasted_iota(jnp.int32, (Q, LANES), 1)
    row_q = lax.broadcasted_iota(jnp.int32, (Q, LANES), 0)
    left = lane_q < ATT_HEAD_DIM
    head_mean = jnp.where((row_q // ATT_HEAD_DIM) == (lane_q // ATT_HEAD_DIM),
                          1.0 / ATT_HEAD_DIM, 0.0).astype(bf16)
    lane2 = lax.broadcasted_iota(jnp.int32, (2 * Q, LANES), 1) < ATT_HEAD_DIM
    qg = ATT_N_HEADS // ATT_N_KV
    rows = qg * Q
    ri = lax.broadcasted_iota(jnp.int32, (rows, Q), 0) % Q
    cj = lax.broadcasted_iota(jnp.int32, (rows, Q), 1)
    upper = cj > ri
    half = ROPE_DIM // 2

    n_sb = R // Q
    nq = COL_Q // LANES
    nk = COL_KV // LANES
    chunks = []
    for sb in range(n_sb):
        rs = slice(sb * Q, (sb + 1) * Q)
        chunks += [q_ref[rs, c * LANES:(c + 1) * LANES].astype(f32) for c in range(nq)]
        chunks += [k_ref[rs, c * LANES:(c + 1) * LANES].astype(f32) for c in range(nk)]
    u_all = jnp.concatenate(chunks, axis=0)
    sq = u_all * u_all
    sq_hi = sq.astype(bf16)
    sq_lo = (sq - sq_hi.astype(f32)).astype(bf16)
    un_all = u_all * lax.rsqrt(_dot(sq_hi, head_mean) + _dot(sq_lo, head_mean) + RMS_EPS)
    ssd_step(SSD_PER_STAGE)

    terms = []
    for sb in range(n_sb):
        rs = slice(sb * Q, (sb + 1) * Q)
        cs = jnp.concatenate([cos_ref[:, rs], sin_ref[:, rs], jnp.ones((half, LANES), f32),
                              jnp.zeros((LANES - 3 * half, LANES), f32)], axis=0)
        terms += list(_split3(cs.T))
    pat_all = _dot(jnp.concatenate(terms, axis=0), sel_ref[...])

    pats, kds = [], []
    for sb in range(n_sb):
        rs = slice(sb * Q, (sb + 1) * Q)
        pat = pat_all[3 * sb * Q:(3 * sb + 1) * Q] + pat_all[(3 * sb + 1) * Q:(3 * sb + 2) * Q] \
            + pat_all[(3 * sb + 2) * Q:(3 * sb + 3) * Q]
        cpat = pat[:, 0:LANES]
        s1pat = pat[:, LANES:2 * LANES]
        s2pat = pat[:, 2 * LANES:3 * LANES]

        def norm_rope(idx, gpat):
            tn = un_all[idx * Q:(idx + 1) * Q] * gpat
            return (tn * cpat + pltpu.roll(tn, LANES - ROPE_DIM // 2, 1) * s1pat
                    + pltpu.roll(tn, ROPE_DIM // 2, 1) * s2pat)

        kd = []
        for cidx in range(nk):
            sl = slice(cidx * LANES, (cidx + 1) * LANES)
            k_cur = norm_rope(sb * (nq + nk) + nq + cidx, kg_ref[...])
            v_cur = v_ref[rs, sl].astype(f32)
            k_all = jnp.concatenate([kprev_ref[:, sl], k_cur], axis=0)
            v_all = jnp.concatenate([vprev_ref[:, sl], v_cur], axis=0)
            kprev_ref[:, sl] = k_cur
            vprev_ref[:, sl] = v_cur
            k_sw = pltpu.roll(k_all, ATT_HEAD_DIM, 1)
            v_sw = pltpu.roll(v_all, ATT_HEAD_DIM, 1)
            kd.append(jnp.where(lane2, k_all, k_sw).astype(bf16))
            kd.append(jnp.where(lane2, k_sw, k_all).astype(bf16))
            for half_i, vv in enumerate((jnp.where(lane2, v_all, v_sw), jnp.where(lane2, v_sw, v_all))):
                vd_ref[sb * ATT_N_KV + 2 * cidx + half_i] = vv.astype(bf16)

        pats.append((cpat, s1pat, s2pat))
        kds.append(kd)
        ssd_step(SSD_PER_STAGE)

    def rope_q(sb, cidx):
        cpat, s1pat, s2pat = pats[sb]
        tn = un_all[(sb * (nq + nk) + cidx) * Q:(sb * (nq + nk) + cidx + 1) * Q] * qg_ref[...]
        return (tn * cpat + pltpu.roll(tn, LANES - ROPE_DIM // 2, 1) * s1pat
                + pltpu.roll(tn, ROPE_DIM // 2, 1) * s2pat) * ATT_SCALE

    def score_tile(g):
        sb, h = divmod(g, ATT_N_KV)
        parts = []
        for cidx in (2 * h, 2 * h + 1):
            qc = rope_q(sb, cidx)
            parts.append(jnp.where(left, qc, 0.0).astype(bf16))
            parts.append(jnp.where(left, 0.0, qc).astype(bf16))
        lhs = jnp.concatenate(parts, axis=0)
        s_both = _dot_nt(lhs, kds[sb][h])
        s_prev = s_both[:, 0:Q]
        s_cur = s_both[:, Q:2 * Q]
        if sb == 0:
            return jnp.where(upper & (n > 0), s_prev, jnp.where(upper, NEG_BIG, s_cur))
        return jnp.where(upper, s_prev, s_cur)

    ones = jnp.ones((Q, Q), bf16)

    def softmax_block(s):
        sink = jnp.concatenate([jnp.full((Q, 1), sink_ref[i], f32) for i in range(ATT_N_HEADS)], axis=0)
        m = jnp.maximum(jnp.max(s, axis=-1, keepdims=True), sink)
        p = jnp.exp(s - m)
        p_hi = p.astype(bf16)
        p_lo = (p - p_hi.astype(f32)).astype(bf16)
        denom = _dot(p_hi, ones) + _dot(p_lo, ones) + jnp.exp(sink - m)
        return (p / denom).astype(bf16)

    def pv_tile(g, pf):
        sb, h = divmod(g, ATT_N_KV)
        rs = slice(sb * Q, (sb + 1) * Q)
        zero = jnp.zeros_like(pf)
        p_both = jnp.concatenate([jnp.where(upper, pf, zero), jnp.where(upper, zero, pf)], axis=1)
        o = _dot(p_both, vd_ref[g])
        for r in range(2):
            cidx = 2 * h + r
            att_ref[rs, cidx * LANES:(cidx + 1) * LANES] = jnp.where(
                left, o[(2 * r) * Q:(2 * r + 1) * Q], o[(2 * r + 1) * Q:(2 * r + 2) * Q]).astype(bf16)

    def epilogue(rs, cnt):
        E = rs.stop - rs.start
        lane = lax.broadcasted_iota(jnp.int32, (E, LANES), 1)
        y_att = _dot(att_ref[rs, :], wao_ref[...])
        merged = _sigmoid(ga_ref[rs, :].astype(f32)) * y_att + ms_ref[rs, :]
        x1 = x_ref[rs, :] + _dot(merged.astype(bf16), wo_ref[...])
        x1_ref[rs, :] = x1
        h2 = x1 * lax.rsqrt(jnp.mean(x1 * x1, axis=-1, keepdims=True) + RMS_EPS) * n2g_ref[...]
        h2_ref[rs, :] = _pack_rows(h2)

        hi = h2.astype(bf16)
        lo = (h2 - hi.astype(f32)).astype(bf16)
        logits = _dot(hi, wrh_ref[...]) + _dot(lo, wrh_ref[...]) + _dot(hi, wrl_ref[...]) + br_ref[...]
        big = 4 * LANES
        gl = jnp.where(lane < MOE_N_GROUPS, logits, NEG_BIG)
        gmax = jnp.max(gl, axis=-1, keepdims=True)
        gsel = jnp.min(jnp.where(gl == gmax, lane, big), axis=-1, keepdims=True)
        pg = 1.0 / jnp.sum(jnp.exp(gl - gmax), axis=-1, keepdims=True)
        lo_l = MOE_N_GROUPS + MOE_EPG * gsel
        el = jnp.where((lane >= lo_l) & (lane < lo_l + MOE_EPG), logits, NEG_BIG)
        v1 = jnp.max(el, axis=-1, keepdims=True)
        i1 = jnp.min(jnp.where(el == v1, lane, big), axis=-1, keepdims=True)
        el2 = jnp.where(lane == i1, NEG_BIG, el)
        v2 = jnp.max(el2, axis=-1, keepdims=True)
        i2 = jnp.min(jnp.where(el2 == v2, lane, big), axis=-1, keepdims=True)
        e1 = i1 - MOE_N_GROUPS
        e2 = i2 - MOE_N_GROUPS
        tt = jnp.exp(v2 - v1)
        w1 = pg * (1.0 / (1.0 + tt))
        w2 = pg * (tt / (1.0 + tt))

        onehot = jnp.where((lane == e1) | (lane == e2), 1.0, 0.0)
        strict = jnp.where(lax.broadcasted_iota(jnp.int32, (E, E), 0) > lax.broadcasted_iota(jnp.int32, (E, E), 1),
                           1.0, 0.0).astype(bf16)
        base = _dot(strict, onehot.astype(bf16)) + cnt
        r1 = jnp.sum(jnp.where(lane == e1, base, 0.0), axis=-1, keepdims=True)
        r2 = jnp.sum(jnp.where(lane == e2, base, 0.0), axis=-1, keepdims=True)

        rec = jnp.zeros((E, LANES), f32)
        for off, val in ((ROUTE_E, e1.astype(f32)), (ROUTE_E + 1, e2.astype(f32)),
                         (ROUTE_RANK, r1), (ROUTE_RANK + 1, r2), (ROUTE_GATE, w1), (ROUTE_GATE + 1, w2)):
            rec = jnp.where(lane == off, val, rec)
        route_ref[rs, :] = rec
        for o in range(0, E, Q):
            route_t_ref[:, rs.start + o:rs.start + o + Q] = rec[o:o + Q].T[0:8, :]
        return cnt + jnp.sum(onehot, axis=0, keepdims=True)

    s_tiles, p_tiles = {}, {}
    for stage in range(n_sb + 2):
        if stage < n_sb:
            s_tiles[stage] = jnp.concatenate(
                [score_tile(stage * ATT_N_KV + h) for h in range(ATT_N_KV)], axis=0)
            ssd_step(SSD_PER_STAGE)
        if 0 <= stage - 1 < n_sb:
            p_tiles[stage - 1] = softmax_block(s_tiles.pop(stage - 1))
            ssd_step(SSD_PER_STAGE)
        if 0 <= stage - 2 < n_sb:
            pb = p_tiles.pop(stage - 2)
            for h in range(ATT_N_KV):
                pv_tile((stage - 2) * ATT_N_KV + h, pb[h * rows:(h + 1) * rows])
            ssd_step(SSD_PER_STAGE)

    for _ in ssd_gen:
        pass
    ms_ref[...] = _sigmoid(gs_ref[...].astype(f32)) * _dot(hn_ref[...], wssd_ref[...])
    cnt = epilogue(slice(0, R), cnt_scr[0:1, :])

    cnt_scr[...] = jnp.broadcast_to(cnt, cnt_scr.shape)
    cnt_ref[...] = jnp.broadcast_to(cnt, cnt_ref.shape)


def _mixer(q, k, v, cos_t, sin_t, ga, x2, xbc, z, dt, gs, q_norm_g, k_norm_g, sinks, w_attn_out, w_out,
           norm2_g, w_rg, b_rg, w_re, b_re, conv_w, conv_b, dt_bias, a_log, d_skip, ssd_norm_g, w_ssd_out,
           batch, seq):
    t = batch * seq
    Q = ATT_SUB * ATT_BLOCK
    nb = seq // Q
    pad_h = LANES - SSD_N_HEADS
    dtb = jnp.pad(dt_bias, (0, pad_h)).reshape(1, LANES)
    alog = jnp.pad(a_log, (0, pad_h)).reshape(1, LANES)
    dexp = jnp.repeat(d_skip, SSD_HEAD_DIM).reshape(1, SSD_D_INNER)
    rep = LANES // ATT_HEAD_DIM
    qg = jnp.tile(q_norm_g, rep).reshape(1, LANES)
    kg = jnp.tile(k_norm_g, rep).reshape(1, LANES)
    n_log = MOE_N_GROUPS + MOE_N_EXPERTS
    w_r = jnp.concatenate([w_rg, jnp.transpose(w_re, (1, 0, 2)).reshape(D_MODEL, MOE_N_EXPERTS),
                           jnp.zeros((D_MODEL, LANES - n_log), f32)], axis=1)
    b_r = jnp.concatenate([b_rg, b_re.reshape(-1), jnp.zeros((LANES - n_log,), f32)]).reshape(1, LANES)
    w_r_hi = w_r.astype(bf16)
    w_r_lo = (w_r - w_r_hi.astype(f32)).astype(bf16)
    const = lambda b, n, *_: (0, 0)
    row = lambda b, n, *_: (b * nb + n, 0)
    col = lambda b, n, *_: (0, b * nb + n)
    full = lambda shape: pl.BlockSpec(shape, const, pipeline_mode=pl.Buffered(1))
    return pl.pallas_call(
        _mixer_kernel,
        grid_spec=pltpu.PrefetchScalarGridSpec(
            num_scalar_prefetch=1,
            grid=(batch, nb),
            in_specs=[pl.BlockSpec((Q, COL_Q), row),
                      pl.BlockSpec((Q, COL_KV), row),
                      pl.BlockSpec((Q, COL_KV), row),
                      pl.BlockSpec((ROPE_DIM // 2, Q), col),
                      pl.BlockSpec((ROPE_DIM // 2, Q), col),
                      pl.BlockSpec((Q, D_MODEL), row),
                      pl.BlockSpec((Q, D_MODEL), row),
                      pl.BlockSpec((Q, SSD_CONV_DIM), row),
                      pl.BlockSpec((Q, SSD_D_INNER), row),
                      pl.BlockSpec((Q, LANES), row),
                      pl.BlockSpec((Q, D_MODEL), row),
                      full((LANES, 3 * LANES)),
                      pl.BlockSpec((1, LANES), const),
                      pl.BlockSpec((1, LANES), const),
                      full((COL_Q, D_MODEL)),
                      full((D_MODEL, D_MODEL)),
                      pl.BlockSpec((1, D_MODEL), const),
                      full((D_MODEL, LANES)),
                      full((D_MODEL, LANES)),
                      pl.BlockSpec((1, LANES), const),
                      pl.BlockSpec((SSD_CONV, SSD_CONV_DIM), const),
                      pl.BlockSpec((1, SSD_CONV_DIM), const),
                      pl.BlockSpec((1, LANES), const),
                      pl.BlockSpec((1, LANES), const),
                      pl.BlockSpec((1, SSD_D_INNER), const),
                      pl.BlockSpec((1, SSD_D_INNER), const),
                      full((SSD_D_INNER, D_MODEL))],
            out_specs=[pl.BlockSpec((Q, D_MODEL), row),
                       pl.BlockSpec((Q, HALF_D), row),
                       pl.BlockSpec((Q, LANES), row),
                       pl.BlockSpec((8, Q), col),
                       pl.BlockSpec((8, LANES), const)],
            scratch_shapes=[pltpu.VMEM((ATT_BLOCK, COL_KV), f32),
                            pltpu.VMEM((ATT_BLOCK, COL_KV), f32),
                            pltpu.VMEM((Q, COL_Q), bf16),
                            pltpu.VMEM((8, LANES), f32),
                            pltpu.VMEM((ATT_SUB * ATT_N_KV, 2 * ATT_BLOCK, LANES), bf16),
                            pltpu.VMEM((2 * CONV_HALO, SSD_CONV_DIM), f32),
                            pltpu.VMEM((SSD_N_HEADS // 2, SSD_D_STATE, LANES), f32),
                            pltpu.VMEM((Q, SSD_D_INNER), f32),
                            pltpu.VMEM((Q, SSD_N_GROUPS * SSD_D_STATE), f32),
                            pltpu.VMEM((Q, SSD_N_GROUPS * SSD_D_STATE), f32),
                            pltpu.VMEM((Q, SSD_D_INNER), f32),
                            pltpu.VMEM((Q, SSD_D_INNER), bf16),
                            pltpu.VMEM((Q, D_MODEL), f32)],
        ),
        out_shape=[jax.ShapeDtypeStruct((t, D_MODEL), f32),
                   jax.ShapeDtypeStruct((t, HALF_D), jnp.uint32),
                   jax.ShapeDtypeStruct((t, LANES), f32),
                   jax.ShapeDtypeStruct((8, t), f32),
                   jax.ShapeDtypeStruct((8, LANES), f32)],
        compiler_params=pltpu.CompilerParams(dimension_semantics=("arbitrary", "arbitrary"),
                                             vmem_limit_bytes=VMEM_LIMIT),
        name="mixer",
    )(sinks, q, k, v, cos_t, sin_t, ga, x2, xbc, z, dt, gs,
      _rope_select(), qg, kg, w_attn_out.astype(bf16), w_out.astype(bf16), norm2_g.reshape(1, D_MODEL),
      w_r_hi, w_r_lo, b_r,
      conv_w, conv_b.reshape(1, -1), dtb, alog, dexp, ssd_norm_g.reshape(1, -1), w_ssd_out.astype(bf16))


MOE_TM = 256
MOE_UNROLL = 8


def _row_copy(src, i, dst, j, sem):
    return pltpu.make_async_copy(src.at[pl.ds(i, 1), :], dst.at[pl.ds(j, 1), :], sem)


def _dispatch_kernel(dest0_ref, dest1_ref, zblk_ref, h2_ref, xs_hbm, zero_ref, sem, zsem):
    tm = h2_ref.shape[0]
    base = pl.program_id(0) * tm

    @pl.when(pl.program_id(0) == 0)
    def _():
        zero_ref[...] = jnp.zeros(zero_ref.shape, zero_ref.dtype)

        def zcopy(i):
            start = pl.multiple_of(zblk_ref[i] * MOE_BLOCK, MOE_BLOCK)
            return pltpu.make_async_copy(zero_ref, xs_hbm.at[pl.ds(start, MOE_BLOCK), :], zsem)

        def zstart(i, carry):
            @pl.when(zblk_ref[i] >= 0)
            def _():
                zcopy(i).start()
            return carry

        def zwait(i, carry):
            @pl.when(zblk_ref[i] >= 0)
            def _():
                zcopy(i).wait()
            return carry

        lax.fori_loop(0, zblk_ref.shape[0], zstart, 0)
        lax.fori_loop(0, zblk_ref.shape[0], zwait, 0)

    def issue(i, carry):
        _row_copy(h2_ref, i, xs_hbm, dest0_ref[base + i], sem).start()
        _row_copy(h2_ref, i, xs_hbm, dest1_ref[base + i], sem).start()
        return carry

    lax.fori_loop(0, tm, issue, 0, unroll=MOE_UNROLL)
    for _ in range(2):
        pltpu.make_async_copy(h2_ref, xs_hbm.at[pl.ds(0, tm), :], sem).wait()


def _dispatch(h2, dest0, dest1, zero_blocks, n_rows):
    t = h2.shape[0]
    tm = min(MOE_TM, t)
    return pl.pallas_call(
        _dispatch_kernel,
        grid_spec=pltpu.PrefetchScalarGridSpec(
            num_scalar_prefetch=3,
            grid=(t // tm,),
            in_specs=[pl.BlockSpec((tm, HALF_D), lambda i, d0, d1, zb: (i, 0))],
            out_specs=pl.BlockSpec(memory_space=pl.ANY),
            scratch_shapes=[pltpu.VMEM((MOE_BLOCK, HALF_D), h2.dtype),
                            pltpu.SemaphoreType.DMA(()),
                            pltpu.SemaphoreType.DMA(())],
        ),
        out_shape=jax.ShapeDtypeStruct((n_rows, HALF_D), h2.dtype),
        compiler_params=pltpu.CompilerParams(dimension_semantics=("arbitrary",)),
        name="dispatch",
    )(dest0, dest1, zero_blocks, h2)


EXPERT_SUB = 2


def _expert_kernel(be_ref, nu_ref, xs_ref, *refs):
    w_refs = refs[:3 * EXPERT_SUB]
    ys_ref, wgu_scr, wd_scr = refs[3 * EXPERT_SUB:]
    i = pl.program_id(0)
    rows = MOE_BLOCK

    for j in range(EXPERT_SUB):
        blk = EXPERT_SUB * i + j
        fresh = (i == 0) | (be_ref[blk] != be_ref[jnp.maximum(blk - EXPERT_SUB, 0)])

        @pl.when((blk < nu_ref[0]) & fresh)
        def _(j=j):
            wg_ref, wu_ref, wd_ref = w_refs[3 * j:3 * j + 3]
            wgu_scr[j, :, 0:MOE_D_FF] = wg_ref[0].astype(bf16)
            wgu_scr[j, :, MOE_D_FF:2 * MOE_D_FF] = wu_ref[0].astype(bf16)
            wd_scr[j] = wd_ref[0].astype(bf16)

    def block(j):
        rs = slice(j * rows, (j + 1) * rows)
        x_lo, x_hi = _unpack_rows(xs_ref[rs, :])
        gu = (_dot(x_lo.astype(bf16), wgu_scr[j, 0:HALF_D, :])
              + _dot(x_hi.astype(bf16), wgu_scr[j, HALF_D:D_MODEL, :]))
        hid = _silu(gu[:, :MOE_D_FF]) * gu[:, MOE_D_FF:]
        ys_ref[rs, :] = _pack_rows(_dot(hid.astype(bf16), wd_scr[j]))

    n_live = jnp.clip(nu_ref[0] - EXPERT_SUB * i, 0, EXPERT_SUB)
    for live in range(EXPERT_SUB + 1):
        @pl.when(n_live == live)
        def _(live=live):
            for j in range(live):
                block(j)
            if live < EXPERT_SUB:
                ys_ref[live * rows:, :] = jnp.zeros(((EXPERT_SUB - live) * rows, HALF_D), ys_ref.dtype)


def _experts(xs, block_e, n_used, w_g, w_u, w_d):
    n_rows = xs.shape[0]
    rows = EXPERT_SUB * MOE_BLOCK
    nstep = n_rows // rows
    blk_in = lambda i, be, nu: (jnp.minimum(i, (nu[0] - 1) // EXPERT_SUB), 0)
    blk_out = lambda i, be, nu: (i, 0)
    w_specs = []
    for j in range(EXPERT_SUB):
        wsel = lambda i, be, nu, j=j: (be[EXPERT_SUB * i + j], 0, 0)
        w_specs += [pl.BlockSpec((1, D_MODEL, MOE_D_FF), wsel),
                    pl.BlockSpec((1, D_MODEL, MOE_D_FF), wsel),
                    pl.BlockSpec((1, MOE_D_FF, D_MODEL), wsel)]
    return pl.pallas_call(
        _expert_kernel,
        grid_spec=pltpu.PrefetchScalarGridSpec(
            num_scalar_prefetch=2,
            grid=(nstep,),
            in_specs=[pl.BlockSpec((rows, HALF_D), blk_in)] + w_specs,
            out_specs=pl.BlockSpec((rows, HALF_D), blk_out),
            scratch_shapes=[pltpu.VMEM((EXPERT_SUB, D_MODEL, 2 * MOE_D_FF), bf16),
                            pltpu.VMEM((EXPERT_SUB, MOE_D_FF, D_MODEL), bf16)],
        ),
        out_shape=jax.ShapeDtypeStruct((n_rows, HALF_D), xs.dtype),
        compiler_params=pltpu.CompilerParams(dimension_semantics=("arbitrary",),
                                             vmem_limit_bytes=VMEM_LIMIT),
        name="experts",
    )(block_e, n_used, xs, *([w_g, w_u, w_d] * EXPERT_SUB))


def _combine_kernel(dest0_ref, dest1_ref, x1_ref, route_ref, ys_hbm, out_ref, y0_ref, y1_ref, sem):
    tm = x1_ref.shape[0]
    step = pl.program_id(0)

    def gather_tile(tile, slot):
        base = tile * tm

        def issue(r, carry):
            pltpu.make_async_copy(ys_hbm.at[pl.ds(dest0_ref[base + r], 1), :],
                                  y0_ref.at[slot, pl.ds(r, 1), :], sem.at[slot]).start()
            pltpu.make_async_copy(ys_hbm.at[pl.ds(dest1_ref[base + r], 1), :],
                                  y1_ref.at[slot, pl.ds(r, 1), :], sem.at[slot]).start()
            return carry

        lax.fori_loop(0, tm, issue, 0, unroll=MOE_UNROLL)

    @pl.when(step == 0)
    def _():
        gather_tile(0, 0)

    @pl.when(step + 1 < pl.num_programs(0))
    def _():
        gather_tile(step + 1, (step + 1) % 2)

    slot = step % 2
    for y_ref in (y0_ref, y1_ref):
        pltpu.make_async_copy(ys_hbm.at[pl.ds(0, tm), :], y_ref.at[slot], sem.at[slot]).wait()
    g0 = route_ref[:, ROUTE_GATE:ROUTE_GATE + 1]
    g1 = route_ref[:, ROUTE_GATE + 1:ROUTE_GATE + 2]
    y0_lo, y0_hi = _unpack_rows(y0_ref[slot])
    y1_lo, y1_hi = _unpack_rows(y1_ref[slot])
    out_ref[:, 0:HALF_D] = x1_ref[:, 0:HALF_D] + y0_lo * g0 + y1_lo * g1
    out_ref[:, HALF_D:D_MODEL] = x1_ref[:, HALF_D:D_MODEL] + y0_hi * g0 + y1_hi * g1


def _combine(x1, route, ys, dest0, dest1):
    t = x1.shape[0]
    tm = min(MOE_TM, t)
    row = lambda i, d0, d1: (i, 0)
    return pl.pallas_call(
        _combine_kernel,
        grid_spec=pltpu.PrefetchScalarGridSpec(
            num_scalar_prefetch=2,
            grid=(t // tm,),
            in_specs=[pl.BlockSpec((tm, D_MODEL), row),
                      pl.BlockSpec((tm, LANES), row),
                      pl.BlockSpec(memory_space=pl.ANY)],
            out_specs=pl.BlockSpec((tm, D_MODEL), row),
            scratch_shapes=[pltpu.VMEM((2, tm, HALF_D), ys.dtype),
                            pltpu.VMEM((2, tm, HALF_D), ys.dtype),
                            pltpu.SemaphoreType.DMA((2,))],
        ),
        out_shape=jax.ShapeDtypeStruct((t, D_MODEL), f32),
        compiler_params=pltpu.CompilerParams(dimension_semantics=("arbitrary",)),
        name="combine",
    )(dest0, dest1, x1, route, ys)


def _moe(x1, h2, route, route_t, counts_rec, w_gate_e, w_up_e, w_down_e):
    t = x1.shape[0]
    n_rows = 2 * t + MOE_N_EXPERTS * MOE_BLOCK
    nblk = n_rows // MOE_BLOCK
    counts = counts_rec[0, :MOE_N_EXPERTS].astype(jnp.int32)
    padded = (counts + MOE_BLOCK - 1) // MOE_BLOCK * MOE_BLOCK
    pend = jnp.cumsum(padded)
    pstart = pend - padded
    expert_ids = jnp.arange(MOE_N_EXPERTS, dtype=jnp.int32)[:, None]

    def sorted_row(k):
        eid = route_t[ROUTE_E + k].astype(jnp.int32)
        rank = route_t[ROUTE_RANK + k].astype(jnp.int32)
        return jnp.sum(jnp.where(eid[None, :] == expert_ids, pstart[:, None], 0), axis=0) + rank

    dest0, dest1 = sorted_row(0), sorted_row(1)
    block_start = jnp.arange(nblk, dtype=jnp.int32) * MOE_BLOCK
    block_e = jnp.minimum(jnp.sum((pend[None, :] <= block_start[:, None]).astype(jnp.int32), axis=1),
                          MOE_N_EXPERTS - 1)
    n_used = (pend[-1:] // MOE_BLOCK).astype(jnp.int32)
    last_blk = jnp.where(counts > 0, pend // MOE_BLOCK - 1, -1)
    tail = n_used[0] + jnp.arange(MOE_N_EXPERTS, dtype=jnp.int32)
    zero_blocks = jnp.concatenate([last_blk, jnp.where(tail < nblk, tail, -1)]).astype(jnp.int32)
    xs = _dispatch(h2, dest0, dest1, zero_blocks, n_rows)
    ys = _experts(xs, block_e, n_used, w_gate_e, w_up_e, w_down_e)
    return _combine(x1, route, ys, dest0, dest1)


def _layer(x, positions, norm1_g, w_in, conv_w, conv_b, dt_bias, a_log, d_skip, ssd_norm_g, w_ssd_out,
           q_norm_g, k_norm_g, sinks, w_attn_out, w_out, norm2_g, w_rg, b_rg, w_re, b_re,
           w_gate_e, w_up_e, w_down_e):
    batch, seq, d = x.shape
    x2 = x.reshape(batch * seq, d)
    cos_t, sin_t = _rope_tables(positions)
    z, xbc, q, k, v, gs, ga, dt = _inproj(x2, norm1_g, w_in)
    x1, h2, route, route_t, counts = _mixer(q, k, v, cos_t, sin_t, ga, x2, xbc, z, dt, gs, q_norm_g, k_norm_g,
                                            sinks, w_attn_out, w_out, norm2_g, w_rg, b_rg, w_re, b_re,
                                            conv_w, conv_b, dt_bias, a_log, d_skip, ssd_norm_g, w_ssd_out,
                                            batch, seq)
    out = _moe(x1, h2, route, route_t, counts, w_gate_e, w_up_e, w_down_e)
    return out.reshape(batch, seq, d)


def kernel(x, positions, norm1_g, w_in, conv_w, conv_b, dt_bias, a_log, d_skip, ssd_norm_g, w_ssd_out,
           q_norm_g, k_norm_g, sinks, w_attn_out, w_out, norm2_g, w_router_group, b_router_group,
           w_router_expert, b_router_expert, w_gate_e, w_up_e, w_down_e):
    for l in range(norm1_g.shape[0]):
        x = _layer(x, positions, norm1_g[l], w_in[l], conv_w[l], conv_b[l], dt_bias[l], a_log[l],
                   d_skip[l], ssd_norm_g[l], w_ssd_out[l], q_norm_g[l], k_norm_g[l], sinks[l],
                   w_attn_out[l], w_out[l], norm2_g[l], w_router_group[l], b_router_group[l],
                   w_router_expert[l], b_router_expert[l], w_gate_e[l], w_up_e[l], w_down_e[l])
    return x
```

```python
import functools

import numpy as np

import jax
import jax.numpy as jnp
from jax import lax
from jax.experimental import pallas as pl
from jax.experimental.pallas import tpu as pltpu

f32 = jnp.float32
bf16 = jnp.bfloat16

D_MODEL = 1024
SSD_D_INNER = 2048
SSD_HEAD_DIM = 64
SSD_N_HEADS = 32
SSD_N_GROUPS = 4
SSD_D_STATE = 128
SSD_CONV = 4
SSD_CHUNK = 128
SSD_CONV_DIM = 3072
ATT_HEAD_DIM = 64
ATT_N_HEADS = 16
ATT_N_KV = 4
ATT_BLOCK = 128
ATT_SCALE = ATT_HEAD_DIM ** -0.5
ROPE_THETA = 500000.0
ROPE_DIM = 16
MOE_N_GROUPS = 8
MOE_EPG = 8
MOE_N_EXPERTS = 64
MOE_D_FF = 256
MOE_BLOCK = 256
RMS_EPS = 1e-6

LANES = 128
CONV_HALO = 8
NEG_BIG = -1e30
VMEM_LIMIT = 56 * 1024 * 1024

COL_Z = SSD_D_INNER
COL_XBC = SSD_CONV_DIM
COL_DT = SSD_N_HEADS
COL_Q = ATT_N_HEADS * ATT_HEAD_DIM
COL_KV = ATT_N_KV * ATT_HEAD_DIM


def _sigmoid(x):
    return 1.0 / (1.0 + jnp.exp(-x))


def _silu(x):
    return x * _sigmoid(x)


def _split3(x):
    hi = x.astype(bf16)
    r1 = x - hi.astype(f32)
    mid = r1.astype(bf16)
    lo = (r1 - mid.astype(f32)).astype(bf16)
    return hi, mid, lo


HALF_D = D_MODEL // 2
_HI_MASK = np.uint32(0xFFFF0000)


def _pack_rows(x):
    bits = pltpu.bitcast(x.astype(bf16).astype(f32), jnp.uint32)
    return (bits[:, HALF_D:] & _HI_MASK) | (bits[:, :HALF_D] >> 16)


def _unpack_rows(p):
    return pltpu.bitcast(p << 16, f32), pltpu.bitcast(p & _HI_MASK, f32)


def _dot(a, b):
    return jnp.dot(a, b, preferred_element_type=f32)


def _dot_nt(a, b):
    return lax.dot_general(a, b, (((1,), (1,)), ((), ())), preferred_element_type=f32)


def _trig_kernel(freq_ref, pos_ref, cos_ref, sin_ref):
    ang = pos_ref[...].astype(f32) * freq_ref[pl.program_id(0)]
    cos_ref[0] = jnp.cos(ang)
    sin_ref[0] = jnp.sin(ang)


def _rope_select():
    half = ROPE_DIM // 2
    sel = np.zeros((LANES, 3 * LANES), np.float32)
    for lane in range(LANES):
        m = lane % ATT_HEAD_DIM
        if m < half:
            sel[m, lane] = 1.0
            sel[half + m, LANES + lane] = -1.0
        elif m < ROPE_DIM:
            sel[m - half, lane] = 1.0
            sel[m, 2 * LANES + lane] = 1.0
        else:
            sel[ROPE_DIM, lane] = 1.0
    return jnp.asarray(sel, bf16)


def _rope_tables(positions):
    t = positions.size
    half = ROPE_DIM // 2
    inv_freq = ROPE_THETA ** (-jnp.arange(0, ROPE_DIM, 2, dtype=f32) / ROPE_DIM)
    pos2d = positions.reshape(t // LANES, LANES)
    cos_t, sin_t = pl.pallas_call(
        _trig_kernel,
        grid_spec=pltpu.PrefetchScalarGridSpec(
            num_scalar_prefetch=1,
            grid=(half,),
            in_specs=[pl.BlockSpec((t // LANES, LANES), lambda j, f: (0, 0))],
            out_specs=[pl.BlockSpec((1, t // LANES, LANES), lambda j, f: (j, 0, 0))] * 2,
        ),
        out_shape=[jax.ShapeDtypeStruct((half, t // LANES, LANES), f32)] * 2,
        name="trig",
    )(inv_freq, pos2d)
    return cos_t.reshape(half, t), sin_t.reshape(half, t)


INPROJ_TM = 512
INPROJ_CH = 512


def _inproj_kernel(x_ref, g_ref, w_ref,
                   z_ref, xbc_ref, q_ref, k_ref, v_ref, gs_ref, ga_ref, dt_ref, h_scr):
    x = x_ref[...]
    ms = jnp.mean(x * x, axis=-1, keepdims=True)
    h_scr[...] = (x * lax.rsqrt(ms + RMS_EPS) * g_ref[...]).astype(bf16)
    off = 0
    for ref in (z_ref, xbc_ref, q_ref, k_ref, v_ref, gs_ref, ga_ref, dt_ref):
        width = ref.shape[1]
        for c in range(0, width, INPROJ_CH):
            cw = min(INPROJ_CH, width - c)
            ref[:, c:c + cw] = _dot(h_scr[...], w_ref[:, off + c:off + c + cw]).astype(ref.dtype)
        off += width


WPREP_ROWS = 128


def _wprep_kernel(w_ref, o_ref):
    s1 = COL_Z + COL_XBC
    s2 = s1 + COL_DT
    n_tail = w_ref.shape[1] - s2
    o_ref[:, 0:s1] = w_ref[:, 0:s1].astype(bf16)
    o_ref[:, s1:s1 + n_tail] = w_ref[:, s2:s2 + n_tail].astype(bf16)
    lane = lax.broadcasted_iota(jnp.int32, (w_ref.shape[0], LANES), 1)
    o_ref[:, s1 + n_tail:s1 + n_tail + LANES] = jnp.where(lane < COL_DT, w_ref[:, s1:s1 + LANES], 0.0).astype(bf16)


def _inproj(x2, norm1_g, w_in):
    t = x2.shape[0]
    tm = min(INPROJ_TM, t)
    n_in = w_in.shape[1]
    n_all = n_in - COL_DT + LANES
    w_all = pl.pallas_call(
        _wprep_kernel,
        grid=(D_MODEL // WPREP_ROWS,),
        in_specs=[pl.BlockSpec((WPREP_ROWS, n_in), lambda i: (i, 0))],
        out_specs=pl.BlockSpec((WPREP_ROWS, n_all), lambda i: (i, 0)),
        out_shape=jax.ShapeDtypeStruct((D_MODEL, n_all), bf16),
        name="wprep",
    )(w_in)
    widths = (COL_Z, COL_XBC, COL_Q, COL_KV, COL_KV, D_MODEL, D_MODEL)
    const = lambda i: (0, 0)
    row = lambda i: (i, 0)
    outs = pl.pallas_call(
        _inproj_kernel,
        grid=(t // tm,),
        in_specs=[pl.BlockSpec((tm, D_MODEL), row),
                  pl.BlockSpec((1, D_MODEL), const),
                  pl.BlockSpec((D_MODEL, w_all.shape[1]), const, pipeline_mode=pl.Buffered(1))],
        out_specs=[pl.BlockSpec((tm, w), row) for w in widths] + [pl.BlockSpec((tm, LANES), row)],
        out_shape=[jax.ShapeDtypeStruct((t, w), bf16) for w in widths]
                  + [jax.ShapeDtypeStruct((t, LANES), f32)],
        scratch_shapes=[pltpu.VMEM((tm, D_MODEL), bf16)],
        compiler_params=pltpu.CompilerParams(dimension_semantics=("arbitrary",),
                                             vmem_limit_bytes=VMEM_LIMIT),
        name="inproj",
    )(x2, norm1_g.reshape(1, D_MODEL), w_all)
    return outs


SSD_CONV_YIELD = 1024

def _ssd_chunk_phases(rs, xbc_ref, z_ref, dt_ref, cw_ref, cb_ref, dtb_ref, alog_ref, dexp_ref, ng_ref,
                      ext_ref, st_ref, xs_ref, bm_ref, cm_ref, y_ref, hn_ref):
    L = SSD_CHUNK
    r0 = rs.start

    bf_tile = 2 * CONV_HALO
    ext_ref[CONV_HALO:2 * CONV_HALO, :] = xbc_ref[r0:r0 + bf_tile, :].astype(f32)[0:CONV_HALO]
    n_sh = SSD_CONV - 1
    sr = lax.broadcasted_iota(jnp.int32, (n_sh * L, L), 0)
    sc = lax.broadcasted_iota(jnp.int32, (n_sh * L, L), 1)
    shift = jnp.where((sr % L) - sc == (sr // L) + 1, 1.0, 0.0).astype(bf16)
    cch = 128
    for cc in range(0, SSD_CONV_DIM, cch):
        if cc and cc % SSD_CONV_YIELD == 0:
            yield
        cs_ = slice(cc, cc + cch)
        xb = xbc_ref[rs, cs_]
        sh = _dot(shift, xb)
        w_now = cw_ref[SSD_CONV - 1:SSD_CONV, cs_]
        acc = cb_ref[:, cs_] + xb.astype(f32) * w_now
        top = cb_ref[:, cs_] + ext_ref[CONV_HALO:2 * CONV_HALO, cs_] * w_now
        for j in range(1, SSD_CONV):
            w_j = cw_ref[SSD_CONV - 1 - j:SSD_CONV - j, cs_]
            acc = acc + sh[(j - 1) * L:j * L] * w_j
            top = top + ext_ref[CONV_HALO - j:2 * CONV_HALO - j, cs_] * w_j
        if cc < SSD_D_INNER:
            dst, o = xs_ref, cc
        elif cc < SSD_D_INNER + SSD_N_GROUPS * SSD_D_STATE:
            dst, o = bm_ref, cc - SSD_D_INNER
        else:
            dst, o = cm_ref, cc - SSD_D_INNER - SSD_N_GROUPS * SSD_D_STATE
        dst[rs, o:o + cch] = _silu(acc)
        dst[r0:r0 + CONV_HALO, o:o + cch] = _silu(top)
    ext_ref[0:CONV_HALO, :] = xbc_ref[rs.stop - bf_tile:rs.stop, :].astype(f32)[CONV_HALO:bf_tile]
    yield

    lane_row = lax.broadcasted_iota(jnp.int32, (1, LANES), 1)
    row_i = lax.broadcasted_iota(jnp.int32, (L, L), 0)
    col_i = lax.broadcasted_iota(jnp.int32, (L, L), 1)
    causal = row_i >= col_i
    left = col_i < SSD_HEAD_DIM

    xdt = dt_ref[rs, :] + dtb_ref[...]
    dtv = jnp.maximum(xdt, 0.0) + jnp.log1p(jnp.exp(-jnp.abs(xdt)))
    a = jnp.where(lane_row < SSD_N_HEADS, -jnp.exp(alog_ref[...]), 0.0)
    d_a = dtv * a
    tril = jnp.where(causal, 1.0, 0.0).astype(bf16)
    hi, mid, lo3 = _split3(d_a)
    a_cum = _dot(tril, hi) + _dot(tril, mid) + _dot(tril, lo3)
    a_end = a_cum[L - 1:L, :]
    exp_a = jnp.exp(a_cum)
    w_end = jnp.exp(a_end - a_cum) * dtv
    cd = jnp.exp(a_end)
    a_t = a_cum.T
    dt_t = dtv.T
    w_t = w_end.T
    yield

    n_pairs = SSD_N_HEADS // 2
    pairs_per_group = n_pairs // SSD_N_GROUPS
    for g in range(SSD_N_GROUPS):
        b_g = bm_ref[rs, g * SSD_D_STATE:(g + 1) * SSD_D_STATE]
        c_g = cm_ref[rs, g * SSD_D_STATE:(g + 1) * SSD_D_STATE]
        cb = _dot_nt(c_g.astype(bf16), b_g.astype(bf16))
        b_t = b_g.T
        for pi in range(pairs_per_group):
            i = g * pairs_per_group + pi
            xpair = xs_ref[rs, i * LANES:(i + 1) * LANES]
            xpair_b = xpair.astype(bf16)
            s_prev = st_ref[i]
            rhs = jnp.concatenate([xpair_b, s_prev.astype(bf16)], axis=0)
            ys = []
            sn = []
            for h in (2 * i, 2 * i + 1):
                acol = jnp.broadcast_to(a_cum[:, h:h + 1], (L, L))
                arow = jnp.broadcast_to(a_t[h:h + 1, :], (L, L))
                dtrow = jnp.broadcast_to(dt_t[h:h + 1, :], (L, L))
                dec = jnp.exp(jnp.where(causal, acol - arow, NEG_BIG))
                m = cb * dec * dtrow
                cs = c_g * jnp.broadcast_to(exp_a[:, h:h + 1], (L, L))
                lhs = jnp.concatenate([m.astype(bf16), cs.astype(bf16)], axis=1)
                ys.append(_dot(lhs, rhs))
                btw = (b_t * jnp.broadcast_to(w_t[h:h + 1, :], (L, L))).astype(bf16)
                sn.append(_dot(btw, xpair_b))
            h0 = 2 * i
            cd_pair = jnp.where(lane_row < SSD_HEAD_DIM, cd[:, h0:h0 + 1], cd[:, h0 + 1:h0 + 2])
            st_ref[i] = jnp.where(left, sn[0], sn[1]) + s_prev * cd_pair
            y_pair = jnp.where(left, ys[0], ys[1])
            y_ref[rs, i * LANES:(i + 1) * LANES] = y_pair + xpair * dexp_ref[:, i * LANES:(i + 1) * LANES]
            if pi % 2 == 1:
                yield

    gw = SSD_D_INNER // SSD_N_GROUPS
    for g in range(SSD_N_GROUPS):
        sl = slice(g * gw, (g + 1) * gw)
        yz = y_ref[rs, sl] * _silu(z_ref[rs, sl].astype(f32))
        ms = jnp.mean(yz * yz, axis=-1, keepdims=True)
        hn_ref[rs, sl] = (yz * lax.rsqrt(ms + RMS_EPS) * ng_ref[:, sl]).astype(bf16)
        if g % 2 == 1:
            yield


ROUTE_E, ROUTE_RANK, ROUTE_GATE = 0, 2, 4
ATT_SUB = 2
SSD_PER_STAGE = 3


def _mixer_kernel(sink_ref, q_ref, k_ref, v_ref, cos_ref, sin_ref, ga_ref, x_ref,
                  xbc_ref, z_ref, dt_ref, gs_ref,
                  sel_ref, qg_ref, kg_ref, wao_ref, wo_ref, n2g_ref, wrh_ref, wrl_ref, br_ref,
                  cw_ref, cb_ref, dtb_ref, alog_ref, dexp_ref, ng_ref, wssd_ref,
                  x1_ref, h2_ref, route_ref, route_t_ref, cnt_ref,
                  kprev_ref, vprev_ref, att_ref, cnt_scr, vd_ref,
                  ext_ref, st_ref, xs_ref, bm_ref, cm_ref, y_ref, hn_ref, ms_ref):
    Q = ATT_BLOCK
    b = pl.program_id(0)
    n = pl.program_id(1)

    @pl.when(n == 0)
    def _():
        kprev_ref[...] = jnp.zeros(kprev_ref.shape, f32)
        vprev_ref[...] = jnp.zeros(vprev_ref.shape, f32)
        ext_ref[0:CONV_HALO, :] = jnp.zeros((CONV_HALO, SSD_CONV_DIM), f32)
        st_ref[...] = jnp.zeros(st_ref.shape, f32)

    @pl.when((b == 0) & (n == 0))
    def _():
        cnt_scr[...] = jnp.zeros(cnt_scr.shape, f32)

    R = q_ref.shape[0]

    def ssd_all():
        for c in range(R // SSD_CHUNK):
            yield from _ssd_chunk_phases(slice(c * SSD_CHUNK, (c + 1) * SSD_CHUNK), xbc_ref, z_ref, dt_ref,
                                         cw_ref, cb_ref, dtb_ref, alog_ref, dexp_ref, ng_ref,
                                         ext_ref, st_ref, xs_ref, bm_ref, cm_ref, y_ref, hn_ref)

    ssd_gen = ssd_all()

    def ssd_step(k):
        for _ in range(k):
            next(ssd_gen, None)
    lane_q = lax.broadcasted_iota(jnp.int32, (Q, LANES), 1)
    row_q = lax.broadcasted_iota(jnp.int32, (Q, LANES), 0)
    left = lane_q < ATT_HEAD_DIM
    head_mean = jnp.where((row_q // ATT_HEAD_DIM) == (lane_q // ATT_HEAD_DIM),
                          1.0 / ATT_HEAD_DIM, 0.0).astype(bf16)
    lane2 = lax.broadcasted_iota(jnp.int32, (2 * Q, LANES), 1) < ATT_HEAD_DIM
    qg = ATT_N_HEADS // ATT_N_KV
    rows = qg * Q
    ri = lax.broadcasted_iota(jnp.int32, (rows, Q), 0) % Q
    cj = lax.broadcasted_iota(jnp.int32, (rows, Q), 1)
    upper = cj > ri
    half = ROPE_DIM // 2

    n_sb = R // Q
    nq = COL_Q // LANES
    nk = COL_KV // LANES
    chunks = []
    for sb in range(n_sb):
        rs = slice(sb * Q, (sb + 1) * Q)
        chunks += [q_ref[rs, c * LANES:(c + 1) * LANES].astype(f32) for c in range(nq)]
        chunks += [k_ref[rs, c * LANES:(c + 1) * LANES].astype(f32) for c in range(nk)]
    u_all = jnp.concatenate(chunks, axis=0)
    sq = u_all * u_all
    sq_hi = sq.astype(bf16)
    sq_lo = (sq - sq_hi.astype(f32)).astype(bf16)
    un_all = u_all * lax.rsqrt(_dot(sq_hi, head_mean) + _dot(sq_lo, head_mean) + RMS_EPS)
    ssd_step(SSD_PER_STAGE)

    terms = []
    for sb in range(n_sb):
        rs = slice(sb * Q, (sb + 1) * Q)
        cs = jnp.concatenate([cos_ref[:, rs], sin_ref[:, rs], jnp.ones((half, LANES), f32),
                              jnp.zeros((LANES - 3 * half, LANES), f32)], axis=0)
        terms += list(_split3(cs.T))
    pat_all = _dot(jnp.concatenate(terms, axis=0), sel_ref[...])

    pats, kds = [], []
    for sb in range(n_sb):
        rs = slice(sb * Q, (sb + 1) * Q)
        pat = pat_all[3 * sb * Q:(3 * sb + 1) * Q] + pat_all[(3 * sb + 1) * Q:(3 * sb + 2) * Q] \
            + pat_all[(3 * sb + 2) * Q:(3 * sb + 3) * Q]
        cpat = pat[:, 0:LANES]
        s1pat = pat[:, LANES:2 * LANES]
        s2pat = pat[:, 2 * LANES:3 * LANES]

        def norm_rope(idx, gpat):
            tn = un_all[idx * Q:(idx + 1) * Q] * gpat
            return (tn * cpat + pltpu.roll(tn, LANES - ROPE_DIM // 2, 1) * s1pat
                    + pltpu.roll(tn, ROPE_DIM // 2, 1) * s2pat)

        kd = []
        for cidx in range(nk):
            sl = slice(cidx * LANES, (cidx + 1) * LANES)
            k_cur = norm_rope(sb * (nq + nk) + nq + cidx, kg_ref[...])
            v_cur = v_ref[rs, sl].astype(f32)
            k_all = jnp.concatenate([kprev_ref[:, sl], k_cur], axis=0)
            v_all = jnp.concatenate([vprev_ref[:, sl], v_cur], axis=0)
            kprev_ref[:, sl] = k_cur
            vprev_ref[:, sl] = v_cur
            k_sw = pltpu.roll(k_all, ATT_HEAD_DIM, 1)
            v_sw = pltpu.roll(v_all, ATT_HEAD_DIM, 1)
            kd.append(jnp.where(lane2, k_all, k_sw).astype(bf16))
            kd.append(jnp.where(lane2, k_sw, k_all).astype(bf16))
            for half_i, vv in enumerate((jnp.where(lane2, v_all, v_sw), jnp.where(lane2, v_sw, v_all))):
                vd_ref[sb * ATT_N_KV + 2 * cidx + half_i] = vv.astype(bf16)

        pats.append((cpat, s1pat, s2pat))
        kds.append(kd)
        ssd_step(SSD_PER_STAGE)

    def rope_q(sb, cidx):
        cpat, s1pat, s2pat = pats[sb]
        tn = un_all[(sb * (nq + nk) + cidx) * Q:(sb * (nq + nk) + cidx + 1) * Q] * qg_ref[...]
        return (tn * cpat + pltpu.roll(tn, LANES - ROPE_DIM // 2, 1) * s1pat
                + pltpu.roll(tn, ROPE_DIM // 2, 1) * s2pat) * ATT_SCALE

    def score_tile(g):
        sb, h = divmod(g, ATT_N_KV)
        parts = []
        for cidx in (2 * h, 2 * h + 1):
            qc = rope_q(sb, cidx)
            parts.append(jnp.where(left, qc, 0.0).astype(bf16))
            parts.append(jnp.where(left, 0.0, qc).astype(bf16))
        lhs = jnp.concatenate(parts, axis=0)
        s_both = _dot_nt(lhs, kds[sb][h])
        s_prev = s_both[:, 0:Q]
        s_cur = s_both[:, Q:2 * Q]
        if sb == 0:
            return jnp.where(upper & (n > 0), s_prev, jnp.where(upper, NEG_BIG, s_cur))
        return jnp.where(upper, s_prev, s_cur)

    ones = jnp.ones((Q, Q), bf16)

    def softmax_block(s):
        sink = jnp.concatenate([jnp.full((Q, 1), sink_ref[i], f32) for i in range(ATT_N_HEADS)], axis=0)
        m = jnp.maximum(jnp.max(s, axis=-1, keepdims=True), sink)
        p = jnp.exp(s - m)
        p_hi = p.astype(bf16)
        p_lo = (p - p_hi.astype(f32)).astype(bf16)
        denom = _dot(p_hi, ones) + _dot(p_lo, ones) + jnp.exp(sink - m)
        return (p / denom).astype(bf16)

    def pv_tile(g, pf):
        sb, h = divmod(g, ATT_N_KV)
        rs = slice(sb * Q, (sb + 1) * Q)
        zero = jnp.zeros_like(pf)
        p_both = jnp.concatenate([jnp.where(upper, pf, zero), jnp.where(upper, zero, pf)], axis=1)
        o = _dot(p_both, vd_ref[g])
        for r in range(2):
            cidx = 2 * h + r
            att_ref[rs, cidx * LANES:(cidx + 1) * LANES] = jnp.where(
                left, o[(2 * r) * Q:(2 * r + 1) * Q], o[(2 * r + 1) * Q:(2 * r + 2) * Q]).astype(bf16)

    def epilogue(rs, cnt):
        E = rs.stop - rs.start
        lane = lax.broadcasted_iota(jnp.int32, (E, LANES), 1)
        y_att = _dot(att_ref[rs, :], wao_ref[...])
        merged = _sigmoid(ga_ref[rs, :].astype(f32)) * y_att + ms_ref[rs, :]
        x1 = x_ref[rs, :] + _dot(merged.astype(bf16), wo_ref[...])
        x1_ref[rs, :] = x1
        h2 = x1 * lax.rsqrt(jnp.mean(x1 * x1, axis=-1, keepdims=True) + RMS_EPS) * n2g_ref[...]
        h2_ref[rs, :] = _pack_rows(h2)

        hi = h2.astype(bf16)
        lo = (h2 - hi.astype(f32)).astype(bf16)
        logits = _dot(hi, wrh_ref[...]) + _dot(lo, wrh_ref[...]) + _dot(hi, wrl_ref[...]) + br_ref[...]
        big = 4 * LANES
        gl = jnp.where(lane < MOE_N_GROUPS, logits, NEG_BIG)
        gmax = jnp.max(gl, axis=-1, keepdims=True)
        gsel = jnp.min(jnp.where(gl == gmax, lane, big), axis=-1, keepdims=True)
        pg = 1.0 / jnp.sum(jnp.exp(gl - gmax), axis=-1, keepdims=True)
        lo_l = MOE_N_GROUPS + MOE_EPG * gsel
        el = jnp.where((lane >= lo_l) & (lane < lo_l + MOE_EPG), logits, NEG_BIG)
        v1 = jnp.max(el, axis=-1, keepdims=True)
        i1 = jnp.min(jnp.where(el == v1, lane, big), axis=-1, keepdims=True)
        el2 = jnp.where(lane == i1, NEG_BIG, el)
        v2 = jnp.max(el2, axis=-1, keepdims=True)
        i2 = jnp.min(jnp.where(el2 == v2, lane, big), axis=-1, keepdims=True)
        e1 = i1 - MOE_N_GROUPS
        e2 = i2 - MOE_N_GROUPS
        tt = jnp.exp(v2 - v1)
        w1 = pg * (1.0 / (1.0 + tt))
        w2 = pg * (tt / (1.0 + tt))

        onehot = jnp.where((lane == e1) | (lane == e2), 1.0, 0.0)
        strict = jnp.where(lax.broadcasted_iota(jnp.int32, (E, E), 0) > lax.broadcasted_iota(jnp.int32, (E, E), 1),
                           1.0, 0.0).astype(bf16)
        base = _dot(strict, onehot.astype(bf16)) + cnt
        r1 = jnp.sum(jnp.where(lane == e1, base, 0.0), axis=-1, keepdims=True)
        r2 = jnp.sum(jnp.where(lane == e2, base, 0.0), axis=-1, keepdims=True)

        rec = jnp.zeros((E, LANES), f32)
        for off, val in ((ROUTE_E, e1.astype(f32)), (ROUTE_E + 1, e2.astype(f32)),
                         (ROUTE_RANK, r1), (ROUTE_RANK + 1, r2), (ROUTE_GATE, w1), (ROUTE_GATE + 1, w2)):
            rec = jnp.where(lane == off, val, rec)
        route_ref[rs, :] = rec
        for o in range(0, E, Q):
            route_t_ref[:, rs.start + o:rs.start + o + Q] = rec[o:o + Q].T[0:8, :]
        return cnt + jnp.sum(onehot, axis=0, keepdims=True)

    s_tiles, p_tiles = {}, {}
    for stage in range(n_sb + 2):
        if stage < n_sb:
            s_tiles[stage] = jnp.concatenate(
                [score_tile(stage * ATT_N_KV + h) for h in range(ATT_N_KV)], axis=0)
            ssd_step(SSD_PER_STAGE)
        if 0 <= stage - 1 < n_sb:
            p_tiles[stage - 1] = softmax_block(s_tiles.pop(stage - 1))
            ssd_step(SSD_PER_STAGE)
        if 0 <= stage - 2 < n_sb:
            pb = p_tiles.pop(stage - 2)
            for h in range(ATT_N_KV):
                pv_tile((stage - 2) * ATT_N_KV + h, pb[h * rows:(h + 1) * rows])
            ssd_step(SSD_PER_STAGE)

    for _ in ssd_gen:
        pass
    ms_ref[...] = _sigmoid(gs_ref[...].astype(f32)) * _dot(hn_ref[...], wssd_ref[...])
    cnt = epilogue(slice(0, R), cnt_scr[0:1, :])

    cnt_scr[...] = jnp.broadcast_to(cnt, cnt_scr.shape)
    cnt_ref[...] = jnp.broadcast_to(cnt, cnt_ref.shape)


def _mixer(q, k, v, cos_t, sin_t, ga, x2, xbc, z, dt, gs, q_norm_g, k_norm_g, sinks, w_attn_out, w_out,
           norm2_g, w_rg, b_rg, w_re, b_re, conv_w, conv_b, dt_bias, a_log, d_skip, ssd_norm_g, w_ssd_out,
           batch, seq):
    t = batch * seq
    Q = ATT_SUB * ATT_BLOCK
    nb = seq // Q
    pad_h = LANES - SSD_N_HEADS
    dtb = jnp.pad(dt_bias, (0, pad_h)).reshape(1, LANES)
    alog = jnp.pad(a_log, (0, pad_h)).reshape(1, LANES)
    dexp = jnp.repeat(d_skip, SSD_HEAD_DIM).reshape(1, SSD_D_INNER)
    rep = LANES // ATT_HEAD_DIM
    qg = jnp.tile(q_norm_g, rep).reshape(1, LANES)
    kg = jnp.tile(k_norm_g, rep).reshape(1, LANES)
    n_log = MOE_N_GROUPS + MOE_N_EXPERTS
    w_r = jnp.concatenate([w_rg, jnp.transpose(w_re, (1, 0, 2)).reshape(D_MODEL, MOE_N_EXPERTS),
                           jnp.zeros((D_MODEL, LANES - n_log), f32)], axis=1)
    b_r = jnp.concatenate([b_rg, b_re.reshape(-1), jnp.zeros((LANES - n_log,), f32)]).reshape(1, LANES)
    w_r_hi = w_r.astype(bf16)
    w_r_lo = (w_r - w_r_hi.astype(f32)).astype(bf16)
    const = lambda b, n, *_: (0, 0)
    row = lambda b, n, *_: (b * nb + n, 0)
    col = lambda b, n, *_: (0, b * nb + n)
    full = lambda shape: pl.BlockSpec(shape, const, pipeline_mode=pl.Buffered(1))
    return pl.pallas_call(
        _mixer_kernel,
        grid_spec=pltpu.PrefetchScalarGridSpec(
            num_scalar_prefetch=1,
            grid=(batch, nb),
            in_specs=[pl.BlockSpec((Q, COL_Q), row),
                      pl.BlockSpec((Q, COL_KV), row),
                      pl.BlockSpec((Q, COL_KV), row),
                      pl.BlockSpec((ROPE_DIM // 2, Q), col),
                      pl.BlockSpec((ROPE_DIM // 2, Q), col),
                      pl.BlockSpec((Q, D_MODEL), row),
                      pl.BlockSpec((Q, D_MODEL), row),
                      pl.BlockSpec((Q, SSD_CONV_DIM), row),
                      pl.BlockSpec((Q, SSD_D_INNER), row),
                      pl.BlockSpec((Q, LANES), row),
                      pl.BlockSpec((Q, D_MODEL), row),
                      full((LANES, 3 * LANES)),
                      pl.BlockSpec((1, LANES), const),
                      pl.BlockSpec((1, LANES), const),
                      full((COL_Q, D_MODEL)),
                      full((D_MODEL, D_MODEL)),
                      pl.BlockSpec((1, D_MODEL), const),
                      full((D_MODEL, LANES)),
                      full((D_MODEL, LANES)),
                      pl.BlockSpec((1, LANES), const),
                      pl.BlockSpec((SSD_CONV, SSD_CONV_DIM), const),
                      pl.BlockSpec((1, SSD_CONV_DIM), const),
                      pl.BlockSpec((1, LANES), const),
                      pl.BlockSpec((1, LANES), const),
                      pl.BlockSpec((1, SSD_D_INNER), const),
                      pl.BlockSpec((1, SSD_D_INNER), const),
                      full((SSD_D_INNER, D_MODEL))],
            out_specs=[pl.BlockSpec((Q, D_MODEL), row),
                       pl.BlockSpec((Q, HALF_D), row),
                       pl.BlockSpec((Q, LANES), row),
                       pl.BlockSpec((8, Q), col),
                       pl.BlockSpec((8, LANES), const)],
            scratch_shapes=[pltpu.VMEM((ATT_BLOCK, COL_KV), f32),
                            pltpu.VMEM((ATT_BLOCK, COL_KV), f32),
                            pltpu.VMEM((Q, COL_Q), bf16),
                            pltpu.VMEM((8, LANES), f32),
                            pltpu.VMEM((ATT_SUB * ATT_N_KV, 2 * ATT_BLOCK, LANES), bf16),
                            pltpu.VMEM((2 * CONV_HALO, SSD_CONV_DIM), f32),
                            pltpu.VMEM((SSD_N_HEADS // 2, SSD_D_STATE, LANES), f32),
                            pltpu.VMEM((Q, SSD_D_INNER), f32),
                            pltpu.VMEM((Q, SSD_N_GROUPS * SSD_D_STATE), f32),
                            pltpu.VMEM((Q, SSD_N_GROUPS * SSD_D_STATE), f32),
                            pltpu.VMEM((Q, SSD_D_INNER), f32),
                            pltpu.VMEM((Q, SSD_D_INNER), bf16),
                            pltpu.VMEM((Q, D_MODEL), f32)],
        ),
        out_shape=[jax.ShapeDtypeStruct((t, D_MODEL), f32),
                   jax.ShapeDtypeStruct((t, HALF_D), jnp.uint32),
                   jax.ShapeDtypeStruct((t, LANES), f32),
                   jax.ShapeDtypeStruct((8, t), f32),
                   jax.ShapeDtypeStruct((8, LANES), f32)],
        compiler_params=pltpu.CompilerParams(dimension_semantics=("arbitrary", "arbitrary"),
                                             vmem_limit_bytes=VMEM_LIMIT),
        name="mixer",
    )(sinks, q, k, v, cos_t, sin_t, ga, x2, xbc, z, dt, gs,
      _rope_select(), qg, kg, w_attn_out.astype(bf16), w_out.astype(bf16), norm2_g.reshape(1, D_MODEL),
      w_r_hi, w_r_lo, b_r,
      conv_w, conv_b.reshape(1, -1), dtb, alog, dexp, ssd_norm_g.reshape(1, -1), w_ssd_out.astype(bf16))


MOE_TM = 512
MOE_UNROLL = 8


def _row_copy(src, i, dst, j, sem):
    return pltpu.make_async_copy(src.at[pl.ds(i, 1), :], dst.at[pl.ds(j, 1), :], sem)


def _dispatch_kernel(dest0_ref, dest1_ref, zblk_ref, h2_hbm, xs_hbm, zero_ref, sem, zsem, *, tm):
    step = pl.program_id(0)
    base = step * tm

    @pl.when(pl.program_id(0) == 0)
    def _():
        zero_ref[...] = jnp.zeros(zero_ref.shape, zero_ref.dtype)

        def zcopy(i):
            start = pl.multiple_of(zblk_ref[i] * MOE_BLOCK, MOE_BLOCK)
            return pltpu.make_async_copy(zero_ref, xs_hbm.at[pl.ds(start, MOE_BLOCK), :], zsem)

        def zstart(i, carry):
            @pl.when(zblk_ref[i] >= 0)
            def _():
                zcopy(i).start()
            return carry

        def zwait(i, carry):
            @pl.when(zblk_ref[i] >= 0)
            def _():
                zcopy(i).wait()
            return carry

        lax.fori_loop(0, zblk_ref.shape[0], zstart, 0)
        lax.fori_loop(0, zblk_ref.shape[0], zwait, 0)

    par = step % 2

    def issue(i, carry):
        _row_copy(h2_hbm, base + i, xs_hbm, dest0_ref[base + i], sem.at[par]).start()
        _row_copy(h2_hbm, base + i, xs_hbm, dest1_ref[base + i], sem.at[par]).start()
        return carry

    lax.fori_loop(0, tm, issue, 0, unroll=MOE_UNROLL)

    def drain(s):
        for _ in range(2):
            pltpu.make_async_copy(h2_hbm.at[pl.ds(0, tm), :], xs_hbm.at[pl.ds(0, tm), :], sem.at[s]).wait()

    @pl.when(step > 0)
    def _():
        drain(1 - par)

    @pl.when(step == pl.num_programs(0) - 1)
    def _():
        drain(par)


def _dispatch(h2, dest0, dest1, zero_blocks, n_rows):
    t = h2.shape[0]
    tm = min(MOE_TM, t)
    return pl.pallas_call(
        functools.partial(_dispatch_kernel, tm=tm),
        grid_spec=pltpu.PrefetchScalarGridSpec(
            num_scalar_prefetch=3,
            grid=(t // tm,),
            in_specs=[pl.BlockSpec(memory_space=pl.ANY)],
            out_specs=pl.BlockSpec(memory_space=pl.ANY),
            scratch_shapes=[pltpu.VMEM((MOE_BLOCK, HALF_D), h2.dtype),
                            pltpu.SemaphoreType.DMA((2,)),
                            pltpu.SemaphoreType.DMA(())],
        ),
        out_shape=jax.ShapeDtypeStruct((n_rows, HALF_D), h2.dtype),
        compiler_params=pltpu.CompilerParams(dimension_semantics=("arbitrary",)),
        name="dispatch",
    )(dest0, dest1, zero_blocks, h2)


EXPERT_SUB = 2


def _expert_kernel(be_ref, nu_ref, xs_ref, *refs):
    w_refs = refs[:3 * EXPERT_SUB]
    ys_ref, wgu_scr, wd_scr = refs[3 * EXPERT_SUB:]
    i = pl.program_id(0)
    rows = MOE_BLOCK

    for j in range(EXPERT_SUB):
        blk = EXPERT_SUB * i + j
        fresh = (i == 0) | (be_ref[blk] != be_ref[jnp.maximum(blk - EXPERT_SUB, 0)])

        @pl.when((blk < nu_ref[0]) & fresh)
        def _(j=j):
            wg_ref, wu_ref, wd_ref = w_refs[3 * j:3 * j + 3]
            wgu_scr[j, :, 0:MOE_D_FF] = wg_ref[0].astype(bf16)
            wgu_scr[j, :, MOE_D_FF:2 * MOE_D_FF] = wu_ref[0].astype(bf16)
            wd_scr[j] = wd_ref[0].astype(bf16)

    def block(j):
        rs = slice(j * rows, (j + 1) * rows)
        x_lo, x_hi = _unpack_rows(xs_ref[rs, :])
        gu = (_dot(x_lo.astype(bf16), wgu_scr[j, 0:HALF_D, :])
              + _dot(x_hi.astype(bf16), wgu_scr[j, HALF_D:D_MODEL, :]))
        hid = _silu(gu[:, :MOE_D_FF]) * gu[:, MOE_D_FF:]
        ys_ref[rs, :] = _pack_rows(_dot(hid.astype(bf16), wd_scr[j]))

    n_live = jnp.clip(nu_ref[0] - EXPERT_SUB * i, 0, EXPERT_SUB)
    for live in range(EXPERT_SUB + 1):
        @pl.when(n_live == live)
        def _(live=live):
            for j in range(live):
                block(j)
            if live < EXPERT_SUB:
                ys_ref[live * rows:, :] = jnp.zeros(((EXPERT_SUB - live) * rows, HALF_D), ys_ref.dtype)


def _experts(xs, block_e, n_used, w_g, w_u, w_d):
    n_rows = xs.shape[0]
    rows = EXPERT_SUB * MOE_BLOCK
    nstep = n_rows // rows
    blk_in = lambda i, be, nu: (jnp.minimum(i, (nu[0] - 1) // EXPERT_SUB), 0)
    blk_out = lambda i, be, nu: (i, 0)
    w_specs = []
    for j in range(EXPERT_SUB):
        wsel = lambda i, be, nu, j=j: (be[EXPERT_SUB * i + j], 0, 0)
        w_specs += [pl.BlockSpec((1, D_MODEL, MOE_D_FF), wsel),
                    pl.BlockSpec((1, D_MODEL, MOE_D_FF), wsel),
                    pl.BlockSpec((1, MOE_D_FF, D_MODEL), wsel)]
    return pl.pallas_call(
        _expert_kernel,
        grid_spec=pltpu.PrefetchScalarGridSpec(
            num_scalar_prefetch=2,
            grid=(nstep,),
            in_specs=[pl.BlockSpec((rows, HALF_D), blk_in)] + w_specs,
            out_specs=pl.BlockSpec((rows, HALF_D), blk_out),
            scratch_shapes=[pltpu.VMEM((EXPERT_SUB, D_MODEL, 2 * MOE_D_FF), bf16),
                            pltpu.VMEM((EXPERT_SUB, MOE_D_FF, D_MODEL), bf16)],
        ),
        out_shape=jax.ShapeDtypeStruct((n_rows, HALF_D), xs.dtype),
        compiler_params=pltpu.CompilerParams(dimension_semantics=("arbitrary",),
                                             vmem_limit_bytes=VMEM_LIMIT),
        name="experts",
    )(block_e, n_used, xs, *([w_g, w_u, w_d] * EXPERT_SUB))


def _combine_kernel(dest0_ref, dest1_ref, x1_ref, route_ref, ys_hbm, out_ref, y0_ref, y1_ref, sem):
    tm = x1_ref.shape[0]
    step = pl.program_id(0)

    def gather_tile(tile, slot):
        base = tile * tm

        def issue(r, carry):
            pltpu.make_async_copy(ys_hbm.at[pl.ds(dest0_ref[base + r], 1), :],
                                  y0_ref.at[slot, pl.ds(r, 1), :], sem.at[slot]).start()
            pltpu.make_async_copy(ys_hbm.at[pl.ds(dest1_ref[base + r], 1), :],
                                  y1_ref.at[slot, pl.ds(r, 1), :], sem.at[slot]).start()
            return carry

        lax.fori_loop(0, tm, issue, 0, unroll=MOE_UNROLL)

    @pl.when(step == 0)
    def _():
        gather_tile(0, 0)

    @pl.when(step + 1 < pl.num_programs(0))
    def _():
        gather_tile(step + 1, (step + 1) % 2)

    slot = step % 2
    for y_ref in (y0_ref, y1_ref):
        pltpu.make_async_copy(ys_hbm.at[pl.ds(0, tm), :], y_ref.at[slot], sem.at[slot]).wait()
    g0 = route_ref[:, ROUTE_GATE:ROUTE_GATE + 1]
    g1 = route_ref[:, ROUTE_GATE + 1:ROUTE_GATE + 2]
    y0_lo, y0_hi = _unpack_rows(y0_ref[slot])
    y1_lo, y1_hi = _unpack_rows(y1_ref[slot])
    out_ref[:, 0:HALF_D] = x1_ref[:, 0:HALF_D] + y0_lo * g0 + y1_lo * g1
    out_ref[:, HALF_D:D_MODEL] = x1_ref[:, HALF_D:D_MODEL] + y0_hi * g0 + y1_hi * g1


def _combine(x1, route, ys, dest0, dest1):
    t = x1.shape[0]
    tm = min(MOE_TM, t)
    row = lambda i, d0, d1: (i, 0)
    return pl.pallas_call(
        _combine_kernel,
        grid_spec=pltpu.PrefetchScalarGridSpec(
            num_scalar_prefetch=2,
            grid=(t // tm,),
            in_specs=[pl.BlockSpec((tm, D_MODEL), row),
                      pl.BlockSpec((tm, LANES), row),
                      pl.BlockSpec(memory_space=pl.ANY)],
            out_specs=pl.BlockSpec((tm, D_MODEL), row),
            scratch_shapes=[pltpu.VMEM((2, tm, HALF_D), ys.dtype),
                            pltpu.VMEM((2, tm, HALF_D), ys.dtype),
                            pltpu.SemaphoreType.DMA((2,))],
        ),
        out_shape=jax.ShapeDtypeStruct((t, D_MODEL), f32),
        compiler_params=pltpu.CompilerParams(dimension_semantics=("arbitrary",)),
        name="combine",
    )(dest0, dest1, x1, route, ys)


def _moe(x1, h2, route, route_t, counts_rec, w_gate_e, w_up_e, w_down_e):
    t = x1.shape[0]
    n_rows = 2 * t + MOE_N_EXPERTS * MOE_BLOCK
    nblk = n_rows // MOE_BLOCK
    counts = counts_rec[0, :MOE_N_EXPERTS].astype(jnp.int32)
    padded = (counts + MOE_BLOCK - 1) // MOE_BLOCK * MOE_BLOCK
    pend = jnp.cumsum(padded)
    pstart = pend - padded
    expert_ids = jnp.arange(MOE_N_EXPERTS, dtype=jnp.int32)[:, None]

    def sorted_row(k):
        eid = route_t[ROUTE_E + k].astype(jnp.int32)
        rank = route_t[ROUTE_RANK + k].astype(jnp.int32)
        return jnp.sum(jnp.where(eid[None, :] == expert_ids, pstart[:, None], 0), axis=0) + rank

    dest0, dest1 = sorted_row(0), sorted_row(1)
    block_start = jnp.arange(nblk, dtype=jnp.int32) * MOE_BLOCK
    block_e = jnp.minimum(jnp.sum((pend[None, :] <= block_start[:, None]).astype(jnp.int32), axis=1),
                          MOE_N_EXPERTS - 1)
    n_used = (pend[-1:] // MOE_BLOCK).astype(jnp.int32)
    last_blk = jnp.where(counts > 0, pend // MOE_BLOCK - 1, -1)
    tail = n_used[0] + jnp.arange(MOE_N_EXPERTS, dtype=jnp.int32)
    zero_blocks = jnp.concatenate([last_blk, jnp.where(tail < nblk, tail, -1)]).astype(jnp.int32)
    xs = _dispatch(h2, dest0, dest1, zero_blocks, n_rows)
    ys = _experts(xs, block_e, n_used, w_gate_e, w_up_e, w_down_e)
    return _combine(x1, route, ys, dest0, dest1)


def _layer(x, positions, norm1_g, w_in, conv_w, conv_b, dt_bias, a_log, d_skip, ssd_norm_g, w_ssd_out,
           q_norm_g, k_norm_g, sinks, w_attn_out, w_out, norm2_g, w_rg, b_rg, w_re, b_re,
           w_gate_e, w_up_e, w_down_e):
    batch, seq, d = x.shape
    x2 = x.reshape(batch * seq, d)
    cos_t, sin_t = _rope_tables(positions)
    z, xbc, q, k, v, gs, ga, dt = _inproj(x2, norm1_g, w_in)
    x1, h2, route, route_t, counts = _mixer(q, k, v, cos_t, sin_t, ga, x2, xbc, z, dt, gs, q_norm_g, k_norm_g,
                                            sinks, w_attn_out, w_out, norm2_g, w_rg, b_rg, w_re, b_re,
                                            conv_w, conv_b, dt_bias, a_log, d_skip, ssd_norm_g, w_ssd_out,
                                            batch, seq)
    out = _moe(x1, h2, route, route_t, counts, w_gate_e, w_up_e, w_down_e)
    return out.reshape(batch, seq, d)


def kernel(x, positions, norm1_g, w_in, conv_w, conv_b, dt_bias, a_log, d_skip, ssd_norm_g, w_ssd_out,
           q_norm_g, k_norm_g, sinks, w_attn_out, w_out, norm2_g, w_router_group, b_router_group,
           w_router_expert, b_router_expert, w_gate_e, w_up_e, w_down_e):
    for l in range(norm1_g.shape[0]):
        x = _layer(x, positions, norm1_g[l], w_in[l], conv_w[l], conv_b[l], dt_bias[l], a_log[l],
                   d_skip[l], ssd_norm_g[l], w_ssd_out[l], q_norm_g[l], k_norm_g[l], sinks[l],
                   w_attn_out[l], w_out[l], norm2_g[l], w_router_group[l], b_router_group[l],
                   w_router_expert[l], b_router_expert[l], w_gate_e[l], w_up_e[l], w_down_e[l])
    return x
```

```python
import functools

import numpy as np

import jax
import jax.numpy as jnp
from jax import lax
from jax.experimental import pallas as pl
from jax.experimental.pallas import tpu as pltpu

f32 = jnp.float32
bf16 = jnp.bfloat16

D_MODEL = 1024
SSD_D_INNER = 2048
SSD_HEAD_DIM = 64
SSD_N_HEADS = 32
SSD_N_GROUPS = 4
SSD_D_STATE = 128
SSD_CONV = 4
SSD_CHUNK = 128
SSD_CONV_DIM = 3072
ATT_HEAD_DIM = 64
ATT_N_HEADS = 16
ATT_N_KV = 4
ATT_BLOCK = 128
ATT_SCALE = ATT_HEAD_DIM ** -0.5
ROPE_THETA = 500000.0
ROPE_DIM = 16
MOE_N_GROUPS = 8
MOE_EPG = 8
MOE_N_EXPERTS = 64
MOE_D_FF = 256
MOE_BLOCK = 512
RMS_EPS = 1e-6

LANES = 128
CONV_HALO = 8
NEG_BIG = -1e30
VMEM_LIMIT = 56 * 1024 * 1024

COL_Z = SSD_D_INNER
COL_XBC = SSD_CONV_DIM
COL_DT = SSD_N_HEADS
COL_Q = ATT_N_HEADS * ATT_HEAD_DIM
COL_KV = ATT_N_KV * ATT_HEAD_DIM


def _sigmoid(x):
    return 1.0 / (1.0 + jnp.exp(-x))


def _silu(x):
    return x * _sigmoid(x)


def _split3(x):
    hi = x.astype(bf16)
    r1 = x - hi.astype(f32)
    mid = r1.astype(bf16)
    lo = (r1 - mid.astype(f32)).astype(bf16)
    return hi, mid, lo


HALF_D = D_MODEL // 2
_HI_MASK = np.uint32(0xFFFF0000)


def _pack_rows(x):
    bits = pltpu.bitcast(x.astype(bf16).astype(f32), jnp.uint32)
    return (bits[:, HALF_D:] & _HI_MASK) | (bits[:, :HALF_D] >> 16)


def _unpack_rows(p):
    return pltpu.bitcast(p << 16, f32), pltpu.bitcast(p & _HI_MASK, f32)


def _dot(a, b):
    return jnp.dot(a, b, preferred_element_type=f32)


def _dot_nt(a, b):
    return lax.dot_general(a, b, (((1,), (1,)), ((), ())), preferred_element_type=f32)


def _trig_kernel(freq_ref, pos_ref, cos_ref, sin_ref):
    ang = pos_ref[...].astype(f32) * freq_ref[pl.program_id(0)]
    cos_ref[0] = jnp.cos(ang)
    sin_ref[0] = jnp.sin(ang)


def _rope_select():
    half = ROPE_DIM // 2
    sel = np.zeros((LANES, 3 * LANES), np.float32)
    for lane in range(LANES):
        m = lane % ATT_HEAD_DIM
        if m < half:
            sel[m, lane] = 1.0
            sel[half + m, LANES + lane] = -1.0
        elif m < ROPE_DIM:
            sel[m - half, lane] = 1.0
            sel[m, 2 * LANES + lane] = 1.0
        else:
            sel[ROPE_DIM, lane] = 1.0
    return jnp.asarray(sel, bf16)


def _rope_tables(positions):
    t = positions.size
    half = ROPE_DIM // 2
    inv_freq = ROPE_THETA ** (-jnp.arange(0, ROPE_DIM, 2, dtype=f32) / ROPE_DIM)
    pos2d = positions.reshape(t // LANES, LANES)
    cos_t, sin_t = pl.pallas_call(
        _trig_kernel,
        grid_spec=pltpu.PrefetchScalarGridSpec(
            num_scalar_prefetch=1,
            grid=(half,),
            in_specs=[pl.BlockSpec((t // LANES, LANES), lambda j, f: (0, 0))],
            out_specs=[pl.BlockSpec((1, t // LANES, LANES), lambda j, f: (j, 0, 0))] * 2,
        ),
        out_shape=[jax.ShapeDtypeStruct((half, t // LANES, LANES), f32)] * 2,
        name="trig",
    )(inv_freq, pos2d)
    return cos_t.reshape(half, t), sin_t.reshape(half, t)


INPROJ_TM = 512
INPROJ_CH = 512


def _inproj_kernel(x_ref, g_ref, w_ref,
                   z_ref, xbc_ref, q_ref, k_ref, v_ref, gs_ref, ga_ref, dt_ref, h_scr):
    x = x_ref[...]
    ms = jnp.mean(x * x, axis=-1, keepdims=True)
    h_scr[...] = (x * lax.rsqrt(ms + RMS_EPS) * g_ref[...]).astype(bf16)
    off = 0
    for ref in (z_ref, xbc_ref, q_ref, k_ref, v_ref, gs_ref, ga_ref, dt_ref):
        width = ref.shape[1]
        for c in range(0, width, INPROJ_CH):
            cw = min(INPROJ_CH, width - c)
            ref[:, c:c + cw] = _dot(h_scr[...], w_ref[:, off + c:off + c + cw]).astype(ref.dtype)
        off += width


WPREP_ROWS = 128


def _wprep_kernel(w_ref, o_ref):
    s1 = COL_Z + COL_XBC
    s2 = s1 + COL_DT
    n_tail = w_ref.shape[1] - s2
    o_ref[:, 0:s1] = w_ref[:, 0:s1].astype(bf16)
    o_ref[:, s1:s1 + n_tail] = w_ref[:, s2:s2 + n_tail].astype(bf16)
    lane = lax.broadcasted_iota(jnp.int32, (w_ref.shape[0], LANES), 1)
    o_ref[:, s1 + n_tail:s1 + n_tail + LANES] = jnp.where(lane < COL_DT, w_ref[:, s1:s1 + LANES], 0.0).astype(bf16)


def _inproj(x2, norm1_g, w_in):
    t = x2.shape[0]
    tm = min(INPROJ_TM, t)
    n_in = w_in.shape[1]
    n_all = n_in - COL_DT + LANES
    w_all = pl.pallas_call(
        _wprep_kernel,
        grid=(D_MODEL // WPREP_ROWS,),
        in_specs=[pl.BlockSpec((WPREP_ROWS, n_in), lambda i: (i, 0))],
        out_specs=pl.BlockSpec((WPREP_ROWS, n_all), lambda i: (i, 0)),
        out_shape=jax.ShapeDtypeStruct((D_MODEL, n_all), bf16),
        name="wprep",
    )(w_in)
    widths = (COL_Z, COL_XBC, COL_Q, COL_KV, COL_KV, D_MODEL, D_MODEL)
    const = lambda i: (0, 0)
    row = lambda i: (i, 0)
    outs = pl.pallas_call(
        _inproj_kernel,
        grid=(t // tm,),
        in_specs=[pl.BlockSpec((tm, D_MODEL), row),
                  pl.BlockSpec((1, D_MODEL), const),
                  pl.BlockSpec((D_MODEL, w_all.shape[1]), const, pipeline_mode=pl.Buffered(1))],
        out_specs=[pl.BlockSpec((tm, w), row) for w in widths] + [pl.BlockSpec((tm, LANES), row)],
        out_shape=[jax.ShapeDtypeStruct((t, w), bf16) for w in widths]
                  + [jax.ShapeDtypeStruct((t, LANES), f32)],
        scratch_shapes=[pltpu.VMEM((tm, D_MODEL), bf16)],
        compiler_params=pltpu.CompilerParams(dimension_semantics=("arbitrary",),
                                             vmem_limit_bytes=VMEM_LIMIT),
        name="inproj",
    )(x2, norm1_g.reshape(1, D_MODEL), w_all)
    return outs


SSD_CONV_YIELD = 1024

def _ssd_chunk_phases(rs, xbc_ref, z_ref, dt_ref, cw_ref, cb_ref, dtb_ref, alog_ref, dexp_ref, ng_ref,
                      ext_ref, st_ref, xs_ref, bm_ref, cm_ref, y_ref, hn_ref):
    L = SSD_CHUNK
    r0 = rs.start

    bf_tile = 2 * CONV_HALO
    ext_ref[CONV_HALO:2 * CONV_HALO, :] = xbc_ref[r0:r0 + bf_tile, :].astype(f32)[0:CONV_HALO]
    n_sh = SSD_CONV - 1
    sr = lax.broadcasted_iota(jnp.int32, (n_sh * L, L), 0)
    sc = lax.broadcasted_iota(jnp.int32, (n_sh * L, L), 1)
    shift = jnp.where((sr % L) - sc == (sr // L) + 1, 1.0, 0.0).astype(bf16)
    cch = 128
    for cc in range(0, SSD_CONV_DIM, cch):
        if cc and cc % SSD_CONV_YIELD == 0:
            yield
        cs_ = slice(cc, cc + cch)
        xb = xbc_ref[rs, cs_]
        sh = _dot(shift, xb)
        w_now = cw_ref[SSD_CONV - 1:SSD_CONV, cs_]
        acc = cb_ref[:, cs_] + xb.astype(f32) * w_now
        top = cb_ref[:, cs_] + ext_ref[CONV_HALO:2 * CONV_HALO, cs_] * w_now
        for j in range(1, SSD_CONV):
            w_j = cw_ref[SSD_CONV - 1 - j:SSD_CONV - j, cs_]
            acc = acc + sh[(j - 1) * L:j * L] * w_j
            top = top + ext_ref[CONV_HALO - j:2 * CONV_HALO - j, cs_] * w_j
        if cc < SSD_D_INNER:
            dst, o = xs_ref, cc
        elif cc < SSD_D_INNER + SSD_N_GROUPS * SSD_D_STATE:
            dst, o = bm_ref, cc - SSD_D_INNER
        else:
            dst, o = cm_ref, cc - SSD_D_INNER - SSD_N_GROUPS * SSD_D_STATE
        dst[rs, o:o + cch] = _silu(acc)
        dst[r0:r0 + CONV_HALO, o:o + cch] = _silu(top)
    ext_ref[0:CONV_HALO, :] = xbc_ref[rs.stop - bf_tile:rs.stop, :].astype(f32)[CONV_HALO:bf_tile]
    yield

    lane_row = lax.broadcasted_iota(jnp.int32, (1, LANES), 1)
    row_i = lax.broadcasted_iota(jnp.int32, (L, L), 0)
    col_i = lax.broadcasted_iota(jnp.int32, (L, L), 1)
    causal = row_i >= col_i
    left = col_i < SSD_HEAD_DIM

    xdt = dt_ref[rs, :] + dtb_ref[...]
    dtv = jnp.maximum(xdt, 0.0) + jnp.log1p(jnp.exp(-jnp.abs(xdt)))
    a = jnp.where(lane_row < SSD_N_HEADS, -jnp.exp(alog_ref[...]), 0.0)
    d_a = dtv * a
    tril = jnp.where(causal, 1.0, 0.0).astype(bf16)
    hi, mid, lo3 = _split3(d_a)
    a_cum = _dot(tril, hi) + _dot(tril, mid) + _dot(tril, lo3)
    a_end = a_cum[L - 1:L, :]
    exp_a = jnp.exp(a_cum)
    w_end = jnp.exp(a_end - a_cum) * dtv
    cd = jnp.exp(a_end)
    a_t = a_cum.T
    dt_t = dtv.T
    w_t = w_end.T
    yield

    n_pairs = SSD_N_HEADS // 2
    pairs_per_group = n_pairs // SSD_N_GROUPS
    for g in range(SSD_N_GROUPS):
        b_g = bm_ref[rs, g * SSD_D_STATE:(g + 1) * SSD_D_STATE]
        c_g = cm_ref[rs, g * SSD_D_STATE:(g + 1) * SSD_D_STATE]
        cb = _dot_nt(c_g.astype(bf16), b_g.astype(bf16))
        b_t = b_g.T
        for pi in range(pairs_per_group):
            i = g * pairs_per_group + pi
            xpair = xs_ref[rs, i * LANES:(i + 1) * LANES]
            xpair_b = xpair.astype(bf16)
            s_prev = st_ref[i]
            rhs = jnp.concatenate([xpair_b, s_prev.astype(bf16)], axis=0)
            ys = []
            sn = []
            for h in (2 * i, 2 * i + 1):
                acol = jnp.broadcast_to(a_cum[:, h:h + 1], (L, L))
                arow = jnp.broadcast_to(a_t[h:h + 1, :], (L, L))
                dtrow = jnp.broadcast_to(dt_t[h:h + 1, :], (L, L))
                dec = jnp.exp(jnp.where(causal, acol - arow, NEG_BIG))
                m = cb * dec * dtrow
                cs = c_g * jnp.broadcast_to(exp_a[:, h:h + 1], (L, L))
                lhs = jnp.concatenate([m.astype(bf16), cs.astype(bf16)], axis=1)
                ys.append(_dot(lhs, rhs))
                btw = (b_t * jnp.broadcast_to(w_t[h:h + 1, :], (L, L))).astype(bf16)
                sn.append(_dot(btw, xpair_b))
            h0 = 2 * i
            cd_pair = jnp.where(lane_row < SSD_HEAD_DIM, cd[:, h0:h0 + 1], cd[:, h0 + 1:h0 + 2])
            st_ref[i] = jnp.where(left, sn[0], sn[1]) + s_prev * cd_pair
            y_pair = jnp.where(left, ys[0], ys[1])
            y_ref[rs, i * LANES:(i + 1) * LANES] = y_pair + xpair * dexp_ref[:, i * LANES:(i + 1) * LANES]
            if pi % 2 == 1:
                yield

    gw = SSD_D_INNER // SSD_N_GROUPS
    for g in range(SSD_N_GROUPS):
        sl = slice(g * gw, (g + 1) * gw)
        yz = y_ref[rs, sl] * _silu(z_ref[rs, sl].astype(f32))
        ms = jnp.mean(yz * yz, axis=-1, keepdims=True)
        hn_ref[rs, sl] = (yz * lax.rsqrt(ms + RMS_EPS) * ng_ref[:, sl]).astype(bf16)
        if g % 2 == 1:
            yield


ROUTE_E, ROUTE_RANK, ROUTE_GATE = 0, 2, 4
ATT_SUB = 2
SSD_PER_STAGE = 3


def _mixer_kernel(sink_ref, q_ref, k_ref, v_ref, cos_ref, sin_ref, ga_ref, x_ref,
                  xbc_ref, z_ref, dt_ref, gs_ref,
                  sel_ref, qg_ref, kg_ref, wao_ref, wo_ref, n2g_ref, wrh_ref, wrl_ref, br_ref,
                  cw_ref, cb_ref, dtb_ref, alog_ref, dexp_ref, ng_ref, wssd_ref,
                  x1_ref, h2_ref, route_ref, route_t_ref, cnt_ref,
                  kprev_ref, vprev_ref, att_ref, cnt_scr, vd_ref,
                  ext_ref, st_ref, xs_ref, bm_ref, cm_ref, y_ref, hn_ref, ms_ref):
    Q = ATT_BLOCK
    b = pl.program_id(0)
    n = pl.program_id(1)

    @pl.when(n == 0)
    def _():
        kprev_ref[...] = jnp.zeros(kprev_ref.shape, f32)
        vprev_ref[...] = jnp.zeros(vprev_ref.shape, f32)
        ext_ref[0:CONV_HALO, :] = jnp.zeros((CONV_HALO, SSD_CONV_DIM), f32)
        st_ref[...] = jnp.zeros(st_ref.shape, f32)

    @pl.when((b == 0) & (n == 0))
    def _():
        cnt_scr[...] = jnp.zeros(cnt_scr.shape, f32)

    R = q_ref.shape[0]

    def ssd_all():
        for c in range(R // SSD_CHUNK):
            yield from _ssd_chunk_phases(slice(c * SSD_CHUNK, (c + 1) * SSD_CHUNK), xbc_ref, z_ref, dt_ref,
                                         cw_ref, cb_ref, dtb_ref, alog_ref, dexp_ref, ng_ref,
                                         ext_ref, st_ref, xs_ref, bm_ref, cm_ref, y_ref, hn_ref)

    ssd_gen = ssd_all()

    def ssd_step(k):
        for _ in range(k):
            next(ssd_gen, None)
    lane_q = lax.broadcasted_iota(jnp.int32, (Q, LANES), 1)
    row_q = lax.broadcasted_iota(jnp.int32, (Q, LANES), 0)
    left = lane_q < ATT_HEAD_DIM
    head_mean = jnp.where((row_q // ATT_HEAD_DIM) == (lane_q // ATT_HEAD_DIM),
                          1.0 / ATT_HEAD_DIM, 0.0).astype(bf16)
    lane2 = lax.broadcasted_iota(jnp.int32, (2 * Q, LANES), 1) < ATT_HEAD_DIM
    qg = ATT_N_HEADS // ATT_N_KV
    rows = qg * Q
    ri = lax.broadcasted_iota(jnp.int32, (rows, Q), 0) % Q
    cj = lax.broadcasted_iota(jnp.int32, (rows, Q), 1)
    upper = cj > ri
    half = ROPE_DIM // 2

    n_sb = R // Q
    nq = COL_Q // LANES
    nk = COL_KV // LANES
    chunks = []
    for sb in range(n_sb):
        rs = slice(sb * Q, (sb + 1) * Q)
        chunks += [q_ref[rs, c * LANES:(c + 1) * LANES].astype(f32) for c in range(nq)]
        chunks += [k_ref[rs, c * LANES:(c + 1) * LANES].astype(f32) for c in range(nk)]
    u_all = jnp.concatenate(chunks, axis=0)
    sq = u_all * u_all
    sq_hi = sq.astype(bf16)
    sq_lo = (sq - sq_hi.astype(f32)).astype(bf16)
    un_all = u_all * lax.rsqrt(_dot(sq_hi, head_mean) + _dot(sq_lo, head_mean) + RMS_EPS)
    ssd_step(SSD_PER_STAGE)

    terms = []
    for sb in range(n_sb):
        rs = slice(sb * Q, (sb + 1) * Q)
        cs = jnp.concatenate([cos_ref[:, rs], sin_ref[:, rs], jnp.ones((half, LANES), f32),
                              jnp.zeros((LANES - 3 * half, LANES), f32)], axis=0)
        terms += list(_split3(cs.T))
    pat_all = _dot(jnp.concatenate(terms, axis=0), sel_ref[...])

    pats, kds = [], []
    for sb in range(n_sb):
        rs = slice(sb * Q, (sb + 1) * Q)
        pat = pat_all[3 * sb * Q:(3 * sb + 1) * Q] + pat_all[(3 * sb + 1) * Q:(3 * sb + 2) * Q] \
            + pat_all[(3 * sb + 2) * Q:(3 * sb + 3) * Q]
        cpat = pat[:, 0:LANES]
        s1pat = pat[:, LANES:2 * LANES]
        s2pat = pat[:, 2 * LANES:3 * LANES]

        def norm_rope(idx, gpat):
            tn = un_all[idx * Q:(idx + 1) * Q] * gpat
            return (tn * cpat + pltpu.roll(tn, LANES - ROPE_DIM // 2, 1) * s1pat
                    + pltpu.roll(tn, ROPE_DIM // 2, 1) * s2pat)

        kd = []
        for cidx in range(nk):
            sl = slice(cidx * LANES, (cidx + 1) * LANES)
            k_cur = norm_rope(sb * (nq + nk) + nq + cidx, kg_ref[...])
            v_cur = v_ref[rs, sl].astype(f32)
            k_all = jnp.concatenate([kprev_ref[:, sl], k_cur], axis=0)
            v_all = jnp.concatenate([vprev_ref[:, sl], v_cur], axis=0)
            kprev_ref[:, sl] = k_cur
            vprev_ref[:, sl] = v_cur
            k_sw = pltpu.roll(k_all, ATT_HEAD_DIM, 1)
            v_sw = pltpu.roll(v_all, ATT_HEAD_DIM, 1)
            kd.append(jnp.where(lane2, k_all, k_sw).astype(bf16))
            kd.append(jnp.where(lane2, k_sw, k_all).astype(bf16))
            for half_i, vv in enumerate((jnp.where(lane2, v_all, v_sw), jnp.where(lane2, v_sw, v_all))):
                vd_ref[sb * ATT_N_KV + 2 * cidx + half_i] = vv.astype(bf16)

        pats.append((cpat, s1pat, s2pat))
        kds.append(kd)
        ssd_step(SSD_PER_STAGE)

    def rope_q(sb, cidx):
        cpat, s1pat, s2pat = pats[sb]
        tn = un_all[(sb * (nq + nk) + cidx) * Q:(sb * (nq + nk) + cidx + 1) * Q] * qg_ref[...]
        return (tn * cpat + pltpu.roll(tn, LANES - ROPE_DIM // 2, 1) * s1pat
                + pltpu.roll(tn, ROPE_DIM // 2, 1) * s2pat) * ATT_SCALE

    def score_tile(g):
        sb, h = divmod(g, ATT_N_KV)
        parts = []
        for cidx in (2 * h, 2 * h + 1):
            qc = rope_q(sb, cidx)
            parts.append(jnp.where(left, qc, 0.0).astype(bf16))
            parts.append(jnp.where(left, 0.0, qc).astype(bf16))
        lhs = jnp.concatenate(parts, axis=0)
        s_both = _dot_nt(lhs, kds[sb][h])
        s_prev = s_both[:, 0:Q]
        s_cur = s_both[:, Q:2 * Q]
        if sb == 0:
            return jnp.where(upper & (n > 0), s_prev, jnp.where(upper, NEG_BIG, s_cur))
        return jnp.where(upper, s_prev, s_cur)

    ones = jnp.ones((Q, Q), bf16)

    def softmax_block(s):
        sink = jnp.concatenate([jnp.full((Q, 1), sink_ref[i], f32) for i in range(ATT_N_HEADS)], axis=0)
        m = jnp.maximum(jnp.max(s, axis=-1, keepdims=True), sink)
        p = jnp.exp(s - m)
        p_hi = p.astype(bf16)
        p_lo = (p - p_hi.astype(f32)).astype(bf16)
        denom = _dot(p_hi, ones) + _dot(p_lo, ones) + jnp.exp(sink - m)
        return (p / denom).astype(bf16)

    def pv_tile(g, pf):
        sb, h = divmod(g, ATT_N_KV)
        rs = slice(sb * Q, (sb + 1) * Q)
        zero = jnp.zeros_like(pf)
        p_both = jnp.concatenate([jnp.where(upper, pf, zero), jnp.where(upper, zero, pf)], axis=1)
        o = _dot(p_both, vd_ref[g])
        for r in range(2):
            cidx = 2 * h + r
            att_ref[rs, cidx * LANES:(cidx + 1) * LANES] = jnp.where(
                left, o[(2 * r) * Q:(2 * r + 1) * Q], o[(2 * r + 1) * Q:(2 * r + 2) * Q]).astype(bf16)

    def epilogue(rs, cnt):
        E = rs.stop - rs.start
        lane = lax.broadcasted_iota(jnp.int32, (E, LANES), 1)
        y_att = _dot(att_ref[rs, :], wao_ref[...])
        merged = _sigmoid(ga_ref[rs, :].astype(f32)) * y_att + ms_ref[rs, :]
        x1 = x_ref[rs, :] + _dot(merged.astype(bf16), wo_ref[...])
        x1_ref[rs, :] = x1
        h2 = x1 * lax.rsqrt(jnp.mean(x1 * x1, axis=-1, keepdims=True) + RMS_EPS) * n2g_ref[...]
        h2_ref[rs, :] = _pack_rows(h2)

        hi = h2.astype(bf16)
        lo = (h2 - hi.astype(f32)).astype(bf16)
        logits = _dot(hi, wrh_ref[...]) + _dot(lo, wrh_ref[...]) + _dot(hi, wrl_ref[...]) + br_ref[...]
        big = 4 * LANES
        gl = jnp.where(lane < MOE_N_GROUPS, logits, NEG_BIG)
        gmax = jnp.max(gl, axis=-1, keepdims=True)
        gsel = jnp.min(jnp.where(gl == gmax, lane, big), axis=-1, keepdims=True)
        pg = 1.0 / jnp.sum(jnp.exp(gl - gmax), axis=-1, keepdims=True)
        lo_l = MOE_N_GROUPS + MOE_EPG * gsel
        el = jnp.where((lane >= lo_l) & (lane < lo_l + MOE_EPG), logits, NEG_BIG)
        v1 = jnp.max(el, axis=-1, keepdims=True)
        i1 = jnp.min(jnp.where(el == v1, lane, big), axis=-1, keepdims=True)
        el2 = jnp.where(lane == i1, NEG_BIG, el)
        v2 = jnp.max(el2, axis=-1, keepdims=True)
        i2 = jnp.min(jnp.where(el2 == v2, lane, big), axis=-1, keepdims=True)
        e1 = i1 - MOE_N_GROUPS
        e2 = i2 - MOE_N_GROUPS
        tt = jnp.exp(v2 - v1)
        w1 = pg * (1.0 / (1.0 + tt))
        w2 = pg * (tt / (1.0 + tt))

        onehot = jnp.where((lane == e1) | (lane == e2), 1.0, 0.0)
        strict = jnp.where(lax.broadcasted_iota(jnp.int32, (E, E), 0) > lax.broadcasted_iota(jnp.int32, (E, E), 1),
                           1.0, 0.0).astype(bf16)
        base = _dot(strict, onehot.astype(bf16)) + cnt
        r1 = jnp.sum(jnp.where(lane == e1, base, 0.0), axis=-1, keepdims=True)
        r2 = jnp.sum(jnp.where(lane == e2, base, 0.0), axis=-1, keepdims=True)

        rec = jnp.zeros((E, LANES), f32)
        for off, val in ((ROUTE_E, e1.astype(f32)), (ROUTE_E + 1, e2.astype(f32)),
                         (ROUTE_RANK, r1), (ROUTE_RANK + 1, r2), (ROUTE_GATE, w1), (ROUTE_GATE + 1, w2)):
            rec = jnp.where(lane == off, val, rec)
        route_ref[rs, :] = rec
        for o in range(0, E, Q):
            route_t_ref[:, rs.start + o:rs.start + o + Q] = rec[o:o + Q].T[0:8, :]
        return cnt + jnp.sum(onehot, axis=0, keepdims=True)

    s_tiles, p_tiles = {}, {}
    for stage in range(n_sb + 2):
        if stage < n_sb:
            s_tiles[stage] = jnp.concatenate(
                [score_tile(stage * ATT_N_KV + h) for h in range(ATT_N_KV)], axis=0)
            ssd_step(SSD_PER_STAGE)
        if 0 <= stage - 1 < n_sb:
            p_tiles[stage - 1] = softmax_block(s_tiles.pop(stage - 1))
            ssd_step(SSD_PER_STAGE)
        if 0 <= stage - 2 < n_sb:
            pb = p_tiles.pop(stage - 2)
            for h in range(ATT_N_KV):
                pv_tile((stage - 2) * ATT_N_KV + h, pb[h * rows:(h + 1) * rows])
            ssd_step(SSD_PER_STAGE)

    for _ in ssd_gen:
        pass
    ms_ref[...] = _sigmoid(gs_ref[...].astype(f32)) * _dot(hn_ref[...], wssd_ref[...])
    cnt = epilogue(slice(0, R), cnt_scr[0:1, :])

    cnt_scr[...] = jnp.broadcast_to(cnt, cnt_scr.shape)
    cnt_ref[...] = jnp.broadcast_to(cnt, cnt_ref.shape)


def _mixer(q, k, v, cos_t, sin_t, ga, x2, xbc, z, dt, gs, q_norm_g, k_norm_g, sinks, w_attn_out, w_out,
           norm2_g, w_rg, b_rg, w_re, b_re, conv_w, conv_b, dt_bias, a_log, d_skip, ssd_norm_g, w_ssd_out,
           batch, seq):
    t = batch * seq
    Q = ATT_SUB * ATT_BLOCK
    nb = seq // Q
    pad_h = LANES - SSD_N_HEADS
    dtb = jnp.pad(dt_bias, (0, pad_h)).reshape(1, LANES)
    alog = jnp.pad(a_log, (0, pad_h)).reshape(1, LANES)
    dexp = jnp.repeat(d_skip, SSD_HEAD_DIM).reshape(1, SSD_D_INNER)
    rep = LANES // ATT_HEAD_DIM
    qg = jnp.tile(q_norm_g, rep).reshape(1, LANES)
    kg = jnp.tile(k_norm_g, rep).reshape(1, LANES)
    n_log = MOE_N_GROUPS + MOE_N_EXPERTS
    w_r = jnp.concatenate([w_rg, jnp.transpose(w_re, (1, 0, 2)).reshape(D_MODEL, MOE_N_EXPERTS),
                           jnp.zeros((D_MODEL, LANES - n_log), f32)], axis=1)
    b_r = jnp.concatenate([b_rg, b_re.reshape(-1), jnp.zeros((LANES - n_log,), f32)]).reshape(1, LANES)
    w_r_hi = w_r.astype(bf16)
    w_r_lo = (w_r - w_r_hi.astype(f32)).astype(bf16)
    const = lambda b, n, *_: (0, 0)
    row = lambda b, n, *_: (b * nb + n, 0)
    col = lambda b, n, *_: (0, b * nb + n)
    full = lambda shape: pl.BlockSpec(shape, const, pipeline_mode=pl.Buffered(1))
    return pl.pallas_call(
        _mixer_kernel,
        grid_spec=pltpu.PrefetchScalarGridSpec(
            num_scalar_prefetch=1,
            grid=(batch, nb),
            in_specs=[pl.BlockSpec((Q, COL_Q), row),
                      pl.BlockSpec((Q, COL_KV), row),
                      pl.BlockSpec((Q, COL_KV), row),
                      pl.BlockSpec((ROPE_DIM // 2, Q), col),
                      pl.BlockSpec((ROPE_DIM // 2, Q), col),
                      pl.BlockSpec((Q, D_MODEL), row),
                      pl.BlockSpec((Q, D_MODEL), row),
                      pl.BlockSpec((Q, SSD_CONV_DIM), row),
                      pl.BlockSpec((Q, SSD_D_INNER), row),
                      pl.BlockSpec((Q, LANES), row),
                      pl.BlockSpec((Q, D_MODEL), row),
                      full((LANES, 3 * LANES)),
                      pl.BlockSpec((1, LANES), const),
                      pl.BlockSpec((1, LANES), const),
                      full((COL_Q, D_MODEL)),
                      full((D_MODEL, D_MODEL)),
                      pl.BlockSpec((1, D_MODEL), const),
                      full((D_MODEL, LANES)),
                      full((D_MODEL, LANES)),
                      pl.BlockSpec((1, LANES), const),
                      pl.BlockSpec((SSD_CONV, SSD_CONV_DIM), const),
                      pl.BlockSpec((1, SSD_CONV_DIM), const),
                      pl.BlockSpec((1, LANES), const),
                      pl.BlockSpec((1, LANES), const),
                      pl.BlockSpec((1, SSD_D_INNER), const),
                      pl.BlockSpec((1, SSD_D_INNER), const),
                      full((SSD_D_INNER, D_MODEL))],
            out_specs=[pl.BlockSpec((Q, D_MODEL), row),
                       pl.BlockSpec((Q, HALF_D), row),
                       pl.BlockSpec((Q, LANES), row),
                       pl.BlockSpec((8, Q), col),
                       pl.BlockSpec((8, LANES), const)],
            scratch_shapes=[pltpu.VMEM((ATT_BLOCK, COL_KV), f32),
                            pltpu.VMEM((ATT_BLOCK, COL_KV), f32),
                            pltpu.VMEM((Q, COL_Q), bf16),
                            pltpu.VMEM((8, LANES), f32),
                            pltpu.VMEM((ATT_SUB * ATT_N_KV, 2 * ATT_BLOCK, LANES), bf16),
                            pltpu.VMEM((2 * CONV_HALO, SSD_CONV_DIM), f32),
                            pltpu.VMEM((SSD_N_HEADS // 2, SSD_D_STATE, LANES), f32),
                            pltpu.VMEM((Q, SSD_D_INNER), f32),
                            pltpu.VMEM((Q, SSD_N_GROUPS * SSD_D_STATE), f32),
                            pltpu.VMEM((Q, SSD_N_GROUPS * SSD_D_STATE), f32),
                            pltpu.VMEM((Q, SSD_D_INNER), f32),
                            pltpu.VMEM((Q, SSD_D_INNER), bf16),
                            pltpu.VMEM((Q, D_MODEL), f32)],
        ),
        out_shape=[jax.ShapeDtypeStruct((t, D_MODEL), f32),
                   jax.ShapeDtypeStruct((t, HALF_D), jnp.uint32),
                   jax.ShapeDtypeStruct((t, LANES), f32),
                   jax.ShapeDtypeStruct((8, t), f32),
                   jax.ShapeDtypeStruct((8, LANES), f32)],
        compiler_params=pltpu.CompilerParams(dimension_semantics=("arbitrary", "arbitrary"),
                                             vmem_limit_bytes=VMEM_LIMIT),
        name="mixer",
    )(sinks, q, k, v, cos_t, sin_t, ga, x2, xbc, z, dt, gs,
      _rope_select(), qg, kg, w_attn_out.astype(bf16), w_out.astype(bf16), norm2_g.reshape(1, D_MODEL),
      w_r_hi, w_r_lo, b_r,
      conv_w, conv_b.reshape(1, -1), dtb, alog, dexp, ssd_norm_g.reshape(1, -1), w_ssd_out.astype(bf16))


MOE_TM = 512
MOE_UNROLL = 8


def _row_copy(src, i, dst, j, sem):
    return pltpu.make_async_copy(src.at[pl.ds(i, 1), :], dst.at[pl.ds(j, 1), :], sem)


def _dispatch_kernel(dest0_ref, dest1_ref, zblk_ref, h2_ref, xs_hbm, zero_ref, sem, zsem):
    tm = h2_ref.shape[0]
    base = pl.program_id(0) * tm

    @pl.when(pl.program_id(0) == 0)
    def _():
        zero_ref[...] = jnp.zeros(zero_ref.shape, zero_ref.dtype)

        def zcopy(i):
            start = pl.multiple_of(zblk_ref[i] * MOE_BLOCK, MOE_BLOCK)
            return pltpu.make_async_copy(zero_ref, xs_hbm.at[pl.ds(start, MOE_BLOCK), :], zsem)

        def zstart(i, carry):
            @pl.when(zblk_ref[i] >= 0)
            def _():
                zcopy(i).start()
            return carry

        def zwait(i, carry):
            @pl.when(zblk_ref[i] >= 0)
            def _():
                zcopy(i).wait()
            return carry

        lax.fori_loop(0, zblk_ref.shape[0], zstart, 0)
        lax.fori_loop(0, zblk_ref.shape[0], zwait, 0)

    def issue(i, carry):
        _row_copy(h2_ref, i, xs_hbm, dest0_ref[base + i], sem).start()
        _row_copy(h2_ref, i, xs_hbm, dest1_ref[base + i], sem).start()
        return carry

    lax.fori_loop(0, tm, issue, 0, unroll=MOE_UNROLL)
    for _ in range(2):
        pltpu.make_async_copy(h2_ref, xs_hbm.at[pl.ds(0, tm), :], sem).wait()


def _dispatch(h2, dest0, dest1, zero_blocks, n_rows):
    t = h2.shape[0]
    tm = min(MOE_TM, t)
    return pl.pallas_call(
        _dispatch_kernel,
        grid_spec=pltpu.PrefetchScalarGridSpec(
            num_scalar_prefetch=3,
            grid=(t // tm,),
            in_specs=[pl.BlockSpec((tm, HALF_D), lambda i, d0, d1, zb: (i, 0))],
            out_specs=pl.BlockSpec(memory_space=pl.ANY),
            scratch_shapes=[pltpu.VMEM((MOE_BLOCK, HALF_D), h2.dtype),
                            pltpu.SemaphoreType.DMA(()),
                            pltpu.SemaphoreType.DMA(())],
        ),
        out_shape=jax.ShapeDtypeStruct((n_rows, HALF_D), h2.dtype),
        compiler_params=pltpu.CompilerParams(dimension_semantics=("arbitrary",)),
        name="dispatch",
    )(dest0, dest1, zero_blocks, h2)


EXPERT_SUB = 1


def _expert_kernel(be_ref, nu_ref, xs_ref, *refs):
    w_refs = refs[:3 * EXPERT_SUB]
    ys_ref, wgu_scr, wd_scr = refs[3 * EXPERT_SUB:]
    i = pl.program_id(0)
    rows = MOE_BLOCK

    for j in range(EXPERT_SUB):
        blk = EXPERT_SUB * i + j
        fresh = (i == 0) | (be_ref[blk] != be_ref[jnp.maximum(blk - EXPERT_SUB, 0)])

        @pl.when((blk < nu_ref[0]) & fresh)
        def _(j=j):
            wg_ref, wu_ref, wd_ref = w_refs[3 * j:3 * j + 3]
            wgu_scr[j, :, 0:MOE_D_FF] = wg_ref[0].astype(bf16)
            wgu_scr[j, :, MOE_D_FF:2 * MOE_D_FF] = wu_ref[0].astype(bf16)
            wd_scr[j] = wd_ref[0].astype(bf16)

    def block(j):
        rs = slice(j * rows, (j + 1) * rows)
        x_lo, x_hi = _unpack_rows(xs_ref[rs, :])
        gu = (_dot(x_lo.astype(bf16), wgu_scr[j, 0:HALF_D, :])
              + _dot(x_hi.astype(bf16), wgu_scr[j, HALF_D:D_MODEL, :]))
        hid = _silu(gu[:, :MOE_D_FF]) * gu[:, MOE_D_FF:]
        ys_ref[rs, :] = _pack_rows(_dot(hid.astype(bf16), wd_scr[j]))

    n_live = jnp.clip(nu_ref[0] - EXPERT_SUB * i, 0, EXPERT_SUB)
    for live in range(EXPERT_SUB + 1):
        @pl.when(n_live == live)
        def _(live=live):
            for j in range(live):
                block(j)
            if live < EXPERT_SUB:
                ys_ref[live * rows:, :] = jnp.zeros(((EXPERT_SUB - live) * rows, HALF_D), ys_ref.dtype)


def _experts(xs, block_e, n_used, w_g, w_u, w_d):
    n_rows = xs.shape[0]
    rows = EXPERT_SUB * MOE_BLOCK
    nstep = n_rows // rows
    blk_in = lambda i, be, nu: (jnp.minimum(i, (nu[0] - 1) // EXPERT_SUB), 0)
    blk_out = lambda i, be, nu: (i, 0)
    w_specs = []
    for j in range(EXPERT_SUB):
        wsel = lambda i, be, nu, j=j: (be[EXPERT_SUB * i + j], 0, 0)
        w_specs += [pl.BlockSpec((1, D_MODEL, MOE_D_FF), wsel),
                    pl.BlockSpec((1, D_MODEL, MOE_D_FF), wsel),
                    pl.BlockSpec((1, MOE_D_FF, D_MODEL), wsel)]
    return pl.pallas_call(
        _expert_kernel,
        grid_spec=pltpu.PrefetchScalarGridSpec(
            num_scalar_prefetch=2,
            grid=(nstep,),
            in_specs=[pl.BlockSpec((rows, HALF_D), blk_in)] + w_specs,
            out_specs=pl.BlockSpec((rows, HALF_D), blk_out),
            scratch_shapes=[pltpu.VMEM((EXPERT_SUB, D_MODEL, 2 * MOE_D_FF), bf16),
                            pltpu.VMEM((EXPERT_SUB, MOE_D_FF, D_MODEL), bf16)],
        ),
        out_shape=jax.ShapeDtypeStruct((n_rows, HALF_D), xs.dtype),
        compiler_params=pltpu.CompilerParams(dimension_semantics=("arbitrary",),
                                             vmem_limit_bytes=VMEM_LIMIT),
        name="experts",
    )(block_e, n_used, xs, *([w_g, w_u, w_d] * EXPERT_SUB))


def _combine_kernel(dest0_ref, dest1_ref, x1_ref, route_ref, ys_hbm, out_ref, y0_ref, y1_ref, sem):
    tm = x1_ref.shape[0]
    step = pl.program_id(0)

    def gather_tile(tile, slot):
        base = tile * tm

        def issue(r, carry):
            pltpu.make_async_copy(ys_hbm.at[pl.ds(dest0_ref[base + r], 1), :],
                                  y0_ref.at[slot, pl.ds(r, 1), :], sem.at[slot]).start()
            pltpu.make_async_copy(ys_hbm.at[pl.ds(dest1_ref[base + r], 1), :],
                                  y1_ref.at[slot, pl.ds(r, 1), :], sem.at[slot]).start()
            return carry

        lax.fori_loop(0, tm, issue, 0, unroll=MOE_UNROLL)

    @pl.when(step == 0)
    def _():
        gather_tile(0, 0)

    @pl.when(step + 1 < pl.num_programs(0))
    def _():
        gather_tile(step + 1, (step + 1) % 2)

    slot = step % 2
    for y_ref in (y0_ref, y1_ref):
        pltpu.make_async_copy(ys_hbm.at[pl.ds(0, tm), :], y_ref.at[slot], sem.at[slot]).wait()
    g0 = route_ref[:, ROUTE_GATE:ROUTE_GATE + 1]
    g1 = route_ref[:, ROUTE_GATE + 1:ROUTE_GATE + 2]
    y0_lo, y0_hi = _unpack_rows(y0_ref[slot])
    y1_lo, y1_hi = _unpack_rows(y1_ref[slot])
    out_ref[:, 0:HALF_D] = x1_ref[:, 0:HALF_D] + y0_lo * g0 + y1_lo * g1
    out_ref[:, HALF_D:D_MODEL] = x1_ref[:, HALF_D:D_MODEL] + y0_hi * g0 + y1_hi * g1


def _combine(x1, route, ys, dest0, dest1):
    t = x1.shape[0]
    tm = min(MOE_TM, t)
    row = lambda i, d0, d1: (i, 0)
    return pl.pallas_call(
        _combine_kernel,
        grid_spec=pltpu.PrefetchScalarGridSpec(
            num_scalar_prefetch=2,
            grid=(t // tm,),
            in_specs=[pl.BlockSpec((tm, D_MODEL), row),
                      pl.BlockSpec((tm, LANES), row),
                      pl.BlockSpec(memory_space=pl.ANY)],
            out_specs=pl.BlockSpec((tm, D_MODEL), row),
            scratch_shapes=[pltpu.VMEM((2, tm, HALF_D), ys.dtype),
                            pltpu.VMEM((2, tm, HALF_D), ys.dtype),
                            pltpu.SemaphoreType.DMA((2,))],
        ),
        out_shape=jax.ShapeDtypeStruct((t, D_MODEL), f32),
        compiler_params=pltpu.CompilerParams(dimension_semantics=("arbitrary",)),
        name="combine",
    )(dest0, dest1, x1, route, ys)


def _moe(x1, h2, route, route_t, counts_rec, w_gate_e, w_up_e, w_down_e):
    t = x1.shape[0]
    n_rows = 2 * t + MOE_N_EXPERTS * MOE_BLOCK
    nblk = n_rows // MOE_BLOCK
    counts = counts_rec[0, :MOE_N_EXPERTS].astype(jnp.int32)
    padded = (counts + MOE_BLOCK - 1) // MOE_BLOCK * MOE_BLOCK
    pend = jnp.cumsum(padded)
    pstart = pend - padded
    expert_ids = jnp.arange(MOE_N_EXPERTS, dtype=jnp.int32)[:, None]

    def sorted_row(k):
        eid = route_t[ROUTE_E + k].astype(jnp.int32)
        rank = route_t[ROUTE_RANK + k].astype(jnp.int32)
        return jnp.sum(jnp.where(eid[None, :] == expert_ids, pstart[:, None], 0), axis=0) + rank

    dest0, dest1 = sorted_row(0), sorted_row(1)
    block_start = jnp.arange(nblk, dtype=jnp.int32) * MOE_BLOCK
    block_e = jnp.minimum(jnp.sum((pend[None, :] <= block_start[:, None]).astype(jnp.int32), axis=1),
                          MOE_N_EXPERTS - 1)
    n_used = (pend[-1:] // MOE_BLOCK).astype(jnp.int32)
    last_blk = jnp.where(counts > 0, pend // MOE_BLOCK - 1, -1)
    tail = n_used[0] + jnp.arange(MOE_N_EXPERTS, dtype=jnp.int32)
    zero_blocks = jnp.concatenate([last_blk, jnp.where(tail < nblk, tail, -1)]).astype(jnp.int32)
    xs = _dispatch(h2, dest0, dest1, zero_blocks, n_rows)
    ys = _experts(xs, block_e, n_used, w_gate_e, w_up_e, w_down_e)
    return _combine(x1, route, ys, dest0, dest1)


def _layer(x, positions, norm1_g, w_in, conv_w, conv_b, dt_bias, a_log, d_skip, ssd_norm_g, w_ssd_out,
           q_norm_g, k_norm_g, sinks, w_attn_out, w_out, norm2_g, w_rg, b_rg, w_re, b_re,
           w_gate_e, w_up_e, w_down_e):
    batch, seq, d = x.shape
    x2 = x.reshape(batch * seq, d)
    cos_t, sin_t = _rope_tables(positions)
    z, xbc, q, k, v, gs, ga, dt = _inproj(x2, norm1_g, w_in)
    x1, h2, route, route_t, counts = _mixer(q, k, v, cos_t, sin_t, ga, x2, xbc, z, dt, gs, q_norm_g, k_norm_g,
                                            sinks, w_attn_out, w_out, norm2_g, w_rg, b_rg, w_re, b_re,
                                            conv_w, conv_b, dt_bias, a_log, d_skip, ssd_norm_g, w_ssd_out,
                                            batch, seq)
    out = _moe(x1, h2, route, route_t, counts, w_gate_e, w_up_e, w_down_e)
    return out.reshape(batch, seq, d)


def kernel(x, positions, norm1_g, w_in, conv_w, conv_b, dt_bias, a_log, d_skip, ssd_norm_g, w_ssd_out,
           q_norm_g, k_norm_g, sinks, w_attn_out, w_out, norm2_g, w_router_group, b_router_group,
           w_router_expert, b_router_expert, w_gate_e, w_up_e, w_down_e):
    for l in range(norm1_g.shape[0]):
        x = _layer(x, positions, norm1_g[l], w_in[l], conv_w[l], conv_b[l], dt_bias[l], a_log[l],
                   d_skip[l], ssd_norm_g[l], w_ssd_out[l], q_norm_g[l], k_norm_g[l], sinks[l],
                   w_attn_out[l], w_out[l], norm2_g[l], w_router_group[l], b_router_group[l],
                   w_router_expert[l], b_router_expert[l], w_gate_e[l], w_up_e[l], w_down_e[l])
    return x
```

```python
import functools

import numpy as np

import jax
import jax.numpy as jnp
from jax import lax
from jax.experimental import pallas as pl
from jax.experimental.pallas import tpu as pltpu

f32 = jnp.float32
bf16 = jnp.bfloat16

D_MODEL = 1024
SSD_D_INNER = 2048
SSD_HEAD_DIM = 64
SSD_N_HEADS = 32
SSD_N_GROUPS = 4
SSD_D_STATE = 128
SSD_CONV = 4
SSD_CHUNK = 128
SSD_CONV_DIM = 3072
ATT_HEAD_DIM = 64
ATT_N_HEADS = 16
ATT_N_KV = 4
ATT_BLOCK = 128
ATT_SCALE = ATT_HEAD_DIM ** -0.5
ROPE_THETA = 500000.0
ROPE_DIM = 16
MOE_N_GROUPS = 8
MOE_EPG = 8
MOE_N_EXPERTS = 64
MOE_D_FF = 256
MOE_BLOCK = 512
RMS_EPS = 1e-6

LANES = 128
CONV_HALO = 8
NEG_BIG = -1e30
VMEM_LIMIT = 56 * 1024 * 1024

COL_Z = SSD_D_INNER
COL_XBC = SSD_CONV_DIM
COL_DT = SSD_N_HEADS
COL_Q = ATT_N_HEADS * ATT_HEAD_DIM
COL_KV = ATT_N_KV * ATT_HEAD_DIM


LOG2E = 1.4426950408889634


def _sigmoid(x):
    return 1.0 / (1.0 + jnp.exp2(x * (-LOG2E)))


def _silu(x):
    return x * _sigmoid(x)


def _split3(x):
    hi = x.astype(bf16)
    r1 = x - hi.astype(f32)
    mid = r1.astype(bf16)
    lo = (r1 - mid.astype(f32)).astype(bf16)
    return hi, mid, lo


HALF_D = D_MODEL // 2
_HI_MASK = np.uint32(0xFFFF0000)


def _pack_rows(x):
    bits = pltpu.bitcast(x.astype(bf16).astype(f32), jnp.uint32)
    return (bits[:, HALF_D:] & _HI_MASK) | (bits[:, :HALF_D] >> 16)


def _unpack_rows(p):
    return pltpu.bitcast(p << 16, f32), pltpu.bitcast(p & _HI_MASK, f32)


def _dot(a, b):
    return jnp.dot(a, b, preferred_element_type=f32)


def _dot_nt(a, b):
    return lax.dot_general(a, b, (((1,), (1,)), ((), ())), preferred_element_type=f32)


def _trig_kernel(freq_ref, pos_ref, cos_ref, sin_ref):
    ang = pos_ref[...].astype(f32) * freq_ref[pl.program_id(0)]
    cos_ref[0] = jnp.cos(ang)
    sin_ref[0] = jnp.sin(ang)


def _rope_select():
    half = ROPE_DIM // 2
    sel = np.zeros((LANES, 3 * LANES), np.float32)
    for lane in range(LANES):
        m = lane % ATT_HEAD_DIM
        if m < half:
            sel[m, lane] = 1.0
            sel[half + m, LANES + lane] = -1.0
        elif m < ROPE_DIM:
            sel[m - half, lane] = 1.0
            sel[m, 2 * LANES + lane] = 1.0
        else:
            sel[ROPE_DIM, lane] = 1.0
    return jnp.asarray(sel, bf16)


def _rope_tables(positions):
    t = positions.size
    half = ROPE_DIM // 2
    inv_freq = ROPE_THETA ** (-jnp.arange(0, ROPE_DIM, 2, dtype=f32) / ROPE_DIM)
    pos2d = positions.reshape(t // LANES, LANES)
    cos_t, sin_t = pl.pallas_call(
        _trig_kernel,
        grid_spec=pltpu.PrefetchScalarGridSpec(
            num_scalar_prefetch=1,
            grid=(half,),
            in_specs=[pl.BlockSpec((t // LANES, LANES), lambda j, f: (0, 0))],
            out_specs=[pl.BlockSpec((1, t // LANES, LANES), lambda j, f: (j, 0, 0))] * 2,
        ),
        out_shape=[jax.ShapeDtypeStruct((half, t // LANES, LANES), f32)] * 2,
        name="trig",
    )(inv_freq, pos2d)
    return cos_t.reshape(half, t), sin_t.reshape(half, t)


INPROJ_TM = 512
INPROJ_CH = 512


def _inproj_kernel(x_ref, g_ref, w_ref,
                   z_ref, xbc_ref, q_ref, k_ref, v_ref, gs_ref, ga_ref, dt_ref, h_scr):
    x = x_ref[...]
    ms = jnp.mean(x * x, axis=-1, keepdims=True)
    h_scr[...] = (x * lax.rsqrt(ms + RMS_EPS) * g_ref[...]).astype(bf16)
    off = 0
    for ref in (z_ref, xbc_ref, q_ref, k_ref, v_ref, gs_ref, ga_ref, dt_ref):
        width = ref.shape[1]
        for c in range(0, width, INPROJ_CH):
            cw = min(INPROJ_CH, width - c)
            ref[:, c:c + cw] = _dot(h_scr[...], w_ref[:, off + c:off + c + cw]).astype(ref.dtype)
        off += width


WPREP_ROWS = 128


def _wprep_kernel(w_ref, o_ref):
    s1 = COL_Z + COL_XBC
    s2 = s1 + COL_DT
    n_tail = w_ref.shape[1] - s2
    o_ref[:, 0:s1] = w_ref[:, 0:s1].astype(bf16)
    o_ref[:, s1:s1 + n_tail] = w_ref[:, s2:s2 + n_tail].astype(bf16)
    lane = lax.broadcasted_iota(jnp.int32, (w_ref.shape[0], LANES), 1)
    o_ref[:, s1 + n_tail:s1 + n_tail + LANES] = jnp.where(lane < COL_DT, w_ref[:, s1:s1 + LANES], 0.0).astype(bf16)


def _inproj(x2, norm1_g, w_in):
    t = x2.shape[0]
    tm = min(INPROJ_TM, t)
    n_in = w_in.shape[1]
    n_all = n_in - COL_DT + LANES
    w_all = pl.pallas_call(
        _wprep_kernel,
        grid=(D_MODEL // WPREP_ROWS,),
        in_specs=[pl.BlockSpec((WPREP_ROWS, n_in), lambda i: (i, 0))],
        out_specs=pl.BlockSpec((WPREP_ROWS, n_all), lambda i: (i, 0)),
        out_shape=jax.ShapeDtypeStruct((D_MODEL, n_all), bf16),
        name="wprep",
    )(w_in)
    widths = (COL_Z, COL_XBC, COL_Q, COL_KV, COL_KV, D_MODEL, D_MODEL)
    const = lambda i: (0, 0)
    row = lambda i: (i, 0)
    outs = pl.pallas_call(
        _inproj_kernel,
        grid=(t // tm,),
        in_specs=[pl.BlockSpec((tm, D_MODEL), row),
                  pl.BlockSpec((1, D_MODEL), const),
                  pl.BlockSpec((D_MODEL, w_all.shape[1]), const, pipeline_mode=pl.Buffered(1))],
        out_specs=[pl.BlockSpec((tm, w), row) for w in widths] + [pl.BlockSpec((tm, LANES), row)],
        out_shape=[jax.ShapeDtypeStruct((t, w), bf16) for w in widths]
                  + [jax.ShapeDtypeStruct((t, LANES), f32)],
        scratch_shapes=[pltpu.VMEM((tm, D_MODEL), bf16)],
        compiler_params=pltpu.CompilerParams(dimension_semantics=("arbitrary",),
                                             vmem_limit_bytes=VMEM_LIMIT),
        name="inproj",
    )(x2, norm1_g.reshape(1, D_MODEL), w_all)
    return outs


SSD_CONV_YIELD = 1024

def _ssd_chunk_phases(rs, xbc_ref, z_ref, dt_ref, cw_ref, cb_ref, dtb_ref, alog_ref, dexp_ref, ng_ref,
                      ext_ref, st_ref, xs_ref, bm_ref, cm_ref, y_ref, hn_ref):
    L = SSD_CHUNK
    r0 = rs.start

    bf_tile = 2 * CONV_HALO
    ext_ref[CONV_HALO:2 * CONV_HALO, :] = xbc_ref[r0:r0 + bf_tile, :].astype(f32)[0:CONV_HALO]
    n_sh = SSD_CONV - 1
    sr = lax.broadcasted_iota(jnp.int32, (n_sh * L, L), 0)
    sc = lax.broadcasted_iota(jnp.int32, (n_sh * L, L), 1)
    shift = jnp.where((sr % L) - sc == (sr // L) + 1, 1.0, 0.0).astype(bf16)
    cch = 128
    for cc in range(0, SSD_CONV_DIM, cch):
        if cc and cc % SSD_CONV_YIELD == 0:
            yield
        cs_ = slice(cc, cc + cch)
        xb = xbc_ref[rs, cs_]
        sh = _dot(shift, xb)
        w_now = cw_ref[SSD_CONV - 1:SSD_CONV, cs_]
        acc = cb_ref[:, cs_] + xb.astype(f32) * w_now
        top = cb_ref[:, cs_] + ext_ref[CONV_HALO:2 * CONV_HALO, cs_] * w_now
        for j in range(1, SSD_CONV):
            w_j = cw_ref[SSD_CONV - 1 - j:SSD_CONV - j, cs_]
            acc = acc + sh[(j - 1) * L:j * L] * w_j
            top = top + ext_ref[CONV_HALO - j:2 * CONV_HALO - j, cs_] * w_j
        if cc < SSD_D_INNER:
            dst, o = xs_ref, cc
        elif cc < SSD_D_INNER + SSD_N_GROUPS * SSD_D_STATE:
            dst, o = bm_ref, cc - SSD_D_INNER
        else:
            dst, o = cm_ref, cc - SSD_D_INNER - SSD_N_GROUPS * SSD_D_STATE
        dst[rs, o:o + cch] = _silu(acc)
        dst[r0:r0 + CONV_HALO, o:o + cch] = _silu(top)
    ext_ref[0:CONV_HALO, :] = xbc_ref[rs.stop - bf_tile:rs.stop, :].astype(f32)[CONV_HALO:bf_tile]
    yield

    lane_row = lax.broadcasted_iota(jnp.int32, (1, LANES), 1)
    row_i = lax.broadcasted_iota(jnp.int32, (L, L), 0)
    col_i = lax.broadcasted_iota(jnp.int32, (L, L), 1)
    causal = row_i >= col_i
    left = col_i < SSD_HEAD_DIM

    xdt = dt_ref[rs, :] + dtb_ref[...]
    dtv = jnp.maximum(xdt, 0.0) + jnp.log1p(jnp.exp(-jnp.abs(xdt)))
    a = jnp.where(lane_row < SSD_N_HEADS, -jnp.exp(alog_ref[...]), 0.0)
    d_a = dtv * a
    tril = jnp.where(causal, 1.0, 0.0).astype(bf16)
    hi, mid, lo3 = _split3(d_a)
    a_cum = (_dot(tril, hi) + _dot(tril, mid) + _dot(tril, lo3)) * LOG2E
    a_end = a_cum[L - 1:L, :]
    exp_a = jnp.exp2(a_cum)
    w_end = jnp.exp2(a_end - a_cum) * dtv
    cd = jnp.exp2(a_end)
    a_t = a_cum.T
    dt_t = dtv.T
    w_t = w_end.T
    yield

    n_pairs = SSD_N_HEADS // 2
    pairs_per_group = n_pairs // SSD_N_GROUPS
    for g in range(SSD_N_GROUPS):
        b_g = bm_ref[rs, g * SSD_D_STATE:(g + 1) * SSD_D_STATE]
        c_g = cm_ref[rs, g * SSD_D_STATE:(g + 1) * SSD_D_STATE]
        cb = _dot_nt(c_g.astype(bf16), b_g.astype(bf16))
        b_t = b_g.T
        for pi in range(pairs_per_group):
            i = g * pairs_per_group + pi
            xpair = xs_ref[rs, i * LANES:(i + 1) * LANES]
            xpair_b = xpair.astype(bf16)
            s_prev = st_ref[i]
            rhs = jnp.concatenate([xpair_b, s_prev.astype(bf16)], axis=0)
            ys = []
            sn = []
            for h in (2 * i, 2 * i + 1):
                acol = jnp.broadcast_to(a_cum[:, h:h + 1], (L, L))
                arow = jnp.broadcast_to(a_t[h:h + 1, :], (L, L))
                dtrow = jnp.broadcast_to(dt_t[h:h + 1, :], (L, L))
                dec = jnp.exp2(jnp.where(causal, acol - arow, NEG_BIG))
                m = cb * dec * dtrow
                cs = c_g * jnp.broadcast_to(exp_a[:, h:h + 1], (L, L))
                lhs = jnp.concatenate([m.astype(bf16), cs.astype(bf16)], axis=1)
                ys.append(_dot(lhs, rhs))
                btw = (b_t * jnp.broadcast_to(w_t[h:h + 1, :], (L, L))).astype(bf16)
                sn.append(_dot(btw, xpair_b))
            h0 = 2 * i
            cd_pair = jnp.where(lane_row < SSD_HEAD_DIM, cd[:, h0:h0 + 1], cd[:, h0 + 1:h0 + 2])
            st_ref[i] = jnp.where(left, sn[0], sn[1]) + s_prev * cd_pair
            y_pair = jnp.where(left, ys[0], ys[1])
            y_ref[rs, i * LANES:(i + 1) * LANES] = y_pair + xpair * dexp_ref[:, i * LANES:(i + 1) * LANES]
            if pi % 2 == 1:
                yield

    gw = SSD_D_INNER // SSD_N_GROUPS
    for g in range(SSD_N_GROUPS):
        sl = slice(g * gw, (g + 1) * gw)
        yz = y_ref[rs, sl] * _silu(z_ref[rs, sl].astype(f32))
        ms = jnp.mean(yz * yz, axis=-1, keepdims=True)
        hn_ref[rs, sl] = (yz * lax.rsqrt(ms + RMS_EPS) * ng_ref[:, sl]).astype(bf16)
        if g % 2 == 1:
            yield


ROUTE_E, ROUTE_RANK, ROUTE_GATE = 0, 2, 4
ATT_SUB = 2
SSD_PLAN = (3,) * 9


def _mixer_kernel(sink_ref, q_ref, k_ref, v_ref, cos_ref, sin_ref, ga_ref, x_ref,
                  xbc_ref, z_ref, dt_ref, gs_ref,
                  sel_ref, qg_ref, kg_ref, wao_ref, wo_ref, n2g_ref, wrh_ref, wrl_ref, br_ref,
                  cw_ref, cb_ref, dtb_ref, alog_ref, dexp_ref, ng_ref, wssd_ref,
                  x1_ref, h2_ref, route_ref, route_t_ref, cnt_ref,
                  kprev_ref, vprev_ref, att_ref, cnt_scr, vd_ref,
                  ext_ref, st_ref, xs_ref, bm_ref, cm_ref, y_ref, hn_ref, ms_ref):
    Q = ATT_BLOCK
    b = pl.program_id(0)
    n = pl.program_id(1)

    @pl.when(n == 0)
    def _():
        kprev_ref[...] = jnp.zeros(kprev_ref.shape, f32)
        vprev_ref[...] = jnp.zeros(vprev_ref.shape, f32)
        ext_ref[0:CONV_HALO, :] = jnp.zeros((CONV_HALO, SSD_CONV_DIM), f32)
        st_ref[...] = jnp.zeros(st_ref.shape, f32)

    @pl.when((b == 0) & (n == 0))
    def _():
        cnt_scr[...] = jnp.zeros(cnt_scr.shape, f32)

    R = q_ref.shape[0]

    def ssd_all():
        for c in range(R // SSD_CHUNK):
            yield from _ssd_chunk_phases(slice(c * SSD_CHUNK, (c + 1) * SSD_CHUNK), xbc_ref, z_ref, dt_ref,
                                         cw_ref, cb_ref, dtb_ref, alog_ref, dexp_ref, ng_ref,
                                         ext_ref, st_ref, xs_ref, bm_ref, cm_ref, y_ref, hn_ref)

    ssd_gen = ssd_all()

    plan = iter(SSD_PLAN)

    def ssd_step():
        for _ in range(next(plan, 0)):
            next(ssd_gen, None)
    lane_q = lax.broadcasted_iota(jnp.int32, (Q, LANES), 1)
    row_q = lax.broadcasted_iota(jnp.int32, (Q, LANES), 0)
    left = lane_q < ATT_HEAD_DIM
    head_mean = jnp.where((row_q // ATT_HEAD_DIM) == (lane_q // ATT_HEAD_DIM),
                          1.0 / ATT_HEAD_DIM, 0.0).astype(bf16)
    lane2 = lax.broadcasted_iota(jnp.int32, (2 * Q, LANES), 1) < ATT_HEAD_DIM
    qg = ATT_N_HEADS // ATT_N_KV
    rows = qg * Q
    ri = lax.broadcasted_iota(jnp.int32, (rows, Q), 0) % Q
    cj = lax.broadcasted_iota(jnp.int32, (rows, Q), 1)
    upper = cj > ri
    half = ROPE_DIM // 2

    n_sb = R // Q
    nq = COL_Q // LANES
    nk = COL_KV // LANES
    chunks = []
    for sb in range(n_sb):
        rs = slice(sb * Q, (sb + 1) * Q)
        chunks += [q_ref[rs, c * LANES:(c + 1) * LANES].astype(f32) for c in range(nq)]
        chunks += [k_ref[rs, c * LANES:(c + 1) * LANES].astype(f32) for c in range(nk)]
    u_all = jnp.concatenate(chunks, axis=0)
    sq = u_all * u_all
    sq_hi = sq.astype(bf16)
    sq_lo = (sq - sq_hi.astype(f32)).astype(bf16)
    un_all = u_all * lax.rsqrt(_dot(sq_hi, head_mean) + _dot(sq_lo, head_mean) + RMS_EPS)
    ssd_step()

    terms = []
    for sb in range(n_sb):
        rs = slice(sb * Q, (sb + 1) * Q)
        cs = jnp.concatenate([cos_ref[:, rs], sin_ref[:, rs], jnp.ones((half, LANES), f32),
                              jnp.zeros((LANES - 3 * half, LANES), f32)], axis=0)
        terms += list(_split3(cs.T))
    pat_all = _dot(jnp.concatenate(terms, axis=0), sel_ref[...])

    pats, kds = [], []
    for sb in range(n_sb):
        rs = slice(sb * Q, (sb + 1) * Q)
        pat = pat_all[3 * sb * Q:(3 * sb + 1) * Q] + pat_all[(3 * sb + 1) * Q:(3 * sb + 2) * Q] \
            + pat_all[(3 * sb + 2) * Q:(3 * sb + 3) * Q]
        cpat = pat[:, 0:LANES]
        s1pat = pat[:, LANES:2 * LANES]
        s2pat = pat[:, 2 * LANES:3 * LANES]

        def norm_rope(idx, gpat):
            tn = un_all[idx * Q:(idx + 1) * Q] * gpat
            return (tn * cpat + pltpu.roll(tn, LANES - ROPE_DIM // 2, 1) * s1pat
                    + pltpu.roll(tn, ROPE_DIM // 2, 1) * s2pat)

        kd = []
        for cidx in range(nk):
            sl = slice(cidx * LANES, (cidx + 1) * LANES)
            k_cur = norm_rope(sb * (nq + nk) + nq + cidx, kg_ref[...])
            v_cur = v_ref[rs, sl].astype(f32)
            k_all = jnp.concatenate([kprev_ref[:, sl], k_cur], axis=0)
            v_all = jnp.concatenate([vprev_ref[:, sl], v_cur], axis=0)
            kprev_ref[:, sl] = k_cur
            vprev_ref[:, sl] = v_cur
            k_sw = pltpu.roll(k_all, ATT_HEAD_DIM, 1)
            v_sw = pltpu.roll(v_all, ATT_HEAD_DIM, 1)
            kd.append(jnp.where(lane2, k_all, k_sw).astype(bf16))
            kd.append(jnp.where(lane2, k_sw, k_all).astype(bf16))
            for half_i, vv in enumerate((jnp.where(lane2, v_all, v_sw), jnp.where(lane2, v_sw, v_all))):
                vd_ref[sb * ATT_N_KV + 2 * cidx + half_i] = vv.astype(bf16)

        pats.append((cpat, s1pat, s2pat))
        kds.append(kd)
        ssd_step()

    def rope_q(sb, cidx):
        cpat, s1pat, s2pat = pats[sb]
        tn = un_all[(sb * (nq + nk) + cidx) * Q:(sb * (nq + nk) + cidx + 1) * Q] * qg_ref[...]
        return (tn * cpat + pltpu.roll(tn, LANES - ROPE_DIM // 2, 1) * s1pat
                + pltpu.roll(tn, ROPE_DIM // 2, 1) * s2pat)

    def score_tile(g):
        sb, h = divmod(g, ATT_N_KV)
        parts = []
        for cidx in (2 * h, 2 * h + 1):
            qc = rope_q(sb, cidx)
            parts.append(jnp.where(left, qc, 0.0).astype(bf16))
            parts.append(jnp.where(left, 0.0, qc).astype(bf16))
        lhs = jnp.concatenate(parts, axis=0)
        s_both = _dot_nt(lhs, kds[sb][h])
        s_prev = s_both[:, 0:Q]
        s_cur = s_both[:, Q:2 * Q]
        if sb == 0:
            return jnp.where(upper & (n > 0), s_prev, jnp.where(upper, NEG_BIG, s_cur))
        return jnp.where(upper, s_prev, s_cur)

    ones = jnp.ones((Q, Q), bf16)

    def softmax_block(s):
        sink = jnp.concatenate([jnp.full((Q, 1), sink_ref[i] * LOG2E, f32) for i in range(ATT_N_HEADS)],
                               axis=0)
        m = jnp.maximum(jnp.max(s, axis=-1, keepdims=True), sink)
        p = jnp.exp2(s - m)
        p_hi = p.astype(bf16)
        p_lo = (p - p_hi.astype(f32)).astype(bf16)
        denom = _dot(p_hi, ones) + _dot(p_lo, ones) + jnp.exp2(sink - m)
        return (p / denom).astype(bf16)

    def pv_tile(g, pf):
        sb, h = divmod(g, ATT_N_KV)
        rs = slice(sb * Q, (sb + 1) * Q)
        zero = jnp.zeros_like(pf)
        p_both = jnp.concatenate([jnp.where(upper, pf, zero), jnp.where(upper, zero, pf)], axis=1)
        o = _dot(p_both, vd_ref[g])
        for r in range(2):
            cidx = 2 * h + r
            att_ref[rs, cidx * LANES:(cidx + 1) * LANES] = jnp.where(
                left, o[(2 * r) * Q:(2 * r + 1) * Q], o[(2 * r + 1) * Q:(2 * r + 2) * Q]).astype(bf16)

    def epilogue(rs, cnt):
        E = rs.stop - rs.start
        lane = lax.broadcasted_iota(jnp.int32, (E, LANES), 1)
        y_att = _dot(att_ref[rs, :], wao_ref[...])
        merged = _sigmoid(ga_ref[rs, :].astype(f32)) * y_att + ms_ref[rs, :]
        x1 = x_ref[rs, :] + _dot(merged.astype(bf16), wo_ref[...])
        x1_ref[rs, :] = x1
        h2 = x1 * lax.rsqrt(jnp.mean(x1 * x1, axis=-1, keepdims=True) + RMS_EPS) * n2g_ref[...]
        h2_ref[rs, :] = _pack_rows(h2)

        hi = h2.astype(bf16)
        lo = (h2 - hi.astype(f32)).astype(bf16)
        logits = _dot(hi, wrh_ref[...]) + _dot(lo, wrh_ref[...]) + _dot(hi, wrl_ref[...]) + br_ref[...]
        big = 4 * LANES
        gl = jnp.where(lane < MOE_N_GROUPS, logits, NEG_BIG)
        gmax = jnp.max(gl, axis=-1, keepdims=True)
        gsel = jnp.min(jnp.where(gl == gmax, lane, big), axis=-1, keepdims=True)
        pg = 1.0 / jnp.sum(jnp.exp(gl - gmax), axis=-1, keepdims=True)
        lo_l = MOE_N_GROUPS + MOE_EPG * gsel
        el = jnp.where((lane >= lo_l) & (lane < lo_l + MOE_EPG), logits, NEG_BIG)
        v1 = jnp.max(el, axis=-1, keepdims=True)
        i1 = jnp.min(jnp.where(el == v1, lane, big), axis=-1, keepdims=True)
        el2 = jnp.where(lane == i1, NEG_BIG, el)
        v2 = jnp.max(el2, axis=-1, keepdims=True)
        i2 = jnp.min(jnp.where(el2 == v2, lane, big), axis=-1, keepdims=True)
        e1 = i1 - MOE_N_GROUPS
        e2 = i2 - MOE_N_GROUPS
        tt = jnp.exp(v2 - v1)
        w1 = pg * (1.0 / (1.0 + tt))
        w2 = pg * (tt / (1.0 + tt))

        onehot = jnp.where((lane == e1) | (lane == e2), 1.0, 0.0)
        strict = jnp.where(lax.broadcasted_iota(jnp.int32, (E, E), 0) > lax.broadcasted_iota(jnp.int32, (E, E), 1),
                           1.0, 0.0).astype(bf16)
        base = _dot(strict, onehot.astype(bf16)) + cnt
        r1 = jnp.sum(jnp.where(lane == e1, base, 0.0), axis=-1, keepdims=True)
        r2 = jnp.sum(jnp.where(lane == e2, base, 0.0), axis=-1, keepdims=True)

        rec = jnp.zeros((E, LANES), f32)
        for off, val in ((ROUTE_E, e1.astype(f32)), (ROUTE_E + 1, e2.astype(f32)),
                         (ROUTE_RANK, r1), (ROUTE_RANK + 1, r2), (ROUTE_GATE, w1), (ROUTE_GATE + 1, w2)):
            rec = jnp.where(lane == off, val, rec)
        route_ref[rs, :] = rec
        for o in range(0, E, Q):
            route_t_ref[:, rs.start + o:rs.start + o + Q] = rec[o:o + Q].T[0:8, :]
        return cnt + jnp.sum(onehot, axis=0, keepdims=True)

    s_tiles, p_tiles = {}, {}
    for stage in range(n_sb + 2):
        if stage < n_sb:
            s_tiles[stage] = jnp.concatenate(
                [score_tile(stage * ATT_N_KV + h) for h in range(ATT_N_KV)], axis=0)
            ssd_step()
        if 0 <= stage - 1 < n_sb:
            p_tiles[stage - 1] = softmax_block(s_tiles.pop(stage - 1))
            ssd_step()
        if 0 <= stage - 2 < n_sb:
            pb = p_tiles.pop(stage - 2)
            for h in range(ATT_N_KV):
                pv_tile((stage - 2) * ATT_N_KV + h, pb[h * rows:(h + 1) * rows])
            ssd_step()

    for _ in ssd_gen:
        pass
    ms_ref[...] = _sigmoid(gs_ref[...].astype(f32)) * _dot(hn_ref[...], wssd_ref[...])
    cnt = epilogue(slice(0, R), cnt_scr[0:1, :])

    cnt_scr[...] = jnp.broadcast_to(cnt, cnt_scr.shape)
    cnt_ref[...] = jnp.broadcast_to(cnt, cnt_ref.shape)


def _mixer(q, k, v, cos_t, sin_t, ga, x2, xbc, z, dt, gs, q_norm_g, k_norm_g, sinks, w_attn_out, w_out,
           norm2_g, w_rg, b_rg, w_re, b_re, conv_w, conv_b, dt_bias, a_log, d_skip, ssd_norm_g, w_ssd_out,
           batch, seq):
    t = batch * seq
    Q = ATT_SUB * ATT_BLOCK
    nb = seq // Q
    pad_h = LANES - SSD_N_HEADS
    dtb = jnp.pad(dt_bias, (0, pad_h)).reshape(1, LANES)
    alog = jnp.pad(a_log, (0, pad_h)).reshape(1, LANES)
    dexp = jnp.repeat(d_skip, SSD_HEAD_DIM).reshape(1, SSD_D_INNER)
    rep = LANES // ATT_HEAD_DIM
    qg = (jnp.tile(q_norm_g, rep) * (ATT_SCALE * LOG2E)).reshape(1, LANES)
    kg = jnp.tile(k_norm_g, rep).reshape(1, LANES)
    n_log = MOE_N_GROUPS + MOE_N_EXPERTS
    w_r = jnp.concatenate([w_rg, jnp.transpose(w_re, (1, 0, 2)).reshape(D_MODEL, MOE_N_EXPERTS),
                           jnp.zeros((D_MODEL, LANES - n_log), f32)], axis=1)
    b_r = jnp.concatenate([b_rg, b_re.reshape(-1), jnp.zeros((LANES - n_log,), f32)]).reshape(1, LANES)
    w_r_hi = w_r.astype(bf16)
    w_r_lo = (w_r - w_r_hi.astype(f32)).astype(bf16)
    const = lambda b, n, *_: (0, 0)
    row = lambda b, n, *_: (b * nb + n, 0)
    col = lambda b, n, *_: (0, b * nb + n)
    full = lambda shape: pl.BlockSpec(shape, const, pipeline_mode=pl.Buffered(1))
    return pl.pallas_call(
        _mixer_kernel,
        grid_spec=pltpu.PrefetchScalarGridSpec(
            num_scalar_prefetch=1,
            grid=(batch, nb),
            in_specs=[pl.BlockSpec((Q, COL_Q), row),
                      pl.BlockSpec((Q, COL_KV), row),
                      pl.BlockSpec((Q, COL_KV), row),
                      pl.BlockSpec((ROPE_DIM // 2, Q), col),
                      pl.BlockSpec((ROPE_DIM // 2, Q), col),
                      pl.BlockSpec((Q, D_MODEL), row),
                      pl.BlockSpec((Q, D_MODEL), row),
                      pl.BlockSpec((Q, SSD_CONV_DIM), row),
                      pl.BlockSpec((Q, SSD_D_INNER), row),
                      pl.BlockSpec((Q, LANES), row),
                      pl.BlockSpec((Q, D_MODEL), row),
                      full((LANES, 3 * LANES)),
                      pl.BlockSpec((1, LANES), const),
                      pl.BlockSpec((1, LANES), const),
                      full((COL_Q, D_MODEL)),
                      full((D_MODEL, D_MODEL)),
                      pl.BlockSpec((1, D_MODEL), const),
                      full((D_MODEL, LANES)),
                      full((D_MODEL, LANES)),
                      pl.BlockSpec((1, LANES), const),
                      pl.BlockSpec((SSD_CONV, SSD_CONV_DIM), const),
                      pl.BlockSpec((1, SSD_CONV_DIM), const),
                      pl.BlockSpec((1, LANES), const),
                      pl.BlockSpec((1, LANES), const),
                      pl.BlockSpec((1, SSD_D_INNER), const),
                      pl.BlockSpec((1, SSD_D_INNER), const),
                      full((SSD_D_INNER, D_MODEL))],
            out_specs=[pl.BlockSpec((Q, D_MODEL), row),
                       pl.BlockSpec((Q, HALF_D), row),
                       pl.BlockSpec((Q, LANES), row),
                       pl.BlockSpec((8, Q), col),
                       pl.BlockSpec((8, LANES), const)],
            scratch_shapes=[pltpu.VMEM((ATT_BLOCK, COL_KV), f32),
                            pltpu.VMEM((ATT_BLOCK, COL_KV), f32),
                            pltpu.VMEM((Q, COL_Q), bf16),
                            pltpu.VMEM((8, LANES), f32),
                            pltpu.VMEM((ATT_SUB * ATT_N_KV, 2 * ATT_BLOCK, LANES), bf16),
                            pltpu.VMEM((2 * CONV_HALO, SSD_CONV_DIM), f32),
                            pltpu.VMEM((SSD_N_HEADS // 2, SSD_D_STATE, LANES), f32),
                            pltpu.VMEM((Q, SSD_D_INNER), f32),
                            pltpu.VMEM((Q, SSD_N_GROUPS * SSD_D_STATE), f32),
                            pltpu.VMEM((Q, SSD_N_GROUPS * SSD_D_STATE), f32),
                            pltpu.VMEM((Q, SSD_D_INNER), f32),
                            pltpu.VMEM((Q, SSD_D_INNER), bf16),
                            pltpu.VMEM((Q, D_MODEL), f32)],
        ),
        out_shape=[jax.ShapeDtypeStruct((t, D_MODEL), f32),
                   jax.ShapeDtypeStruct((t, HALF_D), jnp.uint32),
                   jax.ShapeDtypeStruct((t, LANES), f32),
                   jax.ShapeDtypeStruct((8, t), f32),
                   jax.ShapeDtypeStruct((8, LANES), f32)],
        compiler_params=pltpu.CompilerParams(dimension_semantics=("arbitrary", "arbitrary"),
                                             vmem_limit_bytes=VMEM_LIMIT),
        name="mixer",
    )(sinks, q, k, v, cos_t, sin_t, ga, x2, xbc, z, dt, gs,
      _rope_select(), qg, kg, w_attn_out.astype(bf16), w_out.astype(bf16), norm2_g.reshape(1, D_MODEL),
      w_r_hi, w_r_lo, b_r,
      conv_w, conv_b.reshape(1, -1), dtb, alog, dexp, ssd_norm_g.reshape(1, -1), w_ssd_out.astype(bf16))


MOE_TM = 512
MOE_UNROLL = 8


def _row_copy(src, i, dst, j, sem):
    return pltpu.make_async_copy(src.at[pl.ds(i, 1), :], dst.at[pl.ds(j, 1), :], sem)


def _dispatch_kernel(dest0_ref, dest1_ref, zblk_ref, h2_ref, xs_hbm, zero_ref, sem, zsem):
    tm = h2_ref.shape[0]
    base = pl.program_id(0) * tm

    @pl.when(pl.program_id(0) == 0)
    def _():
        zero_ref[...] = jnp.zeros(zero_ref.shape, zero_ref.dtype)

        def zcopy(i):
            start = pl.multiple_of(zblk_ref[i] * MOE_BLOCK, MOE_BLOCK)
            return pltpu.make_async_copy(zero_ref, xs_hbm.at[pl.ds(start, MOE_BLOCK), :], zsem)

        def zstart(i, carry):
            @pl.when(zblk_ref[i] >= 0)
            def _():
                zcopy(i).start()
            return carry

        def zwait(i, carry):
            @pl.when(zblk_ref[i] >= 0)
            def _():
                zcopy(i).wait()
            return carry

        lax.fori_loop(0, zblk_ref.shape[0], zstart, 0)
        lax.fori_loop(0, zblk_ref.shape[0], zwait, 0)

    def issue(i, carry):
        _row_copy(h2_ref, i, xs_hbm, dest0_ref[base + i], sem).start()
        _row_copy(h2_ref, i, xs_hbm, dest1_ref[base + i], sem).start()
        return carry

    lax.fori_loop(0, tm, issue, 0, unroll=MOE_UNROLL)
    for _ in range(2):
        pltpu.make_async_copy(h2_ref, xs_hbm.at[pl.ds(0, tm), :], sem).wait()


def _dispatch(h2, dest0, dest1, zero_blocks, n_rows):
    t = h2.shape[0]
    tm = min(MOE_TM, t)
    return pl.pallas_call(
        _dispatch_kernel,
        grid_spec=pltpu.PrefetchScalarGridSpec(
            num_scalar_prefetch=3,
            grid=(t // tm,),
            in_specs=[pl.BlockSpec((tm, HALF_D), lambda i, d0, d1, zb: (i, 0))],
            out_specs=pl.BlockSpec(memory_space=pl.ANY),
            scratch_shapes=[pltpu.VMEM((MOE_BLOCK, HALF_D), h2.dtype),
                            pltpu.SemaphoreType.DMA(()),
                            pltpu.SemaphoreType.DMA(())],
        ),
        out_shape=jax.ShapeDtypeStruct((n_rows, HALF_D), h2.dtype),
        compiler_params=pltpu.CompilerParams(dimension_semantics=("arbitrary",)),
        name="dispatch",
    )(dest0, dest1, zero_blocks, h2)


EXPERT_SUB = 1


def _expert_kernel(be_ref, nu_ref, xs_ref, *refs):
    w_refs = refs[:3 * EXPERT_SUB]
    ys_ref, wgu_scr, wd_scr = refs[3 * EXPERT_SUB:]
    i = pl.program_id(0)
    rows = MOE_BLOCK

    for j in range(EXPERT_SUB):
        blk = EXPERT_SUB * i + j
        fresh = (i == 0) | (be_ref[blk] != be_ref[jnp.maximum(blk - EXPERT_SUB, 0)])

        @pl.when((blk < nu_ref[0]) & fresh)
        def _(j=j):
            wg_ref, wu_ref, wd_ref = w_refs[3 * j:3 * j + 3]
            wgu_scr[j, :, 0:MOE_D_FF] = wg_ref[0].astype(bf16)
            wgu_scr[j, :, MOE_D_FF:2 * MOE_D_FF] = wu_ref[0].astype(bf16)
            wd_scr[j] = wd_ref[0].astype(bf16)

    def block(j):
        rs = slice(j * rows, (j + 1) * rows)
        x_lo, x_hi = _unpack_rows(xs_ref[rs, :])
        gu = (_dot(x_lo.astype(bf16), wgu_scr[j, 0:HALF_D, :])
              + _dot(x_hi.astype(bf16), wgu_scr[j, HALF_D:D_MODEL, :]))
        hid = _silu(gu[:, :MOE_D_FF]) * gu[:, MOE_D_FF:]
        ys_ref[rs, :] = _pack_rows(_dot(hid.astype(bf16), wd_scr[j]))

    n_live = jnp.clip(nu_ref[0] - EXPERT_SUB * i, 0, EXPERT_SUB)
    for live in range(EXPERT_SUB + 1):
        @pl.when(n_live == live)
        def _(live=live):
            for j in range(live):
                block(j)
            if live < EXPERT_SUB:
                ys_ref[live * rows:, :] = jnp.zeros(((EXPERT_SUB - live) * rows, HALF_D), ys_ref.dtype)


def _experts(xs, block_e, n_used, w_g, w_u, w_d):
    n_rows = xs.shape[0]
    rows = EXPERT_SUB * MOE_BLOCK
    nstep = n_rows // rows
    blk_in = lambda i, be, nu: (jnp.minimum(i, (nu[0] - 1) // EXPERT_SUB), 0)
    blk_out = lambda i, be, nu: (i, 0)
    w_specs = []
    for j in range(EXPERT_SUB):
        wsel = lambda i, be, nu, j=j: (be[EXPERT_SUB * i + j], 0, 0)
        w_specs += [pl.BlockSpec((1, D_MODEL, MOE_D_FF), wsel),
                    pl.BlockSpec((1, D_MODEL, MOE_D_FF), wsel),
                    pl.BlockSpec((1, MOE_D_FF, D_MODEL), wsel)]
    return pl.pallas_call(
        _expert_kernel,
        grid_spec=pltpu.PrefetchScalarGridSpec(
            num_scalar_prefetch=2,
            grid=(nstep,),
            in_specs=[pl.BlockSpec((rows, HALF_D), blk_in)] + w_specs,
            out_specs=pl.BlockSpec((rows, HALF_D), blk_out),
            scratch_shapes=[pltpu.VMEM((EXPERT_SUB, D_MODEL, 2 * MOE_D_FF), bf16),
                            pltpu.VMEM((EXPERT_SUB, MOE_D_FF, D_MODEL), bf16)],
        ),
        out_shape=jax.ShapeDtypeStruct((n_rows, HALF_D), xs.dtype),
        compiler_params=pltpu.CompilerParams(dimension_semantics=("arbitrary",),
                                             vmem_limit_bytes=VMEM_LIMIT),
        name="experts",
    )(block_e, n_used, xs, *([w_g, w_u, w_d] * EXPERT_SUB))


def _combine_kernel(dest0_ref, dest1_ref, x1_ref, route_ref, ys_hbm, out_ref, y0_ref, y1_ref, sem):
    tm = x1_ref.shape[0]
    step = pl.program_id(0)

    def gather_tile(tile, slot):
        base = tile * tm

        def issue(r, carry):
            pltpu.make_async_copy(ys_hbm.at[pl.ds(dest0_ref[base + r], 1), :],
                                  y0_ref.at[slot, pl.ds(r, 1), :], sem.at[slot]).start()
            pltpu.make_async_copy(ys_hbm.at[pl.ds(dest1_ref[base + r], 1), :],
                                  y1_ref.at[slot, pl.ds(r, 1), :], sem.at[slot]).start()
            return carry

        lax.fori_loop(0, tm, issue, 0, unroll=MOE_UNROLL)

    @pl.when(step == 0)
    def _():
        gather_tile(0, 0)

    @pl.when(step + 1 < pl.num_programs(0))
    def _():
        gather_tile(step + 1, (step + 1) % 2)

    slot = step % 2
    for y_ref in (y0_ref, y1_ref):
        pltpu.make_async_copy(ys_hbm.at[pl.ds(0, tm), :], y_ref.at[slot], sem.at[slot]).wait()
    g0 = route_ref[:, ROUTE_GATE:ROUTE_GATE + 1]
    g1 = route_ref[:, ROUTE_GATE + 1:ROUTE_GATE + 2]
    y0_lo, y0_hi = _unpack_rows(y0_ref[slot])
    y1_lo, y1_hi = _unpack_rows(y1_ref[slot])
    out_ref[:, 0:HALF_D] = x1_ref[:, 0:HALF_D] + y0_lo * g0 + y1_lo * g1
    out_ref[:, HALF_D:D_MODEL] = x1_ref[:, HALF_D:D_MODEL] + y0_hi * g0 + y1_hi * g1


def _combine(x1, route, ys, dest0, dest1):
    t = x1.shape[0]
    tm = min(MOE_TM, t)
    row = lambda i, d0, d1: (i, 0)
    return pl.pallas_call(
        _combine_kernel,
        grid_spec=pltpu.PrefetchScalarGridSpec(
            num_scalar_prefetch=2,
            grid=(t // tm,),
            in_specs=[pl.BlockSpec((tm, D_MODEL), row),
                      pl.BlockSpec((tm, LANES), row),
                      pl.BlockSpec(memory_space=pl.ANY)],
            out_specs=pl.BlockSpec((tm, D_MODEL), row),
            scratch_shapes=[pltpu.VMEM((2, tm, HALF_D), ys.dtype),
                            pltpu.VMEM((2, tm, HALF_D), ys.dtype),
                            pltpu.SemaphoreType.DMA((2,))],
        ),
        out_shape=jax.ShapeDtypeStruct((t, D_MODEL), f32),
        compiler_params=pltpu.CompilerParams(dimension_semantics=("arbitrary",)),
        name="combine",
    )(dest0, dest1, x1, route, ys)


def _moe(x1, h2, route, route_t, counts_rec, w_gate_e, w_up_e, w_down_e):
    t = x1.shape[0]
    n_rows = 2 * t + MOE_N_EXPERTS * MOE_BLOCK
    nblk = n_rows // MOE_BLOCK
    counts = counts_rec[0, :MOE_N_EXPERTS].astype(jnp.int32)
    padded = (counts + MOE_BLOCK - 1) // MOE_BLOCK * MOE_BLOCK
    pend = jnp.cumsum(padded)
    pstart = pend - padded
    expert_ids = jnp.arange(MOE_N_EXPERTS, dtype=jnp.int32)[:, None]

    def sorted_row(k):
        eid = route_t[ROUTE_E + k].astype(jnp.int32)
        rank = route_t[ROUTE_RANK + k].astype(jnp.int32)
        return jnp.sum(jnp.where(eid[None, :] == expert_ids, pstart[:, None], 0), axis=0) + rank

    dest0, dest1 = sorted_row(0), sorted_row(1)
    block_start = jnp.arange(nblk, dtype=jnp.int32) * MOE_BLOCK
    block_e = jnp.minimum(jnp.sum((pend[None, :] <= block_start[:, None]).astype(jnp.int32), axis=1),
                          MOE_N_EXPERTS - 1)
    n_used = (pend[-1:] // MOE_BLOCK).astype(jnp.int32)
    last_blk = jnp.where(counts > 0, pend // MOE_BLOCK - 1, -1)
    tail = n_used[0] + jnp.arange(MOE_N_EXPERTS, dtype=jnp.int32)
    zero_blocks = jnp.concatenate([last_blk, jnp.where(tail < nblk, tail, -1)]).astype(jnp.int32)
    xs = _dispatch(h2, dest0, dest1, zero_blocks, n_rows)
    ys = _experts(xs, block_e, n_used, w_gate_e, w_up_e, w_down_e)
    return _combine(x1, route, ys, dest0, dest1)


def _layer(x, positions, norm1_g, w_in, conv_w, conv_b, dt_bias, a_log, d_skip, ssd_norm_g, w_ssd_out,
           q_norm_g, k_norm_g, sinks, w_attn_out, w_out, norm2_g, w_rg, b_rg, w_re, b_re,
           w_gate_e, w_up_e, w_down_e):
    batch, seq, d = x.shape
    x2 = x.reshape(batch * seq, d)
    cos_t, sin_t = _rope_tables(positions)
    z, xbc, q, k, v, gs, ga, dt = _inproj(x2, norm1_g, w_in)
    x1, h2, route, route_t, counts = _mixer(q, k, v, cos_t, sin_t, ga, x2, xbc, z, dt, gs, q_norm_g, k_norm_g,
                                            sinks, w_attn_out, w_out, norm2_g, w_rg, b_rg, w_re, b_re,
                                            conv_w, conv_b, dt_bias, a_log, d_skip, ssd_norm_g, w_ssd_out,
                                            batch, seq)
    out = _moe(x1, h2, route, route_t, counts, w_gate_e, w_up_e, w_down_e)
    return out.reshape(batch, seq, d)


def kernel(x, positions, norm1_g, w_in, conv_w, conv_b, dt_bias, a_log, d_skip, ssd_norm_g, w_ssd_out,
           q_norm_g, k_norm_g, sinks, w_attn_out, w_out, norm2_g, w_router_group, b_router_group,
           w_router_expert, b_router_expert, w_gate_e, w_up_e, w_down_e):
    for l in range(norm1_g.shape[0]):
        x = _layer(x, positions, norm1_g[l], w_in[l], conv_w[l], conv_b[l], dt_bias[l], a_log[l],
                   d_skip[l], ssd_norm_g[l], w_ssd_out[l], q_norm_g[l], k_norm_g[l], sinks[l],
                   w_attn_out[l], w_out[l], norm2_g[l], w_router_group[l], b_router_group[l],
                   w_router_expert[l], b_router_expert[l], w_gate_e[l], w_up_e[l], w_down_e[l])
    return x
```

```python
import functools

import numpy as np

import jax
import jax.numpy as jnp
from jax import lax
from jax.experimental import pallas as pl
from jax.experimental.pallas import tpu as pltpu

f32 = jnp.float32
bf16 = jnp.bfloat16

D_MODEL = 1024
SSD_D_INNER = 2048
SSD_HEAD_DIM = 64
SSD_N_HEADS = 32
SSD_N_GROUPS = 4
SSD_D_STATE = 128
SSD_CONV = 4
SSD_CHUNK = 128
SSD_CONV_DIM = 3072
ATT_HEAD_DIM = 64
ATT_N_HEADS = 16
ATT_N_KV = 4
ATT_BLOCK = 128
ATT_SCALE = ATT_HEAD_DIM ** -0.5
ROPE_THETA = 500000.0
ROPE_DIM = 16
MOE_N_GROUPS = 8
MOE_EPG = 8
MOE_N_EXPERTS = 64
MOE_D_FF = 256
MOE_BLOCK = 512
RMS_EPS = 1e-6

LANES = 128
SUBLANES = 8
CONV_HALO = 8
NEG_BIG = -1e30
VMEM_LIMIT = 56 * 1024 * 1024

COL_Z = SSD_D_INNER
COL_XBC = SSD_CONV_DIM
COL_DT = SSD_N_HEADS
COL_Q = ATT_N_HEADS * ATT_HEAD_DIM
COL_KV = ATT_N_KV * ATT_HEAD_DIM


LOG2E = 1.4426950408889634


def _sigmoid(x):
    return 1.0 / (1.0 + jnp.exp2(x * (-LOG2E)))


def _silu(x):
    return x * _sigmoid(x)


def _split3(x):
    hi = x.astype(bf16)
    r1 = x - hi.astype(f32)
    mid = r1.astype(bf16)
    lo = (r1 - mid.astype(f32)).astype(bf16)
    return hi, mid, lo


HALF_D = D_MODEL // 2
_HI_MASK = np.uint32(0xFFFF0000)


def _pack_rows(x):
    bits = pltpu.bitcast(x.astype(bf16).astype(f32), jnp.uint32)
    return (bits[:, HALF_D:] & _HI_MASK) | (bits[:, :HALF_D] >> 16)


def _unpack_rows(p):
    return pltpu.bitcast(p << 16, f32), pltpu.bitcast(p & _HI_MASK, f32)


def _dot(a, b):
    return jnp.dot(a, b, preferred_element_type=f32)


def _dot_nt(a, b):
    return lax.dot_general(a, b, (((1,), (1,)), ((), ())), preferred_element_type=f32)


def _trig_kernel(freq_ref, pos_ref, cos_ref, sin_ref):
    ang = pos_ref[...].astype(f32) * freq_ref[pl.program_id(0)]
    cos_ref[0] = jnp.cos(ang)
    sin_ref[0] = jnp.sin(ang)


def _rope_select():
    half = ROPE_DIM // 2
    sel = np.zeros((LANES, 3 * LANES), np.float32)
    for lane in range(LANES):
        m = lane % ATT_HEAD_DIM
        if m < half:
            sel[m, lane] = 1.0
            sel[half + m, LANES + lane] = -1.0
        elif m < ROPE_DIM:
            sel[m - half, lane] = 1.0
            sel[m, 2 * LANES + lane] = 1.0
        else:
            sel[ROPE_DIM, lane] = 1.0
    return jnp.asarray(sel, bf16)


def _rope_tables(positions):
    t = positions.size
    half = ROPE_DIM // 2
    inv_freq = ROPE_THETA ** (-jnp.arange(0, ROPE_DIM, 2, dtype=f32) / ROPE_DIM)
    pos2d = positions.reshape(t // LANES, LANES)
    cos_t, sin_t = pl.pallas_call(
        _trig_kernel,
        grid_spec=pltpu.PrefetchScalarGridSpec(
            num_scalar_prefetch=1,
            grid=(half,),
            in_specs=[pl.BlockSpec((t // LANES, LANES), lambda j, f: (0, 0))],
            out_specs=[pl.BlockSpec((1, t // LANES, LANES), lambda j, f: (j, 0, 0))] * 2,
        ),
        out_shape=[jax.ShapeDtypeStruct((half, t // LANES, LANES), f32)] * 2,
        name="trig",
    )(inv_freq, pos2d)
    return cos_t.reshape(half, t), sin_t.reshape(half, t)


INPROJ_TM = 512
INPROJ_CH = 512


def _inproj_kernel(x_ref, g_ref, w_ref,
                   z_ref, xbc_ref, q_ref, k_ref, v_ref, gs_ref, ga_ref, dt_ref, h_scr):
    x = x_ref[...]
    ms = jnp.mean(x * x, axis=-1, keepdims=True)
    h_scr[...] = (x * lax.rsqrt(ms + RMS_EPS) * g_ref[...]).astype(bf16)
    off = 0
    for ref in (z_ref, xbc_ref, q_ref, k_ref, v_ref, gs_ref, ga_ref, dt_ref):
        width = ref.shape[1]
        for c in range(0, width, INPROJ_CH):
            cw = min(INPROJ_CH, width - c)
            ref[:, c:c + cw] = _dot(h_scr[...], w_ref[:, off + c:off + c + cw]).astype(ref.dtype)
        off += width


WPREP_ROWS = 128


def _wprep_kernel(w_ref, o_ref):
    s1 = COL_Z + COL_XBC
    s2 = s1 + COL_DT
    n_tail = w_ref.shape[1] - s2
    o_ref[:, 0:s1] = w_ref[:, 0:s1].astype(bf16)
    o_ref[:, s1:s1 + n_tail] = w_ref[:, s2:s2 + n_tail].astype(bf16)
    lane = lax.broadcasted_iota(jnp.int32, (w_ref.shape[0], LANES), 1)
    o_ref[:, s1 + n_tail:s1 + n_tail + LANES] = jnp.where(lane < COL_DT, w_ref[:, s1:s1 + LANES], 0.0).astype(bf16)


def _inproj(x2, norm1_g, w_in):
    t = x2.shape[0]
    tm = min(INPROJ_TM, t)
    n_in = w_in.shape[1]
    n_all = n_in - COL_DT + LANES
    w_all = pl.pallas_call(
        _wprep_kernel,
        grid=(D_MODEL // WPREP_ROWS,),
        in_specs=[pl.BlockSpec((WPREP_ROWS, n_in), lambda i: (i, 0))],
        out_specs=pl.BlockSpec((WPREP_ROWS, n_all), lambda i: (i, 0)),
        out_shape=jax.ShapeDtypeStruct((D_MODEL, n_all), bf16),
        name="wprep",
    )(w_in)
    widths = (COL_Z, COL_XBC, COL_Q, COL_KV, COL_KV, D_MODEL, D_MODEL)
    const = lambda i: (0, 0)
    row = lambda i: (i, 0)
    outs = pl.pallas_call(
        _inproj_kernel,
        grid=(t // tm,),
        in_specs=[pl.BlockSpec((tm, D_MODEL), row),
                  pl.BlockSpec((1, D_MODEL), const),
                  pl.BlockSpec((D_MODEL, w_all.shape[1]), const, pipeline_mode=pl.Buffered(1))],
        out_specs=[pl.BlockSpec((tm, w), row) for w in widths] + [pl.BlockSpec((tm, LANES), row)],
        out_shape=[jax.ShapeDtypeStruct((t, w), bf16) for w in widths]
                  + [jax.ShapeDtypeStruct((t, LANES), f32)],
        scratch_shapes=[pltpu.VMEM((tm, D_MODEL), bf16)],
        compiler_params=pltpu.CompilerParams(dimension_semantics=("arbitrary",),
                                             vmem_limit_bytes=VMEM_LIMIT),
        name="inproj",
    )(x2, norm1_g.reshape(1, D_MODEL), w_all)
    return outs


SSD_CONV_YIELD = 1024

def _ssd_chunk_phases(rs, xbc_ref, z_ref, dt_ref, cw_ref, cb_ref, dtb_ref, alog_ref, dexp_ref, ng_ref,
                      ext_ref, st_ref, xs_ref, bm_ref, cm_ref, y_ref, hn_ref):
    L = SSD_CHUNK
    r0 = rs.start

    bf_tile = 2 * CONV_HALO
    ext_ref[CONV_HALO:2 * CONV_HALO, :] = xbc_ref[r0:r0 + bf_tile, :].astype(f32)[0:CONV_HALO]
    n_sh = SSD_CONV - 1
    sr = lax.broadcasted_iota(jnp.int32, (n_sh * L, L), 0)
    sc = lax.broadcasted_iota(jnp.int32, (n_sh * L, L), 1)
    shift = jnp.where((sr % L) - sc == (sr // L) + 1, 1.0, 0.0).astype(bf16)
    cch = 128
    for cc in range(0, SSD_CONV_DIM, cch):
        if cc and cc % SSD_CONV_YIELD == 0:
            yield
        cs_ = slice(cc, cc + cch)
        xb = xbc_ref[rs, cs_]
        sh = _dot(shift, xb)
        w_now = cw_ref[SSD_CONV - 1:SSD_CONV, cs_]
        acc = cb_ref[:, cs_] + xb.astype(f32) * w_now
        top = cb_ref[:, cs_] + ext_ref[CONV_HALO:2 * CONV_HALO, cs_] * w_now
        for j in range(1, SSD_CONV):
            w_j = cw_ref[SSD_CONV - 1 - j:SSD_CONV - j, cs_]
            acc = acc + sh[(j - 1) * L:j * L] * w_j
            top = top + ext_ref[CONV_HALO - j:2 * CONV_HALO - j, cs_] * w_j
        if cc < SSD_D_INNER:
            dst, o = xs_ref, cc
        elif cc < SSD_D_INNER + SSD_N_GROUPS * SSD_D_STATE:
            dst, o = bm_ref, cc - SSD_D_INNER
        else:
            dst, o = cm_ref, cc - SSD_D_INNER - SSD_N_GROUPS * SSD_D_STATE
        dst[rs, o:o + cch] = _silu(acc)
        dst[r0:r0 + CONV_HALO, o:o + cch] = _silu(top)
    ext_ref[0:CONV_HALO, :] = xbc_ref[rs.stop - bf_tile:rs.stop, :].astype(f32)[CONV_HALO:bf_tile]
    yield

    lane_row = lax.broadcasted_iota(jnp.int32, (1, LANES), 1)
    row_i = lax.broadcasted_iota(jnp.int32, (L, L), 0)
    col_i = lax.broadcasted_iota(jnp.int32, (L, L), 1)
    causal = row_i >= col_i
    left = col_i < SSD_HEAD_DIM

    xdt = dt_ref[rs, :] + dtb_ref[...]
    dtv = jnp.maximum(xdt, 0.0) + jnp.log1p(jnp.exp(-jnp.abs(xdt)))
    a = jnp.where(lane_row < SSD_N_HEADS, -jnp.exp(alog_ref[...]), 0.0)
    d_a = dtv * a
    tril = jnp.where(causal, 1.0, 0.0).astype(bf16)
    hi, mid, lo3 = _split3(d_a)
    a_cum = (_dot(tril, hi) + _dot(tril, mid) + _dot(tril, lo3)) * LOG2E
    a_end = a_cum[L - 1:L, :]
    exp_a = jnp.exp2(a_cum)
    w_end = jnp.exp2(a_end - a_cum) * dtv
    cd = jnp.exp2(a_end)
    a_t = a_cum.T
    dt_t = dtv.T
    w_t = w_end.T
    yield

    n_pairs = SSD_N_HEADS // 2
    pairs_per_group = n_pairs // SSD_N_GROUPS
    for g in range(SSD_N_GROUPS):
        b_g = bm_ref[rs, g * SSD_D_STATE:(g + 1) * SSD_D_STATE]
        c_g = cm_ref[rs, g * SSD_D_STATE:(g + 1) * SSD_D_STATE]
        cb = _dot_nt(c_g.astype(bf16), b_g.astype(bf16))
        b_t = b_g.T
        for pi in range(pairs_per_group):
            i = g * pairs_per_group + pi
            xpair = xs_ref[rs, i * LANES:(i + 1) * LANES]
            xpair_b = xpair.astype(bf16)
            s_prev = st_ref[i]
            rhs = jnp.concatenate([xpair_b, s_prev.astype(bf16)], axis=0)
            ys = []
            sn = []
            for h in (2 * i, 2 * i + 1):
                acol = jnp.broadcast_to(a_cum[:, h:h + 1], (L, L))
                arow = jnp.broadcast_to(a_t[h:h + 1, :], (L, L))
                dtrow = jnp.broadcast_to(dt_t[h:h + 1, :], (L, L))
                dec = jnp.exp2(jnp.where(causal, acol - arow, NEG_BIG))
                m = cb * dec * dtrow
                cs = c_g * jnp.broadcast_to(exp_a[:, h:h + 1], (L, L))
                lhs = jnp.concatenate([m.astype(bf16), cs.astype(bf16)], axis=1)
                ys.append(_dot(lhs, rhs))
                btw = (b_t * jnp.broadcast_to(w_t[h:h + 1, :], (L, L))).astype(bf16)
                sn.append(_dot(btw, xpair_b))
            h0 = 2 * i
            cd_pair = jnp.where(lane_row < SSD_HEAD_DIM, cd[:, h0:h0 + 1], cd[:, h0 + 1:h0 + 2])
            st_ref[i] = jnp.where(left, sn[0], sn[1]) + s_prev * cd_pair
            y_pair = jnp.where(left, ys[0], ys[1])
            y_ref[rs, i * LANES:(i + 1) * LANES] = y_pair + xpair * dexp_ref[:, i * LANES:(i + 1) * LANES]
            if pi % 2 == 1:
                yield

    gw = SSD_D_INNER // SSD_N_GROUPS
    for g in range(SSD_N_GROUPS):
        sl = slice(g * gw, (g + 1) * gw)
        yz = y_ref[rs, sl] * _silu(z_ref[rs, sl].astype(f32))
        ms = jnp.mean(yz * yz, axis=-1, keepdims=True)
        hn_ref[rs, sl] = (yz * lax.rsqrt(ms + RMS_EPS) * ng_ref[:, sl]).astype(bf16)
        if g % 2 == 1:
            yield


ROUTE_E, ROUTE_RANK, ROUTE_GATE = 0, 2, 4
ATT_SUB = 2
SSD_PLAN = (3,) * 9


def _mixer_kernel(sink_ref, q_ref, k_ref, v_ref, cos_ref, sin_ref, ga_ref, x_ref,
                  xbc_ref, z_ref, dt_ref, gs_ref,
                  sel_ref, qg_ref, kg_ref, wao_ref, wo_ref, n2g_ref, wrh_ref, wrl_ref, br_ref,
                  cw_ref, cb_ref, dtb_ref, alog_ref, dexp_ref, ng_ref, wssd_ref,
                  x1_ref, h2_ref, route_ref, route_t_ref, cnt_ref,
                  kprev_ref, vprev_ref, att_ref, cnt_scr, vd_ref,
                  ext_ref, st_ref, xs_ref, bm_ref, cm_ref, y_ref, hn_ref, ms_ref):
    Q = ATT_BLOCK
    b = pl.program_id(0)
    n = pl.program_id(1)

    @pl.when(n == 0)
    def _():
        kprev_ref[...] = jnp.zeros(kprev_ref.shape, f32)
        vprev_ref[...] = jnp.zeros(vprev_ref.shape, f32)
        ext_ref[0:CONV_HALO, :] = jnp.zeros((CONV_HALO, SSD_CONV_DIM), f32)
        st_ref[...] = jnp.zeros(st_ref.shape, f32)

    @pl.when((b == 0) & (n == 0))
    def _():
        cnt_scr[...] = jnp.zeros(cnt_scr.shape, f32)

    R = q_ref.shape[0]

    def ssd_all():
        for c in range(R // SSD_CHUNK):
            yield from _ssd_chunk_phases(slice(c * SSD_CHUNK, (c + 1) * SSD_CHUNK), xbc_ref, z_ref, dt_ref,
                                         cw_ref, cb_ref, dtb_ref, alog_ref, dexp_ref, ng_ref,
                                         ext_ref, st_ref, xs_ref, bm_ref, cm_ref, y_ref, hn_ref)

    ssd_gen = ssd_all()

    plan = iter(SSD_PLAN)

    def ssd_step():
        for _ in range(next(plan, 0)):
            next(ssd_gen, None)
    lane_q = lax.broadcasted_iota(jnp.int32, (Q, LANES), 1)
    row_q = lax.broadcasted_iota(jnp.int32, (Q, LANES), 0)
    left = lane_q < ATT_HEAD_DIM
    head_mean = jnp.where((row_q // ATT_HEAD_DIM) == (lane_q // ATT_HEAD_DIM),
                          1.0 / ATT_HEAD_DIM, 0.0).astype(bf16)
    lane2 = lax.broadcasted_iota(jnp.int32, (2 * Q, LANES), 1) < ATT_HEAD_DIM
    qg = ATT_N_HEADS // ATT_N_KV
    rows = qg * Q
    ri = lax.broadcasted_iota(jnp.int32, (rows, Q), 0) % Q
    cj = lax.broadcasted_iota(jnp.int32, (rows, Q), 1)
    upper = cj > ri
    half = ROPE_DIM // 2

    n_sb = R // Q
    nq = COL_Q // LANES
    nk = COL_KV // LANES
    chunks = []
    for sb in range(n_sb):
        rs = slice(sb * Q, (sb + 1) * Q)
        chunks += [q_ref[rs, c * LANES:(c + 1) * LANES].astype(f32) for c in range(nq)]
        chunks += [k_ref[rs, c * LANES:(c + 1) * LANES].astype(f32) for c in range(nk)]
    u_all = jnp.concatenate(chunks, axis=0)
    sq = u_all * u_all
    sq_hi = sq.astype(bf16)
    sq_lo = (sq - sq_hi.astype(f32)).astype(bf16)
    un_all = u_all * lax.rsqrt(_dot(sq_hi, head_mean) + _dot(sq_lo, head_mean) + RMS_EPS)
    ssd_step()

    terms = []
    for sb in range(n_sb):
        rs = slice(sb * Q, (sb + 1) * Q)
        cs = jnp.concatenate([cos_ref[:, rs], sin_ref[:, rs], jnp.ones((half, LANES), f32),
                              jnp.zeros((LANES - 3 * half, LANES), f32)], axis=0)
        terms += list(_split3(cs.T))
    pat_all = _dot(jnp.concatenate(terms, axis=0), sel_ref[...])

    pats, kds = [], []
    for sb in range(n_sb):
        rs = slice(sb * Q, (sb + 1) * Q)
        pat = pat_all[3 * sb * Q:(3 * sb + 1) * Q] + pat_all[(3 * sb + 1) * Q:(3 * sb + 2) * Q] \
            + pat_all[(3 * sb + 2) * Q:(3 * sb + 3) * Q]
        cpat = pat[:, 0:LANES]
        s1pat = pat[:, LANES:2 * LANES]
        s2pat = pat[:, 2 * LANES:3 * LANES]

        def norm_rope(idx, gpat):
            tn = un_all[idx * Q:(idx + 1) * Q] * gpat
            return (tn * cpat + pltpu.roll(tn, LANES - ROPE_DIM // 2, 1) * s1pat
                    + pltpu.roll(tn, ROPE_DIM // 2, 1) * s2pat)

        kd = []
        for cidx in range(nk):
            sl = slice(cidx * LANES, (cidx + 1) * LANES)
            k_cur = norm_rope(sb * (nq + nk) + nq + cidx, kg_ref[...])
            v_cur = v_ref[rs, sl].astype(f32)
            k_all = jnp.concatenate([kprev_ref[:, sl], k_cur], axis=0)
            v_all = jnp.concatenate([vprev_ref[:, sl], v_cur], axis=0)
            kprev_ref[:, sl] = k_cur
            vprev_ref[:, sl] = v_cur
            k_sw = pltpu.roll(k_all, ATT_HEAD_DIM, 1)
            v_sw = pltpu.roll(v_all, ATT_HEAD_DIM, 1)
            kd.append(jnp.where(lane2, k_all, k_sw).astype(bf16))
            kd.append(jnp.where(lane2, k_sw, k_all).astype(bf16))
            for half_i, vv in enumerate((jnp.where(lane2, v_all, v_sw), jnp.where(lane2, v_sw, v_all))):
                vd_ref[sb * ATT_N_KV + 2 * cidx + half_i] = vv.astype(bf16)

        pats.append((cpat, s1pat, s2pat))
        kds.append(kd)
        ssd_step()

    def rope_q(sb, cidx):
        cpat, s1pat, s2pat = pats[sb]
        tn = un_all[(sb * (nq + nk) + cidx) * Q:(sb * (nq + nk) + cidx + 1) * Q] * qg_ref[...]
        return (tn * cpat + pltpu.roll(tn, LANES - ROPE_DIM // 2, 1) * s1pat
                + pltpu.roll(tn, ROPE_DIM // 2, 1) * s2pat)

    def score_tile(g):
        sb, h = divmod(g, ATT_N_KV)
        parts = []
        for cidx in (2 * h, 2 * h + 1):
            qc = rope_q(sb, cidx)
            parts.append(jnp.where(left, qc, 0.0).astype(bf16))
            parts.append(jnp.where(left, 0.0, qc).astype(bf16))
        lhs = jnp.concatenate(parts, axis=0)
        s_both = _dot_nt(lhs, kds[sb][h])
        s_prev = s_both[:, 0:Q]
        s_cur = s_both[:, Q:2 * Q]
        if sb == 0:
            return jnp.where(upper & (n > 0), s_prev, jnp.where(upper, NEG_BIG, s_cur))
        return jnp.where(upper, s_prev, s_cur)

    ones = jnp.ones((Q, Q), bf16)

    def softmax_block(s):
        sink = jnp.concatenate([jnp.full((Q, 1), sink_ref[i] * LOG2E, f32) for i in range(ATT_N_HEADS)],
                               axis=0)
        m = jnp.maximum(jnp.max(s, axis=-1, keepdims=True), sink)
        p = jnp.exp2(s - m)
        p_hi = p.astype(bf16)
        p_lo = (p - p_hi.astype(f32)).astype(bf16)
        denom = _dot(p_hi, ones) + _dot(p_lo, ones) + jnp.exp2(sink - m)
        return (p / denom).astype(bf16)

    def pv_tile(g, pf):
        sb, h = divmod(g, ATT_N_KV)
        rs = slice(sb * Q, (sb + 1) * Q)
        zero = jnp.zeros_like(pf)
        p_both = jnp.concatenate([jnp.where(upper, pf, zero), jnp.where(upper, zero, pf)], axis=1)
        o = _dot(p_both, vd_ref[g])
        for r in range(2):
            cidx = 2 * h + r
            att_ref[rs, cidx * LANES:(cidx + 1) * LANES] = jnp.where(
                left, o[(2 * r) * Q:(2 * r + 1) * Q], o[(2 * r + 1) * Q:(2 * r + 2) * Q]).astype(bf16)

    def epilogue(rs, cnt):
        E = rs.stop - rs.start
        lane = lax.broadcasted_iota(jnp.int32, (E, LANES), 1)
        y_att = _dot(att_ref[rs, :], wao_ref[...])
        merged = _sigmoid(ga_ref[rs, :].astype(f32)) * y_att + ms_ref[rs, :]
        x1 = x_ref[rs, :] + _dot(merged.astype(bf16), wo_ref[...])
        x1_ref[rs, :] = x1
        h2 = x1 * lax.rsqrt(jnp.mean(x1 * x1, axis=-1, keepdims=True) + RMS_EPS) * n2g_ref[...]
        h2_ref[rs, :] = _pack_rows(h2)

        hi = h2.astype(bf16)
        lo = (h2 - hi.astype(f32)).astype(bf16)
        logits = _dot(hi, wrh_ref[...]) + _dot(lo, wrh_ref[...]) + _dot(hi, wrl_ref[...]) + br_ref[...]
        big = 4 * LANES
        gl = jnp.where(lane < MOE_N_GROUPS, logits, NEG_BIG)
        gmax = jnp.max(gl, axis=-1, keepdims=True)
        gsel = jnp.min(jnp.where(gl == gmax, lane, big), axis=-1, keepdims=True)
        pg = 1.0 / jnp.sum(jnp.exp(gl - gmax), axis=-1, keepdims=True)
        lo_l = MOE_N_GROUPS + MOE_EPG * gsel
        el = jnp.where((lane >= lo_l) & (lane < lo_l + MOE_EPG), logits, NEG_BIG)
        v1 = jnp.max(el, axis=-1, keepdims=True)
        i1 = jnp.min(jnp.where(el == v1, lane, big), axis=-1, keepdims=True)
        el2 = jnp.where(lane == i1, NEG_BIG, el)
        v2 = jnp.max(el2, axis=-1, keepdims=True)
        i2 = jnp.min(jnp.where(el2 == v2, lane, big), axis=-1, keepdims=True)
        e1 = i1 - MOE_N_GROUPS
        e2 = i2 - MOE_N_GROUPS
        tt = jnp.exp(v2 - v1)
        w1 = pg * (1.0 / (1.0 + tt))
        w2 = pg * (tt / (1.0 + tt))

        onehot = jnp.where((lane == e1) | (lane == e2), 1.0, 0.0)
        strict = jnp.where(lax.broadcasted_iota(jnp.int32, (E, E), 0) > lax.broadcasted_iota(jnp.int32, (E, E), 1),
                           1.0, 0.0).astype(bf16)
        base = _dot(strict, onehot.astype(bf16)) + cnt
        r1 = jnp.sum(jnp.where(lane == e1, base, 0.0), axis=-1, keepdims=True)
        r2 = jnp.sum(jnp.where(lane == e2, base, 0.0), axis=-1, keepdims=True)

        rec = jnp.zeros((E, LANES), f32)
        for off, val in ((ROUTE_E, e1.astype(f32)), (ROUTE_E + 1, e2.astype(f32)),
                         (ROUTE_RANK, r1), (ROUTE_RANK + 1, r2), (ROUTE_GATE, w1), (ROUTE_GATE + 1, w2)):
            rec = jnp.where(lane == off, val, rec)
        route_ref[rs, :] = rec
        for o in range(0, E, Q):
            route_t_ref[:, rs.start + o:rs.start + o + Q] = rec[o:o + Q].T[0:8, :]
        return cnt + jnp.sum(onehot, axis=0, keepdims=True)

    s_tiles, p_tiles = {}, {}
    for stage in range(n_sb + 2):
        if stage < n_sb:
            s_tiles[stage] = jnp.concatenate(
                [score_tile(stage * ATT_N_KV + h) for h in range(ATT_N_KV)], axis=0)
            ssd_step()
        if 0 <= stage - 1 < n_sb:
            p_tiles[stage - 1] = softmax_block(s_tiles.pop(stage - 1))
            ssd_step()
        if 0 <= stage - 2 < n_sb:
            pb = p_tiles.pop(stage - 2)
            for h in range(ATT_N_KV):
                pv_tile((stage - 2) * ATT_N_KV + h, pb[h * rows:(h + 1) * rows])
            ssd_step()

    for _ in ssd_gen:
        pass
    ms_ref[...] = _sigmoid(gs_ref[...].astype(f32)) * _dot(hn_ref[...], wssd_ref[...])
    cnt = epilogue(slice(0, R), cnt_scr[0:1, :])

    cnt_scr[...] = jnp.broadcast_to(cnt, cnt_scr.shape)
    cnt_ref[...] = jnp.broadcast_to(cnt, cnt_ref.shape)


def _mixer(q, k, v, cos_t, sin_t, ga, x2, xbc, z, dt, gs, q_norm_g, k_norm_g, sinks, w_attn_out, w_out,
           norm2_g, w_rg, b_rg, w_re, b_re, conv_w, conv_b, dt_bias, a_log, d_skip, ssd_norm_g, w_ssd_out,
           batch, seq):
    t = batch * seq
    Q = ATT_SUB * ATT_BLOCK
    nb = seq // Q
    pad_h = LANES - SSD_N_HEADS
    dtb = jnp.pad(dt_bias, (0, pad_h)).reshape(1, LANES)
    alog = jnp.pad(a_log, (0, pad_h)).reshape(1, LANES)
    dexp = jnp.repeat(d_skip, SSD_HEAD_DIM).reshape(1, SSD_D_INNER)
    rep = LANES // ATT_HEAD_DIM
    qg = (jnp.tile(q_norm_g, rep) * (ATT_SCALE * LOG2E)).reshape(1, LANES)
    kg = jnp.tile(k_norm_g, rep).reshape(1, LANES)
    n_log = MOE_N_GROUPS + MOE_N_EXPERTS
    w_r = jnp.concatenate([w_rg, jnp.transpose(w_re, (1, 0, 2)).reshape(D_MODEL, MOE_N_EXPERTS),
                           jnp.zeros((D_MODEL, LANES - n_log), f32)], axis=1)
    b_r = jnp.concatenate([b_rg, b_re.reshape(-1), jnp.zeros((LANES - n_log,), f32)]).reshape(1, LANES)
    w_r_hi = w_r.astype(bf16)
    w_r_lo = (w_r - w_r_hi.astype(f32)).astype(bf16)
    const = lambda b, n, *_: (0, 0)
    row = lambda b, n, *_: (b * nb + n, 0)
    col = lambda b, n, *_: (0, b * nb + n)
    full = lambda shape: pl.BlockSpec(shape, const, pipeline_mode=pl.Buffered(1))
    return pl.pallas_call(
        _mixer_kernel,
        grid_spec=pltpu.PrefetchScalarGridSpec(
            num_scalar_prefetch=1,
            grid=(batch, nb),
            in_specs=[pl.BlockSpec((Q, COL_Q), row),
                      pl.BlockSpec((Q, COL_KV), row),
                      pl.BlockSpec((Q, COL_KV), row),
                      pl.BlockSpec((ROPE_DIM // 2, Q), col),
                      pl.BlockSpec((ROPE_DIM // 2, Q), col),
                      pl.BlockSpec((Q, D_MODEL), row),
                      pl.BlockSpec((Q, D_MODEL), row),
                      pl.BlockSpec((Q, SSD_CONV_DIM), row),
                      pl.BlockSpec((Q, SSD_D_INNER), row),
                      pl.BlockSpec((Q, LANES), row),
                      pl.BlockSpec((Q, D_MODEL), row),
                      full((LANES, 3 * LANES)),
                      pl.BlockSpec((1, LANES), const),
                      pl.BlockSpec((1, LANES), const),
                      full((COL_Q, D_MODEL)),
                      full((D_MODEL, D_MODEL)),
                      pl.BlockSpec((1, D_MODEL), const),
                      full((D_MODEL, LANES)),
                      full((D_MODEL, LANES)),
                      pl.BlockSpec((1, LANES), const),
                      pl.BlockSpec((SSD_CONV, SSD_CONV_DIM), const),
                      pl.BlockSpec((1, SSD_CONV_DIM), const),
                      pl.BlockSpec((1, LANES), const),
                      pl.BlockSpec((1, LANES), const),
                      pl.BlockSpec((1, SSD_D_INNER), const),
                      pl.BlockSpec((1, SSD_D_INNER), const),
                      full((SSD_D_INNER, D_MODEL))],
            out_specs=[pl.BlockSpec((Q, D_MODEL), row),
                       pl.BlockSpec((Q, HALF_D), row),
                       pl.BlockSpec((Q, LANES), row),
                       pl.BlockSpec((8, Q), col),
                       pl.BlockSpec((8, LANES), const)],
            scratch_shapes=[pltpu.VMEM((ATT_BLOCK, COL_KV), f32),
                            pltpu.VMEM((ATT_BLOCK, COL_KV), f32),
                            pltpu.VMEM((Q, COL_Q), bf16),
                            pltpu.VMEM((8, LANES), f32),
                            pltpu.VMEM((ATT_SUB * ATT_N_KV, 2 * ATT_BLOCK, LANES), bf16),
                            pltpu.VMEM((2 * CONV_HALO, SSD_CONV_DIM), f32),
                            pltpu.VMEM((SSD_N_HEADS // 2, SSD_D_STATE, LANES), f32),
                            pltpu.VMEM((Q, SSD_D_INNER), f32),
                            pltpu.VMEM((Q, SSD_N_GROUPS * SSD_D_STATE), f32),
                            pltpu.VMEM((Q, SSD_N_GROUPS * SSD_D_STATE), f32),
                            pltpu.VMEM((Q, SSD_D_INNER), f32),
                            pltpu.VMEM((Q, SSD_D_INNER), bf16),
                            pltpu.VMEM((Q, D_MODEL), f32)],
        ),
        out_shape=[jax.ShapeDtypeStruct((t, D_MODEL), f32),
                   jax.ShapeDtypeStruct((t, HALF_D), jnp.uint32),
                   jax.ShapeDtypeStruct((t, LANES), f32),
                   jax.ShapeDtypeStruct((8, t), f32),
                   jax.ShapeDtypeStruct((8, LANES), f32)],
        compiler_params=pltpu.CompilerParams(dimension_semantics=("arbitrary", "arbitrary"),
                                             vmem_limit_bytes=VMEM_LIMIT),
        name="mixer",
    )(sinks, q, k, v, cos_t, sin_t, ga, x2, xbc, z, dt, gs,
      _rope_select(), qg, kg, w_attn_out.astype(bf16), w_out.astype(bf16), norm2_g.reshape(1, D_MODEL),
      w_r_hi, w_r_lo, b_r,
      conv_w, conv_b.reshape(1, -1), dtb, alog, dexp, ssd_norm_g.reshape(1, -1), w_ssd_out.astype(bf16))


MOE_TM = 512
MOE_UNROLL = 8


def _row_copy(src, i, dst, j, sem):
    return pltpu.make_async_copy(src.at[pl.ds(i, 1), :], dst.at[pl.ds(j, 1), :], sem)


def _dispatch_kernel(dest0_ref, dest1_ref, zblk_ref, h2_ref, xs_hbm, zero_ref, sem, zsem):
    tm = h2_ref.shape[0] * SUBLANES
    base = pl.program_id(0) * tm

    @pl.when(pl.program_id(0) == 0)
    def _():
        zero_ref[...] = jnp.zeros(zero_ref.shape, zero_ref.dtype)

        def zcopy(i):
            start = pl.multiple_of(zblk_ref[i] * MOE_BLOCK, MOE_BLOCK)
            return pltpu.make_async_copy(zero_ref, xs_hbm.at[pl.ds(start, MOE_BLOCK), :], zsem)

        def zstart(i, carry):
            @pl.when(zblk_ref[i] >= 0)
            def _():
                zcopy(i).start()
            return carry

        def zwait(i, carry):
            @pl.when(zblk_ref[i] >= 0)
            def _():
                zcopy(i).wait()
            return carry

        lax.fori_loop(0, zblk_ref.shape[0], zstart, 0)
        lax.fori_loop(0, zblk_ref.shape[0], zwait, 0)

    def issue(k, carry):
        for u in range(SUBLANES):
            i = k * SUBLANES + u
            src = h2_ref.at[k, pl.ds(u, 1), :]
            pltpu.make_async_copy(src, xs_hbm.at[pl.ds(dest0_ref[base + i], 1), :], sem).start()
            pltpu.make_async_copy(src, xs_hbm.at[pl.ds(dest1_ref[base + i], 1), :], sem).start()
        return carry

    lax.fori_loop(0, tm // SUBLANES, issue, 0)
    for _ in range(2):
        pltpu.make_async_copy(xs_hbm.at[pl.ds(0, tm), :], xs_hbm.at[pl.ds(0, tm), :], sem).wait()


def _dispatch(h2, dest0, dest1, zero_blocks, n_rows):
    t = h2.shape[0]
    tm = min(MOE_TM, t)
    return pl.pallas_call(
        _dispatch_kernel,
        grid_spec=pltpu.PrefetchScalarGridSpec(
            num_scalar_prefetch=3,
            grid=(t // tm,),
            in_specs=[pl.BlockSpec((tm // SUBLANES, SUBLANES, HALF_D), lambda i, d0, d1, zb: (i, 0, 0))],
            out_specs=pl.BlockSpec(memory_space=pl.ANY),
            scratch_shapes=[pltpu.VMEM((MOE_BLOCK, HALF_D), h2.dtype),
                            pltpu.SemaphoreType.DMA(()),
                            pltpu.SemaphoreType.DMA(())],
        ),
        out_shape=jax.ShapeDtypeStruct((n_rows, HALF_D), h2.dtype),
        compiler_params=pltpu.CompilerParams(dimension_semantics=("arbitrary",)),
        name="dispatch",
    )(dest0, dest1, zero_blocks, h2.reshape(t // SUBLANES, SUBLANES, HALF_D))


EXPERT_SUB = 1


def _expert_kernel(be_ref, nu_ref, xs_ref, *refs):
    w_refs = refs[:3 * EXPERT_SUB]
    ys_ref, wgu_scr, wd_scr = refs[3 * EXPERT_SUB:]
    i = pl.program_id(0)
    rows = MOE_BLOCK

    for j in range(EXPERT_SUB):
        blk = EXPERT_SUB * i + j
        fresh = (i == 0) | (be_ref[blk] != be_ref[jnp.maximum(blk - EXPERT_SUB, 0)])

        @pl.when((blk < nu_ref[0]) & fresh)
        def _(j=j):
            wg_ref, wu_ref, wd_ref = w_refs[3 * j:3 * j + 3]
            wgu_scr[j, :, 0:MOE_D_FF] = wg_ref[0].astype(bf16)
            wgu_scr[j, :, MOE_D_FF:2 * MOE_D_FF] = wu_ref[0].astype(bf16)
            wd_scr[j] = wd_ref[0].astype(bf16)

    def block(j):
        rs = slice(j * rows, (j + 1) * rows)
        x_lo, x_hi = _unpack_rows(xs_ref[rs, :])
        gu = (_dot(x_lo.astype(bf16), wgu_scr[j, 0:HALF_D, :])
              + _dot(x_hi.astype(bf16), wgu_scr[j, HALF_D:D_MODEL, :]))
        hid = _silu(gu[:, :MOE_D_FF]) * gu[:, MOE_D_FF:]
        ys_ref[rs, :] = _pack_rows(_dot(hid.astype(bf16), wd_scr[j]))

    n_live = jnp.clip(nu_ref[0] - EXPERT_SUB * i, 0, EXPERT_SUB)
    for live in range(EXPERT_SUB + 1):
        @pl.when(n_live == live)
        def _(live=live):
            for j in range(live):
                block(j)
            if live < EXPERT_SUB:
                ys_ref[live * rows:, :] = jnp.zeros(((EXPERT_SUB - live) * rows, HALF_D), ys_ref.dtype)


def _experts(xs, block_e, n_used, w_g, w_u, w_d):
    n_rows = xs.shape[0]
    rows = EXPERT_SUB * MOE_BLOCK
    nstep = n_rows // rows
    blk_in = lambda i, be, nu: (jnp.minimum(i, (nu[0] - 1) // EXPERT_SUB), 0)
    blk_out = lambda i, be, nu: (i, 0)
    w_specs = []
    for j in range(EXPERT_SUB):
        wsel = lambda i, be, nu, j=j: (be[EXPERT_SUB * i + j], 0, 0)
        w_specs += [pl.BlockSpec((1, D_MODEL, MOE_D_FF), wsel),
                    pl.BlockSpec((1, D_MODEL, MOE_D_FF), wsel),
                    pl.BlockSpec((1, MOE_D_FF, D_MODEL), wsel)]
    return pl.pallas_call(
        _expert_kernel,
        grid_spec=pltpu.PrefetchScalarGridSpec(
            num_scalar_prefetch=2,
            grid=(nstep,),
            in_specs=[pl.BlockSpec((rows, HALF_D), blk_in)] + w_specs,
            out_specs=pl.BlockSpec((rows, HALF_D), blk_out),
            scratch_shapes=[pltpu.VMEM((EXPERT_SUB, D_MODEL, 2 * MOE_D_FF), bf16),
                            pltpu.VMEM((EXPERT_SUB, MOE_D_FF, D_MODEL), bf16)],
        ),
        out_shape=jax.ShapeDtypeStruct((n_rows, HALF_D), xs.dtype),
        compiler_params=pltpu.CompilerParams(dimension_semantics=("arbitrary",),
                                             vmem_limit_bytes=VMEM_LIMIT),
        name="experts",
    )(block_e, n_used, xs, *([w_g, w_u, w_d] * EXPERT_SUB))


def _combine_kernel(dest0_ref, dest1_ref, x1_ref, route_ref, ys_hbm, out_ref, y0_ref, y1_ref, sem):
    tm = x1_ref.shape[0]
    step = pl.program_id(0)

    def gather_tile(tile, slot):
        base = tile * tm

        def issue(k, carry):
            for u in range(SUBLANES):
                r = k * SUBLANES + u
                pltpu.make_async_copy(ys_hbm.at[pl.ds(dest0_ref[base + r], 1), :],
                                      y0_ref.at[slot, k, pl.ds(u, 1), :], sem.at[slot]).start()
                pltpu.make_async_copy(ys_hbm.at[pl.ds(dest1_ref[base + r], 1), :],
                                      y1_ref.at[slot, k, pl.ds(u, 1), :], sem.at[slot]).start()
            return carry

        lax.fori_loop(0, tm // SUBLANES, issue, 0)

    @pl.when(step == 0)
    def _():
        gather_tile(0, 0)

    @pl.when(step + 1 < pl.num_programs(0))
    def _():
        gather_tile(step + 1, (step + 1) % 2)

    slot = step % 2
    for _ in range(2):
        pltpu.make_async_copy(ys_hbm.at[pl.ds(0, tm), :], ys_hbm.at[pl.ds(0, tm), :], sem.at[slot]).wait()
    g0 = route_ref[:, ROUTE_GATE:ROUTE_GATE + 1]
    g1 = route_ref[:, ROUTE_GATE + 1:ROUTE_GATE + 2]
    y0_lo, y0_hi = _unpack_rows(y0_ref[slot].reshape(tm, HALF_D))
    y1_lo, y1_hi = _unpack_rows(y1_ref[slot].reshape(tm, HALF_D))
    out_ref[:, 0:HALF_D] = x1_ref[:, 0:HALF_D] + y0_lo * g0 + y1_lo * g1
    out_ref[:, HALF_D:D_MODEL] = x1_ref[:, HALF_D:D_MODEL] + y0_hi * g0 + y1_hi * g1


def _combine(x1, route, ys, dest0, dest1):
    t = x1.shape[0]
    tm = min(MOE_TM, t)
    row = lambda i, d0, d1: (i, 0)
    return pl.pallas_call(
        _combine_kernel,
        grid_spec=pltpu.PrefetchScalarGridSpec(
            num_scalar_prefetch=2,
            grid=(t // tm,),
            in_specs=[pl.BlockSpec((tm, D_MODEL), row),
                      pl.BlockSpec((tm, LANES), row),
                      pl.BlockSpec(memory_space=pl.ANY)],
            out_specs=pl.BlockSpec((tm, D_MODEL), row),
            scratch_shapes=[pltpu.VMEM((2, tm // SUBLANES, SUBLANES, HALF_D), ys.dtype),
                            pltpu.VMEM((2, tm // SUBLANES, SUBLANES, HALF_D), ys.dtype),
                            pltpu.SemaphoreType.DMA((2,))],
        ),
        out_shape=jax.ShapeDtypeStruct((t, D_MODEL), f32),
        compiler_params=pltpu.CompilerParams(dimension_semantics=("arbitrary",)),
        name="combine",
    )(dest0, dest1, x1, route, ys)


def _moe(x1, h2, route, route_t, counts_rec, w_gate_e, w_up_e, w_down_e):
    t = x1.shape[0]
    n_rows = 2 * t + MOE_N_EXPERTS * MOE_BLOCK
    nblk = n_rows // MOE_BLOCK
    counts = counts_rec[0, :MOE_N_EXPERTS].astype(jnp.int32)
    padded = (counts + MOE_BLOCK - 1) // MOE_BLOCK * MOE_BLOCK
    pend = jnp.cumsum(padded)
    pstart = pend - padded
    expert_ids = jnp.arange(MOE_N_EXPERTS, dtype=jnp.int32)[:, None]

    def sorted_row(k):
        eid = route_t[ROUTE_E + k].astype(jnp.int32)
        rank = route_t[ROUTE_RANK + k].astype(jnp.int32)
        return jnp.sum(jnp.where(eid[None, :] == expert_ids, pstart[:, None], 0), axis=0) + rank

    dest0, dest1 = sorted_row(0), sorted_row(1)
    block_start = jnp.arange(nblk, dtype=jnp.int32) * MOE_BLOCK
    block_e = jnp.minimum(jnp.sum((pend[None, :] <= block_start[:, None]).astype(jnp.int32), axis=1),
                          MOE_N_EXPERTS - 1)
    n_used = (pend[-1:] // MOE_BLOCK).astype(jnp.int32)
    last_blk = jnp.where(counts > 0, pend // MOE_BLOCK - 1, -1)
    tail = n_used[0] + jnp.arange(MOE_N_EXPERTS, dtype=jnp.int32)
    zero_blocks = jnp.concatenate([last_blk, jnp.where(tail < nblk, tail, -1)]).astype(jnp.int32)
    xs = _dispatch(h2, dest0, dest1, zero_blocks, n_rows)
    ys = _experts(xs, block_e, n_used, w_gate_e, w_up_e, w_down_e)
    return _combine(x1, route, ys, dest0, dest1)


def _layer(x, positions, norm1_g, w_in, conv_w, conv_b, dt_bias, a_log, d_skip, ssd_norm_g, w_ssd_out,
           q_norm_g, k_norm_g, sinks, w_attn_out, w_out, norm2_g, w_rg, b_rg, w_re, b_re,
           w_gate_e, w_up_e, w_down_e):
    batch, seq, d = x.shape
    x2 = x.reshape(batch * seq, d)
    cos_t, sin_t = _rope_tables(positions)
    z, xbc, q, k, v, gs, ga, dt = _inproj(x2, norm1_g, w_in)
    x1, h2, route, route_t, counts = _mixer(q, k, v, cos_t, sin_t, ga, x2, xbc, z, dt, gs, q_norm_g, k_norm_g,
                                            sinks, w_attn_out, w_out, norm2_g, w_rg, b_rg, w_re, b_re,
                                            conv_w, conv_b, dt_bias, a_log, d_skip, ssd_norm_g, w_ssd_out,
                                            batch, seq)
    out = _moe(x1, h2, route, route_t, counts, w_gate_e, w_up_e, w_down_e)
    return out.reshape(batch, seq, d)


def kernel(x, positions, norm1_g, w_in, conv_w, conv_b, dt_bias, a_log, d_skip, ssd_norm_g, w_ssd_out,
           q_norm_g, k_norm_g, sinks, w_attn_out, w_out, norm2_g, w_router_group, b_router_group,
           w_router_expert, b_router_expert, w_gate_e, w_up_e, w_down_e):
    for l in range(norm1_g.shape[0]):
        x = _layer(x, positions, norm1_g[l], w_in[l], conv_w[l], conv_b[l], dt_bias[l], a_log[l],
                   d_skip[l], ssd_norm_g[l], w_ssd_out[l], q_norm_g[l], k_norm_g[l], sinks[l],
                   w_attn_out[l], w_out[l], norm2_g[l], w_router_group[l], b_router_group[l],
                   w_router_expert[l], b_router_expert[l], w_gate_e[l], w_up_e[l], w_down_e[l])
    return x
```

```python
import numpy as np

import jax
import jax.numpy as jnp
from jax import lax
from jax.experimental import pallas as pl
from jax.experimental.pallas import tpu as pltpu

f32 = jnp.float32
bf16 = jnp.bfloat16

D_MODEL = 1024
SSD_D_INNER = 2048
SSD_HEAD_DIM = 64
SSD_N_HEADS = 32
SSD_N_GROUPS = 4
SSD_D_STATE = 128
SSD_CONV = 4
SSD_CHUNK = 128
SSD_CONV_DIM = 3072
ATT_HEAD_DIM = 64
ATT_N_HEADS = 16
ATT_N_KV = 4
ATT_BLOCK = 128
ATT_SCALE = ATT_HEAD_DIM ** -0.5
ROPE_THETA = 500000.0
ROPE_DIM = 16
MOE_N_GROUPS = 8
MOE_EPG = 8
MOE_N_EXPERTS = 64
MOE_D_FF = 256
MOE_BLOCK = 512
RMS_EPS = 1e-6

LANES = 128
SUBLANES = 8
CONV_HALO = 8
NEG_BIG = -1e30
VMEM_LIMIT = 56 * 1024 * 1024

COL_Z = SSD_D_INNER
COL_XBC = SSD_CONV_DIM
COL_DT = SSD_N_HEADS
COL_Q = ATT_N_HEADS * ATT_HEAD_DIM
COL_KV = ATT_N_KV * ATT_HEAD_DIM


LOG2E = 1.4426950408889634


def _sigmoid(x):
    return 1.0 / (1.0 + jnp.exp2(x * (-LOG2E)))


def _silu(x):
    return x * _sigmoid(x)


def _split3(x):
    hi = x.astype(bf16)
    r1 = x - hi.astype(f32)
    mid = r1.astype(bf16)
    lo = (r1 - mid.astype(f32)).astype(bf16)
    return hi, mid, lo


HALF_D = D_MODEL // 2
_HI_MASK = np.uint32(0xFFFF0000)


def _pack_rows(x):
    bits = pltpu.bitcast(x.astype(bf16).astype(f32), jnp.uint32)
    return (bits[:, HALF_D:] & _HI_MASK) | (bits[:, :HALF_D] >> 16)


def _unpack_rows(p):
    return pltpu.bitcast(p << 16, f32), pltpu.bitcast(p & _HI_MASK, f32)


def _dot(a, b):
    return jnp.dot(a, b, preferred_element_type=f32)


def _dot_nt(a, b):
    return lax.dot_general(a, b, (((1,), (1,)), ((), ())), preferred_element_type=f32)


def _trig_kernel(freq_ref, pos_ref, cos_ref, sin_ref):
    ang = pos_ref[...].astype(f32) * freq_ref[pl.program_id(0)]
    cos_ref[0] = jnp.cos(ang)
    sin_ref[0] = jnp.sin(ang)


def _rope_select():
    half = ROPE_DIM // 2
    sel = np.zeros((LANES, 3 * LANES), np.float32)
    for lane in range(LANES):
        m = lane % ATT_HEAD_DIM
        if m < half:
            sel[m, lane] = 1.0
            sel[half + m, LANES + lane] = -1.0
        elif m < ROPE_DIM:
            sel[m - half, lane] = 1.0
            sel[m, 2 * LANES + lane] = 1.0
        else:
            sel[ROPE_DIM, lane] = 1.0
    return jnp.asarray(sel, bf16)


def _rope_tables(positions):
    t = positions.size
    half = ROPE_DIM // 2
    inv_freq = ROPE_THETA ** (-jnp.arange(0, ROPE_DIM, 2, dtype=f32) / ROPE_DIM)
    pos2d = positions.reshape(t // LANES, LANES)
    cos_t, sin_t = pl.pallas_call(
        _trig_kernel,
        grid_spec=pltpu.PrefetchScalarGridSpec(
            num_scalar_prefetch=1,
            grid=(half,),
            in_specs=[pl.BlockSpec((t // LANES, LANES), lambda j, f: (0, 0))],
            out_specs=[pl.BlockSpec((1, t // LANES, LANES), lambda j, f: (j, 0, 0))] * 2,
        ),
        out_shape=[jax.ShapeDtypeStruct((half, t // LANES, LANES), f32)] * 2,
        name="trig",
    )(inv_freq, pos2d)
    return cos_t.reshape(half, t), sin_t.reshape(half, t)


INPROJ_TM = 512
INPROJ_CH = 512


def _inproj_kernel(x_ref, g_ref, w_ref,
                   z_ref, xbc_ref, q_ref, k_ref, v_ref, gs_ref, ga_ref, dt_ref, h_scr):
    x = x_ref[...]
    ms = jnp.mean(x * x, axis=-1, keepdims=True)
    h_scr[...] = (x * lax.rsqrt(ms + RMS_EPS) * g_ref[...]).astype(bf16)
    off = 0
    for ref in (z_ref, xbc_ref, q_ref, k_ref, v_ref, gs_ref, ga_ref, dt_ref):
        width = ref.shape[1]
        for c in range(0, width, INPROJ_CH):
            cw = min(INPROJ_CH, width - c)
            ref[:, c:c + cw] = _dot(h_scr[...], w_ref[:, off + c:off + c + cw]).astype(ref.dtype)
        off += width


WPREP_ROWS = 128


def _wprep_kernel(w_ref, o_ref):
    s1 = COL_Z + COL_XBC
    s2 = s1 + COL_DT
    n_tail = w_ref.shape[1] - s2
    o_ref[:, 0:s1] = w_ref[:, 0:s1].astype(bf16)
    o_ref[:, s1:s1 + n_tail] = w_ref[:, s2:s2 + n_tail].astype(bf16)
    lane = lax.broadcasted_iota(jnp.int32, (w_ref.shape[0], LANES), 1)
    o_ref[:, s1 + n_tail:s1 + n_tail + LANES] = jnp.where(lane < COL_DT, w_ref[:, s1:s1 + LANES], 0.0).astype(bf16)


def _inproj(x2, norm1_g, w_in):
    t = x2.shape[0]
    tm = min(INPROJ_TM, t)
    n_in = w_in.shape[1]
    n_all = n_in - COL_DT + LANES
    w_all = pl.pallas_call(
        _wprep_kernel,
        grid=(D_MODEL // WPREP_ROWS,),
        in_specs=[pl.BlockSpec((WPREP_ROWS, n_in), lambda i: (i, 0))],
        out_specs=pl.BlockSpec((WPREP_ROWS, n_all), lambda i: (i, 0)),
        out_shape=jax.ShapeDtypeStruct((D_MODEL, n_all), bf16),
        name="wprep",
    )(w_in)
    widths = (COL_Z, COL_XBC, COL_Q, COL_KV, COL_KV, D_MODEL, D_MODEL)
    const = lambda i: (0, 0)
    row = lambda i: (i, 0)
    outs = pl.pallas_call(
        _inproj_kernel,
        grid=(t // tm,),
        in_specs=[pl.BlockSpec((tm, D_MODEL), row),
                  pl.BlockSpec((1, D_MODEL), const),
                  pl.BlockSpec((D_MODEL, w_all.shape[1]), const, pipeline_mode=pl.Buffered(1))],
        out_specs=[pl.BlockSpec((tm, w), row) for w in widths] + [pl.BlockSpec((tm, LANES), row)],
        out_shape=[jax.ShapeDtypeStruct((t, w), bf16) for w in widths]
                  + [jax.ShapeDtypeStruct((t, LANES), f32)],
        scratch_shapes=[pltpu.VMEM((tm, D_MODEL), bf16)],
        compiler_params=pltpu.CompilerParams(dimension_semantics=("arbitrary",),
                                             vmem_limit_bytes=VMEM_LIMIT),
        name="inproj",
    )(x2, norm1_g.reshape(1, D_MODEL), w_all)
    return outs


SSD_CONV_YIELD = 1024

def _ssd_chunk_phases(rs, xbc_ref, z_ref, dt_ref, cw_ref, cb_ref, dtb_ref, alog_ref, dexp_ref, ng_ref,
                      ext_ref, st_ref, xs_ref, bm_ref, cm_ref, y_ref, hn_ref):
    L = SSD_CHUNK
    r0 = rs.start

    bf_tile = 2 * CONV_HALO
    ext_ref[CONV_HALO:2 * CONV_HALO, :] = xbc_ref[r0:r0 + bf_tile, :].astype(f32)[0:CONV_HALO]
    n_sh = SSD_CONV - 1
    sr = lax.broadcasted_iota(jnp.int32, (n_sh * L, L), 0)
    sc = lax.broadcasted_iota(jnp.int32, (n_sh * L, L), 1)
    shift = jnp.where((sr % L) - sc == (sr // L) + 1, 1.0, 0.0).astype(bf16)
    cch = 128
    for cc in range(0, SSD_CONV_DIM, cch):
        if cc and cc % SSD_CONV_YIELD == 0:
            yield
        cs_ = slice(cc, cc + cch)
        xb = xbc_ref[rs, cs_]
        sh = _dot(shift, xb)
        w_now = cw_ref[SSD_CONV - 1:SSD_CONV, cs_]
        acc = cb_ref[:, cs_] + xb.astype(f32) * w_now
        top = cb_ref[:, cs_] + ext_ref[CONV_HALO:2 * CONV_HALO, cs_] * w_now
        for j in range(1, SSD_CONV):
            w_j = cw_ref[SSD_CONV - 1 - j:SSD_CONV - j, cs_]
            acc = acc + sh[(j - 1) * L:j * L] * w_j
            top = top + ext_ref[CONV_HALO - j:2 * CONV_HALO - j, cs_] * w_j
        if cc < SSD_D_INNER:
            dst, o = xs_ref, cc
        elif cc < SSD_D_INNER + SSD_N_GROUPS * SSD_D_STATE:
            dst, o = bm_ref, cc - SSD_D_INNER
        else:
            dst, o = cm_ref, cc - SSD_D_INNER - SSD_N_GROUPS * SSD_D_STATE
        dst[rs, o:o + cch] = _silu(acc)
        dst[r0:r0 + CONV_HALO, o:o + cch] = _silu(top)
    ext_ref[0:CONV_HALO, :] = xbc_ref[rs.stop - bf_tile:rs.stop, :].astype(f32)[CONV_HALO:bf_tile]
    yield

    lane_row = lax.broadcasted_iota(jnp.int32, (1, LANES), 1)
    row_i = lax.broadcasted_iota(jnp.int32, (L, L), 0)
    col_i = lax.broadcasted_iota(jnp.int32, (L, L), 1)
    causal = row_i >= col_i
    left = col_i < SSD_HEAD_DIM

    xdt = dt_ref[rs, :] + dtb_ref[...]
    dtv = jnp.maximum(xdt, 0.0) + jnp.log1p(jnp.exp(-jnp.abs(xdt)))
    a = jnp.where(lane_row < SSD_N_HEADS, -jnp.exp(alog_ref[...]), 0.0)
    d_a = dtv * a
    tril = jnp.where(causal, 1.0, 0.0).astype(bf16)
    hi, mid, lo3 = _split3(d_a)
    a_cum = (_dot(tril, hi) + _dot(tril, mid) + _dot(tril, lo3)) * LOG2E
    a_end = a_cum[L - 1:L, :]
    exp_a = jnp.exp2(a_cum)
    w_end = jnp.exp2(a_end - a_cum) * dtv
    cd = jnp.exp2(a_end)
    a_t = a_cum.T
    dt_t = dtv.T
    w_t = w_end.T
    yield

    n_pairs = SSD_N_HEADS // 2
    pairs_per_group = n_pairs // SSD_N_GROUPS
    for g in range(SSD_N_GROUPS):
        b_g = bm_ref[rs, g * SSD_D_STATE:(g + 1) * SSD_D_STATE]
        c_g = cm_ref[rs, g * SSD_D_STATE:(g + 1) * SSD_D_STATE]
        cb = _dot_nt(c_g.astype(bf16), b_g.astype(bf16))
        b_t = b_g.T
        for pi in range(pairs_per_group):
            i = g * pairs_per_group + pi
            xpair = xs_ref[rs, i * LANES:(i + 1) * LANES]
            xpair_b = xpair.astype(bf16)
            s_prev = st_ref[i]
            rhs = jnp.concatenate([xpair_b, s_prev.astype(bf16)], axis=0)
            ys = []
            sn = []
            for h in (2 * i, 2 * i + 1):
                acol = jnp.broadcast_to(a_cum[:, h:h + 1], (L, L))
                arow = jnp.broadcast_to(a_t[h:h + 1, :], (L, L))
                dtrow = jnp.broadcast_to(dt_t[h:h + 1, :], (L, L))
                dec = jnp.exp2(jnp.where(causal, acol - arow, NEG_BIG))
                m = cb * dec * dtrow
                cs = c_g * jnp.broadcast_to(exp_a[:, h:h + 1], (L, L))
                lhs = jnp.concatenate([m.astype(bf16), cs.astype(bf16)], axis=1)
                ys.append(_dot(lhs, rhs))
                btw = (b_t * jnp.broadcast_to(w_t[h:h + 1, :], (L, L))).astype(bf16)
                sn.append(_dot(btw, xpair_b))
            h0 = 2 * i
            cd_pair = jnp.where(lane_row < SSD_HEAD_DIM, cd[:, h0:h0 + 1], cd[:, h0 + 1:h0 + 2])
            st_ref[i] = jnp.where(left, sn[0], sn[1]) + s_prev * cd_pair
            y_pair = jnp.where(left, ys[0], ys[1])
            y_ref[rs, i * LANES:(i + 1) * LANES] = y_pair + xpair * dexp_ref[:, i * LANES:(i + 1) * LANES]
            if pi % 2 == 1:
                yield

    gw = SSD_D_INNER // SSD_N_GROUPS
    for g in range(SSD_N_GROUPS):
        sl = slice(g * gw, (g + 1) * gw)
        yz = y_ref[rs, sl] * _silu(z_ref[rs, sl].astype(f32))
        ms = jnp.mean(yz * yz, axis=-1, keepdims=True)
        hn_ref[rs, sl] = (yz * lax.rsqrt(ms + RMS_EPS) * ng_ref[:, sl]).astype(bf16)
        if g % 2 == 1:
            yield


ROUTE_E, ROUTE_RANK, ROUTE_GATE = 0, 2, 4
ATT_SUB = 2
SSD_PLAN = (3,) * 9


def _mixer_kernel(sink_ref, q_ref, k_ref, v_ref, cos_ref, sin_ref, ga_ref, x_ref,
                  xbc_ref, z_ref, dt_ref, gs_ref,
                  sel_ref, qg_ref, kg_ref, wao_ref, wo_ref, n2g_ref, wrh_ref, wrl_ref, br_ref,
                  cw_ref, cb_ref, dtb_ref, alog_ref, dexp_ref, ng_ref, wssd_ref,
                  x1_ref, h2_ref, route_ref, route_t_ref, cnt_ref,
                  kprev_ref, vprev_ref, att_ref, cnt_scr, vd_ref,
                  ext_ref, st_ref, xs_ref, bm_ref, cm_ref, y_ref, hn_ref, ms_ref):
    Q = ATT_BLOCK
    b = pl.program_id(0)
    n = pl.program_id(1)

    @pl.when(n == 0)
    def _():
        kprev_ref[...] = jnp.zeros(kprev_ref.shape, f32)
        vprev_ref[...] = jnp.zeros(vprev_ref.shape, f32)
        ext_ref[0:CONV_HALO, :] = jnp.zeros((CONV_HALO, SSD_CONV_DIM), f32)
        st_ref[...] = jnp.zeros(st_ref.shape, f32)

    @pl.when((b == 0) & (n == 0))
    def _():
        cnt_scr[...] = jnp.zeros(cnt_scr.shape, f32)

    R = q_ref.shape[0]

    def ssd_all():
        for c in range(R // SSD_CHUNK):
            yield from _ssd_chunk_phases(slice(c * SSD_CHUNK, (c + 1) * SSD_CHUNK), xbc_ref, z_ref, dt_ref,
                                         cw_ref, cb_ref, dtb_ref, alog_ref, dexp_ref, ng_ref,
                                         ext_ref, st_ref, xs_ref, bm_ref, cm_ref, y_ref, hn_ref)

    ssd_gen = ssd_all()

    plan = iter(SSD_PLAN)

    def ssd_step():
        for _ in range(next(plan, 0)):
            next(ssd_gen, None)
    lane_q = lax.broadcasted_iota(jnp.int32, (Q, LANES), 1)
    row_q = lax.broadcasted_iota(jnp.int32, (Q, LANES), 0)
    left = lane_q < ATT_HEAD_DIM
    head_mean = jnp.where((row_q // ATT_HEAD_DIM) == (lane_q // ATT_HEAD_DIM),
                          1.0 / ATT_HEAD_DIM, 0.0).astype(bf16)
    lane2 = lax.broadcasted_iota(jnp.int32, (2 * Q, LANES), 1) < ATT_HEAD_DIM
    qg = ATT_N_HEADS // ATT_N_KV
    rows = qg * Q
    ri = lax.broadcasted_iota(jnp.int32, (rows, Q), 0) % Q
    cj = lax.broadcasted_iota(jnp.int32, (rows, Q), 1)
    upper = cj > ri
    half = ROPE_DIM // 2

    n_sb = R // Q
    nq = COL_Q // LANES
    nk = COL_KV // LANES
    chunks = []
    for sb in range(n_sb):
        rs = slice(sb * Q, (sb + 1) * Q)
        chunks += [q_ref[rs, c * LANES:(c + 1) * LANES].astype(f32) for c in range(nq)]
        chunks += [k_ref[rs, c * LANES:(c + 1) * LANES].astype(f32) for c in range(nk)]
    u_all = jnp.concatenate(chunks, axis=0)
    sq = u_all * u_all
    sq_hi = sq.astype(bf16)
    sq_lo = (sq - sq_hi.astype(f32)).astype(bf16)
    un_all = u_all * lax.rsqrt(_dot(sq_hi, head_mean) + _dot(sq_lo, head_mean) + RMS_EPS)
    ssd_step()

    terms = []
    for sb in range(n_sb):
        rs = slice(sb * Q, (sb + 1) * Q)
        cs = jnp.concatenate([cos_ref[:, rs], sin_ref[:, rs], jnp.ones((half, LANES), f32),
                              jnp.zeros((LANES - 3 * half, LANES), f32)], axis=0)
        terms += list(_split3(cs.T))
    pat_all = _dot(jnp.concatenate(terms, axis=0), sel_ref[...])

    pats, kds = [], []
    for sb in range(n_sb):
        rs = slice(sb * Q, (sb + 1) * Q)
        pat = pat_all[3 * sb * Q:(3 * sb + 1) * Q] + pat_all[(3 * sb + 1) * Q:(3 * sb + 2) * Q] \
            + pat_all[(3 * sb + 2) * Q:(3 * sb + 3) * Q]
        cpat = pat[:, 0:LANES]
        s1pat = pat[:, LANES:2 * LANES]
        s2pat = pat[:, 2 * LANES:3 * LANES]

        def norm_rope(idx, gpat):
            tn = un_all[idx * Q:(idx + 1) * Q] * gpat
            return (tn * cpat + pltpu.roll(tn, LANES - ROPE_DIM // 2, 1) * s1pat
                    + pltpu.roll(tn, ROPE_DIM // 2, 1) * s2pat)

        kd = []
        for cidx in range(nk):
            sl = slice(cidx * LANES, (cidx + 1) * LANES)
            k_cur = norm_rope(sb * (nq + nk) + nq + cidx, kg_ref[...])
            v_cur = v_ref[rs, sl].astype(f32)
            k_all = jnp.concatenate([kprev_ref[:, sl], k_cur], axis=0)
            v_all = jnp.concatenate([vprev_ref[:, sl], v_cur], axis=0)
            kprev_ref[:, sl] = k_cur
            vprev_ref[:, sl] = v_cur
            k_sw = pltpu.roll(k_all, ATT_HEAD_DIM, 1)
            v_sw = pltpu.roll(v_all, ATT_HEAD_DIM, 1)
            kd.append(jnp.where(lane2, k_all, k_sw).astype(bf16))
            kd.append(jnp.where(lane2, k_sw, k_all).astype(bf16))
            for half_i, vv in enumerate((jnp.where(lane2, v_all, v_sw), jnp.where(lane2, v_sw, v_all))):
                vd_ref[sb * ATT_N_KV + 2 * cidx + half_i] = vv.astype(bf16)

        pats.append((cpat, s1pat, s2pat))
        kds.append(kd)
        ssd_step()

    def rope_q(sb, cidx):
        cpat, s1pat, s2pat = pats[sb]
        tn = un_all[(sb * (nq + nk) + cidx) * Q:(sb * (nq + nk) + cidx + 1) * Q] * qg_ref[...]
        return (tn * cpat + pltpu.roll(tn, LANES - ROPE_DIM // 2, 1) * s1pat
                + pltpu.roll(tn, ROPE_DIM // 2, 1) * s2pat)

    def score_tile(g):
        sb, h = divmod(g, ATT_N_KV)
        parts = []
        for cidx in (2 * h, 2 * h + 1):
            qc = rope_q(sb, cidx)
            parts.append(jnp.where(left, qc, 0.0).astype(bf16))
            parts.append(jnp.where(left, 0.0, qc).astype(bf16))
        lhs = jnp.concatenate(parts, axis=0)
        s_both = _dot_nt(lhs, kds[sb][h])
        s_prev = s_both[:, 0:Q]
        s_cur = s_both[:, Q:2 * Q]
        if sb == 0:
            return jnp.where(upper & (n > 0), s_prev, jnp.where(upper, NEG_BIG, s_cur))
        return jnp.where(upper, s_prev, s_cur)

    ones = jnp.ones((Q, Q), bf16)

    def softmax_block(s):
        sink = jnp.concatenate([jnp.full((Q, 1), sink_ref[i] * LOG2E, f32) for i in range(ATT_N_HEADS)],
                               axis=0)
        m = jnp.maximum(jnp.max(s, axis=-1, keepdims=True), sink)
        p = jnp.exp2(s - m)
        p_hi = p.astype(bf16)
        p_lo = (p - p_hi.astype(f32)).astype(bf16)
        denom = _dot(p_hi, ones) + _dot(p_lo, ones) + jnp.exp2(sink - m)
        return (p / denom).astype(bf16)

    def pv_tile(g, pf):
        sb, h = divmod(g, ATT_N_KV)
        rs = slice(sb * Q, (sb + 1) * Q)
        zero = jnp.zeros_like(pf)
        p_both = jnp.concatenate([jnp.where(upper, pf, zero), jnp.where(upper, zero, pf)], axis=1)
        o = _dot(p_both, vd_ref[g])
        for r in range(2):
            cidx = 2 * h + r
            att_ref[rs, cidx * LANES:(cidx + 1) * LANES] = jnp.where(
                left, o[(2 * r) * Q:(2 * r + 1) * Q], o[(2 * r + 1) * Q:(2 * r + 2) * Q]).astype(bf16)

    def epilogue(rs, cnt):
        E = rs.stop - rs.start
        lane = lax.broadcasted_iota(jnp.int32, (E, LANES), 1)
        y_att = _dot(att_ref[rs, :], wao_ref[...])
        merged = _sigmoid(ga_ref[rs, :].astype(f32)) * y_att + ms_ref[rs, :]
        x1 = x_ref[rs, :] + _dot(merged.astype(bf16), wo_ref[...])
        x1_ref[rs, :] = x1
        h2 = x1 * lax.rsqrt(jnp.mean(x1 * x1, axis=-1, keepdims=True) + RMS_EPS) * n2g_ref[...]
        h2_ref[rs, :] = _pack_rows(h2)

        hi = h2.astype(bf16)
        lo = (h2 - hi.astype(f32)).astype(bf16)
        logits = _dot(hi, wrh_ref[...]) + _dot(lo, wrh_ref[...]) + _dot(hi, wrl_ref[...]) + br_ref[...]
        big = 4 * LANES
        gl = jnp.where(lane < MOE_N_GROUPS, logits, NEG_BIG)
        gmax = jnp.max(gl, axis=-1, keepdims=True)
        gsel = jnp.min(jnp.where(gl == gmax, lane, big), axis=-1, keepdims=True)
        pg = 1.0 / jnp.sum(jnp.exp(gl - gmax), axis=-1, keepdims=True)
        lo_l = MOE_N_GROUPS + MOE_EPG * gsel
        el = jnp.where((lane >= lo_l) & (lane < lo_l + MOE_EPG), logits, NEG_BIG)
        v1 = jnp.max(el, axis=-1, keepdims=True)
        i1 = jnp.min(jnp.where(el == v1, lane, big), axis=-1, keepdims=True)
        el2 = jnp.where(lane == i1, NEG_BIG, el)
        v2 = jnp.max(el2, axis=-1, keepdims=True)
        i2 = jnp.min(jnp.where(el2 == v2, lane, big), axis=-1, keepdims=True)
        e1 = i1 - MOE_N_GROUPS
        e2 = i2 - MOE_N_GROUPS
        tt = jnp.exp(v2 - v1)
        w1 = pg * (1.0 / (1.0 + tt))
        w2 = pg * (tt / (1.0 + tt))

        onehot = jnp.where((lane == e1) | (lane == e2), 1.0, 0.0)
        strict = jnp.where(lax.broadcasted_iota(jnp.int32, (E, E), 0) > lax.broadcasted_iota(jnp.int32, (E, E), 1),
                           1.0, 0.0).astype(bf16)
        base = _dot(strict, onehot.astype(bf16)) + cnt
        r1 = jnp.sum(jnp.where(lane == e1, base, 0.0), axis=-1, keepdims=True)
        r2 = jnp.sum(jnp.where(lane == e2, base, 0.0), axis=-1, keepdims=True)

        rec = jnp.zeros((E, LANES), f32)
        for off, val in ((ROUTE_E, e1.astype(f32)), (ROUTE_E + 1, e2.astype(f32)),
                         (ROUTE_RANK, r1), (ROUTE_RANK + 1, r2), (ROUTE_GATE, w1), (ROUTE_GATE + 1, w2)):
            rec = jnp.where(lane == off, val, rec)
        route_ref[rs, :] = rec
        for o in range(0, E, Q):
            route_t_ref[:, rs.start + o:rs.start + o + Q] = rec[o:o + Q].T[0:8, :]
        return cnt + jnp.sum(onehot, axis=0, keepdims=True)

    s_tiles, p_tiles = {}, {}
    for stage in range(n_sb + 2):
        if stage < n_sb:
            s_tiles[stage] = jnp.concatenate(
                [score_tile(stage * ATT_N_KV + h) for h in range(ATT_N_KV)], axis=0)
            ssd_step()
        if 0 <= stage - 1 < n_sb:
            p_tiles[stage - 1] = softmax_block(s_tiles.pop(stage - 1))
            ssd_step()
        if 0 <= stage - 2 < n_sb:
            pb = p_tiles.pop(stage - 2)
            for h in range(ATT_N_KV):
                pv_tile((stage - 2) * ATT_N_KV + h, pb[h * rows:(h + 1) * rows])
            ssd_step()

    for _ in ssd_gen:
        pass
    ms_ref[...] = _sigmoid(gs_ref[...].astype(f32)) * _dot(hn_ref[...], wssd_ref[...])
    cnt = epilogue(slice(0, R), cnt_scr[0:1, :])

    cnt_scr[...] = jnp.broadcast_to(cnt, cnt_scr.shape)
    cnt_ref[...] = jnp.broadcast_to(cnt, cnt_ref.shape)


def _mixer(q, k, v, cos_t, sin_t, ga, x2, xbc, z, dt, gs, q_norm_g, k_norm_g, sinks, w_attn_out, w_out,
           norm2_g, w_rg, b_rg, w_re, b_re, conv_w, conv_b, dt_bias, a_log, d_skip, ssd_norm_g, w_ssd_out,
           batch, seq):
    t = batch * seq
    Q = ATT_SUB * ATT_BLOCK
    nb = seq // Q
    pad_h = LANES - SSD_N_HEADS
    dtb = jnp.pad(dt_bias, (0, pad_h)).reshape(1, LANES)
    alog = jnp.pad(a_log, (0, pad_h)).reshape(1, LANES)
    dexp = jnp.repeat(d_skip, SSD_HEAD_DIM).reshape(1, SSD_D_INNER)
    rep = LANES // ATT_HEAD_DIM
    qg = (jnp.tile(q_norm_g, rep) * (ATT_SCALE * LOG2E)).reshape(1, LANES)
    kg = jnp.tile(k_norm_g, rep).reshape(1, LANES)
    n_log = MOE_N_GROUPS + MOE_N_EXPERTS
    w_r = jnp.concatenate([w_rg, jnp.transpose(w_re, (1, 0, 2)).reshape(D_MODEL, MOE_N_EXPERTS),
                           jnp.zeros((D_MODEL, LANES - n_log), f32)], axis=1)
    b_r = jnp.concatenate([b_rg, b_re.reshape(-1), jnp.zeros((LANES - n_log,), f32)]).reshape(1, LANES)
    w_r_hi = w_r.astype(bf16)
    w_r_lo = (w_r - w_r_hi.astype(f32)).astype(bf16)
    const = lambda b, n, *_: (0, 0)
    row = lambda b, n, *_: (b * nb + n, 0)
    col = lambda b, n, *_: (0, b * nb + n)
    full = lambda shape: pl.BlockSpec(shape, const, pipeline_mode=pl.Buffered(1))
    return pl.pallas_call(
        _mixer_kernel,
        grid_spec=pltpu.PrefetchScalarGridSpec(
            num_scalar_prefetch=1,
            grid=(batch, nb),
            in_specs=[pl.BlockSpec((Q, COL_Q), row),
                      pl.BlockSpec((Q, COL_KV), row),
                      pl.BlockSpec((Q, COL_KV), row),
                      pl.BlockSpec((ROPE_DIM // 2, Q), col),
                      pl.BlockSpec((ROPE_DIM // 2, Q), col),
                      pl.BlockSpec((Q, D_MODEL), row),
                      pl.BlockSpec((Q, D_MODEL), row),
                      pl.BlockSpec((Q, SSD_CONV_DIM), row),
                      pl.BlockSpec((Q, SSD_D_INNER), row),
                      pl.BlockSpec((Q, LANES), row),
                      pl.BlockSpec((Q, D_MODEL), row),
                      full((LANES, 3 * LANES)),
                      pl.BlockSpec((1, LANES), const),
                      pl.BlockSpec((1, LANES), const),
                      full((COL_Q, D_MODEL)),
                      full((D_MODEL, D_MODEL)),
                      pl.BlockSpec((1, D_MODEL), const),
                      full((D_MODEL, LANES)),
                      full((D_MODEL, LANES)),
                      pl.BlockSpec((1, LANES), const),
                      pl.BlockSpec((SSD_CONV, SSD_CONV_DIM), const),
                      pl.BlockSpec((1, SSD_CONV_DIM), const),
                      pl.BlockSpec((1, LANES), const),
                      pl.BlockSpec((1, LANES), const),
                      pl.BlockSpec((1, SSD_D_INNER), const),
                      pl.BlockSpec((1, SSD_D_INNER), const),
                      full((SSD_D_INNER, D_MODEL))],
            out_specs=[pl.BlockSpec((Q, D_MODEL), row),
                       pl.BlockSpec((Q, HALF_D), row),
                       pl.BlockSpec((Q, LANES), row),
                       pl.BlockSpec((8, Q), col),
                       pl.BlockSpec((8, LANES), const)],
            scratch_shapes=[pltpu.VMEM((ATT_BLOCK, COL_KV), f32),
                            pltpu.VMEM((ATT_BLOCK, COL_KV), f32),
                            pltpu.VMEM((Q, COL_Q), bf16),
                            pltpu.VMEM((8, LANES), f32),
                            pltpu.VMEM((ATT_SUB * ATT_N_KV, 2 * ATT_BLOCK, LANES), bf16),
                            pltpu.VMEM((2 * CONV_HALO, SSD_CONV_DIM), f32),
                            pltpu.VMEM((SSD_N_HEADS // 2, SSD_D_STATE, LANES), f32),
                            pltpu.VMEM((Q, SSD_D_INNER), f32),
                            pltpu.VMEM((Q, SSD_N_GROUPS * SSD_D_STATE), f32),
                            pltpu.VMEM((Q, SSD_N_GROUPS * SSD_D_STATE), f32),
                            pltpu.VMEM((Q, SSD_D_INNER), f32),
                            pltpu.VMEM((Q, SSD_D_INNER), bf16),
                            pltpu.VMEM((Q, D_MODEL), f32)],
        ),
        out_shape=[jax.ShapeDtypeStruct((t, D_MODEL), f32),
                   jax.ShapeDtypeStruct((t, HALF_D), jnp.uint32),
                   jax.ShapeDtypeStruct((t, LANES), f32),
                   jax.ShapeDtypeStruct((8, t), f32),
                   jax.ShapeDtypeStruct((8, LANES), f32)],
        compiler_params=pltpu.CompilerParams(dimension_semantics=("arbitrary", "arbitrary"),
                                             vmem_limit_bytes=VMEM_LIMIT),
        name="mixer",
    )(sinks, q, k, v, cos_t, sin_t, ga, x2, xbc, z, dt, gs,
      _rope_select(), qg, kg, w_attn_out.astype(bf16), w_out.astype(bf16), norm2_g.reshape(1, D_MODEL),
      w_r_hi, w_r_lo, b_r,
      conv_w, conv_b.reshape(1, -1), dtb, alog, dexp, ssd_norm_g.reshape(1, -1), w_ssd_out.astype(bf16))


MOE_TM = 1024


def _dispatch_kernel(dest0_ref, dest1_ref, zblk_ref, h2_ref, xs_hbm, zero_ref, sem, zsem):
    tm = h2_ref.shape[0] * SUBLANES
    base = pl.program_id(0) * tm

    @pl.when(pl.program_id(0) == 0)
    def _():
        zero_ref[...] = jnp.zeros(zero_ref.shape, zero_ref.dtype)

        def zcopy(i):
            start = pl.multiple_of(zblk_ref[i] * MOE_BLOCK, MOE_BLOCK)
            return pltpu.make_async_copy(zero_ref, xs_hbm.at[pl.ds(start, MOE_BLOCK), :], zsem)

        def zstart(i, carry):
            @pl.when(zblk_ref[i] >= 0)
            def _():
                zcopy(i).start()
            return carry

        def zwait(i, carry):
            @pl.when(zblk_ref[i] >= 0)
            def _():
                zcopy(i).wait()
            return carry

        lax.fori_loop(0, zblk_ref.shape[0], zstart, 0)
        lax.fori_loop(0, zblk_ref.shape[0], zwait, 0)

    def issue(k, carry):
        for u in range(SUBLANES):
            i = k * SUBLANES + u
            src = h2_ref.at[k, pl.ds(u, 1), :]
            pltpu.make_async_copy(src, xs_hbm.at[pl.ds(dest0_ref[base + i], 1), :], sem).start()
            pltpu.make_async_copy(src, xs_hbm.at[pl.ds(dest1_ref[base + i], 1), :], sem).start()
        return carry

    lax.fori_loop(0, tm // SUBLANES, issue, 0)
    for _ in range(2):
        pltpu.make_async_copy(xs_hbm.at[pl.ds(0, tm), :], xs_hbm.at[pl.ds(0, tm), :], sem).wait()


def _dispatch(h2, dest0, dest1, zero_blocks, n_rows):
    t = h2.shape[0]
    tm = min(MOE_TM, t)
    return pl.pallas_call(
        _dispatch_kernel,
        grid_spec=pltpu.PrefetchScalarGridSpec(
            num_scalar_prefetch=3,
            grid=(t // tm,),
            in_specs=[pl.BlockSpec((tm // SUBLANES, SUBLANES, HALF_D), lambda i, d0, d1, zb: (i, 0, 0))],
            out_specs=pl.BlockSpec(memory_space=pl.ANY),
            scratch_shapes=[pltpu.VMEM((MOE_BLOCK, HALF_D), h2.dtype),
                            pltpu.SemaphoreType.DMA(()),
                            pltpu.SemaphoreType.DMA(())],
        ),
        out_shape=jax.ShapeDtypeStruct((n_rows, HALF_D), h2.dtype),
        compiler_params=pltpu.CompilerParams(dimension_semantics=("arbitrary",)),
        name="dispatch",
    )(dest0, dest1, zero_blocks, h2.reshape(t // SUBLANES, SUBLANES, HALF_D))


EXPERT_SUB = 1


def _expert_kernel(be_ref, nu_ref, xs_ref, *refs):
    w_refs = refs[:3 * EXPERT_SUB]
    ys_ref, wgu_scr, wd_scr = refs[3 * EXPERT_SUB:]
    i = pl.program_id(0)
    rows = MOE_BLOCK

    for j in range(EXPERT_SUB):
        blk = EXPERT_SUB * i + j
        fresh = (i == 0) | (be_ref[blk] != be_ref[jnp.maximum(blk - EXPERT_SUB, 0)])

        @pl.when((blk < nu_ref[0]) & fresh)
        def _(j=j):
            wg_ref, wu_ref, wd_ref = w_refs[3 * j:3 * j + 3]
            wgu_scr[j, :, 0:MOE_D_FF] = wg_ref[0].astype(bf16)
            wgu_scr[j, :, MOE_D_FF:2 * MOE_D_FF] = wu_ref[0].astype(bf16)
            wd_scr[j] = wd_ref[0].astype(bf16)

    def block(j):
        rs = slice(j * rows, (j + 1) * rows)
        x_lo, x_hi = _unpack_rows(xs_ref[rs, :])
        gu = (_dot(x_lo.astype(bf16), wgu_scr[j, 0:HALF_D, :])
              + _dot(x_hi.astype(bf16), wgu_scr[j, HALF_D:D_MODEL, :]))
        hid = _silu(gu[:, :MOE_D_FF]) * gu[:, MOE_D_FF:]
        ys_ref[rs, :] = _pack_rows(_dot(hid.astype(bf16), wd_scr[j]))

    n_live = jnp.clip(nu_ref[0] - EXPERT_SUB * i, 0, EXPERT_SUB)
    for live in range(EXPERT_SUB + 1):
        @pl.when(n_live == live)
        def _(live=live):
            for j in range(live):
                block(j)
            if live < EXPERT_SUB:
                ys_ref[live * rows:, :] = jnp.zeros(((EXPERT_SUB - live) * rows, HALF_D), ys_ref.dtype)


def _experts(xs, block_e, n_used, w_g, w_u, w_d):
    n_rows = xs.shape[0]
    rows = EXPERT_SUB * MOE_BLOCK
    nstep = n_rows // rows
    blk_in = lambda i, be, nu: (jnp.minimum(i, (nu[0] - 1) // EXPERT_SUB), 0)
    blk_out = lambda i, be, nu: (i, 0)
    w_specs = []
    for j in range(EXPERT_SUB):
        wsel = lambda i, be, nu, j=j: (be[EXPERT_SUB * i + j], 0, 0)
        w_specs += [pl.BlockSpec((1, D_MODEL, MOE_D_FF), wsel),
                    pl.BlockSpec((1, D_MODEL, MOE_D_FF), wsel),
                    pl.BlockSpec((1, MOE_D_FF, D_MODEL), wsel)]
    return pl.pallas_call(
        _expert_kernel,
        grid_spec=pltpu.PrefetchScalarGridSpec(
            num_scalar_prefetch=2,
            grid=(nstep,),
            in_specs=[pl.BlockSpec((rows, HALF_D), blk_in)] + w_specs,
            out_specs=pl.BlockSpec((rows, HALF_D), blk_out),
            scratch_shapes=[pltpu.VMEM((EXPERT_SUB, D_MODEL, 2 * MOE_D_FF), bf16),
                            pltpu.VMEM((EXPERT_SUB, MOE_D_FF, D_MODEL), bf16)],
        ),
        out_shape=jax.ShapeDtypeStruct((n_rows, HALF_D), xs.dtype),
        compiler_params=pltpu.CompilerParams(dimension_semantics=("arbitrary",),
                                             vmem_limit_bytes=VMEM_LIMIT),
        name="experts",
    )(block_e, n_used, xs, *([w_g, w_u, w_d] * EXPERT_SUB))


def _combine_kernel(dest0_ref, dest1_ref, x1_ref, route_ref, ys_hbm, out_ref, y0_ref, y1_ref, sem):
    tm = x1_ref.shape[0]
    step = pl.program_id(0)

    def gather_tile(tile, slot):
        base = tile * tm

        def issue(k, carry):
            for u in range(SUBLANES):
                r = k * SUBLANES + u
                pltpu.make_async_copy(ys_hbm.at[pl.ds(dest0_ref[base + r], 1), :],
                                      y0_ref.at[slot, k, pl.ds(u, 1), :], sem.at[slot]).start()
                pltpu.make_async_copy(ys_hbm.at[pl.ds(dest1_ref[base + r], 1), :],
                                      y1_ref.at[slot, k, pl.ds(u, 1), :], sem.at[slot]).start()
            return carry

        lax.fori_loop(0, tm // SUBLANES, issue, 0)

    @pl.when(step == 0)
    def _():
        gather_tile(0, 0)

    @pl.when(step + 1 < pl.num_programs(0))
    def _():
        gather_tile(step + 1, (step + 1) % 2)

    slot = step % 2
    for _ in range(2):
        pltpu.make_async_copy(ys_hbm.at[pl.ds(0, tm), :], ys_hbm.at[pl.ds(0, tm), :], sem.at[slot]).wait()
    g0 = route_ref[:, ROUTE_GATE:ROUTE_GATE + 1]
    g1 = route_ref[:, ROUTE_GATE + 1:ROUTE_GATE + 2]
    y0_lo, y0_hi = _unpack_rows(y0_ref[slot].reshape(tm, HALF_D))
    y1_lo, y1_hi = _unpack_rows(y1_ref[slot].reshape(tm, HALF_D))
    out_ref[:, 0:HALF_D] = x1_ref[:, 0:HALF_D] + y0_lo * g0 + y1_lo * g1
    out_ref[:, HALF_D:D_MODEL] = x1_ref[:, HALF_D:D_MODEL] + y0_hi * g0 + y1_hi * g1


def _combine(x1, route, ys, dest0, dest1):
    t = x1.shape[0]
    tm = min(MOE_TM, t)
    row = lambda i, d0, d1: (i, 0)
    return pl.pallas_call(
        _combine_kernel,
        grid_spec=pltpu.PrefetchScalarGridSpec(
            num_scalar_prefetch=2,
            grid=(t // tm,),
            in_specs=[pl.BlockSpec((tm, D_MODEL), row),
                      pl.BlockSpec((tm, LANES), row),
                      pl.BlockSpec(memory_space=pl.ANY)],
            out_specs=pl.BlockSpec((tm, D_MODEL), row),
            scratch_shapes=[pltpu.VMEM((2, tm // SUBLANES, SUBLANES, HALF_D), ys.dtype),
                            pltpu.VMEM((2, tm // SUBLANES, SUBLANES, HALF_D), ys.dtype),
                            pltpu.SemaphoreType.DMA((2,))],
        ),
        out_shape=jax.ShapeDtypeStruct((t, D_MODEL), f32),
        compiler_params=pltpu.CompilerParams(dimension_semantics=("arbitrary",)),
        name="combine",
    )(dest0, dest1, x1, route, ys)


def _moe(x1, h2, route, route_t, counts_rec, w_gate_e, w_up_e, w_down_e):
    t = x1.shape[0]
    n_rows = 2 * t + MOE_N_EXPERTS * MOE_BLOCK
    nblk = n_rows // MOE_BLOCK
    counts = counts_rec[0, :MOE_N_EXPERTS].astype(jnp.int32)
    padded = (counts + MOE_BLOCK - 1) // MOE_BLOCK * MOE_BLOCK
    pend = jnp.cumsum(padded)
    pstart = pend - padded
    expert_ids = jnp.arange(MOE_N_EXPERTS, dtype=jnp.int32)[:, None]

    def sorted_row(k):
        eid = route_t[ROUTE_E + k].astype(jnp.int32)
        rank = route_t[ROUTE_RANK + k].astype(jnp.int32)
        return jnp.sum(jnp.where(eid[None, :] == expert_ids, pstart[:, None], 0), axis=0) + rank

    dest0, dest1 = sorted_row(0), sorted_row(1)
    block_start = jnp.arange(nblk, dtype=jnp.int32) * MOE_BLOCK
    block_e = jnp.minimum(jnp.sum((pend[None, :] <= block_start[:, None]).astype(jnp.int32), axis=1),
                          MOE_N_EXPERTS - 1)
    n_used = (pend[-1:] // MOE_BLOCK).astype(jnp.int32)
    last_blk = jnp.where(counts > 0, pend // MOE_BLOCK - 1, -1)
    tail = n_used[0] + jnp.arange(MOE_N_EXPERTS, dtype=jnp.int32)
    zero_blocks = jnp.concatenate([last_blk, jnp.where(tail < nblk, tail, -1)]).astype(jnp.int32)
    xs = _dispatch(h2, dest0, dest1, zero_blocks, n_rows)
    ys = _experts(xs, block_e, n_used, w_gate_e, w_up_e, w_down_e)
    return _combine(x1, route, ys, dest0, dest1)


def _layer(x, positions, norm1_g, w_in, conv_w, conv_b, dt_bias, a_log, d_skip, ssd_norm_g, w_ssd_out,
           q_norm_g, k_norm_g, sinks, w_attn_out, w_out, norm2_g, w_rg, b_rg, w_re, b_re,
           w_gate_e, w_up_e, w_down_e):
    batch, seq, d = x.shape
    x2 = x.reshape(batch * seq, d)
    cos_t, sin_t = _rope_tables(positions)
    z, xbc, q, k, v, gs, ga, dt = _inproj(x2, norm1_g, w_in)
    x1, h2, route, route_t, counts = _mixer(q, k, v, cos_t, sin_t, ga, x2, xbc, z, dt, gs, q_norm_g, k_norm_g,
                                            sinks, w_attn_out, w_out, norm2_g, w_rg, b_rg, w_re, b_re,
                                            conv_w, conv_b, dt_bias, a_log, d_skip, ssd_norm_g, w_ssd_out,
                                            batch, seq)
    out = _moe(x1, h2, route, route_t, counts, w_gate_e, w_up_e, w_down_e)
    return out.reshape(batch, seq, d)


def kernel(x, positions, norm1_g, w_in, conv_w, conv_b, dt_bias, a_log, d_skip, ssd_norm_g, w_ssd_out,
           q_norm_g, k_norm_g, sinks, w_attn_out, w_out, norm2_g, w_router_group, b_router_group,
           w_router_expert, b_router_expert, w_gate_e, w_up_e, w_down_e):
    for l in range(norm1_g.shape[0]):
        x = _layer(x, positions, norm1_g[l], w_in[l], conv_w[l], conv_b[l], dt_bias[l], a_log[l],
                   d_skip[l], ssd_norm_g[l], w_ssd_out[l], q_norm_g[l], k_norm_g[l], sinks[l],
                   w_attn_out[l], w_out[l], norm2_g[l], w_router_group[l], b_router_group[l],
                   w_router_expert[l], b_router_expert[l], w_gate_e[l], w_up_e[l], w_down_e[l])
    return x
```

```python
import numpy as np

import jax
import jax.numpy as jnp
from jax import lax
from jax.experimental import pallas as pl
from jax.experimental.pallas import tpu as pltpu

f32 = jnp.float32
bf16 = jnp.bfloat16

D_MODEL = 1024
SSD_D_INNER = 2048
SSD_HEAD_DIM = 64
SSD_N_HEADS = 32
SSD_N_GROUPS = 4
SSD_D_STATE = 128
SSD_CONV = 4
SSD_CHUNK = 128
SSD_CONV_DIM = 3072
ATT_HEAD_DIM = 64
ATT_N_HEADS = 16
ATT_N_KV = 4
ATT_BLOCK = 128
ATT_SCALE = ATT_HEAD_DIM ** -0.5
ROPE_THETA = 500000.0
ROPE_DIM = 16
MOE_N_GROUPS = 8
MOE_EPG = 8
MOE_N_EXPERTS = 64
MOE_D_FF = 256
MOE_BLOCK = 512
RMS_EPS = 1e-6

LANES = 128
SUBLANES = 8
CONV_HALO = 8
NEG_BIG = -1e30
VMEM_LIMIT = 56 * 1024 * 1024

COL_Z = SSD_D_INNER
COL_XBC = SSD_CONV_DIM
COL_DT = SSD_N_HEADS
COL_Q = ATT_N_HEADS * ATT_HEAD_DIM
COL_KV = ATT_N_KV * ATT_HEAD_DIM


LOG2E = 1.4426950408889634


def _sigmoid(x):
    return 1.0 / (1.0 + jnp.exp2(x * (-LOG2E)))


def _silu(x):
    return x * _sigmoid(x)


def _split3(x):
    hi = x.astype(bf16)
    r1 = x - hi.astype(f32)
    mid = r1.astype(bf16)
    lo = (r1 - mid.astype(f32)).astype(bf16)
    return hi, mid, lo


HALF_D = D_MODEL // 2
_HI_MASK = np.uint32(0xFFFF0000)


def _pack_rows(x):
    bits = pltpu.bitcast(x.astype(bf16).astype(f32), jnp.uint32)
    return (bits[:, HALF_D:] & _HI_MASK) | (bits[:, :HALF_D] >> 16)


def _unpack_rows(p):
    return pltpu.bitcast(p << 16, f32), pltpu.bitcast(p & _HI_MASK, f32)


def _dot(a, b):
    return jnp.dot(a, b, preferred_element_type=f32)


def _dot_nt(a, b):
    return lax.dot_general(a, b, (((1,), (1,)), ((), ())), preferred_element_type=f32)


def _trig_kernel(freq_ref, pos_ref, cos_ref, sin_ref):
    ang = pos_ref[...].astype(f32) * freq_ref[pl.program_id(0)]
    cos_ref[0] = jnp.cos(ang)
    sin_ref[0] = jnp.sin(ang)


def _rope_select():
    half = ROPE_DIM // 2
    sel = np.zeros((LANES, 3 * LANES), np.float32)
    for lane in range(LANES):
        m = lane % ATT_HEAD_DIM
        if m < half:
            sel[m, lane] = 1.0
            sel[half + m, LANES + lane] = -1.0
        elif m < ROPE_DIM:
            sel[m - half, lane] = 1.0
            sel[m, 2 * LANES + lane] = 1.0
        else:
            sel[ROPE_DIM, lane] = 1.0
    return jnp.asarray(sel, bf16)


def _rope_tables(positions):
    t = positions.size
    half = ROPE_DIM // 2
    inv_freq = ROPE_THETA ** (-jnp.arange(0, ROPE_DIM, 2, dtype=f32) / ROPE_DIM)
    pos2d = positions.reshape(t // LANES, LANES)
    cos_t, sin_t = pl.pallas_call(
        _trig_kernel,
        grid_spec=pltpu.PrefetchScalarGridSpec(
            num_scalar_prefetch=1,
            grid=(half,),
            in_specs=[pl.BlockSpec((t // LANES, LANES), lambda j, f: (0, 0))],
            out_specs=[pl.BlockSpec((1, t // LANES, LANES), lambda j, f: (j, 0, 0))] * 2,
        ),
        out_shape=[jax.ShapeDtypeStruct((half, t // LANES, LANES), f32)] * 2,
        name="trig",
    )(inv_freq, pos2d)
    return cos_t.reshape(half, t), sin_t.reshape(half, t)


INPROJ_TM = 512
INPROJ_CH = 512


def _inproj_kernel(x_ref, g_ref, w_ref,
                   z_ref, xbc_ref, q_ref, k_ref, v_ref, gs_ref, ga_ref, dt_ref, h_scr):
    x = x_ref[...]
    ms = jnp.mean(x * x, axis=-1, keepdims=True)
    h_scr[...] = (x * lax.rsqrt(ms + RMS_EPS) * g_ref[...]).astype(bf16)
    off = 0
    for ref in (z_ref, xbc_ref, q_ref, k_ref, v_ref, gs_ref, ga_ref, dt_ref):
        width = ref.shape[1]
        for c in range(0, width, INPROJ_CH):
            cw = min(INPROJ_CH, width - c)
            ref[:, c:c + cw] = _dot(h_scr[...], w_ref[:, off + c:off + c + cw]).astype(ref.dtype)
        off += width


WPREP_ROWS = 128


def _wprep_kernel(w_ref, o_ref):
    s1 = COL_Z + COL_XBC
    s2 = s1 + COL_DT
    n_tail = w_ref.shape[1] - s2
    o_ref[:, 0:s1] = w_ref[:, 0:s1].astype(bf16)
    o_ref[:, s1:s1 + n_tail] = w_ref[:, s2:s2 + n_tail].astype(bf16)
    lane = lax.broadcasted_iota(jnp.int32, (w_ref.shape[0], LANES), 1)
    o_ref[:, s1 + n_tail:s1 + n_tail + LANES] = jnp.where(lane < COL_DT, w_ref[:, s1:s1 + LANES], 0.0).astype(bf16)


def _inproj(x2, norm1_g, w_in):
    t = x2.shape[0]
    tm = min(INPROJ_TM, t)
    n_in = w_in.shape[1]
    n_all = n_in - COL_DT + LANES
    w_all = pl.pallas_call(
        _wprep_kernel,
        grid=(D_MODEL // WPREP_ROWS,),
        in_specs=[pl.BlockSpec((WPREP_ROWS, n_in), lambda i: (i, 0))],
        out_specs=pl.BlockSpec((WPREP_ROWS, n_all), lambda i: (i, 0)),
        out_shape=jax.ShapeDtypeStruct((D_MODEL, n_all), bf16),
        name="wprep",
    )(w_in)
    widths = (COL_Z, COL_XBC, COL_Q, COL_KV, COL_KV, D_MODEL, D_MODEL)
    const = lambda i: (0, 0)
    row = lambda i: (i, 0)
    outs = pl.pallas_call(
        _inproj_kernel,
        grid=(t // tm,),
        in_specs=[pl.BlockSpec((tm, D_MODEL), row),
                  pl.BlockSpec((1, D_MODEL), const),
                  pl.BlockSpec((D_MODEL, w_all.shape[1]), const, pipeline_mode=pl.Buffered(1))],
        out_specs=[pl.BlockSpec((tm, w), row) for w in widths] + [pl.BlockSpec((tm, LANES), row)],
        out_shape=[jax.ShapeDtypeStruct((t, w), bf16) for w in widths]
                  + [jax.ShapeDtypeStruct((t, LANES), f32)],
        scratch_shapes=[pltpu.VMEM((tm, D_MODEL), bf16)],
        compiler_params=pltpu.CompilerParams(dimension_semantics=("arbitrary",),
                                             vmem_limit_bytes=VMEM_LIMIT),
        name="inproj",
    )(x2, norm1_g.reshape(1, D_MODEL), w_all)
    return outs


SSD_CONV_YIELD = 1024

def _ssd_chunk_phases(rs, xbc_ref, z_ref, dt_ref, cw_ref, cb_ref, dtb_ref, alog_ref, dexp_ref, ng_ref,
                      ext_ref, st_ref, xs_ref, bm_ref, cm_ref, y_ref, hn_ref):
    L = SSD_CHUNK
    r0 = rs.start

    bf_tile = 2 * CONV_HALO
    ext_ref[CONV_HALO:2 * CONV_HALO, :] = xbc_ref[r0:r0 + bf_tile, :].astype(f32)[0:CONV_HALO]
    n_sh = SSD_CONV - 1
    sr = lax.broadcasted_iota(jnp.int32, (n_sh * L, L), 0)
    sc = lax.broadcasted_iota(jnp.int32, (n_sh * L, L), 1)
    shift = jnp.where((sr % L) - sc == (sr // L) + 1, 1.0, 0.0).astype(bf16)
    cch = 128
    for cc in range(0, SSD_CONV_DIM, cch):
        if cc and cc % SSD_CONV_YIELD == 0:
            yield
        cs_ = slice(cc, cc + cch)
        xb = xbc_ref[rs, cs_]
        sh = _dot(shift, xb)
        w_now = cw_ref[SSD_CONV - 1:SSD_CONV, cs_]
        acc = cb_ref[:, cs_] + xb.astype(f32) * w_now
        top = cb_ref[:, cs_] + ext_ref[CONV_HALO:2 * CONV_HALO, cs_] * w_now
        for j in range(1, SSD_CONV):
            w_j = cw_ref[SSD_CONV - 1 - j:SSD_CONV - j, cs_]
            acc = acc + sh[(j - 1) * L:j * L] * w_j
            top = top + ext_ref[CONV_HALO - j:2 * CONV_HALO - j, cs_] * w_j
        if cc < SSD_D_INNER:
            dst, o = xs_ref, cc
        elif cc < SSD_D_INNER + SSD_N_GROUPS * SSD_D_STATE:
            dst, o = bm_ref, cc - SSD_D_INNER
        else:
            dst, o = cm_ref, cc - SSD_D_INNER - SSD_N_GROUPS * SSD_D_STATE
        dst[rs, o:o + cch] = _silu(acc)
        dst[r0:r0 + CONV_HALO, o:o + cch] = _silu(top)
    ext_ref[0:CONV_HALO, :] = xbc_ref[rs.stop - bf_tile:rs.stop, :].astype(f32)[CONV_HALO:bf_tile]
    yield

    lane_row = lax.broadcasted_iota(jnp.int32, (1, LANES), 1)
    row_i = lax.broadcasted_iota(jnp.int32, (L, L), 0)
    col_i = lax.broadcasted_iota(jnp.int32, (L, L), 1)
    causal = row_i >= col_i
    left = col_i < SSD_HEAD_DIM

    xdt = dt_ref[rs, :] + dtb_ref[...]
    dtv = jnp.maximum(xdt, 0.0) + jnp.log1p(jnp.exp(-jnp.abs(xdt)))
    a = jnp.where(lane_row < SSD_N_HEADS, -jnp.exp(alog_ref[...]), 0.0)
    d_a = dtv * a
    tril = jnp.where(causal, 1.0, 0.0).astype(bf16)
    hi, mid, lo3 = _split3(d_a)
    a_cum = (_dot(tril, hi) + _dot(tril, mid) + _dot(tril, lo3)) * LOG2E
    a_end = a_cum[L - 1:L, :]
    exp_a = jnp.exp2(a_cum)
    w_end = jnp.exp2(a_end - a_cum) * dtv
    cd = jnp.exp2(a_end)
    a_t = a_cum.T
    dt_t = dtv.T
    w_t = w_end.T
    yield

    n_pairs = SSD_N_HEADS // 2
    pairs_per_group = n_pairs // SSD_N_GROUPS
    for g in range(SSD_N_GROUPS):
        b_g = bm_ref[rs, g * SSD_D_STATE:(g + 1) * SSD_D_STATE]
        c_g = cm_ref[rs, g * SSD_D_STATE:(g + 1) * SSD_D_STATE]
        cb = _dot_nt(c_g.astype(bf16), b_g.astype(bf16))
        b_t = b_g.T
        for pi in range(pairs_per_group):
            i = g * pairs_per_group + pi
            xpair = xs_ref[rs, i * LANES:(i + 1) * LANES]
            xpair_b = xpair.astype(bf16)
            s_prev = st_ref[i]
            rhs = jnp.concatenate([xpair_b, s_prev.astype(bf16)], axis=0)
            ys = []
            sn = []
            for h in (2 * i, 2 * i + 1):
                acol = jnp.broadcast_to(a_cum[:, h:h + 1], (L, L))
                arow = jnp.broadcast_to(a_t[h:h + 1, :], (L, L))
                dtrow = jnp.broadcast_to(dt_t[h:h + 1, :], (L, L))
                dec = jnp.exp2(jnp.where(causal, acol - arow, NEG_BIG))
                m = cb * dec * dtrow
                cs = c_g * jnp.broadcast_to(exp_a[:, h:h + 1], (L, L))
                lhs = jnp.concatenate([m.astype(bf16), cs.astype(bf16)], axis=1)
                ys.append(_dot(lhs, rhs))
                btw = (b_t * jnp.broadcast_to(w_t[h:h + 1, :], (L, L))).astype(bf16)
                sn.append(_dot(btw, xpair_b))
            h0 = 2 * i
            cd_pair = jnp.where(lane_row < SSD_HEAD_DIM, cd[:, h0:h0 + 1], cd[:, h0 + 1:h0 + 2])
            st_ref[i] = jnp.where(left, sn[0], sn[1]) + s_prev * cd_pair
            y_pair = jnp.where(left, ys[0], ys[1])
            y_ref[rs, i * LANES:(i + 1) * LANES] = y_pair + xpair * dexp_ref[:, i * LANES:(i + 1) * LANES]
            if pi % 2 == 1:
                yield

    gw = SSD_D_INNER // SSD_N_GROUPS
    for g in range(SSD_N_GROUPS):
        sl = slice(g * gw, (g + 1) * gw)
        yz = y_ref[rs, sl] * _silu(z_ref[rs, sl].astype(f32))
        ms = jnp.mean(yz * yz, axis=-1, keepdims=True)
        hn_ref[rs, sl] = (yz * lax.rsqrt(ms + RMS_EPS) * ng_ref[:, sl]).astype(bf16)
        if g % 2 == 1:
            yield


ROUTE_E, ROUTE_RANK, ROUTE_GATE = 0, 2, 4
ATT_SUB = 2
SSD_PLAN = (3,) * 9


def _mixer_kernel(sink_ref, q_ref, k_ref, v_ref, cos_ref, sin_ref, ga_ref, x_ref,
                  xbc_ref, z_ref, dt_ref, gs_ref,
                  sel_ref, qg_ref, kg_ref, wao_ref, wo_ref, n2g_ref, wrh_ref, wrl_ref, br_ref,
                  cw_ref, cb_ref, dtb_ref, alog_ref, dexp_ref, ng_ref, wssd_ref,
                  x1_ref, h2_ref, route_ref, route_t_ref, cnt_ref,
                  kprev_ref, vprev_ref, att_ref, cnt_scr, vd_ref,
                  ext_ref, st_ref, xs_ref, bm_ref, cm_ref, y_ref, hn_ref, ms_ref):
    Q = ATT_BLOCK
    b = pl.program_id(0)
    n = pl.program_id(1)

    @pl.when(n == 0)
    def _():
        kprev_ref[...] = jnp.zeros(kprev_ref.shape, f32)
        vprev_ref[...] = jnp.zeros(vprev_ref.shape, f32)
        ext_ref[0:CONV_HALO, :] = jnp.zeros((CONV_HALO, SSD_CONV_DIM), f32)
        st_ref[...] = jnp.zeros(st_ref.shape, f32)

    @pl.when((b == 0) & (n == 0))
    def _():
        cnt_scr[...] = jnp.zeros(cnt_scr.shape, f32)

    R = q_ref.shape[0]

    def ssd_all():
        for c in range(R // SSD_CHUNK):
            yield from _ssd_chunk_phases(slice(c * SSD_CHUNK, (c + 1) * SSD_CHUNK), xbc_ref, z_ref, dt_ref,
                                         cw_ref, cb_ref, dtb_ref, alog_ref, dexp_ref, ng_ref,
                                         ext_ref, st_ref, xs_ref, bm_ref, cm_ref, y_ref, hn_ref)

    ssd_gen = ssd_all()

    plan = iter(SSD_PLAN)

    def ssd_step():
        for _ in range(next(plan, 0)):
            next(ssd_gen, None)
    lane_q = lax.broadcasted_iota(jnp.int32, (Q, LANES), 1)
    row_q = lax.broadcasted_iota(jnp.int32, (Q, LANES), 0)
    left = lane_q < ATT_HEAD_DIM
    head_mean = jnp.where((row_q // ATT_HEAD_DIM) == (lane_q // ATT_HEAD_DIM),
                          1.0 / ATT_HEAD_DIM, 0.0).astype(bf16)
    lane2 = lax.broadcasted_iota(jnp.int32, (2 * Q, LANES), 1) < ATT_HEAD_DIM
    qg = ATT_N_HEADS // ATT_N_KV
    rows = qg * Q
    ri = lax.broadcasted_iota(jnp.int32, (rows, Q), 0) % Q
    cj = lax.broadcasted_iota(jnp.int32, (rows, Q), 1)
    upper = cj > ri
    half = ROPE_DIM // 2

    n_sb = R // Q
    nq = COL_Q // LANES
    nk = COL_KV // LANES
    chunks = []
    for sb in range(n_sb):
        rs = slice(sb * Q, (sb + 1) * Q)
        chunks += [q_ref[rs, c * LANES:(c + 1) * LANES].astype(f32) for c in range(nq)]
        chunks += [k_ref[rs, c * LANES:(c + 1) * LANES].astype(f32) for c in range(nk)]
    u_all = jnp.concatenate(chunks, axis=0)
    sq = u_all * u_all
    sq_hi = sq.astype(bf16)
    sq_lo = (sq - sq_hi.astype(f32)).astype(bf16)
    un_all = u_all * lax.rsqrt(_dot(sq_hi, head_mean) + _dot(sq_lo, head_mean) + RMS_EPS)
    ssd_step()

    terms = []
    for sb in range(n_sb):
        rs = slice(sb * Q, (sb + 1) * Q)
        cs = jnp.concatenate([cos_ref[:, rs], sin_ref[:, rs], jnp.ones((half, LANES), f32),
                              jnp.zeros((LANES - 3 * half, LANES), f32)], axis=0)
        terms += list(_split3(cs.T))
    pat_all = _dot(jnp.concatenate(terms, axis=0), sel_ref[...])

    pats, kds = [], []
    for sb in range(n_sb):
        rs = slice(sb * Q, (sb + 1) * Q)
        pat = pat_all[3 * sb * Q:(3 * sb + 1) * Q] + pat_all[(3 * sb + 1) * Q:(3 * sb + 2) * Q] \
            + pat_all[(3 * sb + 2) * Q:(3 * sb + 3) * Q]
        cpat = pat[:, 0:LANES]
        s1pat = pat[:, LANES:2 * LANES]
        s2pat = pat[:, 2 * LANES:3 * LANES]

        def norm_rope(idx, gpat):
            tn = un_all[idx * Q:(idx + 1) * Q] * gpat
            return (tn * cpat + pltpu.roll(tn, LANES - ROPE_DIM // 2, 1) * s1pat
                    + pltpu.roll(tn, ROPE_DIM // 2, 1) * s2pat)

        kd = []
        for cidx in range(nk):
            sl = slice(cidx * LANES, (cidx + 1) * LANES)
            k_cur = norm_rope(sb * (nq + nk) + nq + cidx, kg_ref[...])
            v_cur = v_ref[rs, sl].astype(f32)
            k_all = jnp.concatenate([kprev_ref[:, sl], k_cur], axis=0)
            v_all = jnp.concatenate([vprev_ref[:, sl], v_cur], axis=0)
            kprev_ref[:, sl] = k_cur
            vprev_ref[:, sl] = v_cur
            k_sw = pltpu.roll(k_all, ATT_HEAD_DIM, 1)
            v_sw = pltpu.roll(v_all, ATT_HEAD_DIM, 1)
            kd.append(jnp.where(lane2, k_all, k_sw).astype(bf16))
            kd.append(jnp.where(lane2, k_sw, k_all).astype(bf16))
            for half_i, vv in enumerate((jnp.where(lane2, v_all, v_sw), jnp.where(lane2, v_sw, v_all))):
                vd_ref[sb * ATT_N_KV + 2 * cidx + half_i] = vv.astype(bf16)

        pats.append((cpat, s1pat, s2pat))
        kds.append(kd)
        ssd_step()

    def rope_q(sb, cidx):
        cpat, s1pat, s2pat = pats[sb]
        tn = un_all[(sb * (nq + nk) + cidx) * Q:(sb * (nq + nk) + cidx + 1) * Q] * qg_ref[...]
        return (tn * cpat + pltpu.roll(tn, LANES - ROPE_DIM // 2, 1) * s1pat
                + pltpu.roll(tn, ROPE_DIM // 2, 1) * s2pat)

    def score_tile(g):
        sb, h = divmod(g, ATT_N_KV)
        parts = []
        for cidx in (2 * h, 2 * h + 1):
            qc = rope_q(sb, cidx)
            parts.append(jnp.where(left, qc, 0.0).astype(bf16))
            parts.append(jnp.where(left, 0.0, qc).astype(bf16))
        lhs = jnp.concatenate(parts, axis=0)
        s_both = _dot_nt(lhs, kds[sb][h])
        s_prev = s_both[:, 0:Q]
        s_cur = s_both[:, Q:2 * Q]
        if sb == 0:
            return jnp.where(upper & (n > 0), s_prev, jnp.where(upper, NEG_BIG, s_cur))
        return jnp.where(upper, s_prev, s_cur)

    ones = jnp.ones((Q, Q), bf16)

    def softmax_block(s):
        sink = jnp.concatenate([jnp.full((Q, 1), sink_ref[i] * LOG2E, f32) for i in range(ATT_N_HEADS)],
                               axis=0)
        m = jnp.maximum(jnp.max(s, axis=-1, keepdims=True), sink)
        p = jnp.exp2(s - m)
        p_hi = p.astype(bf16)
        p_lo = (p - p_hi.astype(f32)).astype(bf16)
        denom = _dot(p_hi, ones) + _dot(p_lo, ones) + jnp.exp2(sink - m)
        return (p / denom).astype(bf16)

    def pv_tile(g, pf):
        sb, h = divmod(g, ATT_N_KV)
        rs = slice(sb * Q, (sb + 1) * Q)
        zero = jnp.zeros_like(pf)
        p_both = jnp.concatenate([jnp.where(upper, pf, zero), jnp.where(upper, zero, pf)], axis=1)
        o = _dot(p_both, vd_ref[g])
        for r in range(2):
            cidx = 2 * h + r
            att_ref[rs, cidx * LANES:(cidx + 1) * LANES] = jnp.where(
                left, o[(2 * r) * Q:(2 * r + 1) * Q], o[(2 * r + 1) * Q:(2 * r + 2) * Q]).astype(bf16)

    def epilogue(rs, cnt):
        E = rs.stop - rs.start
        lane = lax.broadcasted_iota(jnp.int32, (E, LANES), 1)
        y_att = _dot(att_ref[rs, :], wao_ref[...])
        merged = _sigmoid(ga_ref[rs, :].astype(f32)) * y_att + ms_ref[rs, :]
        x1 = x_ref[rs, :] + _dot(merged.astype(bf16), wo_ref[...])
        x1_ref[rs, :] = x1
        h2 = x1 * lax.rsqrt(jnp.mean(x1 * x1, axis=-1, keepdims=True) + RMS_EPS) * n2g_ref[...]
        h2_ref[rs, :] = _pack_rows(h2)

        hi = h2.astype(bf16)
        lo = (h2 - hi.astype(f32)).astype(bf16)
        logits = _dot(hi, wrh_ref[...]) + _dot(lo, wrh_ref[...]) + _dot(hi, wrl_ref[...]) + br_ref[...]
        big = 4 * LANES
        gl = jnp.where(lane < MOE_N_GROUPS, logits, NEG_BIG)
        gmax = jnp.max(gl, axis=-1, keepdims=True)
        gsel = jnp.min(jnp.where(gl == gmax, lane, big), axis=-1, keepdims=True)
        pg = 1.0 / jnp.sum(jnp.exp(gl - gmax), axis=-1, keepdims=True)
        lo_l = MOE_N_GROUPS + MOE_EPG * gsel
        el = jnp.where((lane >= lo_l) & (lane < lo_l + MOE_EPG), logits, NEG_BIG)
        v1 = jnp.max(el, axis=-1, keepdims=True)
        i1 = jnp.min(jnp.where(el == v1, lane, big), axis=-1, keepdims=True)
        el2 = jnp.where(lane == i1, NEG_BIG, el)
        v2 = jnp.max(el2, axis=-1, keepdims=True)
        i2 = jnp.min(jnp.where(el2 == v2, lane, big), axis=-1, keepdims=True)
        e1 = i1 - MOE_N_GROUPS
        e2 = i2 - MOE_N_GROUPS
        tt = jnp.exp(v2 - v1)
        w1 = pg * (1.0 / (1.0 + tt))
        w2 = pg * (tt / (1.0 + tt))

        onehot = jnp.where((lane == e1) | (lane == e2), 1.0, 0.0)
        strict = jnp.where(lax.broadcasted_iota(jnp.int32, (E, E), 0) > lax.broadcasted_iota(jnp.int32, (E, E), 1),
                           1.0, 0.0).astype(bf16)
        base = _dot(strict, onehot.astype(bf16)) + cnt
        r1 = jnp.sum(jnp.where(lane == e1, base, 0.0), axis=-1, keepdims=True)
        r2 = jnp.sum(jnp.where(lane == e2, base, 0.0), axis=-1, keepdims=True)

        rec = jnp.zeros((E, LANES), f32)
        for off, val in ((ROUTE_E, e1.astype(f32)), (ROUTE_E + 1, e2.astype(f32)),
                         (ROUTE_RANK, r1), (ROUTE_RANK + 1, r2), (ROUTE_GATE, w1), (ROUTE_GATE + 1, w2)):
            rec = jnp.where(lane == off, val, rec)
        route_ref[rs, :] = rec
        for o in range(0, E, Q):
            route_t_ref[:, rs.start + o:rs.start + o + Q] = rec[o:o + Q].T[0:8, :]
        return cnt + jnp.sum(onehot, axis=0, keepdims=True)

    s_tiles, p_tiles = {}, {}
    for stage in range(n_sb + 2):
        if stage < n_sb:
            s_tiles[stage] = jnp.concatenate(
                [score_tile(stage * ATT_N_KV + h) for h in range(ATT_N_KV)], axis=0)
            ssd_step()
        if 0 <= stage - 1 < n_sb:
            p_tiles[stage - 1] = softmax_block(s_tiles.pop(stage - 1))
            ssd_step()
        if 0 <= stage - 2 < n_sb:
            pb = p_tiles.pop(stage - 2)
            for h in range(ATT_N_KV):
                pv_tile((stage - 2) * ATT_N_KV + h, pb[h * rows:(h + 1) * rows])
            ssd_step()

    for _ in ssd_gen:
        pass
    ms_ref[...] = _sigmoid(gs_ref[...].astype(f32)) * _dot(hn_ref[...], wssd_ref[...])
    cnt = epilogue(slice(0, R), cnt_scr[0:1, :])

    cnt_scr[...] = jnp.broadcast_to(cnt, cnt_scr.shape)
    cnt_ref[...] = jnp.broadcast_to(cnt, cnt_ref.shape)


def _mixer(q, k, v, cos_t, sin_t, ga, x2, xbc, z, dt, gs, q_norm_g, k_norm_g, sinks, w_attn_out, w_out,
           norm2_g, w_rg, b_rg, w_re, b_re, conv_w, conv_b, dt_bias, a_log, d_skip, ssd_norm_g, w_ssd_out,
           batch, seq):
    t = batch * seq
    Q = ATT_SUB * ATT_BLOCK
    nb = seq // Q
    pad_h = LANES - SSD_N_HEADS
    dtb = jnp.pad(dt_bias, (0, pad_h)).reshape(1, LANES)
    alog = jnp.pad(a_log, (0, pad_h)).reshape(1, LANES)
    dexp = jnp.repeat(d_skip, SSD_HEAD_DIM).reshape(1, SSD_D_INNER)
    rep = LANES // ATT_HEAD_DIM
    qg = (jnp.tile(q_norm_g, rep) * (ATT_SCALE * LOG2E)).reshape(1, LANES)
    kg = jnp.tile(k_norm_g, rep).reshape(1, LANES)
    n_log = MOE_N_GROUPS + MOE_N_EXPERTS
    w_r = jnp.concatenate([w_rg, jnp.transpose(w_re, (1, 0, 2)).reshape(D_MODEL, MOE_N_EXPERTS),
                           jnp.zeros((D_MODEL, LANES - n_log), f32)], axis=1)
    b_r = jnp.concatenate([b_rg, b_re.reshape(-1), jnp.zeros((LANES - n_log,), f32)]).reshape(1, LANES)
    w_r_hi = w_r.astype(bf16)
    w_r_lo = (w_r - w_r_hi.astype(f32)).astype(bf16)
    const = lambda b, n, *_: (0, 0)
    row = lambda b, n, *_: (b * nb + n, 0)
    col = lambda b, n, *_: (0, b * nb + n)
    full = lambda shape: pl.BlockSpec(shape, const, pipeline_mode=pl.Buffered(1))
    return pl.pallas_call(
        _mixer_kernel,
        grid_spec=pltpu.PrefetchScalarGridSpec(
            num_scalar_prefetch=1,
            grid=(batch, nb),
            in_specs=[pl.BlockSpec((Q, COL_Q), row),
                      pl.BlockSpec((Q, COL_KV), row),
                      pl.BlockSpec((Q, COL_KV), row),
                      pl.BlockSpec((ROPE_DIM // 2, Q), col),
                      pl.BlockSpec((ROPE_DIM // 2, Q), col),
                      pl.BlockSpec((Q, D_MODEL), row),
                      pl.BlockSpec((Q, D_MODEL), row),
                      pl.BlockSpec((Q, SSD_CONV_DIM), row),
                      pl.BlockSpec((Q, SSD_D_INNER), row),
                      pl.BlockSpec((Q, LANES), row),
                      pl.BlockSpec((Q, D_MODEL), row),
                      full((LANES, 3 * LANES)),
                      pl.BlockSpec((1, LANES), const),
                      pl.BlockSpec((1, LANES), const),
                      full((COL_Q, D_MODEL)),
                      full((D_MODEL, D_MODEL)),
                      pl.BlockSpec((1, D_MODEL), const),
                      full((D_MODEL, LANES)),
                      full((D_MODEL, LANES)),
                      pl.BlockSpec((1, LANES), const),
                      pl.BlockSpec((SSD_CONV, SSD_CONV_DIM), const),
                      pl.BlockSpec((1, SSD_CONV_DIM), const),
                      pl.BlockSpec((1, LANES), const),
                      pl.BlockSpec((1, LANES), const),
                      pl.BlockSpec((1, SSD_D_INNER), const),
                      pl.BlockSpec((1, SSD_D_INNER), const),
                      full((SSD_D_INNER, D_MODEL))],
            out_specs=[pl.BlockSpec((Q, D_MODEL), row),
                       pl.BlockSpec((Q, HALF_D), row),
                       pl.BlockSpec((Q, LANES), row),
                       pl.BlockSpec((8, Q), col),
                       pl.BlockSpec((8, LANES), const)],
            scratch_shapes=[pltpu.VMEM((ATT_BLOCK, COL_KV), f32),
                            pltpu.VMEM((ATT_BLOCK, COL_KV), f32),
                            pltpu.VMEM((Q, COL_Q), bf16),
                            pltpu.VMEM((8, LANES), f32),
                            pltpu.VMEM((ATT_SUB * ATT_N_KV, 2 * ATT_BLOCK, LANES), bf16),
                            pltpu.VMEM((2 * CONV_HALO, SSD_CONV_DIM), f32),
                            pltpu.VMEM((SSD_N_HEADS // 2, SSD_D_STATE, LANES), f32),
                            pltpu.VMEM((Q, SSD_D_INNER), f32),
                            pltpu.VMEM((Q, SSD_N_GROUPS * SSD_D_STATE), f32),
                            pltpu.VMEM((Q, SSD_N_GROUPS * SSD_D_STATE), f32),
                            pltpu.VMEM((Q, SSD_D_INNER), f32),
                            pltpu.VMEM((Q, SSD_D_INNER), bf16),
                            pltpu.VMEM((Q, D_MODEL), f32)],
        ),
        out_shape=[jax.ShapeDtypeStruct((t, D_MODEL), f32),
                   jax.ShapeDtypeStruct((t, HALF_D), jnp.uint32),
                   jax.ShapeDtypeStruct((t, LANES), f32),
                   jax.ShapeDtypeStruct((8, t), f32),
                   jax.ShapeDtypeStruct((8, LANES), f32)],
        compiler_params=pltpu.CompilerParams(dimension_semantics=("arbitrary", "arbitrary"),
                                             vmem_limit_bytes=VMEM_LIMIT),
        name="mixer",
    )(sinks, q, k, v, cos_t, sin_t, ga, x2, xbc, z, dt, gs,
      _rope_select(), qg, kg, w_attn_out.astype(bf16), w_out.astype(bf16), norm2_g.reshape(1, D_MODEL),
      w_r_hi, w_r_lo, b_r,
      conv_w, conv_b.reshape(1, -1), dtb, alog, dexp, ssd_norm_g.reshape(1, -1), w_ssd_out.astype(bf16))


DISPATCH_TM = 1024
COMBINE_TM = 512


def _dispatch_kernel(dest0_ref, dest1_ref, zblk_ref, h2_ref, xs_hbm, zero_ref, sem, zsem):
    tm = h2_ref.shape[0] * SUBLANES
    base = pl.program_id(0) * tm

    @pl.when(pl.program_id(0) == 0)
    def _():
        zero_ref[...] = jnp.zeros(zero_ref.shape, zero_ref.dtype)

        def zcopy(i):
            start = pl.multiple_of(zblk_ref[i] * MOE_BLOCK, MOE_BLOCK)
            return pltpu.make_async_copy(zero_ref, xs_hbm.at[pl.ds(start, MOE_BLOCK), :], zsem)

        def zstart(i, carry):
            @pl.when(zblk_ref[i] >= 0)
            def _():
                zcopy(i).start()
            return carry

        def zwait(i, carry):
            @pl.when(zblk_ref[i] >= 0)
            def _():
                zcopy(i).wait()
            return carry

        lax.fori_loop(0, zblk_ref.shape[0], zstart, 0)
        lax.fori_loop(0, zblk_ref.shape[0], zwait, 0)

    def issue(k, carry):
        for u in range(SUBLANES):
            i = k * SUBLANES + u
            src = h2_ref.at[k, pl.ds(u, 1), :]
            pltpu.make_async_copy(src, xs_hbm.at[pl.ds(dest0_ref[base + i], 1), :], sem).start()
            pltpu.make_async_copy(src, xs_hbm.at[pl.ds(dest1_ref[base + i], 1), :], sem).start()
        return carry

    lax.fori_loop(0, tm // SUBLANES, issue, 0)
    for _ in range(2):
        pltpu.make_async_copy(xs_hbm.at[pl.ds(0, tm), :], xs_hbm.at[pl.ds(0, tm), :], sem).wait()


def _dispatch(h2, dest0, dest1, zero_blocks, n_rows):
    t = h2.shape[0]
    tm = min(DISPATCH_TM, t)
    return pl.pallas_call(
        _dispatch_kernel,
        grid_spec=pltpu.PrefetchScalarGridSpec(
            num_scalar_prefetch=3,
            grid=(t // tm,),
            in_specs=[pl.BlockSpec((tm // SUBLANES, SUBLANES, HALF_D), lambda i, d0, d1, zb: (i, 0, 0))],
            out_specs=pl.BlockSpec(memory_space=pl.ANY),
            scratch_shapes=[pltpu.VMEM((MOE_BLOCK, HALF_D), h2.dtype),
                            pltpu.SemaphoreType.DMA(()),
                            pltpu.SemaphoreType.DMA(())],
        ),
        out_shape=jax.ShapeDtypeStruct((n_rows, HALF_D), h2.dtype),
        compiler_params=pltpu.CompilerParams(dimension_semantics=("arbitrary",)),
        name="dispatch",
    )(dest0, dest1, zero_blocks, h2.reshape(t // SUBLANES, SUBLANES, HALF_D))


EXPERT_SUB = 1


def _expert_kernel(be_ref, nu_ref, xs_ref, *refs):
    w_refs = refs[:3 * EXPERT_SUB]
    ys_ref, wgu_scr, wd_scr = refs[3 * EXPERT_SUB:]
    i = pl.program_id(0)
    rows = MOE_BLOCK

    for j in range(EXPERT_SUB):
        blk = EXPERT_SUB * i + j
        fresh = (i == 0) | (be_ref[blk] != be_ref[jnp.maximum(blk - EXPERT_SUB, 0)])

        @pl.when((blk < nu_ref[0]) & fresh)
        def _(j=j):
            wg_ref, wu_ref, wd_ref = w_refs[3 * j:3 * j + 3]
            wgu_scr[j, :, 0:MOE_D_FF] = wg_ref[0].astype(bf16)
            wgu_scr[j, :, MOE_D_FF:2 * MOE_D_FF] = wu_ref[0].astype(bf16)
            wd_scr[j] = wd_ref[0].astype(bf16)

    def block(j):
        rs = slice(j * rows, (j + 1) * rows)
        x_lo, x_hi = _unpack_rows(xs_ref[rs, :])
        gu = (_dot(x_lo.astype(bf16), wgu_scr[j, 0:HALF_D, :])
              + _dot(x_hi.astype(bf16), wgu_scr[j, HALF_D:D_MODEL, :]))
        hid = _silu(gu[:, :MOE_D_FF]) * gu[:, MOE_D_FF:]
        ys_ref[rs, :] = _pack_rows(_dot(hid.astype(bf16), wd_scr[j]))

    n_live = jnp.clip(nu_ref[0] - EXPERT_SUB * i, 0, EXPERT_SUB)
    for live in range(EXPERT_SUB + 1):
        @pl.when(n_live == live)
        def _(live=live):
            for j in range(live):
                block(j)
            if live < EXPERT_SUB:
                ys_ref[live * rows:, :] = jnp.zeros(((EXPERT_SUB - live) * rows, HALF_D), ys_ref.dtype)


def _experts(xs, block_e, n_used, w_g, w_u, w_d):
    n_rows = xs.shape[0]
    rows = EXPERT_SUB * MOE_BLOCK
    nstep = n_rows // rows
    blk_in = lambda i, be, nu: (jnp.minimum(i, (nu[0] - 1) // EXPERT_SUB), 0)
    blk_out = lambda i, be, nu: (i, 0)
    w_specs = []
    for j in range(EXPERT_SUB):
        wsel = lambda i, be, nu, j=j: (be[EXPERT_SUB * i + j], 0, 0)
        w_specs += [pl.BlockSpec((1, D_MODEL, MOE_D_FF), wsel),
                    pl.BlockSpec((1, D_MODEL, MOE_D_FF), wsel),
                    pl.BlockSpec((1, MOE_D_FF, D_MODEL), wsel)]
    return pl.pallas_call(
        _expert_kernel,
        grid_spec=pltpu.PrefetchScalarGridSpec(
            num_scalar_prefetch=2,
            grid=(nstep,),
            in_specs=[pl.BlockSpec((rows, HALF_D), blk_in)] + w_specs,
            out_specs=pl.BlockSpec((rows, HALF_D), blk_out),
            scratch_shapes=[pltpu.VMEM((EXPERT_SUB, D_MODEL, 2 * MOE_D_FF), bf16),
                            pltpu.VMEM((EXPERT_SUB, MOE_D_FF, D_MODEL), bf16)],
        ),
        out_shape=jax.ShapeDtypeStruct((n_rows, HALF_D), xs.dtype),
        compiler_params=pltpu.CompilerParams(dimension_semantics=("arbitrary",),
                                             vmem_limit_bytes=VMEM_LIMIT),
        name="experts",
    )(block_e, n_used, xs, *([w_g, w_u, w_d] * EXPERT_SUB))


def _combine_kernel(dest0_ref, dest1_ref, x1_ref, route_ref, ys_hbm, out_ref, y0_ref, y1_ref, sem):
    tm = x1_ref.shape[0]
    step = pl.program_id(0)

    def gather_tile(tile, slot):
        base = tile * tm

        def issue(k, carry):
            for u in range(SUBLANES):
                r = k * SUBLANES + u
                pltpu.make_async_copy(ys_hbm.at[pl.ds(dest0_ref[base + r], 1), :],
                                      y0_ref.at[slot, k, pl.ds(u, 1), :], sem.at[slot]).start()
                pltpu.make_async_copy(ys_hbm.at[pl.ds(dest1_ref[base + r], 1), :],
                                      y1_ref.at[slot, k, pl.ds(u, 1), :], sem.at[slot]).start()
            return carry

        lax.fori_loop(0, tm // SUBLANES, issue, 0)

    @pl.when(step == 0)
    def _():
        gather_tile(0, 0)

    @pl.when(step + 1 < pl.num_programs(0))
    def _():
        gather_tile(step + 1, (step + 1) % 2)

    slot = step % 2
    for _ in range(2):
        pltpu.make_async_copy(ys_hbm.at[pl.ds(0, tm), :], ys_hbm.at[pl.ds(0, tm), :], sem.at[slot]).wait()
    g0 = route_ref[:, ROUTE_GATE:ROUTE_GATE + 1]
    g1 = route_ref[:, ROUTE_GATE + 1:ROUTE_GATE + 2]
    y0_lo, y0_hi = _unpack_rows(y0_ref[slot].reshape(tm, HALF_D))
    y1_lo, y1_hi = _unpack_rows(y1_ref[slot].reshape(tm, HALF_D))
    out_ref[:, 0:HALF_D] = x1_ref[:, 0:HALF_D] + y0_lo * g0 + y1_lo * g1
    out_ref[:, HALF_D:D_MODEL] = x1_ref[:, HALF_D:D_MODEL] + y0_hi * g0 + y1_hi * g1


def _combine(x1, route, ys, dest0, dest1):
    t = x1.shape[0]
    tm = min(COMBINE_TM, t)
    row = lambda i, d0, d1: (i, 0)
    return pl.pallas_call(
        _combine_kernel,
        grid_spec=pltpu.PrefetchScalarGridSpec(
            num_scalar_prefetch=2,
            grid=(t // tm,),
            in_specs=[pl.BlockSpec((tm, D_MODEL), row),
                      pl.BlockSpec((tm, LANES), row),
                      pl.BlockSpec(memory_space=pl.ANY)],
            out_specs=pl.BlockSpec((tm, D_MODEL), row),
            scratch_shapes=[pltpu.VMEM((2, tm // SUBLANES, SUBLANES, HALF_D), ys.dtype),
                            pltpu.VMEM((2, tm // SUBLANES, SUBLANES, HALF_D), ys.dtype),
                            pltpu.SemaphoreType.DMA((2,))],
        ),
        out_shape=jax.ShapeDtypeStruct((t, D_MODEL), f32),
        compiler_params=pltpu.CompilerParams(dimension_semantics=("arbitrary",)),
        name="combine",
    )(dest0, dest1, x1, route, ys)


def _moe(x1, h2, route, route_t, counts_rec, w_gate_e, w_up_e, w_down_e):
    t = x1.shape[0]
    n_rows = 2 * t + MOE_N_EXPERTS * MOE_BLOCK
    nblk = n_rows // MOE_BLOCK
    counts = counts_rec[0, :MOE_N_EXPERTS].astype(jnp.int32)
    padded = (counts + MOE_BLOCK - 1) // MOE_BLOCK * MOE_BLOCK
    pend = jnp.cumsum(padded)
    pstart = pend - padded
    expert_ids = jnp.arange(MOE_N_EXPERTS, dtype=jnp.int32)[:, None]

    def sorted_row(k):
        eid = route_t[ROUTE_E + k].astype(jnp.int32)
        rank = route_t[ROUTE_RANK + k].astype(jnp.int32)
        return jnp.sum(jnp.where(eid[None, :] == expert_ids, pstart[:, None], 0), axis=0) + rank

    dest0, dest1 = sorted_row(0), sorted_row(1)
    block_start = jnp.arange(nblk, dtype=jnp.int32) * MOE_BLOCK
    block_e = jnp.minimum(jnp.sum((pend[None, :] <= block_start[:, None]).astype(jnp.int32), axis=1),
                          MOE_N_EXPERTS - 1)
    n_used = (pend[-1:] // MOE_BLOCK).astype(jnp.int32)
    last_blk = jnp.where(counts > 0, pend // MOE_BLOCK - 1, -1)
    tail = n_used[0] + jnp.arange(MOE_N_EXPERTS, dtype=jnp.int32)
    zero_blocks = jnp.concatenate([last_blk, jnp.where(tail < nblk, tail, -1)]).astype(jnp.int32)
    xs = _dispatch(h2, dest0, dest1, zero_blocks, n_rows)
    ys = _experts(xs, block_e, n_used, w_gate_e, w_up_e, w_down_e)
    return _combine(x1, route, ys, dest0, dest1)


def _layer(x, positions, norm1_g, w_in, conv_w, conv_b, dt_bias, a_log, d_skip, ssd_norm_g, w_ssd_out,
           q_norm_g, k_norm_g, sinks, w_attn_out, w_out, norm2_g, w_rg, b_rg, w_re, b_re,
           w_gate_e, w_up_e, w_down_e):
    batch, seq, d = x.shape
    x2 = x.reshape(batch * seq, d)
    cos_t, sin_t = _rope_tables(positions)
    z, xbc, q, k, v, gs, ga, dt = _inproj(x2, norm1_g, w_in)
    x1, h2, route, route_t, counts = _mixer(q, k, v, cos_t, sin_t, ga, x2, xbc, z, dt, gs, q_norm_g, k_norm_g,
                                            sinks, w_attn_out, w_out, norm2_g, w_rg, b_rg, w_re, b_re,
                                            conv_w, conv_b, dt_bias, a_log, d_skip, ssd_norm_g, w_ssd_out,
                                            batch, seq)
    out = _moe(x1, h2, route, route_t, counts, w_gate_e, w_up_e, w_down_e)
    return out.reshape(batch, seq, d)


def kernel(x, positions, norm1_g, w_in, conv_w, conv_b, dt_bias, a_log, d_skip, ssd_norm_g, w_ssd_out,
           q_norm_g, k_norm_g, sinks, w_attn_out, w_out, norm2_g, w_router_group, b_router_group,
           w_router_expert, b_router_expert, w_gate_e, w_up_e, w_down_e):
    for l in range(norm1_g.shape[0]):
        x = _layer(x, positions, norm1_g[l], w_in[l], conv_w[l], conv_b[l], dt_bias[l], a_log[l],
                   d_skip[l], ssd_norm_g[l], w_ssd_out[l], q_norm_g[l], k_norm_g[l], sinks[l],
                   w_attn_out[l], w_out[l], norm2_g[l], w_router_group[l], b_router_group[l],
                   w_router_expert[l], b_router_expert[l], w_gate_e[l], w_up_e[l], w_down_e[l])
    return x
```
